```python
import jax, jax.numpy as jnp
from jax import lax
import numpy as np

D_MODEL = 1024
BATCH = 16
SEQ = 256
DEPTH = 1
DEC_BATCH = 2
DEC_SEQ = 4096
PAST_LEN = 256

GRID_W = 64
NH_M = 4
DH_M = D_MODEL // NH_M
D_MLSTM = NH_M * DH_M
CHUNK = 64
N_Q = 8
N_KV = 2
G_Q = N_Q // N_KV
DH_A = 128
D_ATT_Q = N_Q * DH_A
D_ATT_KV = N_KV * DH_A
N_FREQ = DH_A // 4
ROPE_BASE = 10000.0
D_FF = 4 * D_MODEL
Q_BLOCK = 128
EPS = 1e-6
ALPHA = (2 * DEPTH) ** 0.25
BETA = (8 * DEPTH) ** -0.25
SPLITS = (D_MLSTM, D_MLSTM, D_MLSTM, D_MLSTM, 4 * NH_M, D_ATT_Q, D_ATT_KV, D_ATT_KV, D_MODEL, D_MODEL)
D_IN = sum(SPLITS)

kernel_name = 'hybrid_mlstm_gqa_dit_step'


def _split(x, sizes):
    idx = np.cumsum(sizes)[:-1].tolist()
    return jnp.split(x, idx, axis=-1)


def layer_norm(x, g, b):
    xf = x.astype(jnp.float32)
    mu = jnp.mean(xf, -1, keepdims=True)
    var = jnp.mean(jnp.square(xf - mu), -1, keepdims=True)
    return ((xf - mu) * lax.rsqrt(var + EPS)).astype(x.dtype) * g + b


def rms_norm(x, g):
    xf = x.astype(jnp.float32)
    return (xf * lax.rsqrt(jnp.mean(jnp.square(xf), -1, keepdims=True) + EPS)).astype(x.dtype) * g


def axial_rope_tables(n_tokens):
    rows = n_tokens // GRID_W
    row = jnp.repeat(jnp.arange(rows), GRID_W)
    col = jnp.tile(jnp.arange(GRID_W), rows)
    inv = ROPE_BASE ** (-jnp.arange(N_FREQ, dtype=jnp.float32) / N_FREQ)
    ang = jnp.stack([row, col], -1).astype(jnp.float32)[..., None] * inv
    ang = jnp.broadcast_to(ang[:, :, None, :], (n_tokens, 2, 2, N_FREQ))
    return jnp.cos(ang), jnp.sin(ang)


def apply_rope(x, cos, sin):
    xs = x.reshape(*x.shape[:-1], 2, 2, N_FREQ)
    rot = jnp.concatenate([-xs[..., 1:, :], xs[..., :1, :]], axis=-2)
    out = xs * cos[None, :, None] + rot * sin[None, :, None]
    return out.reshape(x.shape).astype(x.dtype)


def block_attention(q, k, v):
    B, T = q.shape[:2]
    nb = T // Q_BLOCK
    qb = jnp.moveaxis(q.reshape(B, nb, Q_BLOCK, N_KV, G_Q, DH_A), 1, 0)
    scale = DH_A ** -0.5

    def one_block(qblk):
        s = jnp.einsum('bqhgd,bkhd->bhgqk', qblk, k).astype(jnp.float32) * scale
        p = jax.nn.softmax(s, axis=-1).astype(v.dtype)
        return jnp.einsum('bhgqk,bkhd->bqhgd', p, v)

    o = lax.map(one_block, qb)
    return jnp.moveaxis(o, 0, 1).reshape(B, T, D_ATT_Q)


def mlstm_chunked(q, k, v, ig, lf, C0, n0, m0):
    B, H, T, _ = q.shape
    nc = T // CHUNK

    def to_chunks(a):
        return jnp.moveaxis(a.reshape(B, H, nc, CHUNK, *a.shape[3:]), 2, 0)

    causal = jnp.tril(jnp.ones((CHUNK, CHUNK), bool))

    def step(carry, xs):
        C, n, m = carry
        qc, kc, vc, ic, fc = xs
        b = jnp.cumsum(fc, -1)
        dmat = jnp.where(causal, b[..., :, None] - b[..., None, :] + ic[..., None, :], -jnp.inf)
        g_inter = b + m[..., None]
        m_row = jnp.maximum(g_inter, jnp.max(dmat, -1))
        s = jnp.einsum('bhid,bhjd->bhij', qc, kc) * jnp.exp(dmat - m_row[..., None])
        w_inter = jnp.exp(g_inter - m_row)
        num = w_inter[..., None] * jnp.einsum('bhid,bhde->bhie', qc, C) + jnp.einsum('bhij,bhje->bhie', s, vc)
        den = w_inter * jnp.einsum('bhid,bhd->bhi', qc, n) + jnp.sum(s, -1)
        h = num / jnp.maximum(jnp.abs(den), jnp.exp(-m_row))[..., None]
        b_last = b[..., -1]
        w_end = b_last[..., None] - b + ic
        m_new = jnp.maximum(b_last + m, jnp.max(w_end, -1))
        decay = jnp.exp(b_last + m - m_new)
        wk = jnp.exp(w_end - m_new[..., None])[..., None] * kc
        C_new = decay[..., None, None] * C + jnp.einsum('bhjd,bhje->bhde', wk, vc)
        n_new = decay[..., None] * n + jnp.sum(wk, -2)
        return (C_new, n_new, m_new), h

    carry0 = (C0.astype(jnp.float32), n0.astype(jnp.float32), m0.astype(jnp.float32))
    (C, n, m), hs = lax.scan(step, carry0, (to_chunks(q), to_chunks(k), to_chunks(v), to_chunks(ig), to_chunks(lf)))
    h = jnp.moveaxis(hs, 0, 2).reshape(B, H, T, -1)
    return h, C, n, m


def mlstm_bidir(q, k, v, gates, C0, n0, m0):
    B, T = q.shape[:2]
    tr = lambda a: jnp.transpose(a.astype(jnp.float32), (0, 2, 1, 3))
    qf, kf, vf = tr(q), tr(k) * (DH_M ** -0.5), tr(v)
    g4 = jnp.transpose(gates.astype(jnp.float32).reshape(B, T, 4, NH_M), (2, 0, 3, 1))
    ig_f, lf_f, ig_b, lf_b = g4[0], jax.nn.log_sigmoid(g4[1]), g4[2], jax.nn.log_sigmoid(g4[3])
    hf, Cf, nf, mf = mlstm_chunked(qf, kf, vf, ig_f, lf_f, C0[:, 0], n0[:, 0], m0[:, 0])
    flip = lambda a: jnp.flip(a, axis=2)
    hb, Cb, nb, mb = mlstm_chunked(flip(qf), flip(kf), flip(vf), flip(ig_b), flip(lf_b), C0[:, 1], n0[:, 1], m0[:, 1])
    h = jnp.transpose(hf + flip(hb), (0, 2, 1, 3)).astype(q.dtype)
    return h, jnp.stack([Cf, Cb], 1), jnp.stack([nf, nb], 1), jnp.stack([mf, mb], 1)


def mixer(h, w_in, b_gates, mlstm_norm_g, q_norm_g, k_norm_g, w_bm, w_ba, w_out, C0, n0, m0, ctx_kv, rope):
    B, T, _ = h.shape
    qm, km, vm, om, gates, qa, ka, va, gm, ga = _split(h @ w_in, SPLITS)
    shp_m = (B, T, NH_M, DH_M)
    hm, C, n, m = mlstm_bidir(qm.reshape(shp_m), km.reshape(shp_m), vm.reshape(shp_m), gates + b_gates, C0, n0, m0)
    hm = rms_norm(hm, mlstm_norm_g.reshape(NH_M, DH_M)).reshape(B, T, D_MLSTM) * jax.nn.sigmoid(om)
    qa = rms_norm(qa.reshape(B, T, N_Q, DH_A), q_norm_g)
    ka = rms_norm(ka.reshape(B, T, N_KV, DH_A), k_norm_g)
    va = va.reshape(B, T, N_KV, DH_A)
    if rope is None:
        keys, vals = ka, va
    else:
        qa = apply_rope(qa, *rope)
        ka = apply_rope(ka, *rope)
        keys = jnp.concatenate([ka, ctx_kv[0].astype(ka.dtype)], axis=1)
        vals = jnp.concatenate([va, ctx_kv[1].astype(va.dtype)], axis=1)
    ha = block_attention(qa.reshape(B, T, N_KV, G_Q, DH_A), keys, vals)
    merged = jax.nn.sigmoid(gm) * (hm @ w_bm) + jax.nn.sigmoid(ga) * (ha @ w_ba)
    return merged @ w_out, (ka, va), (C, n, m)


def trunk_layer(x, mod, w_in, b_gates, mlstm_norm_g, q_norm_g, k_norm_g, w_bm, w_ba, w_out,
                ln1_g, ln1_b, w_up, w_down, ln2_g, ln2_b, C0, n0, m0, ctx_kv, rope):
    sh1, sc1, g1, sh2, sc2, g2 = jnp.split(mod[:, None, :], 6, axis=-1)
    h = x * (1 + sc1) + sh1
    mix, kv, st = mixer(h, w_in, b_gates, mlstm_norm_g, q_norm_g, k_norm_g, w_bm, w_ba, w_out, C0, n0, m0, ctx_kv, rope)
    x = layer_norm(ALPHA * x + g1 * mix, ln1_g, ln1_b)
    h = x * (1 + sc2) + sh2
    ff = jnp.square(jax.nn.relu(h @ w_up)) @ w_down
    x = layer_norm(ALPHA * x + g2 * ff, ln2_g, ln2_b)
    return x, kv, st


def setup_inputs(seed: int = 0) -> dict:
    key = jax.random.key(seed)
    ks = jax.random.split(key, 26)
    nrm = lambda k, shape, s: jax.random.normal(k, shape, jnp.float32) * s
    col_scale = jnp.asarray(np.concatenate(
        [np.full(s, BETA if i in (2, 7) else 1.0, np.float32) for i, s in enumerate(SPLITS)]))
    gate_offset = jnp.repeat(jnp.array([0.0, 3.0, 0.0, 3.0], jnp.float32), NH_M)
    return {
        'x_prompt': nrm(ks[0], (BATCH, SEQ, D_MODEL), 1.0),
        'x_sample': nrm(ks[1], (DEC_BATCH, DEC_SEQ, D_MODEL), 1.0),
        'cache_k': nrm(ks[2], (DEC_BATCH, DEPTH, PAST_LEN, N_KV, DH_A), 1.0),
        'cache_v': nrm(ks[3], (DEC_BATCH, DEPTH, PAST_LEN, N_KV, DH_A), 0.5),
        'state_C': nrm(ks[4], (DEC_BATCH, DEPTH, 2, NH_M, DH_M, DH_M), 0.05),
        'state_n': nrm(ks[5], (DEC_BATCH, DEPTH, 2, NH_M, DH_M), 0.05),
        'state_m': nrm(ks[6], (DEC_BATCH, DEPTH, 2, NH_M), 0.5),
        'c': nrm(ks[7], (DEC_BATCH, D_MODEL), 1.0),
        'c_ctx': nrm(ks[8], (D_MODEL,), 1.0),
        'w_mod': nrm(ks[9], (DEPTH, D_MODEL, 6 * D_MODEL), D_MODEL ** -0.5),
        'b_mod': nrm(ks[10], (DEPTH, 6 * D_MODEL), 0.02),
        'w_in': nrm(ks[11], (DEPTH, D_MODEL, D_IN), D_MODEL ** -0.5) * col_scale,
        'b_gates': nrm(ks[12], (DEPTH, 4 * NH_M), 0.1) + gate_offset,
        'mlstm_norm_g': 1.0 + nrm(ks[13], (DEPTH, D_MLSTM), 0.02),
        'q_norm_g': 1.0 + nrm(ks[14], (DEPTH, DH_A), 0.02),
        'k_norm_g': 1.0 + nrm(ks[15], (DEPTH, DH_A), 0.02),
        'w_bm': nrm(ks[16], (DEPTH, D_MLSTM, D_MODEL), D_MLSTM ** -0.5),
        'w_ba': nrm(ks[17], (DEPTH, D_ATT_Q, D_MODEL), D_ATT_Q ** -0.5),
        'w_out': nrm(ks[18], (DEPTH, D_MODEL, D_MODEL), BETA * D_MODEL ** -0.5),
        'ln1_g': 1.0 + nrm(ks[19], (DEPTH, D_MODEL), 0.02),
        'ln1_b': nrm(ks[20], (DEPTH, D_MODEL), 0.02),
        'w_up': nrm(ks[21], (DEPTH, D_MODEL, D_FF), D_MODEL ** -0.5),
        'w_down': nrm(ks[22], (DEPTH, D_FF, D_MODEL), BETA * D_FF ** -0.5),
        'ln2_g': 1.0 + nrm(ks[23], (DEPTH, D_MODEL), 0.02),
        'ln2_b': nrm(ks[24], (DEPTH, D_MODEL), 0.02),
    }


def reference(x_prompt, x_sample, cache_k, cache_v, state_C, state_n, state_m, c, c_ctx,
              w_mod, b_mod, w_in, b_gates, mlstm_norm_g, q_norm_g, k_norm_g, w_bm, w_ba, w_out,
              ln1_g, ln1_b, w_up, w_down, ln2_g, ln2_b):
    B = x_prompt.shape[0]
    zC = jnp.zeros((B, 2, NH_M, DH_M, DH_M), jnp.float32)
    zn = jnp.zeros((B, 2, NH_M, DH_M), jnp.float32)
    zm = jnp.zeros((B, 2, NH_M), jnp.float32)
    rope = axial_rope_tables(x_sample.shape[1])
    xp, xs = x_prompt, x_sample
    ks_, vs_, Cs_, ns_, ms_ = [], [], [], [], []
    for l in range(DEPTH):
        lw = (w_in[l], b_gates[l], mlstm_norm_g[l], q_norm_g[l], k_norm_g[l], w_bm[l], w_ba[l], w_out[l],
              ln1_g[l], ln1_b[l], w_up[l], w_down[l], ln2_g[l], ln2_b[l])
        mod_ctx = jax.nn.silu(c_ctx)[None, :] @ w_mod[l] + b_mod[l]
        mod_lat = jax.nn.silu(c) @ w_mod[l] + b_mod[l]
        xp, (k_l, v_l), (C_l, n_l, m_l) = trunk_layer(xp, mod_ctx, *lw, zC, zn, zm, None, None)
        ks_.append(k_l); vs_.append(v_l); Cs_.append(C_l); ns_.append(n_l); ms_.append(m_l)
        xs, _, _ = trunk_layer(xs, mod_lat, *lw, state_C[:, l], state_n[:, l], state_m[:, l],
                               (cache_k[:, l], cache_v[:, l]), rope)
    return (xp, xs, jnp.stack(ks_, 1), jnp.stack(vs_, 1), jnp.stack(Cs_, 1), jnp.stack(ns_, 1), jnp.stack(ms_, 1))
```

```python
import functools

import jax
import jax.numpy as jnp
import numpy as np
from jax import lax
from jax.experimental import pallas as pl
from jax.experimental.pallas import tpu as pltpu

D_MODEL = 1024
NH_M = 4
DH_M = 256
N_Q = 8
N_KV = 2
G_Q = N_Q // N_KV
DH_A = 128
D_FF = 4 * D_MODEL
GRID_W = 64
N_FREQ = DH_A // 4
ROPE_BASE = 10000.0
EPS = 1e-6
DEPTH = 1
ALPHA = (2 * DEPTH) ** 0.25

CHUNK = 256
TOK_TILE = 256
Q_TILE = 256
K_TILE = 512
MOD_ROWS = 8

V7X_VMEM_BYTES = 64 * 1024 * 1024
VMEM_LIMIT = V7X_VMEM_BYTES - 8 * 1024 * 1024

F32 = jnp.float32
BF16 = jnp.bfloat16


def _dot(a, b):
    return jnp.dot(a, b, preferred_element_type=F32)


def _dot_nt(a, b):
    return lax.dot_general(a, b, (((1,), (1,)), ((), ())), preferred_element_type=F32)


def _resident(shape):
    nd = len(shape)
    return pl.BlockSpec(shape, lambda *_: (0,) * nd, pipeline_mode=pl.Buffered(1))


def _params(semantics):
    return pltpu.CompilerParams(dimension_semantics=semantics, vmem_limit_bytes=VMEM_LIMIT)


def _mod_kernel(c_ref, w_ref, b_ref, o_ref):
    c = c_ref[...]
    s = c * jax.nn.sigmoid(c)
    o_ref[...] = _dot(s.astype(BF16), w_ref[...].astype(BF16)) + b_ref[...]


def _modulation(c_rows, w_mod, b_mod):
    n_out = w_mod.shape[1]
    blk = D_MODEL
    return pl.pallas_call(
        _mod_kernel,
        grid=(n_out // blk,),
        in_specs=[pl.BlockSpec((MOD_ROWS, D_MODEL), lambda j: (0, 0)),
                  pl.BlockSpec((D_MODEL, blk), lambda j: (0, j)),
                  pl.BlockSpec((1, blk), lambda j: (0, j))],
        out_specs=pl.BlockSpec((MOD_ROWS, blk), lambda j: (0, j)),
        out_shape=jax.ShapeDtypeStruct((MOD_ROWS, n_out), F32),
        compiler_params=_params(("parallel",)),
        name="modulation",
    )(c_rows, w_mod, b_mod.reshape(1, n_out))


def _log_sigmoid(x):
    return jnp.minimum(x, 0.0) - jnp.log1p(jnp.exp(-jnp.abs(x)))


def _rms(t, g):
    return t * lax.rsqrt(jnp.mean(t * t, axis=-1, keepdims=True) + EPS) * g


def _proj_kernel(rope, *refs):
    if rope:
        (x_ref, mod_ref, wm_ref, wgt_ref, bg_ref, wa_ref, wmg_ref, qg_ref, kg_ref, cos_ref, sin_ref,
         qm_o, km_o, vm_o, som_o, gr_o, qa_o, ka_o, va_o, sgm_o, sga_o) = refs
    else:
        (x_ref, mod_ref, wm_ref, wgt_ref, bg_ref, wa_ref, wmg_ref, qg_ref, kg_ref,
         qm_o, km_o, vm_o, som_o, gr_o, qa_o, ka_o, va_o, sgm_o, sga_o, kc_o, vc_o) = refs
    tm = x_ref.shape[1]
    mod = mod_ref[0]
    h = (x_ref[0] * (1.0 + mod[1:2]) + mod[0:1]).astype(BF16)

    qm_o[0] = _dot(h, wm_ref[:, 0:D_MODEL]).astype(BF16)
    km_o[0] = (_dot(h, wm_ref[:, D_MODEL:2 * D_MODEL]) * (DH_M ** -0.5)).astype(BF16)
    vm_o[0] = _dot(h, wm_ref[:, 2 * D_MODEL:3 * D_MODEL]).astype(BF16)
    som_o[0] = jax.nn.sigmoid(_dot(h, wm_ref[:, 3 * D_MODEL:4 * D_MODEL]))

    gates = _dot_nt(wgt_ref[...], h) + bg_ref[...]
    lf = _log_sigmoid(gates)
    row = lax.broadcasted_iota(jnp.int32, (tm, tm), 0)
    col = lax.broadcasted_iota(jnp.int32, (tm, tm), 1)
    tri = jnp.where(row <= col, 1.0, 0.0).astype(BF16)
    hi = lf.astype(BF16)
    r1 = lf - hi.astype(F32)
    mid = r1.astype(BF16)
    lo = (r1 - mid.astype(F32)).astype(BF16)
    cum = (_dot(hi, tri) + _dot(mid, tri) + _dot(lo, tri))[0:8]
    lf8 = lf[0:8]
    tot = cum[:, tm - 1:tm]
    rev = tot - cum + lf8
    is_fwd = lax.broadcasted_iota(jnp.int32, (8, tm), 0) < NH_M
    a = jnp.where(is_fwd, cum, rev)
    cc = gates[8:16] - a
    totb = jnp.broadcast_to(tot, (8, tm))
    zero_row = jnp.zeros((1, tm), F32)
    for hh in range(NH_M):
        rows = (a[hh:hh + 1], cc[hh:hh + 1], totb[hh:hh + 1],
                a[NH_M + hh:NH_M + hh + 1], cc[NH_M + hh:NH_M + hh + 1], totb[NH_M + hh:NH_M + hh + 1],
                zero_row, zero_row)
        for k, r in enumerate(rows):
            gr_o[0, hh, k:k + 1, :] = r

    if rope:
        cos = cos_ref[...]
        sin_s = sin_ref[...]
        lane = lax.broadcasted_iota(jnp.int32, (tm, DH_A), 1)
        first_half = (lane % (2 * N_FREQ)) < N_FREQ

        def rot(t):
            partner = jnp.where(first_half, pltpu.roll(t, DH_A - N_FREQ, 1), pltpu.roll(t, N_FREQ, 1))
            return t * cos + partner * sin_s
    else:
        rot = lambda t: t

    qg = qg_ref[...]
    kg = kg_ref[...]
    for g in range(N_Q):
        t = _rms(_dot(h, wa_ref[:, g * DH_A:(g + 1) * DH_A]), qg)
        qa_o[0, :, g * DH_A:(g + 1) * DH_A] = (rot(t) * (DH_A ** -0.5)).astype(BF16)
    for g in range(N_KV):
        off = N_Q * DH_A + g * DH_A
        t = _rms(_dot(h, wa_ref[:, off:off + DH_A]), kg)
        if not rope:
            kc_o[0, :, g * DH_A:(g + 1) * DH_A] = t
        ka_o[0, :, g * DH_A:(g + 1) * DH_A] = rot(t).astype(BF16)
    off = (N_Q + N_KV) * DH_A
    v = _dot(h, wa_ref[:, off:off + N_KV * DH_A])
    if not rope:
        vc_o[0] = v
    va_o[0] = v.astype(BF16)

    sgm_o[0] = jax.nn.sigmoid(_dot(h, wmg_ref[:, 0:D_MODEL]))
    sga_o[0] = jax.nn.sigmoid(_dot(h, wmg_ref[:, D_MODEL:2 * D_MODEL]))


def _projection(x, mod, wts, rope_tables):
    B, T, _ = x.shape
    tm = TOK_TILE
    nt = T // tm
    rope = rope_tables is not None
    per_batch_mod = mod.shape[0] > 1
    w_m4, w_gt, b_g, w_a, w_mg, q_g, k_g = wts

    tok = lambda width: pl.BlockSpec((1, tm, width), lambda b, t: (b, t, 0))
    in_specs = [tok(D_MODEL),
                pl.BlockSpec((1, 6, D_MODEL), (lambda b, t: (b, 0, 0)) if per_batch_mod else (lambda b, t: (0, 0, 0))),
                _resident(w_m4.shape), _resident(w_gt.shape), _resident(b_g.shape), _resident(w_a.shape),
                _resident(w_mg.shape), _resident(q_g.shape), _resident(k_g.shape)]
    args = [x, mod, w_m4, w_gt, b_g, w_a, w_mg, q_g, k_g]
    if rope:
        in_specs += [pl.BlockSpec((tm, DH_A), lambda b, t: (t, 0))] * 2
        args += list(rope_tables)

    kv_w = N_KV * DH_A
    outs = [((B, T, D_MODEL), BF16, tok(D_MODEL)),
            ((B, T, D_MODEL), BF16, tok(D_MODEL)),
            ((B, T, D_MODEL), BF16, tok(D_MODEL)),
            ((B, T, D_MODEL), F32, tok(D_MODEL)),
            ((B, NH_M, 8, T), F32, pl.BlockSpec((1, NH_M, 8, tm), lambda b, t: (b, 0, 0, t))),
            ((B, T, D_MODEL), BF16, tok(D_MODEL)),
            ((B, T, kv_w), BF16, tok(kv_w)),
            ((B, T, kv_w), BF16, tok(kv_w)),
            ((B, T, D_MODEL), F32, tok(D_MODEL)),
            ((B, T, D_MODEL), F32, tok(D_MODEL))]
    if not rope:
        outs += [((B, T, kv_w), F32, tok(kv_w)), ((B, T, kv_w), F32, tok(kv_w))]

    return pl.pallas_call(
        functools.partial(_proj_kernel, rope),
        grid=(B, nt),
        in_specs=in_specs,
        out_specs=[o[2] for o in outs],
        out_shape=[jax.ShapeDtypeStruct(o[0], o[1]) for o in outs],
        compiler_params=_params(("parallel", "parallel")),
        name="projection_lat" if rope else "projection_ctx",
    )(*args)


def _mlstm_kernel(has_init, emit_state, nc, *refs):
    refs = list(refs)
    (qf_ref, kf_ref, vf_ref, qb_ref, kb_ref, vb_ref, gf_ref, gb_ref, som_ref, gn_ref) = refs[:10]
    refs = refs[10:]
    if has_init:
        c0_ref, n0_ref, m0_ref = refs[:3]
        refs = refs[3:]
    hm_o = refs[0]
    refs = refs[1:]
    if emit_state:
        c_o, n_o, m_o = refs[:3]
        refs = refs[3:]
    acc_ref, c_scr, n_scr, m_scr = refs

    s = pl.program_id(2)
    L = CHUNK
    use_state = has_init or nc > 1
    keep_state = emit_state or nc > 1

    @pl.when(s == 0)
    def _init():
        acc_ref[...] = jnp.zeros_like(acc_ref)
        if has_init:
            c_scr[...] = c0_ref[0, :, 0]
            n_scr[...] = n0_ref[0, :, 0]
            m_scr[...] = m0_ref[0, :, 0]
        else:
            c_scr[...] = jnp.zeros_like(c_scr)
            n_scr[...] = jnp.zeros_like(n_scr)
            m_scr[...] = jnp.zeros_like(m_scr)

    row = lax.broadcasted_iota(jnp.int32, (L, L), 0)
    col = lax.broadcasted_iota(jnp.int32, (L, L), 1)
    eye = row == col

    def chain(d, q_ref, k_ref, v_ref, g_ref, chunk_idx):
        q = q_ref[0]
        k = k_ref[0]
        v = v_ref[0]
        g = g_ref[0, 0]
        a_row = g[3 * d:3 * d + 1]
        c_row = g[3 * d + 1:3 * d + 2]
        tot = g[3 * d + 2:3 * d + 3, 0:1]
        m_prev = m_scr[d]
        mask = (col <= row) if d == 0 else (col >= row)
        cm = jnp.where(mask, c_row, -jnp.inf)
        a_col = jnp.sum(jnp.where(eye, a_row, 0.0), axis=-1, keepdims=True)
        c_col = jnp.sum(jnp.where(eye, c_row, 0.0), axis=-1, keepdims=True)
        m_rel = jnp.maximum(m_prev, jnp.max(cm, axis=-1, keepdims=True))
        sc = _dot_nt(q, k) * jnp.exp(cm - m_rel)
        num = _dot(sc.astype(BF16), v)
        den = jnp.sum(sc, axis=-1, keepdims=True)
        if use_state:
            w_inter = jnp.exp(m_prev - m_rel)
            num = num + w_inter * _dot(q, c_scr[d].astype(BF16))
            den = den + w_inter * jnp.sum(q.astype(F32) * n_scr[d], axis=-1, keepdims=True)
        h = num * (1.0 / jnp.maximum(jnp.abs(den), jnp.exp(-(a_col + m_rel))))
        start = pl.multiple_of(chunk_idx * L, L)
        acc_ref[pl.ds(start, L), :] += h

        if keep_state:
            m_new = tot + jnp.maximum(m_prev, jnp.max(c_row, axis=-1, keepdims=True))
            wk = jnp.exp(tot + c_col - m_new) * k.astype(F32)
            c_upd = _dot(wk.T.astype(BF16), v)
            n_upd = jnp.sum(wk, axis=0, keepdims=True)
            if use_state:
                decay = jnp.exp(tot + m_prev - m_new)
                c_upd = decay * c_scr[d] + c_upd
                n_upd = decay * n_scr[d] + n_upd
            c_scr[d] = c_upd
            n_scr[d] = n_upd
            m_scr[d] = m_new

    chain(0, qf_ref, kf_ref, vf_ref, gf_ref, s)
    chain(1, qb_ref, kb_ref, vb_ref, gb_ref, nc - 1 - s)

    @pl.when(s == nc - 1)
    def _finish():
        hsum = acc_ref[...]
        hn = hsum * lax.rsqrt(jnp.mean(hsum * hsum, axis=-1, keepdims=True) + EPS) * gn_ref[...]
        hm_o[0] = (hn * som_ref[0]).astype(BF16)
        if emit_state:
            c_o[0, :, 0] = c_scr[...]
            n_o[0, :, 0] = n_scr[...]
            m_o[0, :, 0] = m_scr[...]


def _mlstm(qm, km, vm, gr, som, gnorm, init_state, emit_state):
    B, T, _ = qm.shape
    L = CHUNK
    nc = T // L
    has_init = init_state is not None

    fwd = pl.BlockSpec((1, L, DH_M), lambda b, h, s: (b, s, h))
    bwd = pl.BlockSpec((1, L, DH_M), lambda b, h, s: (b, nc - 1 - s, h))
    gfwd = pl.BlockSpec((1, 1, 8, L), lambda b, h, s: (b, h, 0, s))
    gbwd = pl.BlockSpec((1, 1, 8, L), lambda b, h, s: (b, h, 0, nc - 1 - s))
    seq = pl.BlockSpec((1, T, DH_M), lambda b, h, s: (b, 0, h))
    c_spec = pl.BlockSpec((1, 2, 1, DH_M, DH_M), lambda b, h, s: (b, 0, h, 0, 0))
    n_spec = pl.BlockSpec((1, 2, 1, 1, DH_M), lambda b, h, s: (b, 0, h, 0, 0))
    m_spec = pl.BlockSpec((1, 2, 1, 1, 1), lambda b, h, s: (b, 0, h, 0, 0))

    in_specs = [fwd, fwd, fwd, bwd, bwd, bwd, gfwd, gbwd, seq,
                pl.BlockSpec((1, DH_M), lambda b, h, s: (0, h))]
    args = [qm, km, vm, qm, km, vm, gr, gr, som, gnorm]
    if has_init:
        in_specs += [c_spec, n_spec, m_spec]
        args += list(init_state)
    out_specs = [seq]
    out_shape = [jax.ShapeDtypeStruct((B, T, D_MODEL), BF16)]
    if emit_state:
        out_specs += [c_spec, n_spec, m_spec]
        out_shape += [jax.ShapeDtypeStruct((B, 2, NH_M, DH_M, DH_M), F32),
                      jax.ShapeDtypeStruct((B, 2, NH_M, 1, DH_M), F32),
                      jax.ShapeDtypeStruct((B, 2, NH_M, 1, 1), F32)]

    return pl.pallas_call(
        functools.partial(_mlstm_kernel, has_init, emit_state, nc),
        grid=(B, NH_M, nc),
        in_specs=in_specs,
        out_specs=out_specs,
        out_shape=out_shape,
        scratch_shapes=[pltpu.VMEM((T, DH_M), F32),
                        pltpu.VMEM((2, DH_M, DH_M), F32),
                        pltpu.VMEM((2, 1, DH_M), F32),
                        pltpu.VMEM((2, 1, 1), F32)],
        compiler_params=_params(("parallel", "parallel", "arbitrary")),
        name="mlstm_lat" if has_init else "mlstm_ctx",
    )(*args)


def _attn_kernel(n_lat_tiles, has_ctx, *refs):
    if has_ctx:
        q_ref, k_ref, v_ref, kc_ref, vc_ref, o_ref, qs_ref, m_ref, l_ref, acc_ref = refs
    else:
        q_ref, k_ref, v_ref, o_ref, qs_ref, m_ref, l_ref, acc_ref = refs
    tq = q_ref.shape[1]
    for g in range(G_Q):
        qs_ref[g * tq:(g + 1) * tq, :] = q_ref[0, :, g * DH_A:(g + 1) * DH_A]
    m_ref[...] = jnp.full_like(m_ref, -jnp.inf)
    l_ref[...] = jnp.zeros_like(l_ref)
    acc_ref[...] = jnp.zeros_like(acc_ref)

    def step(k, v):
        s = _dot_nt(qs_ref[...], k)
        m_old = m_ref[...]
        m_new = jnp.maximum(m_old, jnp.max(s, axis=-1, keepdims=True))
        alpha = jnp.exp(m_old - m_new)
        p = jnp.exp(s - m_new)
        l_ref[...] = alpha * l_ref[...] + jnp.sum(p, axis=-1, keepdims=True)
        acc_ref[...] = alpha * acc_ref[...] + _dot(p.astype(BF16), v)
        m_ref[...] = m_new

    tk = k_ref.shape[1] // n_lat_tiles

    def body(i, carry):
        start = pl.multiple_of(i * tk, tk)
        step(k_ref[0, pl.ds(start, tk), :], v_ref[0, pl.ds(start, tk), :])
        return carry

    lax.fori_loop(0, n_lat_tiles, body, 0)
    if has_ctx:
        step(kc_ref[0], vc_ref[0])

    out = acc_ref[...] * (1.0 / l_ref[...])
    for g in range(G_Q):
        o_ref[0, :, g * DH_A:(g + 1) * DH_A] = out[g * tq:(g + 1) * tq, :].astype(BF16)


def _attention(qa, ka, va, ctx_kv):
    B, T, _ = qa.shape
    tq = min(Q_TILE, T)
    nq = T // tq
    has_ctx = ctx_kv is not None
    n_lat_tiles = max(1, T // K_TILE)
    qspec = pl.BlockSpec((1, tq, G_Q * DH_A), lambda b, h, i: (b, i, h))
    kvspec = lambda tk: pl.BlockSpec((1, tk, DH_A), lambda b, h, i: (b, 0, h))
    in_specs = [qspec, kvspec(T), kvspec(T)]
    args = [qa, ka, va]
    if has_ctx:
        tc = ctx_kv[0].shape[1]
        in_specs += [kvspec(tc), kvspec(tc)]
        args += list(ctx_kv)
    return pl.pallas_call(
        functools.partial(_attn_kernel, n_lat_tiles, has_ctx),
        grid=(B, N_KV, nq),
        in_specs=in_specs,
        out_specs=qspec,
        out_shape=jax.ShapeDtypeStruct((B, T, D_MODEL), BF16),
        scratch_shapes=[pltpu.VMEM((G_Q * tq, DH_A), BF16),
                        pltpu.VMEM((G_Q * tq, 1), F32),
                        pltpu.VMEM((G_Q * tq, 1), F32),
                        pltpu.VMEM((G_Q * tq, DH_A), F32)],
        compiler_params=_params(("parallel", "parallel", "parallel")),
        name="attention_lat" if has_ctx else "attention_ctx",
    )(*args)


def _layer_norm(y, g, b):
    mu = jnp.mean(y, axis=-1, keepdims=True)
    yc = y - mu
    var = jnp.mean(yc * yc, axis=-1, keepdims=True)
    return yc * lax.rsqrt(var + EPS) * g + b


def _tail_kernel(x_ref, mod_ref, hm_ref, ha_ref, sgm_ref, sga_ref, wbm_ref, wba_ref, wout_ref,
                 wup_ref, wdown_ref, ln_ref, o_ref):
    mod = mod_ref[0]
    ln = ln_ref[...]
    x = x_ref[0]
    merged = sgm_ref[0] * _dot(hm_ref[0], wbm_ref[...]) + sga_ref[0] * _dot(ha_ref[0], wba_ref[...])
    mix = _dot(merged.astype(BF16), wout_ref[...])
    x1 = _layer_norm(ALPHA * x + mod[2:3] * mix, ln[0:1], ln[1:2])
    h = (x1 * (1.0 + mod[4:5]) + mod[3:4]).astype(BF16)
    ff = jnp.zeros_like(x1)
    for j in range(D_FF // D_MODEL):
        u = jnp.maximum(_dot(h, wup_ref[:, j * D_MODEL:(j + 1) * D_MODEL]), 0.0)
        ff = ff + _dot((u * u).astype(BF16), wdown_ref[j * D_MODEL:(j + 1) * D_MODEL, :])
    o_ref[0] = _layer_norm(ALPHA * x1 + mod[5:6] * ff, ln[2:3], ln[3:4])


def _tail(x, mod, hm, ha, sgm, sga, wts, name):
    B, T, _ = x.shape
    tm = TOK_TILE
    per_batch_mod = mod.shape[0] > 1
    tok = pl.BlockSpec((1, tm, D_MODEL), lambda b, t: (b, t, 0))
    in_specs = [tok,
                pl.BlockSpec((1, 6, D_MODEL), (lambda b, t: (b, 0, 0)) if per_batch_mod else (lambda b, t: (0, 0, 0))),
                tok, tok, tok, tok] + [_resident(w.shape) for w in wts]
    return pl.pallas_call(
        _tail_kernel,
        grid=(B, T // tm),
        in_specs=in_specs,
        out_specs=tok,
        out_shape=jax.ShapeDtypeStruct((B, T, D_MODEL), F32),
        compiler_params=_params(("parallel", "parallel")),
        name=name,
    )(x, mod, hm, ha, sgm, sga, *wts)


def _rope_tables(n_tokens):
    rows = n_tokens // GRID_W
    row = jnp.repeat(jnp.arange(rows), GRID_W)
    col = jnp.tile(jnp.arange(GRID_W), rows)
    inv = ROPE_BASE ** (-jnp.arange(N_FREQ, dtype=F32) / N_FREQ)
    ang = jnp.stack([row, col], -1).astype(F32)[..., None] * inv
    ang = jnp.broadcast_to(ang[:, :, None, :], (n_tokens, 2, 2, N_FREQ))
    sign = jnp.asarray([-1.0, 1.0], F32)[None, None, :, None]
    return jnp.cos(ang).reshape(n_tokens, DH_A), (jnp.sin(ang) * sign).reshape(n_tokens, DH_A)


def kernel(x_prompt, x_sample, cache_k, cache_v, state_C, state_n, state_m, c, c_ctx, w_mod, b_mod, w_in,
           b_gates, mlstm_norm_g, q_norm_g, k_norm_g, w_bm, w_ba, w_out, ln1_g, ln1_b, w_up, w_down,
           ln2_g, ln2_b):
    B, T, _ = x_prompt.shape
    Bd, Td, _ = x_sample.shape
    l = 0

    w = w_in[l]
    o_g = 4 * D_MODEL
    o_a = o_g + 4 * NH_M
    o_mg = o_a + (N_Q + 2 * N_KV) * DH_A
    gate_rows = np.array([4, 5, 6, 7, 12, 13, 14, 15, 0, 1, 2, 3, 8, 9, 10, 11])
    proj_w = (w[:, :o_g].astype(BF16),
              w[:, o_g:o_a].T[gate_rows].astype(BF16),
              b_gates[l][gate_rows].reshape(4 * NH_M, 1),
              w[:, o_a:o_mg].astype(BF16),
              w[:, o_mg:].astype(BF16),
              q_norm_g[l].reshape(1, DH_A),
              k_norm_g[l].reshape(1, DH_A))
    tail_w = (w_bm[l].astype(BF16), w_ba[l].astype(BF16), w_out[l].astype(BF16),
              w_up[l].astype(BF16), w_down[l].astype(BF16),
              jnp.stack([ln1_g[l], ln1_b[l], ln2_g[l], ln2_b[l]]))
    gnorm = mlstm_norm_g[l].reshape(1, D_MODEL)

    c_rows = jnp.concatenate([c_ctx[None, :], c, jnp.zeros((MOD_ROWS - 1 - Bd, D_MODEL), F32)], axis=0)
    mod = _modulation(c_rows, w_mod[l], b_mod[l]).reshape(MOD_ROWS, 6, D_MODEL)
    mod_ctx, mod_lat = mod[0:1], mod[1:1 + Bd]

    (qm, km, vm, som, gr, qa, ka, va, sgm, sga, k_new, v_new) = _projection(x_prompt, mod_ctx, proj_w, None)
    hm, c_new, n_new, m_new = _mlstm(qm, km, vm, gr, som, gnorm, None, True)
    ha = _attention(qa, ka, va, None)
    y_prompt = _tail(x_prompt, mod_ctx, hm, ha, sgm, sga, tail_w, "tail_ctx")

    (qm, km, vm, som, gr, qa, ka, va, sgm, sga) = _projection(x_sample, mod_lat, proj_w, _rope_tables(Td))
    past = cache_k.shape[2]
    init = (state_C[:, l], state_n[:, l].reshape(Bd, 2, NH_M, 1, DH_M), state_m[:, l].reshape(Bd, 2, NH_M, 1, 1))
    hm, = _mlstm(qm, km, vm, gr, som, gnorm, init, False)
    ctx_kv = (cache_k[:, l].reshape(Bd, past, N_KV * DH_A).astype(BF16),
              cache_v[:, l].reshape(Bd, past, N_KV * DH_A).astype(BF16))
    ha = _attention(qa, ka, va, ctx_kv)
    y_sample = _tail(x_sample, mod_lat, hm, ha, sgm, sga, tail_w, "tail_lat")

    return (y_prompt, y_sample,
            k_new.reshape(B, 1, T, N_KV, DH_A), v_new.reshape(B, 1, T, N_KV, DH_A),
            c_new.reshape(B, 1, 2, NH_M, DH_M, DH_M), n_new.reshape(B, 1, 2, NH_M, DH_M),
            m_new.reshape(B, 1, 2, NH_M))
```

```python
import functools

import jax
import jax.numpy as jnp
import numpy as np
from jax import lax
from jax.experimental import pallas as pl
from jax.experimental.pallas import tpu as pltpu

D_MODEL = 1024
NH_M = 4
DH_M = 256
N_Q = 8
N_KV = 2
G_Q = N_Q // N_KV
DH_A = 128
D_FF = 4 * D_MODEL
GRID_W = 64
N_FREQ = DH_A // 4
ROPE_BASE = 10000.0
EPS = 1e-6
DEPTH = 1
ALPHA = (2 * DEPTH) ** 0.25

CHUNK = 256
TOK_TILE = 256
Q_TILE = 256
K_TILE = 512
ATTN_LOOKAHEAD = 3
V_ROWS = DH_A + 16
MOD_ROWS = 8

V7X_VMEM_BYTES = 64 * 1024 * 1024
VMEM_LIMIT = V7X_VMEM_BYTES - 8 * 1024 * 1024

F32 = jnp.float32
BF16 = jnp.bfloat16


def _dot(a, b):
    return jnp.dot(a, b, preferred_element_type=F32)


def _dot_nt(a, b):
    return lax.dot_general(a, b, (((1,), (1,)), ((), ())), preferred_element_type=F32)


def _resident(shape):
    nd = len(shape)
    return pl.BlockSpec(shape, lambda *_: (0,) * nd, pipeline_mode=pl.Buffered(1))


def _params(semantics):
    return pltpu.CompilerParams(dimension_semantics=semantics, vmem_limit_bytes=VMEM_LIMIT)


def _mod_kernel(c_ref, w_ref, b_ref, o_ref):
    c = c_ref[...]
    s = c * jax.nn.sigmoid(c)
    o_ref[...] = _dot(s.astype(BF16), w_ref[...].astype(BF16)) + b_ref[...]


def _modulation(c_rows, w_mod, b_mod):
    n_out = w_mod.shape[1]
    blk = D_MODEL
    return pl.pallas_call(
        _mod_kernel,
        grid=(n_out // blk,),
        in_specs=[pl.BlockSpec((MOD_ROWS, D_MODEL), lambda j: (0, 0)),
                  pl.BlockSpec((D_MODEL, blk), lambda j: (0, j)),
                  pl.BlockSpec((1, blk), lambda j: (0, j))],
        out_specs=pl.BlockSpec((MOD_ROWS, blk), lambda j: (0, j)),
        out_shape=jax.ShapeDtypeStruct((MOD_ROWS, n_out), F32),
        compiler_params=_params(("parallel",)),
        name="modulation",
    )(c_rows, w_mod, b_mod.reshape(1, n_out))


def _log_sigmoid(x):
    return jnp.minimum(x, 0.0) - jnp.log1p(jnp.exp(-jnp.abs(x)))


def _rms(t, g):
    return t * lax.rsqrt(jnp.mean(t * t, axis=-1, keepdims=True) + EPS) * g


def _proj_kernel(rope, *refs):
    if rope:
        (x_ref, mod_ref, wm_ref, wgt_ref, bg_ref, wa_ref, wvt_ref, wmg_ref, qg_ref, kg_ref, cos_ref, sin_ref,
         qm_o, km_o, vm_o, som_o, gr_o, qa_o, ka_o, vt_o, sgm_o, sga_o) = refs
    else:
        (x_ref, mod_ref, wm_ref, wgt_ref, bg_ref, wa_ref, wvt_ref, wmg_ref, qg_ref, kg_ref,
         qm_o, km_o, vm_o, som_o, gr_o, qa_o, ka_o, vt_o, sgm_o, sga_o, kc_o, vc_o) = refs
    tm = x_ref.shape[1]
    mod = mod_ref[0]
    h = (x_ref[0] * (1.0 + mod[1:2]) + mod[0:1]).astype(BF16)

    qm_o[0] = _dot(h, wm_ref[:, 0:D_MODEL]).astype(BF16)
    km_o[0] = (_dot(h, wm_ref[:, D_MODEL:2 * D_MODEL]) * (DH_M ** -0.5)).astype(BF16)
    vm_o[0] = _dot(h, wm_ref[:, 2 * D_MODEL:3 * D_MODEL]).astype(BF16)
    som_o[0] = jax.nn.sigmoid(_dot(h, wm_ref[:, 3 * D_MODEL:4 * D_MODEL]))

    gates = _dot_nt(wgt_ref[...], h) + bg_ref[...]
    lf = _log_sigmoid(gates)
    row = lax.broadcasted_iota(jnp.int32, (tm, tm), 0)
    col = lax.broadcasted_iota(jnp.int32, (tm, tm), 1)
    tri = jnp.where(row <= col, 1.0, 0.0).astype(BF16)
    hi = lf.astype(BF16)
    r1 = lf - hi.astype(F32)
    mid = r1.astype(BF16)
    lo = (r1 - mid.astype(F32)).astype(BF16)
    cum = (_dot(hi, tri) + _dot(mid, tri) + _dot(lo, tri))[0:8]
    lf8 = lf[0:8]
    tot = cum[:, tm - 1:tm]
    rev = tot - cum + lf8
    is_fwd = lax.broadcasted_iota(jnp.int32, (8, tm), 0) < NH_M
    a = jnp.where(is_fwd, cum, rev)
    cc = gates[8:16] - a
    totb = jnp.broadcast_to(tot, (8, tm))
    zero_row = jnp.zeros((1, tm), F32)
    for hh in range(NH_M):
        rows = (a[hh:hh + 1], cc[hh:hh + 1], totb[hh:hh + 1],
                a[NH_M + hh:NH_M + hh + 1], cc[NH_M + hh:NH_M + hh + 1], totb[NH_M + hh:NH_M + hh + 1],
                zero_row, zero_row)
        for k, r in enumerate(rows):
            gr_o[0, hh, k:k + 1, :] = r

    if rope:
        cos = cos_ref[...]
        sin_s = sin_ref[...]
        lane = lax.broadcasted_iota(jnp.int32, (tm, DH_A), 1)
        first_half = (lane % (2 * N_FREQ)) < N_FREQ

        def rot(t):
            partner = jnp.where(first_half, pltpu.roll(t, DH_A - N_FREQ, 1), pltpu.roll(t, N_FREQ, 1))
            return t * cos + partner * sin_s
    else:
        rot = lambda t: t

    qg = qg_ref[...]
    kg = kg_ref[...]
    for g in range(N_Q):
        t = _rms(_dot(h, wa_ref[:, g * DH_A:(g + 1) * DH_A]), qg)
        qa_o[0, :, g * DH_A:(g + 1) * DH_A] = (rot(t) * (DH_A ** -0.5)).astype(BF16)
    for g in range(N_KV):
        off = N_Q * DH_A + g * DH_A
        t = _rms(_dot(h, wa_ref[:, off:off + DH_A]), kg)
        if not rope:
            kc_o[0, :, g * DH_A:(g + 1) * DH_A] = t
        ka_o[0, :, g * DH_A:(g + 1) * DH_A] = rot(t).astype(BF16)
    if not rope:
        off = (N_Q + N_KV) * DH_A
        vc_o[0] = _dot(h, wa_ref[:, off:off + N_KV * DH_A])
    vt = _dot_nt(wvt_ref[...], h)
    for g in range(N_KV):
        vt_o[0, g, 0:DH_A, :] = vt[g * DH_A:(g + 1) * DH_A].astype(BF16)
        vt_o[0, g, DH_A:V_ROWS, :] = jnp.ones((V_ROWS - DH_A, tm), BF16)

    sgm_o[0] = jax.nn.sigmoid(_dot(h, wmg_ref[:, 0:D_MODEL]))
    sga_o[0] = jax.nn.sigmoid(_dot(h, wmg_ref[:, D_MODEL:2 * D_MODEL]))


def _projection(x, mod, wts, rope_tables):
    B, T, _ = x.shape
    tm = TOK_TILE
    nt = T // tm
    rope = rope_tables is not None
    per_batch_mod = mod.shape[0] > 1
    w_m4, w_gt, b_g, w_a, w_vt, w_mg, q_g, k_g = wts

    tok = lambda width: pl.BlockSpec((1, tm, width), lambda b, t: (b, t, 0))
    in_specs = [tok(D_MODEL),
                pl.BlockSpec((1, 6, D_MODEL), (lambda b, t: (b, 0, 0)) if per_batch_mod else (lambda b, t: (0, 0, 0))),
                _resident(w_m4.shape), _resident(w_gt.shape), _resident(b_g.shape), _resident(w_a.shape),
                _resident(w_vt.shape), _resident(w_mg.shape), _resident(q_g.shape), _resident(k_g.shape)]
    args = [x, mod, w_m4, w_gt, b_g, w_a, w_vt, w_mg, q_g, k_g]
    if rope:
        in_specs += [pl.BlockSpec((tm, DH_A), lambda b, t: (t, 0))] * 2
        args += list(rope_tables)

    kv_w = N_KV * DH_A
    outs = [((B, T, D_MODEL), BF16, tok(D_MODEL)),
            ((B, T, D_MODEL), BF16, tok(D_MODEL)),
            ((B, T, D_MODEL), BF16, tok(D_MODEL)),
            ((B, T, D_MODEL), F32, tok(D_MODEL)),
            ((B, NH_M, 8, T), F32, pl.BlockSpec((1, NH_M, 8, tm), lambda b, t: (b, 0, 0, t))),
            ((B, T, D_MODEL), BF16, tok(D_MODEL)),
            ((B, T, kv_w), BF16, tok(kv_w)),
            ((B, N_KV, V_ROWS, T), BF16,
             pl.BlockSpec((1, N_KV, V_ROWS, tm), lambda b, t: (b, 0, 0, t))),
            ((B, T, D_MODEL), F32, tok(D_MODEL)),
            ((B, T, D_MODEL), F32, tok(D_MODEL))]
    if not rope:
        outs += [((B, T, kv_w), F32, tok(kv_w)), ((B, T, kv_w), F32, tok(kv_w))]

    return pl.pallas_call(
        functools.partial(_proj_kernel, rope),
        grid=(B, nt),
        in_specs=in_specs,
        out_specs=[o[2] for o in outs],
        out_shape=[jax.ShapeDtypeStruct(o[0], o[1]) for o in outs],
        compiler_params=_params(("parallel", "parallel")),
        name="projection_lat" if rope else "projection_ctx",
    )(*args)


def _mlstm_kernel(has_init, emit_state, nc, *refs):
    refs = list(refs)
    (qf_ref, kf_ref, vf_ref, qb_ref, kb_ref, vb_ref, gf_ref, gb_ref, som_ref, gn_ref) = refs[:10]
    refs = refs[10:]
    if has_init:
        c0_ref, n0_ref, m0_ref = refs[:3]
        refs = refs[3:]
    hm_o = refs[0]
    refs = refs[1:]
    if emit_state:
        c_o, n_o, m_o = refs[:3]
        refs = refs[3:]
    acc_ref, c_scr, n_scr, m_scr = refs

    s = pl.program_id(2)
    L = CHUNK
    use_state = has_init or nc > 1
    keep_state = emit_state or nc > 1

    @pl.when(s == 0)
    def _init():
        acc_ref[...] = jnp.zeros_like(acc_ref)
        if has_init:
            c_scr[...] = c0_ref[0, :, 0]
            n_scr[...] = n0_ref[0, :, 0]
            m_scr[...] = m0_ref[0, :, 0]
        else:
            c_scr[...] = jnp.zeros_like(c_scr)
            n_scr[...] = jnp.zeros_like(n_scr)
            m_scr[...] = jnp.zeros_like(m_scr)

    row = lax.broadcasted_iota(jnp.int32, (L, L), 0)
    col = lax.broadcasted_iota(jnp.int32, (L, L), 1)
    eye = row == col

    def chain(d, q_ref, k_ref, v_ref, g_ref, chunk_idx):
        q = q_ref[0]
        k = k_ref[0]
        v = v_ref[0]
        g = g_ref[0, 0]
        a_row = g[3 * d:3 * d + 1]
        c_row = g[3 * d + 1:3 * d + 2]
        tot = g[3 * d + 2:3 * d + 3, 0:1]
        m_prev = m_scr[d]
        mask = (col <= row) if d == 0 else (col >= row)
        cm = jnp.where(mask, c_row, -jnp.inf)
        a_col = jnp.sum(jnp.where(eye, a_row, 0.0), axis=-1, keepdims=True)
        c_col = jnp.sum(jnp.where(eye, c_row, 0.0), axis=-1, keepdims=True)
        m_rel = jnp.maximum(m_prev, jnp.max(cm, axis=-1, keepdims=True))
        sc = _dot_nt(q, k) * jnp.exp(cm - m_rel)
        num = _dot(sc.astype(BF16), v)
        den = jnp.sum(sc, axis=-1, keepdims=True)
        if use_state:
            w_inter = jnp.exp(m_prev - m_rel)
            num = num + w_inter * _dot(q, c_scr[d].astype(BF16))
            den = den + w_inter * jnp.sum(q.astype(F32) * n_scr[d], axis=-1, keepdims=True)
        h = num * (1.0 / jnp.maximum(jnp.abs(den), jnp.exp(-(a_col + m_rel))))
        start = pl.multiple_of(chunk_idx * L, L)
        acc_ref[pl.ds(start, L), :] += h

        if keep_state:
            m_new = tot + jnp.maximum(m_prev, jnp.max(c_row, axis=-1, keepdims=True))
            wk = jnp.exp(tot + c_col - m_new) * k.astype(F32)
            c_upd = _dot(wk.T.astype(BF16), v)
            n_upd = jnp.sum(wk, axis=0, keepdims=True)
            if use_state:
                decay = jnp.exp(tot + m_prev - m_new)
                c_upd = decay * c_scr[d] + c_upd
                n_upd = decay * n_scr[d] + n_upd
            c_scr[d] = c_upd
            n_scr[d] = n_upd
            m_scr[d] = m_new

    chain(0, qf_ref, kf_ref, vf_ref, gf_ref, s)
    chain(1, qb_ref, kb_ref, vb_ref, gb_ref, nc - 1 - s)

    @pl.when(s == nc - 1)
    def _finish():
        hsum = acc_ref[...]
        hn = hsum * lax.rsqrt(jnp.mean(hsum * hsum, axis=-1, keepdims=True) + EPS) * gn_ref[...]
        hm_o[0] = (hn * som_ref[0]).astype(BF16)
        if emit_state:
            c_o[0, :, 0] = c_scr[...]
            n_o[0, :, 0] = n_scr[...]
            m_o[0, :, 0] = m_scr[...]


def _mlstm(qm, km, vm, gr, som, gnorm, init_state, emit_state):
    B, T, _ = qm.shape
    L = CHUNK
    nc = T // L
    has_init = init_state is not None

    fwd = pl.BlockSpec((1, L, DH_M), lambda b, h, s: (b, s, h))
    bwd = pl.BlockSpec((1, L, DH_M), lambda b, h, s: (b, nc - 1 - s, h))
    gfwd = pl.BlockSpec((1, 1, 8, L), lambda b, h, s: (b, h, 0, s))
    gbwd = pl.BlockSpec((1, 1, 8, L), lambda b, h, s: (b, h, 0, nc - 1 - s))
    seq = pl.BlockSpec((1, T, DH_M), lambda b, h, s: (b, 0, h))
    c_spec = pl.BlockSpec((1, 2, 1, DH_M, DH_M), lambda b, h, s: (b, 0, h, 0, 0))
    n_spec = pl.BlockSpec((1, 2, 1, 1, DH_M), lambda b, h, s: (b, 0, h, 0, 0))
    m_spec = pl.BlockSpec((1, 2, 1, 1, 1), lambda b, h, s: (b, 0, h, 0, 0))

    in_specs = [fwd, fwd, fwd, bwd, bwd, bwd, gfwd, gbwd, seq,
                pl.BlockSpec((1, DH_M), lambda b, h, s: (0, h))]
    args = [qm, km, vm, qm, km, vm, gr, gr, som, gnorm]
    if has_init:
        in_specs += [c_spec, n_spec, m_spec]
        args += list(init_state)
    out_specs = [seq]
    out_shape = [jax.ShapeDtypeStruct((B, T, D_MODEL), BF16)]
    if emit_state:
        out_specs += [c_spec, n_spec, m_spec]
        out_shape += [jax.ShapeDtypeStruct((B, 2, NH_M, DH_M, DH_M), F32),
                      jax.ShapeDtypeStruct((B, 2, NH_M, 1, DH_M), F32),
                      jax.ShapeDtypeStruct((B, 2, NH_M, 1, 1), F32)]

    return pl.pallas_call(
        functools.partial(_mlstm_kernel, has_init, emit_state, nc),
        grid=(B, NH_M, nc),
        in_specs=in_specs,
        out_specs=out_specs,
        out_shape=out_shape,
        scratch_shapes=[pltpu.VMEM((T, DH_M), F32),
                        pltpu.VMEM((2, DH_M, DH_M), F32),
                        pltpu.VMEM((2, 1, DH_M), F32),
                        pltpu.VMEM((2, 1, 1), F32)],
        compiler_params=_params(("parallel", "parallel", "arbitrary")),
        name="mlstm_lat" if has_init else "mlstm_ctx",
    )(*args)


def _attn_kernel(n_lat_tiles, has_ctx, *refs):
    if has_ctx:
        q_ref, k_ref, vt_ref, kc_ref, vct_ref, o_ref = refs
    else:
        q_ref, k_ref, vt_ref, o_ref = refs
    tk = k_ref.shape[1] // n_lat_tiles
    tiles = [(lambda i=i: k_ref[0, i * tk:(i + 1) * tk, :], lambda i=i: vt_ref[0, 0, :, i * tk:(i + 1) * tk])
             for i in range(n_lat_tiles)]
    if has_ctx:
        tiles.append((lambda: kc_ref[0], lambda: vct_ref[0, 0]))
    chains = [(t, g) for t in range(len(tiles)) for g in range(G_Q)]

    def scores(t, g):
        return _dot_nt(tiles[t][0](), q_ref[0, :, g * DH_A:(g + 1) * DH_A])

    m = [None] * G_Q
    acc = [None] * G_Q
    pending = []
    for idx in range(len(chains) + ATTN_LOOKAHEAD):
        if idx < len(chains):
            pending.append(scores(*chains[idx]))
        if idx < ATTN_LOOKAHEAD:
            continue
        t, g = chains[idx - ATTN_LOOKAHEAD]
        st = pending.pop(0)
        m_tile = jnp.max(st, axis=0, keepdims=True)
        m_new = m_tile if t == 0 else jnp.maximum(m[g], m_tile)
        pv = _dot(tiles[t][1](), jnp.exp(st - m_new).astype(BF16))
        acc[g] = pv if t == 0 else jnp.exp(m[g] - m_new) * acc[g] + pv
        m[g] = m_new

    for g in range(G_Q):
        out = acc[g][0:DH_A] * (1.0 / acc[g][DH_A:DH_A + 1])
        o_ref[0, :, g * DH_A:(g + 1) * DH_A] = out.T.astype(BF16)


def _attention(qa, ka, vt, ctx_kv):
    B, T, _ = qa.shape
    tq = min(Q_TILE, T)
    nq = T // tq
    has_ctx = ctx_kv is not None
    n_lat_tiles = max(1, T // K_TILE)
    qspec = pl.BlockSpec((1, tq, G_Q * DH_A), lambda b, h, i: (b, i, h))
    kspec = lambda tk: pl.BlockSpec((1, tk, DH_A), lambda b, h, i: (b, 0, h))
    vspec = lambda tk: pl.BlockSpec((1, 1, V_ROWS, tk), lambda b, h, i: (b, h, 0, 0))
    in_specs = [qspec, kspec(T), vspec(T)]
    args = [qa, ka, vt]
    if has_ctx:
        tc = ctx_kv[0].shape[1]
        in_specs += [kspec(tc), vspec(tc)]
        args += list(ctx_kv)
    return pl.pallas_call(
        functools.partial(_attn_kernel, n_lat_tiles, has_ctx),
        grid=(B, N_KV, nq),
        in_specs=in_specs,
        out_specs=qspec,
        out_shape=jax.ShapeDtypeStruct((B, T, D_MODEL), BF16),
        compiler_params=_params(("parallel", "parallel", "parallel")),
        name="attention_lat" if has_ctx else "attention_ctx",
    )(*args)


def _layer_norm(y, g, b):
    mu = jnp.mean(y, axis=-1, keepdims=True)
    yc = y - mu
    var = jnp.mean(yc * yc, axis=-1, keepdims=True)
    return yc * lax.rsqrt(var + EPS) * g + b


def _tail_kernel(x_ref, mod_ref, hm_ref, ha_ref, sgm_ref, sga_ref, wbm_ref, wba_ref, wout_ref,
                 wup_ref, wdown_ref, ln_ref, o_ref):
    mod = mod_ref[0]
    ln = ln_ref[...]
    x = x_ref[0]
    merged = sgm_ref[0] * _dot(hm_ref[0], wbm_ref[...]) + sga_ref[0] * _dot(ha_ref[0], wba_ref[...])
    mix = _dot(merged.astype(BF16), wout_ref[...])
    x1 = _layer_norm(ALPHA * x + mod[2:3] * mix, ln[0:1], ln[1:2])
    h = (x1 * (1.0 + mod[4:5]) + mod[3:4]).astype(BF16)
    ff = jnp.zeros_like(x1)
    for j in range(D_FF // D_MODEL):
        u = jnp.maximum(_dot(h, wup_ref[:, j * D_MODEL:(j + 1) * D_MODEL]), 0.0)
        ff = ff + _dot((u * u).astype(BF16), wdown_ref[j * D_MODEL:(j + 1) * D_MODEL, :])
    o_ref[0] = _layer_norm(ALPHA * x1 + mod[5:6] * ff, ln[2:3], ln[3:4])


def _tail(x, mod, hm, ha, sgm, sga, wts, name):
    B, T, _ = x.shape
    tm = TOK_TILE
    per_batch_mod = mod.shape[0] > 1
    tok = pl.BlockSpec((1, tm, D_MODEL), lambda b, t: (b, t, 0))
    in_specs = [tok,
                pl.BlockSpec((1, 6, D_MODEL), (lambda b, t: (b, 0, 0)) if per_batch_mod else (lambda b, t: (0, 0, 0))),
                tok, tok, tok, tok] + [_resident(w.shape) for w in wts]
    return pl.pallas_call(
        _tail_kernel,
        grid=(B, T // tm),
        in_specs=in_specs,
        out_specs=tok,
        out_shape=jax.ShapeDtypeStruct((B, T, D_MODEL), F32),
        compiler_params=_params(("parallel", "parallel")),
        name=name,
    )(x, mod, hm, ha, sgm, sga, *wts)


def _rope_tables(n_tokens):
    rows = n_tokens // GRID_W
    row = jnp.repeat(jnp.arange(rows), GRID_W)
    col = jnp.tile(jnp.arange(GRID_W), rows)
    inv = ROPE_BASE ** (-jnp.arange(N_FREQ, dtype=F32) / N_FREQ)
    ang = jnp.stack([row, col], -1).astype(F32)[..., None] * inv
    ang = jnp.broadcast_to(ang[:, :, None, :], (n_tokens, 2, 2, N_FREQ))
    sign = jnp.asarray([-1.0, 1.0], F32)[None, None, :, None]
    return jnp.cos(ang).reshape(n_tokens, DH_A), (jnp.sin(ang) * sign).reshape(n_tokens, DH_A)


def kernel(x_prompt, x_sample, cache_k, cache_v, state_C, state_n, state_m, c, c_ctx, w_mod, b_mod, w_in,
           b_gates, mlstm_norm_g, q_norm_g, k_norm_g, w_bm, w_ba, w_out, ln1_g, ln1_b, w_up, w_down,
           ln2_g, ln2_b):
    B, T, _ = x_prompt.shape
    Bd, Td, _ = x_sample.shape
    l = 0

    w = w_in[l]
    o_g = 4 * D_MODEL
    o_a = o_g + 4 * NH_M
    o_mg = o_a + (N_Q + 2 * N_KV) * DH_A
    gate_rows = np.array([4, 5, 6, 7, 12, 13, 14, 15, 0, 1, 2, 3, 8, 9, 10, 11])
    proj_w = (w[:, :o_g].astype(BF16),
              w[:, o_g:o_a].T[gate_rows].astype(BF16),
              b_gates[l][gate_rows].reshape(4 * NH_M, 1),
              w[:, o_a:o_mg].astype(BF16),
              w[:, o_a + (N_Q + N_KV) * DH_A:o_mg].T.astype(BF16),
              w[:, o_mg:].astype(BF16),
              q_norm_g[l].reshape(1, DH_A),
              k_norm_g[l].reshape(1, DH_A))
    tail_w = (w_bm[l].astype(BF16), w_ba[l].astype(BF16), w_out[l].astype(BF16),
              w_up[l].astype(BF16), w_down[l].astype(BF16),
              jnp.stack([ln1_g[l], ln1_b[l], ln2_g[l], ln2_b[l]]))
    gnorm = mlstm_norm_g[l].reshape(1, D_MODEL)

    c_rows = jnp.concatenate([c_ctx[None, :], c, jnp.zeros((MOD_ROWS - 1 - Bd, D_MODEL), F32)], axis=0)
    mod = _modulation(c_rows, w_mod[l], b_mod[l]).reshape(MOD_ROWS, 6, D_MODEL)
    mod_ctx, mod_lat = mod[0:1], mod[1:1 + Bd]

    (qm, km, vm, som, gr, qa, ka, vt, sgm, sga, k_new, v_new) = _projection(x_prompt, mod_ctx, proj_w, None)
    hm, c_new, n_new, m_new = _mlstm(qm, km, vm, gr, som, gnorm, None, True)
    ha = _attention(qa, ka, vt, None)
    y_prompt = _tail(x_prompt, mod_ctx, hm, ha, sgm, sga, tail_w, "tail_ctx")

    (qm, km, vm, som, gr, qa, ka, vt, sgm, sga) = _projection(x_sample, mod_lat, proj_w, _rope_tables(Td))
    past = cache_k.shape[2]
    init = (state_C[:, l], state_n[:, l].reshape(Bd, 2, NH_M, 1, DH_M), state_m[:, l].reshape(Bd, 2, NH_M, 1, 1))
    hm, = _mlstm(qm, km, vm, gr, som, gnorm, init, False)
    vct = jnp.transpose(cache_v[:, l], (0, 2, 3, 1)).astype(BF16)
    vct = jnp.concatenate([vct, jnp.ones((Bd, N_KV, V_ROWS - DH_A, past), BF16)], axis=2)
    ctx_kv = (cache_k[:, l].reshape(Bd, past, N_KV * DH_A).astype(BF16), vct)
    ha = _attention(qa, ka, vt, ctx_kv)
    y_sample = _tail(x_sample, mod_lat, hm, ha, sgm, sga, tail_w, "tail_lat")

    return (y_prompt, y_sample,
            k_new.reshape(B, 1, T, N_KV, DH_A), v_new.reshape(B, 1, T, N_KV, DH_A),
            c_new.reshape(B, 1, 2, NH_M, DH_M, DH_M), n_new.reshape(B, 1, 2, NH_M, DH_M),
            m_new.reshape(B, 1, 2, NH_M))
```

```python
import functools

import jax
import jax.numpy as jnp
import numpy as np
from jax import lax
from jax.experimental import pallas as pl
from jax.experimental.pallas import tpu as pltpu

D_MODEL = 1024
NH_M = 4
DH_M = 256
N_Q = 8
N_KV = 2
G_Q = N_Q // N_KV
DH_A = 128
D_FF = 4 * D_MODEL
GRID_W = 64
N_FREQ = DH_A // 4
ROPE_BASE = 10000.0
EPS = 1e-6
DEPTH = 1
ALPHA = (2 * DEPTH) ** 0.25

CHUNK = 256
TOK_TILE = 256
Q_TILE = 256
K_TILE = 512
Q_SCALE = DH_A ** -0.5 * np.log2(np.e)
ATTN_LOOKAHEAD = 4
VM_ROWS = DH_M + 16
V_ROWS = DH_A + 16
MOD_ROWS = 8

V7X_VMEM_BYTES = 64 * 1024 * 1024
VMEM_LIMIT = V7X_VMEM_BYTES - 8 * 1024 * 1024

F32 = jnp.float32
BF16 = jnp.bfloat16


def _dot(a, b):
    return jnp.dot(a, b, preferred_element_type=F32)


def _dot_nt(a, b):
    return lax.dot_general(a, b, (((1,), (1,)), ((), ())), preferred_element_type=F32)


def _resident(shape):
    nd = len(shape)
    return pl.BlockSpec(shape, lambda *_: (0,) * nd, pipeline_mode=pl.Buffered(1))


def _params(semantics):
    return pltpu.CompilerParams(dimension_semantics=semantics, vmem_limit_bytes=VMEM_LIMIT)


def _mod_kernel(c_ref, w_ref, b_ref, o_ref):
    c = c_ref[...]
    s = c * jax.nn.sigmoid(c)
    o_ref[...] = _dot(s.astype(BF16), w_ref[...].astype(BF16)) + b_ref[...]


def _modulation(c_rows, w_mod, b_mod):
    n_out = w_mod.shape[1]
    blk = D_MODEL
    return pl.pallas_call(
        _mod_kernel,
        grid=(n_out // blk,),
        in_specs=[pl.BlockSpec((MOD_ROWS, D_MODEL), lambda j: (0, 0)),
                  pl.BlockSpec((D_MODEL, blk), lambda j: (0, j)),
                  pl.BlockSpec((1, blk), lambda j: (0, j))],
        out_specs=pl.BlockSpec((MOD_ROWS, blk), lambda j: (0, j)),
        out_shape=jax.ShapeDtypeStruct((MOD_ROWS, n_out), F32),
        compiler_params=_params(("parallel",)),
        name="modulation",
    )(c_rows, w_mod, b_mod.reshape(1, n_out))


def _log_sigmoid(x):
    return jnp.minimum(x, 0.0) - jnp.log1p(jnp.exp(-jnp.abs(x)))


def _rms(t, g):
    return t * lax.rsqrt(jnp.mean(t * t, axis=-1, keepdims=True) + EPS) * g


def _proj_kernel(rope, *refs):
    if rope:
        (x_ref, mod_ref, wm_ref, wmt_ref, wgt_ref, bg_ref, wa_ref, wvt_ref, wmg_ref, qg_ref, kg_ref, cos_ref, sin_ref,
         qm_o, km_o, vmt_o, somt_o, gr_o, qa_o, ka_o, vt_o, sgm_o, sga_o) = refs
    else:
        (x_ref, mod_ref, wm_ref, wmt_ref, wgt_ref, bg_ref, wa_ref, wvt_ref, wmg_ref, qg_ref, kg_ref,
         qm_o, km_o, vmt_o, somt_o, gr_o, qa_o, ka_o, vt_o, sgm_o, sga_o, kc_o, vc_o) = refs
    tm = x_ref.shape[1]
    mod = mod_ref[0]
    h = (x_ref[0] * (1.0 + mod[1:2]) + mod[0:1]).astype(BF16)

    qm_o[0] = _dot(h, wm_ref[:, 0:D_MODEL]).astype(BF16)
    km_o[0] = (_dot(h, wm_ref[:, D_MODEL:2 * D_MODEL]) * (DH_M ** -0.5)).astype(BF16)
    for hh in range(NH_M):
        vmt = _dot_nt(wmt_ref[hh * DH_M:(hh + 1) * DH_M, :], h)
        vmt_o[0, hh, 0:DH_M, :] = vmt.astype(BF16)
        vmt_o[0, hh, DH_M:VM_ROWS, :] = jnp.ones((VM_ROWS - DH_M, tm), BF16)
    somt_o[0] = jax.nn.sigmoid(_dot_nt(wmt_ref[D_MODEL:2 * D_MODEL, :], h))

    gates = _dot_nt(wgt_ref[...], h) + bg_ref[...]
    lf = _log_sigmoid(gates)
    row = lax.broadcasted_iota(jnp.int32, (tm, tm), 0)
    col = lax.broadcasted_iota(jnp.int32, (tm, tm), 1)
    tri = jnp.where(row <= col, 1.0, 0.0).astype(BF16)
    hi = lf.astype(BF16)
    r1 = lf - hi.astype(F32)
    mid = r1.astype(BF16)
    lo = (r1 - mid.astype(F32)).astype(BF16)
    cum = (_dot(hi, tri) + _dot(mid, tri) + _dot(lo, tri))[0:8]
    lf8 = lf[0:8]
    tot = cum[:, tm - 1:tm]
    rev = tot - cum + lf8
    is_fwd = lax.broadcasted_iota(jnp.int32, (8, tm), 0) < NH_M
    a = jnp.where(is_fwd, cum, rev)
    cc = gates[8:16] - a
    totb = jnp.broadcast_to(tot, (8, tm))
    zero_row = jnp.zeros((1, tm), F32)
    for hh in range(NH_M):
        rows = (a[hh:hh + 1], cc[hh:hh + 1], totb[hh:hh + 1],
                a[NH_M + hh:NH_M + hh + 1], cc[NH_M + hh:NH_M + hh + 1], totb[NH_M + hh:NH_M + hh + 1],
                zero_row, zero_row)
        for k, r in enumerate(rows):
            gr_o[0, hh, k:k + 1, :] = r

    if rope:
        cos = cos_ref[...]
        sin_s = sin_ref[...]
        lane = lax.broadcasted_iota(jnp.int32, (tm, DH_A), 1)
        first_half = (lane % (2 * N_FREQ)) < N_FREQ

        def rot(t):
            partner = jnp.where(first_half, pltpu.roll(t, DH_A - N_FREQ, 1), pltpu.roll(t, N_FREQ, 1))
            return t * cos + partner * sin_s
    else:
        rot = lambda t: t

    qg = qg_ref[...]
    kg = kg_ref[...]
    for g in range(N_Q):
        t = _rms(_dot(h, wa_ref[:, g * DH_A:(g + 1) * DH_A]), qg)
        qa_o[0, :, g * DH_A:(g + 1) * DH_A] = (rot(t) * Q_SCALE).astype(BF16)
    for g in range(N_KV):
        off = N_Q * DH_A + g * DH_A
        t = _rms(_dot(h, wa_ref[:, off:off + DH_A]), kg)
        if not rope:
            kc_o[0, :, g * DH_A:(g + 1) * DH_A] = t
        ka_o[0, :, g * DH_A:(g + 1) * DH_A] = rot(t).astype(BF16)
    if not rope:
        off = (N_Q + N_KV) * DH_A
        vc_o[0] = _dot(h, wa_ref[:, off:off + N_KV * DH_A])
    vt = _dot_nt(wvt_ref[...], h)
    for g in range(N_KV):
        vt_o[0, g, 0:DH_A, :] = vt[g * DH_A:(g + 1) * DH_A].astype(BF16)
        vt_o[0, g, DH_A:V_ROWS, :] = jnp.ones((V_ROWS - DH_A, tm), BF16)

    sgm_o[0] = jax.nn.sigmoid(_dot(h, wmg_ref[:, 0:D_MODEL]))
    sga_o[0] = jax.nn.sigmoid(_dot(h, wmg_ref[:, D_MODEL:2 * D_MODEL]))


def _projection(x, mod, wts, rope_tables):
    B, T, _ = x.shape
    tm = TOK_TILE
    nt = T // tm
    rope = rope_tables is not None
    per_batch_mod = mod.shape[0] > 1
    tok = lambda width: pl.BlockSpec((1, tm, width), lambda b, t: (b, t, 0))
    in_specs = [tok(D_MODEL),
                pl.BlockSpec((1, 6, D_MODEL), (lambda b, t: (b, 0, 0)) if per_batch_mod else (lambda b, t: (0, 0, 0)))]
    in_specs += [_resident(w.shape) for w in wts]
    args = [x, mod, *wts]
    if rope:
        in_specs += [pl.BlockSpec((tm, DH_A), lambda b, t: (t, 0))] * 2
        args += list(rope_tables)

    kv_w = N_KV * DH_A
    outs = [((B, T, D_MODEL), BF16, tok(D_MODEL)),
            ((B, T, D_MODEL), BF16, tok(D_MODEL)),
            ((B, NH_M, VM_ROWS, T), BF16,
             pl.BlockSpec((1, NH_M, VM_ROWS, tm), lambda b, t: (b, 0, 0, t))),
            ((B, D_MODEL, T), F32, pl.BlockSpec((1, D_MODEL, tm), lambda b, t: (b, 0, t))),
            ((B, NH_M, 8, T), F32, pl.BlockSpec((1, NH_M, 8, tm), lambda b, t: (b, 0, 0, t))),
            ((B, T, D_MODEL), BF16, tok(D_MODEL)),
            ((B, T, kv_w), BF16, tok(kv_w)),
            ((B, N_KV, V_ROWS, T), BF16,
             pl.BlockSpec((1, N_KV, V_ROWS, tm), lambda b, t: (b, 0, 0, t))),
            ((B, T, D_MODEL), F32, tok(D_MODEL)),
            ((B, T, D_MODEL), F32, tok(D_MODEL))]
    if not rope:
        outs += [((B, T, kv_w), F32, tok(kv_w)), ((B, T, kv_w), F32, tok(kv_w))]

    return pl.pallas_call(
        functools.partial(_proj_kernel, rope),
        grid=(B, nt),
        in_specs=in_specs,
        out_specs=[o[2] for o in outs],
        out_shape=[jax.ShapeDtypeStruct(o[0], o[1]) for o in outs],
        compiler_params=_params(("parallel", "parallel")),
        name="projection_lat" if rope else "projection_ctx",
    )(*args)


def _mlstm_kernel(has_init, emit_state, nc, *refs):
    refs = list(refs)
    q_ref, k_ref, vt_ref, g_ref, somt_ref, gn_ref = refs[:6]
    refs = refs[6:]
    if has_init:
        ct0_ref, m0_ref = refs[:2]
        refs = refs[2:]
    hm_o = refs[0]
    refs = refs[1:]
    if emit_state:
        c_o, n_o, m_o = refs[:3]
        refs = refs[3:]
    acc_ref, = refs

    L = CHUNK
    row = lax.broadcasted_iota(jnp.int32, (L, L), 0)
    col = lax.broadcasted_iota(jnp.int32, (L, L), 1)
    eye = row == col
    masks = (row <= col, row >= col)

    span = lambda c: slice(c * L, (c + 1) * L)
    chunk_of = lambda d, s: s if d == 0 else nc - 1 - s
    scores = lambda d, s: _dot_nt(k_ref[0, span(chunk_of(d, s)), :], q_ref[0, span(chunk_of(d, s)), :])

    state = [ct0_ref[0, d, 0] if has_init else None for d in range(2)]
    m_run = [m0_ref[0, d, 0] if has_init else jnp.zeros((1, 1), F32) for d in range(2)]
    arrived = [False] * nc

    def finish_chunk(c, ht):
        ht = acc_ref[c] + ht
        hn = ht * lax.rsqrt(jnp.mean(ht * ht, axis=0, keepdims=True) + EPS) * gn_ref[...]
        hm_o[0, span(c), :] = (hn * somt_ref[0, :, span(c)]).T.astype(BF16)

    st_next = [scores(d, 0) for d in range(2)]
    for s in range(nc):
        st_cur = st_next
        if s + 1 < nc:
            st_next = [scores(d, s + 1) for d in range(2)]
        inter = [None, None]
        if state[0] is not None:
            inter = [_dot_nt(state[d].astype(BF16), q_ref[0, span(chunk_of(d, s)), :]) for d in range(2)]
        for d in range(2):
            c = chunk_of(d, s)
            k = k_ref[0, span(c), :]
            vt = vt_ref[0, 0, :, span(c)]
            g = g_ref[0, 0, :, span(c)]
            a_row = g[3 * d:3 * d + 1]
            c_row = g[3 * d + 1:3 * d + 2]
            tot = g[3 * d + 2:3 * d + 3, 0:1]
            m_prev = m_run[d]
            c_col = jnp.sum(jnp.where(eye, c_row, 0.0), axis=-1, keepdims=True)

            keep_state = emit_state or s + 1 < nc
            if keep_state:
                m_new = tot + jnp.maximum(m_prev, jnp.max(c_row, axis=-1, keepdims=True))
                wk = (jnp.exp(tot + c_col - m_new) * k.astype(F32)).astype(BF16)
                upd = _dot(vt, wk)
                new_state = upd if state[d] is None else jnp.exp(tot + m_prev - m_new) * state[d] + upd

            cm = jnp.where(masks[d], c_col, -jnp.inf)
            m_rel = jnp.maximum(m_prev, jnp.max(cm, axis=0, keepdims=True))
            sp = (st_cur[d] * jnp.exp(cm - m_rel)).astype(BF16)
            numt = _dot(vt, sp)
            if inter[d] is not None:
                numt = numt + jnp.exp(m_prev - m_rel) * inter[d]
            den = numt[DH_M:DH_M + 1]
            ht = numt[0:DH_M] * (1.0 / jnp.maximum(jnp.abs(den), jnp.exp(-(a_row + m_rel))))
            if arrived[c]:
                finish_chunk(c, ht)
            else:
                acc_ref[c] = ht
                arrived[c] = True
            if keep_state:
                state[d] = new_state
                m_run[d] = m_new

    if emit_state:
        for d in range(2):
            c_o[0, d, 0] = state[d][0:DH_M].T
            n_o[0, d, 0] = state[d][DH_M:DH_M + 1]
            m_o[0, d, 0] = m_run[d]


def _mlstm(qm, km, vmt, gr, somt, gnorm_col, init_state, emit_state):
    B, T, _ = qm.shape
    nc = T // CHUNK
    has_init = init_state is not None

    seq = pl.BlockSpec((1, T, DH_M), lambda b, h: (b, 0, h))
    c_spec = pl.BlockSpec((1, 2, 1, DH_M, DH_M), lambda b, h: (b, 0, h, 0, 0))
    n_spec = pl.BlockSpec((1, 2, 1, 1, DH_M), lambda b, h: (b, 0, h, 0, 0))
    m_spec = pl.BlockSpec((1, 2, 1, 1, 1), lambda b, h: (b, 0, h, 0, 0))

    in_specs = [seq, seq,
                pl.BlockSpec((1, 1, VM_ROWS, T), lambda b, h: (b, h, 0, 0)),
                pl.BlockSpec((1, 1, 8, T), lambda b, h: (b, h, 0, 0)),
                pl.BlockSpec((1, DH_M, T), lambda b, h: (b, h, 0)),
                pl.BlockSpec((DH_M, 1), lambda b, h: (h, 0))]
    args = [qm, km, vmt, gr, somt, gnorm_col]
    if has_init:
        in_specs += [pl.BlockSpec((1, 2, 1, VM_ROWS, DH_M), lambda b, h: (b, 0, h, 0, 0)), m_spec]
        args += list(init_state)
    out_specs = [seq]
    out_shape = [jax.ShapeDtypeStruct((B, T, D_MODEL), BF16)]
    if emit_state:
        out_specs += [c_spec, n_spec, m_spec]
        out_shape += [jax.ShapeDtypeStruct((B, 2, NH_M, DH_M, DH_M), F32),
                      jax.ShapeDtypeStruct((B, 2, NH_M, 1, DH_M), F32),
                      jax.ShapeDtypeStruct((B, 2, NH_M, 1, 1), F32)]

    return pl.pallas_call(
        functools.partial(_mlstm_kernel, has_init, emit_state, nc),
        grid=(B, NH_M),
        in_specs=in_specs,
        out_specs=out_specs,
        out_shape=out_shape,
        scratch_shapes=[pltpu.VMEM((nc, DH_M, CHUNK), F32)],
        compiler_params=_params(("parallel", "parallel")),
        name="mlstm_lat" if has_init else "mlstm_ctx",
    )(*args)


def _attn_kernel(n_lat_tiles, has_ctx, *refs):
    if has_ctx:
        q_ref, k_ref, vt_ref, kc_ref, vct_ref, o_ref = refs
    else:
        q_ref, k_ref, vt_ref, o_ref = refs
    tk = k_ref.shape[1] // n_lat_tiles
    tiles = [(lambda i=i: k_ref[0, i * tk:(i + 1) * tk, :], lambda i=i: vt_ref[0, 0, :, i * tk:(i + 1) * tk])
             for i in range(n_lat_tiles)]
    if has_ctx:
        tiles.append((lambda: kc_ref[0], lambda: vct_ref[0, 0]))
    chains = [(t, g) for t in range(len(tiles)) for g in range(G_Q)]

    def scores(t, g):
        return _dot_nt(tiles[t][0](), q_ref[0, :, g * DH_A:(g + 1) * DH_A])

    m = [None] * G_Q
    acc = [None] * G_Q
    pending = []
    for idx in range(len(chains) + ATTN_LOOKAHEAD):
        if idx < len(chains):
            pending.append(scores(*chains[idx]))
        if idx < ATTN_LOOKAHEAD:
            continue
        t, g = chains[idx - ATTN_LOOKAHEAD]
        st = pending.pop(0)
        m_tile = jnp.max(st, axis=0, keepdims=True)
        m_new = m_tile if t == 0 else jnp.maximum(m[g], m_tile)
        pv = _dot(tiles[t][1](), jnp.exp2(st - m_new).astype(BF16))
        acc[g] = pv if t == 0 else jnp.exp2(m[g] - m_new) * acc[g] + pv
        m[g] = m_new

    for g in range(G_Q):
        out = acc[g][0:DH_A] * (1.0 / acc[g][DH_A:DH_A + 1])
        o_ref[0, :, g * DH_A:(g + 1) * DH_A] = out.T.astype(BF16)


def _attention(qa, ka, vt, ctx_kv):
    B, T, _ = qa.shape
    tq = min(Q_TILE, T)
    nq = T // tq
    has_ctx = ctx_kv is not None
    n_lat_tiles = max(1, T // K_TILE)
    qspec = pl.BlockSpec((1, tq, G_Q * DH_A), lambda b, h, i: (b, i, h))
    kspec = lambda tk: pl.BlockSpec((1, tk, DH_A), lambda b, h, i: (b, 0, h))
    vspec = lambda tk: pl.BlockSpec((1, 1, V_ROWS, tk), lambda b, h, i: (b, h, 0, 0))
    in_specs = [qspec, kspec(T), vspec(T)]
    args = [qa, ka, vt]
    if has_ctx:
        tc = ctx_kv[0].shape[1]
        in_specs += [kspec(tc), vspec(tc)]
        args += list(ctx_kv)
    return pl.pallas_call(
        functools.partial(_attn_kernel, n_lat_tiles, has_ctx),
        grid=(B, N_KV, nq),
        in_specs=in_specs,
        out_specs=qspec,
        out_shape=jax.ShapeDtypeStruct((B, T, D_MODEL), BF16),
        compiler_params=_params(("parallel", "parallel", "parallel")),
        name="attention_lat" if has_ctx else "attention_ctx",
    )(*args)


def _layer_norm(y, g, b):
    mu = jnp.mean(y, axis=-1, keepdims=True)
    yc = y - mu
    var = jnp.mean(yc * yc, axis=-1, keepdims=True)
    return yc * lax.rsqrt(var + EPS) * g + b


def _tail_kernel(x_ref, mod_ref, hm_ref, ha_ref, sgm_ref, sga_ref, wbm_ref, wba_ref, wout_ref,
                 wup_ref, wdown_ref, ln_ref, o_ref):
    mod = mod_ref[0]
    ln = ln_ref[...]
    x = x_ref[0]
    merged = sgm_ref[0] * _dot(hm_ref[0], wbm_ref[...]) + sga_ref[0] * _dot(ha_ref[0], wba_ref[...])
    mix = _dot(merged.astype(BF16), wout_ref[...])
    x1 = _layer_norm(ALPHA * x + mod[2:3] * mix, ln[0:1], ln[1:2])
    h = (x1 * (1.0 + mod[4:5]) + mod[3:4]).astype(BF16)
    ff = jnp.zeros_like(x1)
    for j in range(D_FF // D_MODEL):
        u = jnp.maximum(_dot(h, wup_ref[:, j * D_MODEL:(j + 1) * D_MODEL]), 0.0)
        ff = ff + _dot((u * u).astype(BF16), wdown_ref[j * D_MODEL:(j + 1) * D_MODEL, :])
    o_ref[0] = _layer_norm(ALPHA * x1 + mod[5:6] * ff, ln[2:3], ln[3:4])


def _tail(x, mod, hm, ha, sgm, sga, wts, name):
    B, T, _ = x.shape
    tm = TOK_TILE
    per_batch_mod = mod.shape[0] > 1
    tok = pl.BlockSpec((1, tm, D_MODEL), lambda b, t: (b, t, 0))
    in_specs = [tok,
                pl.BlockSpec((1, 6, D_MODEL), (lambda b, t: (b, 0, 0)) if per_batch_mod else (lambda b, t: (0, 0, 0))),
                tok, tok, tok, tok] + [_resident(w.shape) for w in wts]
    return pl.pallas_call(
        _tail_kernel,
        grid=(B, T // tm),
        in_specs=in_specs,
        out_specs=tok,
        out_shape=jax.ShapeDtypeStruct((B, T, D_MODEL), F32),
        compiler_params=_params(("parallel", "parallel")),
        name=name,
    )(x, mod, hm, ha, sgm, sga, *wts)


def _rope_tables(n_tokens):
    rows = n_tokens // GRID_W
    row = jnp.repeat(jnp.arange(rows), GRID_W)
    col = jnp.tile(jnp.arange(GRID_W), rows)
    inv = ROPE_BASE ** (-jnp.arange(N_FREQ, dtype=F32) / N_FREQ)
    ang = jnp.stack([row, col], -1).astype(F32)[..., None] * inv
    ang = jnp.broadcast_to(ang[:, :, None, :], (n_tokens, 2, 2, N_FREQ))
    sign = jnp.asarray([-1.0, 1.0], F32)[None, None, :, None]
    return jnp.cos(ang).reshape(n_tokens, DH_A), (jnp.sin(ang) * sign).reshape(n_tokens, DH_A)


def kernel(x_prompt, x_sample, cache_k, cache_v, state_C, state_n, state_m, c, c_ctx, w_mod, b_mod, w_in,
           b_gates, mlstm_norm_g, q_norm_g, k_norm_g, w_bm, w_ba, w_out, ln1_g, ln1_b, w_up, w_down,
           ln2_g, ln2_b):
    B, T, _ = x_prompt.shape
    Bd, Td, _ = x_sample.shape
    l = 0

    w = w_in[l]
    o_g = 4 * D_MODEL
    o_a = o_g + 4 * NH_M
    o_mg = o_a + (N_Q + 2 * N_KV) * DH_A
    gate_rows = np.array([4, 5, 6, 7, 12, 13, 14, 15, 0, 1, 2, 3, 8, 9, 10, 11])
    proj_w = (w[:, :2 * D_MODEL].astype(BF16),
              w[:, 2 * D_MODEL:o_g].T.astype(BF16),
              w[:, o_g:o_a].T[gate_rows].astype(BF16),
              b_gates[l][gate_rows].reshape(4 * NH_M, 1),
              w[:, o_a:o_mg].astype(BF16),
              w[:, o_a + (N_Q + N_KV) * DH_A:o_mg].T.astype(BF16),
              w[:, o_mg:].astype(BF16),
              q_norm_g[l].reshape(1, DH_A),
              k_norm_g[l].reshape(1, DH_A))
    tail_w = (w_bm[l].astype(BF16), w_ba[l].astype(BF16), w_out[l].astype(BF16),
              w_up[l].astype(BF16), w_down[l].astype(BF16),
              jnp.stack([ln1_g[l], ln1_b[l], ln2_g[l], ln2_b[l]]))
    gnorm = mlstm_norm_g[l].reshape(D_MODEL, 1)

    c_rows = jnp.concatenate([c_ctx[None, :], c, jnp.zeros((MOD_ROWS - 1 - Bd, D_MODEL), F32)], axis=0)
    mod = _modulation(c_rows, w_mod[l], b_mod[l]).reshape(MOD_ROWS, 6, D_MODEL)
    mod_ctx, mod_lat = mod[0:1], mod[1:1 + Bd]

    (qm, km, vmt, somt, gr, qa, ka, vt, sgm, sga, k_new, v_new) = _projection(x_prompt, mod_ctx, proj_w, None)
    hm, c_new, n_new, m_new = _mlstm(qm, km, vmt, gr, somt, gnorm, None, True)
    ha = _attention(qa, ka, vt, None)
    y_prompt = _tail(x_prompt, mod_ctx, hm, ha, sgm, sga, tail_w, "tail_ctx")

    (qm, km, vmt, somt, gr, qa, ka, vt, sgm, sga) = _projection(x_sample, mod_lat, proj_w, _rope_tables(Td))
    past = cache_k.shape[2]
    n_rows = jnp.broadcast_to(state_n[:, l][:, :, :, None, :], (Bd, 2, NH_M, VM_ROWS - DH_M, DH_M))
    init = (jnp.concatenate([jnp.swapaxes(state_C[:, l], -1, -2), n_rows], axis=-2),
            state_m[:, l].reshape(Bd, 2, NH_M, 1, 1))
    hm, = _mlstm(qm, km, vmt, gr, somt, gnorm, init, False)
    vct = jnp.transpose(cache_v[:, l], (0, 2, 3, 1)).astype(BF16)
    vct = jnp.concatenate([vct, jnp.ones((Bd, N_KV, V_ROWS - DH_A, past), BF16)], axis=2)
    ctx_kv = (cache_k[:, l].reshape(Bd, past, N_KV * DH_A).astype(BF16), vct)
    ha = _attention(qa, ka, vt, ctx_kv)
    y_sample = _tail(x_sample, mod_lat, hm, ha, sgm, sga, tail_w, "tail_lat")

    return (y_prompt, y_sample,
            k_new.reshape(B, 1, T, N_KV, DH_A), v_new.reshape(B, 1, T, N_KV, DH_A),
            c_new.reshape(B, 1, 2, NH_M, DH_M, DH_M), n_new.reshape(B, 1, 2, NH_M, DH_M),
            m_new.reshape(B, 1, 2, NH_M))
```

```python
import functools

import jax
import jax.numpy as jnp
import numpy as np
from jax import lax
from jax.experimental import pallas as pl
from jax.experimental.pallas import tpu as pltpu

D_MODEL = 1024
NH_M = 4
DH_M = 256
N_Q = 8
N_KV = 2
G_Q = N_Q // N_KV
DH_A = 128
D_FF = 4 * D_MODEL
GRID_W = 64
N_FREQ = DH_A // 4
ROPE_BASE = 10000.0
EPS = 1e-6
DEPTH = 1
ALPHA = (2 * DEPTH) ** 0.25

CHUNK = 256
TOK_TILE = 256
TAIL_TILE = 512
Q_TILE = 256
K_TILE = 512
LOG2E = float(np.log2(np.e))
Q_SCALE = DH_A ** -0.5 * LOG2E
ATTN_LOOKAHEAD = 4
VM_ROWS = DH_M + 16
V_ROWS = DH_A + 16
MOD_ROWS = 8

V7X_VMEM_BYTES = 64 * 1024 * 1024
VMEM_LIMIT = V7X_VMEM_BYTES - 8 * 1024 * 1024

F32 = jnp.float32
BF16 = jnp.bfloat16


def _dot(a, b):
    return jnp.dot(a, b, preferred_element_type=F32)


def _dot_nt(a, b):
    return lax.dot_general(a, b, (((1,), (1,)), ((), ())), preferred_element_type=F32)


def _resident(shape):
    nd = len(shape)
    return pl.BlockSpec(shape, lambda *_: (0,) * nd, pipeline_mode=pl.Buffered(1))


def _params(semantics):
    return pltpu.CompilerParams(dimension_semantics=semantics, vmem_limit_bytes=VMEM_LIMIT)


def _mod_kernel(c_ref, w_ref, b_ref, o_ref):
    c = c_ref[...]
    s = c * jax.nn.sigmoid(c)
    o_ref[...] = _dot(s.astype(BF16), w_ref[...].astype(BF16)) + b_ref[...]


def _modulation(c_rows, w_mod, b_mod):
    n_out = w_mod.shape[1]
    blk = D_MODEL
    return pl.pallas_call(
        _mod_kernel,
        grid=(n_out // blk,),
        in_specs=[pl.BlockSpec((MOD_ROWS, D_MODEL), lambda j: (0, 0)),
                  pl.BlockSpec((D_MODEL, blk), lambda j: (0, j)),
                  pl.BlockSpec((1, blk), lambda j: (0, j))],
        out_specs=pl.BlockSpec((MOD_ROWS, blk), lambda j: (0, j)),
        out_shape=jax.ShapeDtypeStruct((MOD_ROWS, n_out), F32),
        compiler_params=_params(("parallel",)),
        name="modulation",
    )(c_rows, w_mod, b_mod.reshape(1, n_out))


def _log_sigmoid(x):
    return jnp.minimum(x, 0.0) - jnp.log1p(jnp.exp(-jnp.abs(x)))


def _cummax_lanes(x, reverse):
    n = x.shape[-1]
    lane = lax.broadcasted_iota(jnp.int32, x.shape, x.ndim - 1)
    step = 1
    while step < n:
        if reverse:
            shifted, valid = pltpu.roll(x, n - step, x.ndim - 1), lane < n - step
        else:
            shifted, valid = pltpu.roll(x, step, x.ndim - 1), lane >= step
        x = jnp.maximum(x, jnp.where(valid, shifted, -jnp.inf))
        step *= 2
    return x


def _rms(t, g):
    return t * lax.rsqrt(jnp.mean(t * t, axis=-1, keepdims=True) + EPS) * g


def _proj_kernel(rope, *refs):
    if rope:
        (x_ref, mod_ref, wm_ref, wmt_ref, wgt_ref, bg_ref, wa_ref, wvt_ref, wmg_ref, qg_ref, kg_ref, cos_ref, sin_ref,
         qm_o, km_o, vmt_o, somt_o, gr_o, qa_o, ka_o, vt_o, sgm_o, sga_o) = refs
    else:
        (x_ref, mod_ref, wm_ref, wmt_ref, wgt_ref, bg_ref, wa_ref, wvt_ref, wmg_ref, qg_ref, kg_ref,
         qm_o, km_o, vmt_o, somt_o, gr_o, qa_o, ka_o, vt_o, sgm_o, sga_o, kc_o, vc_o) = refs
    tm = x_ref.shape[1]
    mod = mod_ref[0]
    h = (x_ref[0] * (1.0 + mod[1:2]) + mod[0:1]).astype(BF16)

    qm_o[0] = _dot(h, wm_ref[:, 0:D_MODEL]).astype(BF16)
    km_o[0] = (_dot(h, wm_ref[:, D_MODEL:2 * D_MODEL]) * (DH_M ** -0.5)).astype(BF16)
    for hh in range(NH_M):
        vmt = _dot_nt(wmt_ref[hh * DH_M:(hh + 1) * DH_M, :], h)
        vmt_o[0, hh, 0:DH_M, :] = vmt.astype(BF16)
        vmt_o[0, hh, DH_M:VM_ROWS, :] = jnp.ones((VM_ROWS - DH_M, tm), BF16)
    somt_o[0] = jax.nn.sigmoid(_dot_nt(wmt_ref[D_MODEL:2 * D_MODEL, :], h))

    gates = _dot_nt(wgt_ref[...], h) + bg_ref[...]
    lf = _log_sigmoid(gates)
    row = lax.broadcasted_iota(jnp.int32, (tm, tm), 0)
    col = lax.broadcasted_iota(jnp.int32, (tm, tm), 1)
    tri = jnp.where(row <= col, 1.0, 0.0).astype(BF16)
    hi = lf.astype(BF16)
    r1 = lf - hi.astype(F32)
    mid = r1.astype(BF16)
    lo = (r1 - mid.astype(F32)).astype(BF16)
    cum = (_dot(hi, tri) + _dot(mid, tri) + _dot(lo, tri))[0:8]
    lf8 = lf[0:8]
    tot = cum[:, tm - 1:tm]
    rev = tot - cum + lf8
    is_fwd = lax.broadcasted_iota(jnp.int32, (8, tm), 0) < NH_M
    a = jnp.where(is_fwd, cum, rev)
    cc = (gates[8:16] - a) * LOG2E
    a = a * LOG2E
    totb = jnp.broadcast_to(tot * LOG2E, (8, tm))
    c_pre = _cummax_lanes(cc, False)
    c_suf = _cummax_lanes(cc, True)
    for hh in range(NH_M):
        rows = (a[hh:hh + 1], cc[hh:hh + 1], totb[hh:hh + 1],
                a[NH_M + hh:NH_M + hh + 1], cc[NH_M + hh:NH_M + hh + 1], totb[NH_M + hh:NH_M + hh + 1],
                c_pre[hh:hh + 1], c_suf[NH_M + hh:NH_M + hh + 1])
        for k, r in enumerate(rows):
            gr_o[0, hh, k:k + 1, :] = r

    if rope:
        cos = cos_ref[...]
        sin_s = sin_ref[...]
        lane = lax.broadcasted_iota(jnp.int32, (tm, DH_A), 1)
        first_half = (lane % (2 * N_FREQ)) < N_FREQ

        def rot(t):
            partner = jnp.where(first_half, pltpu.roll(t, DH_A - N_FREQ, 1), pltpu.roll(t, N_FREQ, 1))
            return t * cos + partner * sin_s
    else:
        rot = lambda t: t

    qg = qg_ref[...]
    kg = kg_ref[...]
    q_all = _dot(h, wa_ref[:, 0:N_Q * DH_A])
    k_all = _dot(h, wa_ref[:, N_Q * DH_A:(N_Q + N_KV) * DH_A])
    for g in range(N_Q):
        t = _rms(q_all[:, g * DH_A:(g + 1) * DH_A], qg)
        qa_o[0, :, g * DH_A:(g + 1) * DH_A] = (rot(t) * Q_SCALE).astype(BF16)
    for g in range(N_KV):
        t = _rms(k_all[:, g * DH_A:(g + 1) * DH_A], kg)
        if not rope:
            kc_o[0, :, g * DH_A:(g + 1) * DH_A] = t
        ka_o[0, :, g * DH_A:(g + 1) * DH_A] = rot(t).astype(BF16)
    if not rope:
        off = (N_Q + N_KV) * DH_A
        vc_o[0] = _dot(h, wa_ref[:, off:off + N_KV * DH_A])
    vt = _dot_nt(wvt_ref[...], h)
    for g in range(N_KV):
        vt_o[0, g, 0:DH_A, :] = vt[g * DH_A:(g + 1) * DH_A].astype(BF16)
        vt_o[0, g, DH_A:V_ROWS, :] = jnp.ones((V_ROWS - DH_A, tm), BF16)

    sgm_o[0] = jax.nn.sigmoid(_dot(h, wmg_ref[:, 0:D_MODEL]))
    sga_o[0] = jax.nn.sigmoid(_dot(h, wmg_ref[:, D_MODEL:2 * D_MODEL]))


def _projection(x, mod, wts, rope_tables):
    B, T, _ = x.shape
    tm = TOK_TILE
    nt = T // tm
    rope = rope_tables is not None
    per_batch_mod = mod.shape[0] > 1
    tok = lambda width: pl.BlockSpec((1, tm, width), lambda b, t: (b, t, 0))
    in_specs = [tok(D_MODEL),
                pl.BlockSpec((1, 6, D_MODEL), (lambda b, t: (b, 0, 0)) if per_batch_mod else (lambda b, t: (0, 0, 0)))]
    in_specs += [_resident(w.shape) for w in wts]
    args = [x, mod, *wts]
    if rope:
        in_specs += [pl.BlockSpec((tm, DH_A), lambda b, t: (t, 0))] * 2
        args += list(rope_tables)

    kv_w = N_KV * DH_A
    outs = [((B, T, D_MODEL), BF16, tok(D_MODEL)),
            ((B, T, D_MODEL), BF16, tok(D_MODEL)),
            ((B, NH_M, VM_ROWS, T), BF16,
             pl.BlockSpec((1, NH_M, VM_ROWS, tm), lambda b, t: (b, 0, 0, t))),
            ((B, D_MODEL, T), F32, pl.BlockSpec((1, D_MODEL, tm), lambda b, t: (b, 0, t))),
            ((B, NH_M, 8, T), F32, pl.BlockSpec((1, NH_M, 8, tm), lambda b, t: (b, 0, 0, t))),
            ((B, T, D_MODEL), BF16, tok(D_MODEL)),
            ((B, T, kv_w), BF16, tok(kv_w)),
            ((B, N_KV, V_ROWS, T), BF16,
             pl.BlockSpec((1, N_KV, V_ROWS, tm), lambda b, t: (b, 0, 0, t))),
            ((B, T, D_MODEL), F32, tok(D_MODEL)),
            ((B, T, D_MODEL), F32, tok(D_MODEL))]
    if not rope:
        outs += [((B, T, kv_w), F32, tok(kv_w)), ((B, T, kv_w), F32, tok(kv_w))]

    return pl.pallas_call(
        functools.partial(_proj_kernel, rope),
        grid=(B, nt),
        in_specs=in_specs,
        out_specs=[o[2] for o in outs],
        out_shape=[jax.ShapeDtypeStruct(o[0], o[1]) for o in outs],
        compiler_params=_params(("parallel", "parallel")),
        name="projection_lat" if rope else "projection_ctx",
    )(*args)


def _mlstm_kernel(has_init, emit_state, nc, nh, *refs):
    refs = list(refs)
    q_ref, k_ref, vt_ref, g_ref, somt_ref, gn_ref = refs[:6]
    refs = refs[6:]
    if has_init:
        ct0_ref, m0_ref = refs[:2]
        refs = refs[2:]
    hm_o = refs[0]
    refs = refs[1:]
    if emit_state:
        c_o, n_o, m_o = refs[:3]
        refs = refs[3:]
    acc_ref, = refs

    L = CHUNK
    row = lax.broadcasted_iota(jnp.int32, (L, L), 0)
    col = lax.broadcasted_iota(jnp.int32, (L, L), 1)
    eye = row == col
    masks = (row <= col, row >= col)

    span = lambda c: slice(c * L, (c + 1) * L)
    feat = lambda hh: slice(hh * DH_M, (hh + 1) * DH_M)
    chunk_of = lambda d, s: s if d == 0 else nc - 1 - s
    steps = [(hh, s) for hh in range(nh) for s in range(nc)]

    def scores(hh, s):
        return [_dot_nt(k_ref[0, span(chunk_of(d, s)), feat(hh)], q_ref[0, span(chunk_of(d, s)), feat(hh)])
                for d in range(2)]

    def finish_chunk(hh, c, ht):
        ht = acc_ref[c] + ht
        hn = ht * lax.rsqrt(jnp.mean(ht * ht, axis=0, keepdims=True) + EPS) * gn_ref[feat(hh), :]
        hm_o[0, span(c), feat(hh)] = (hn * somt_ref[0, feat(hh), span(c)]).T.astype(BF16)

    st_next = scores(*steps[0])
    for idx, (hh, s) in enumerate(steps):
        if s == 0:
            state = [ct0_ref[0, d, hh] if has_init else None for d in range(2)]
            m_run = [m0_ref[0, d, hh] * LOG2E if has_init else jnp.zeros((1, 1), F32) for d in range(2)]
            arrived = [False] * nc
        st_cur = st_next
        if idx + 1 < len(steps):
            st_next = scores(*steps[idx + 1])
        inter = [None, None]
        if state[0] is not None:
            inter = [_dot_nt(state[d].astype(BF16), q_ref[0, span(chunk_of(d, s)), feat(hh)]) for d in range(2)]
        for d in range(2):
            c = chunk_of(d, s)
            k = k_ref[0, span(c), feat(hh)]
            vt = vt_ref[0, hh, :, span(c)]
            g = g_ref[0, hh, :, span(c)]
            a_row = g[3 * d:3 * d + 1]
            c_row = g[3 * d + 1:3 * d + 2]
            tot = g[3 * d + 2:3 * d + 3, 0:1]
            m_prev = m_run[d]
            c_col = jnp.sum(jnp.where(eye, c_row, 0.0), axis=-1, keepdims=True)
            c_run = g[6 + d:7 + d]

            keep_state = emit_state or s + 1 < nc
            if keep_state:
                c_max = c_run[:, L - 1:L] if d == 0 else c_run[:, 0:1]
                m_new = tot + jnp.maximum(m_prev, c_max)
                wk = (jnp.exp2(tot + c_col - m_new) * k.astype(F32)).astype(BF16)
                upd = _dot(vt, wk)
                new_state = upd if state[d] is None else jnp.exp2(tot + m_prev - m_new) * state[d] + upd

            m_rel = jnp.maximum(m_prev, c_run)
            sp = (st_cur[d] * jnp.exp2(jnp.where(masks[d], c_col, -jnp.inf) - m_rel)).astype(BF16)
            numt = _dot(vt, sp)
            if inter[d] is not None:
                numt = numt + jnp.exp2(m_prev - m_rel) * inter[d]
            den = numt[DH_M:DH_M + 1]
            ht = numt[0:DH_M] * (1.0 / jnp.maximum(jnp.abs(den), jnp.exp2(-(a_row + m_rel))))
            if arrived[c]:
                finish_chunk(hh, c, ht)
            else:
                acc_ref[c] = ht
                arrived[c] = True
            if keep_state:
                state[d] = new_state
                m_run[d] = m_new

        if emit_state and s == nc - 1:
            for d in range(2):
                c_o[0, d, hh] = state[d][0:DH_M].T
                n_o[0, d, hh] = state[d][DH_M:DH_M + 1]
                m_o[0, d, hh] = m_run[d] * (1.0 / LOG2E)


def _mlstm(qm, km, vmt, gr, somt, gnorm_col, init_state, emit_state, nh):
    B, T, _ = qm.shape
    nc = T // CHUNK
    has_init = init_state is not None

    seq = pl.BlockSpec((1, T, nh * DH_M), lambda b, h: (b, 0, h))
    c_spec = pl.BlockSpec((1, 2, nh, DH_M, DH_M), lambda b, h: (b, 0, h, 0, 0))
    n_spec = pl.BlockSpec((1, 2, nh, 1, DH_M), lambda b, h: (b, 0, h, 0, 0))
    m_spec = pl.BlockSpec((1, 2, nh, 1, 1), lambda b, h: (b, 0, h, 0, 0))

    in_specs = [seq, seq,
                pl.BlockSpec((1, nh, VM_ROWS, T), lambda b, h: (b, h, 0, 0)),
                pl.BlockSpec((1, nh, 8, T), lambda b, h: (b, h, 0, 0)),
                pl.BlockSpec((1, nh * DH_M, T), lambda b, h: (b, h, 0)),
                pl.BlockSpec((nh * DH_M, 1), lambda b, h: (h, 0))]
    args = [qm, km, vmt, gr, somt, gnorm_col]
    if has_init:
        in_specs += [pl.BlockSpec((1, 2, nh, VM_ROWS, DH_M), lambda b, h: (b, 0, h, 0, 0)), m_spec]
        args += list(init_state)
    out_specs = [seq]
    out_shape = [jax.ShapeDtypeStruct((B, T, D_MODEL), BF16)]
    if emit_state:
        out_specs += [c_spec, n_spec, m_spec]
        out_shape += [jax.ShapeDtypeStruct((B, 2, NH_M, DH_M, DH_M), F32),
                      jax.ShapeDtypeStruct((B, 2, NH_M, 1, DH_M), F32),
                      jax.ShapeDtypeStruct((B, 2, NH_M, 1, 1), F32)]

    return pl.pallas_call(
        functools.partial(_mlstm_kernel, has_init, emit_state, nc, nh),
        grid=(B, NH_M // nh),
        in_specs=in_specs,
        out_specs=out_specs,
        out_shape=out_shape,
        scratch_shapes=[pltpu.VMEM((nc, DH_M, CHUNK), F32)],
        compiler_params=_params(("parallel", "parallel")),
        name="mlstm_lat" if has_init else "mlstm_ctx",
    )(*args)


def _attn_kernel(n_lat_tiles, has_ctx, *refs):
    if has_ctx:
        q_ref, k_ref, vt_ref, kc_ref, vct_ref, o_ref = refs
    else:
        q_ref, k_ref, vt_ref, o_ref = refs
    tk = k_ref.shape[1] // n_lat_tiles
    tiles = [(lambda i=i: k_ref[0, i * tk:(i + 1) * tk, :], lambda i=i: vt_ref[0, 0, :, i * tk:(i + 1) * tk])
             for i in range(n_lat_tiles)]
    if has_ctx:
        tiles.append((lambda: kc_ref[0], lambda: vct_ref[0, 0]))
    chains = [(t, g) for t in range(len(tiles)) for g in range(G_Q)]

    def scores(t, g):
        return _dot_nt(tiles[t][0](), q_ref[0, :, g * DH_A:(g + 1) * DH_A])

    m = [None] * G_Q
    acc = [None] * G_Q
    pending = []
    for idx in range(len(chains) + ATTN_LOOKAHEAD):
        if idx < len(chains):
            pending.append(scores(*chains[idx]))
        if idx < ATTN_LOOKAHEAD:
            continue
        t, g = chains[idx - ATTN_LOOKAHEAD]
        st = pending.pop(0)
        m_tile = jnp.max(st, axis=0, keepdims=True)
        m_new = m_tile if t == 0 else jnp.maximum(m[g], m_tile)
        pv = _dot(tiles[t][1](), jnp.exp2(st - m_new).astype(BF16))
        acc[g] = pv if t == 0 else jnp.exp2(m[g] - m_new) * acc[g] + pv
        m[g] = m_new

    for g in range(G_Q):
        out = acc[g][0:DH_A] * (1.0 / acc[g][DH_A:DH_A + 1])
        o_ref[0, :, g * DH_A:(g + 1) * DH_A] = out.T.astype(BF16)


def _attention(qa, ka, vt, ctx_kv):
    B, T, _ = qa.shape
    tq = min(Q_TILE, T)
    nq = T // tq
    has_ctx = ctx_kv is not None
    n_lat_tiles = max(1, T // K_TILE)
    qspec = pl.BlockSpec((1, tq, G_Q * DH_A), lambda b, h, i: (b, i, h))
    kspec = lambda tk: pl.BlockSpec((1, tk, DH_A), lambda b, h, i: (b, 0, h))
    vspec = lambda tk: pl.BlockSpec((1, 1, V_ROWS, tk), lambda b, h, i: (b, h, 0, 0))
    in_specs = [qspec, kspec(T), vspec(T)]
    args = [qa, ka, vt]
    if has_ctx:
        tc = ctx_kv[0].shape[1]
        in_specs += [kspec(tc), vspec(tc)]
        args += list(ctx_kv)
    return pl.pallas_call(
        functools.partial(_attn_kernel, n_lat_tiles, has_ctx),
        grid=(B, N_KV, nq),
        in_specs=in_specs,
        out_specs=qspec,
        out_shape=jax.ShapeDtypeStruct((B, T, D_MODEL), BF16),
        compiler_params=_params(("parallel", "parallel", "parallel")),
        name="attention_lat" if has_ctx else "attention_ctx",
    )(*args)


def _layer_norm(y, g, b):
    mu = jnp.mean(y, axis=-1, keepdims=True)
    yc = y - mu
    var = jnp.mean(yc * yc, axis=-1, keepdims=True)
    return yc * lax.rsqrt(var + EPS) * g + b


def _tail_kernel(x_ref, mod_ref, hm_ref, ha_ref, sgm_ref, sga_ref, wbm_ref, wba_ref, wout_ref,
                 wup_ref, wdown_ref, ln_ref, o_ref):
    mod = mod_ref[0]
    ln = ln_ref[...]
    x = x_ref[0]
    merged = sgm_ref[0] * _dot(hm_ref[0], wbm_ref[...]) + sga_ref[0] * _dot(ha_ref[0], wba_ref[...])
    mix = _dot(merged.astype(BF16), wout_ref[...])
    x1 = _layer_norm(ALPHA * x + mod[2:3] * mix, ln[0:1], ln[1:2])
    h = (x1 * (1.0 + mod[4:5]) + mod[3:4]).astype(BF16)
    ff = jnp.zeros_like(x1)
    for j in range(D_FF // D_MODEL):
        u = jnp.maximum(_dot(h, wup_ref[:, j * D_MODEL:(j + 1) * D_MODEL]), 0.0)
        ff = ff + _dot((u * u).astype(BF16), wdown_ref[j * D_MODEL:(j + 1) * D_MODEL, :])
    o_ref[0] = _layer_norm(ALPHA * x1 + mod[5:6] * ff, ln[2:3], ln[3:4])


def _tail(x, mod, hm, ha, sgm, sga, wts, name):
    per_batch_mod = mod.shape[0] > 1
    if not per_batch_mod:
        x, hm, ha, sgm, sga = (a.reshape(1, -1, D_MODEL) for a in (x, hm, ha, sgm, sga))
    B, T, _ = x.shape
    tm = TAIL_TILE
    tok = pl.BlockSpec((1, tm, D_MODEL), lambda b, t: (b, t, 0))
    in_specs = [tok,
                pl.BlockSpec((1, 6, D_MODEL), (lambda b, t: (b, 0, 0)) if per_batch_mod else (lambda b, t: (0, 0, 0))),
                tok, tok, tok, tok] + [_resident(w.shape) for w in wts]
    return pl.pallas_call(
        _tail_kernel,
        grid=(B, T // tm),
        in_specs=in_specs,
        out_specs=tok,
        out_shape=jax.ShapeDtypeStruct((B, T, D_MODEL), F32),
        compiler_params=_params(("parallel", "parallel")),
        name=name,
    )(x, mod, hm, ha, sgm, sga, *wts)


def _rope_tables(n_tokens):
    rows = n_tokens // GRID_W
    row = jnp.repeat(jnp.arange(rows), GRID_W)
    col = jnp.tile(jnp.arange(GRID_W), rows)
    inv = ROPE_BASE ** (-jnp.arange(N_FREQ, dtype=F32) / N_FREQ)
    ang = jnp.stack([row, col], -1).astype(F32)[..., None] * inv
    ang = jnp.broadcast_to(ang[:, :, None, :], (n_tokens, 2, 2, N_FREQ))
    sign = jnp.asarray([-1.0, 1.0], F32)[None, None, :, None]
    return jnp.cos(ang).reshape(n_tokens, DH_A), (jnp.sin(ang) * sign).reshape(n_tokens, DH_A)


def kernel(x_prompt, x_sample, cache_k, cache_v, state_C, state_n, state_m, c, c_ctx, w_mod, b_mod, w_in,
           b_gates, mlstm_norm_g, q_norm_g, k_norm_g, w_bm, w_ba, w_out, ln1_g, ln1_b, w_up, w_down,
           ln2_g, ln2_b):
    B, T, _ = x_prompt.shape
    Bd, Td, _ = x_sample.shape
    l = 0

    w = w_in[l]
    o_g = 4 * D_MODEL
    o_a = o_g + 4 * NH_M
    o_mg = o_a + (N_Q + 2 * N_KV) * DH_A
    gate_rows = np.array([4, 5, 6, 7, 12, 13, 14, 15, 0, 1, 2, 3, 8, 9, 10, 11])
    proj_w = (w[:, :2 * D_MODEL].astype(BF16),
              w[:, 2 * D_MODEL:o_g].T.astype(BF16),
              w[:, o_g:o_a].T[gate_rows].astype(BF16),
              b_gates[l][gate_rows].reshape(4 * NH_M, 1),
              w[:, o_a:o_mg].astype(BF16),
              w[:, o_a + (N_Q + N_KV) * DH_A:o_mg].T.astype(BF16),
              w[:, o_mg:].astype(BF16),
              q_norm_g[l].reshape(1, DH_A),
              k_norm_g[l].reshape(1, DH_A))
    tail_w = (w_bm[l].astype(BF16), w_ba[l].astype(BF16), w_out[l].astype(BF16),
              w_up[l].astype(BF16), w_down[l].astype(BF16),
              jnp.stack([ln1_g[l], ln1_b[l], ln2_g[l], ln2_b[l]]))
    gnorm = mlstm_norm_g[l].reshape(D_MODEL, 1)

    c_rows = jnp.concatenate([c_ctx[None, :], c, jnp.zeros((MOD_ROWS - 1 - Bd, D_MODEL), F32)], axis=0)
    mod = _modulation(c_rows, w_mod[l], b_mod[l]).reshape(MOD_ROWS, 6, D_MODEL)
    mod_ctx, mod_lat = mod[0:1], mod[1:1 + Bd]

    (qm, km, vmt, somt, gr, qa, ka, vt, sgm, sga, k_new, v_new) = _projection(x_prompt, mod_ctx, proj_w, None)
    hm, c_new, n_new, m_new = _mlstm(qm, km, vmt, gr, somt, gnorm, None, True, NH_M)
    ha = _attention(qa, ka, vt, None)
    y_prompt = _tail(x_prompt, mod_ctx, hm, ha, sgm, sga, tail_w, "tail_ctx").reshape(x_prompt.shape)

    (qm, km, vmt, somt, gr, qa, ka, vt, sgm, sga) = _projection(x_sample, mod_lat, proj_w, _rope_tables(Td))
    past = cache_k.shape[2]
    n_rows = jnp.broadcast_to(state_n[:, l][:, :, :, None, :], (Bd, 2, NH_M, VM_ROWS - DH_M, DH_M))
    init = (jnp.concatenate([jnp.swapaxes(state_C[:, l], -1, -2), n_rows], axis=-2),
            state_m[:, l].reshape(Bd, 2, NH_M, 1, 1))
    hm, = _mlstm(qm, km, vmt, gr, somt, gnorm, init, False, 1)
    vct = jnp.transpose(cache_v[:, l], (0, 2, 3, 1)).astype(BF16)
    vct = jnp.concatenate([vct, jnp.ones((Bd, N_KV, V_ROWS - DH_A, past), BF16)], axis=2)
    ctx_kv = (cache_k[:, l].reshape(Bd, past, N_KV * DH_A).astype(BF16), vct)
    ha = _attention(qa, ka, vt, ctx_kv)
    y_sample = _tail(x_sample, mod_lat, hm, ha, sgm, sga, tail_w, "tail_lat")

    return (y_prompt, y_sample,
            k_new.reshape(B, 1, T, N_KV, DH_A), v_new.reshape(B, 1, T, N_KV, DH_A),
            c_new.reshape(B, 1, 2, NH_M, DH_M, DH_M), n_new.reshape(B, 1, 2, NH_M, DH_M),
            m_new.reshape(B, 1, 2, NH_M))
```

```python
import functools

import jax
import jax.numpy as jnp
import numpy as np
from jax import lax
from jax.experimental import pallas as pl
from jax.experimental.pallas import tpu as pltpu

D_MODEL = 1024
NH_M = 4
DH_M = 256
N_Q = 8
N_KV = 2
G_Q = N_Q // N_KV
DH_A = 128
D_FF = 4 * D_MODEL
GRID_W = 64
N_FREQ = DH_A // 4
ROPE_BASE = 10000.0
EPS = 1e-6
DEPTH = 1
ALPHA = (2 * DEPTH) ** 0.25

CHUNK = 256
TOK_TILE = 256
TAIL_TILE = 512
Q_TILE = 512
Q_CHAIN = 256
K_TILE = 512
LOG2E = float(np.log2(np.e))
Q_SCALE = DH_A ** -0.5 * LOG2E
ATTN_LOOKAHEAD = 4
VM_ROWS = DH_M + 16
V_ROWS = DH_A + 16
MOD_ROWS = 8

V7X_VMEM_BYTES = 64 * 1024 * 1024
VMEM_LIMIT = V7X_VMEM_BYTES - 8 * 1024 * 1024

F32 = jnp.float32
BF16 = jnp.bfloat16


def _dot(a, b):
    return jnp.dot(a, b, preferred_element_type=F32)


def _dot_nt(a, b):
    return lax.dot_general(a, b, (((1,), (1,)), ((), ())), preferred_element_type=F32)


def _resident(shape):
    nd = len(shape)
    return pl.BlockSpec(shape, lambda *_: (0,) * nd, pipeline_mode=pl.Buffered(1))


def _params(semantics):
    return pltpu.CompilerParams(dimension_semantics=semantics, vmem_limit_bytes=VMEM_LIMIT)


def _mod_kernel(c_ref, w_ref, b_ref, o_ref):
    c = c_ref[...]
    s = c * jax.nn.sigmoid(c)
    o_ref[...] = _dot(s.astype(BF16), w_ref[...].astype(BF16)) + b_ref[...]


def _modulation(c_rows, w_mod, b_mod):
    n_out = w_mod.shape[1]
    blk = D_MODEL
    return pl.pallas_call(
        _mod_kernel,
        grid=(n_out // blk,),
        in_specs=[pl.BlockSpec((MOD_ROWS, D_MODEL), lambda j: (0, 0)),
                  pl.BlockSpec((D_MODEL, blk), lambda j: (0, j)),
                  pl.BlockSpec((1, blk), lambda j: (0, j))],
        out_specs=pl.BlockSpec((MOD_ROWS, blk), lambda j: (0, j)),
        out_shape=jax.ShapeDtypeStruct((MOD_ROWS, n_out), F32),
        compiler_params=_params(("parallel",)),
        name="modulation",
    )(c_rows, w_mod, b_mod.reshape(1, n_out))


def _log_sigmoid(x):
    return jnp.minimum(x, 0.0) - jnp.log1p(jnp.exp(-jnp.abs(x)))


def _cummax_lanes(x, reverse):
    n = x.shape[-1]
    lane = lax.broadcasted_iota(jnp.int32, x.shape, x.ndim - 1)
    step = 1
    while step < n:
        if reverse:
            shifted, valid = pltpu.roll(x, n - step, x.ndim - 1), lane < n - step
        else:
            shifted, valid = pltpu.roll(x, step, x.ndim - 1), lane >= step
        x = jnp.maximum(x, jnp.where(valid, shifted, -jnp.inf))
        step *= 2
    return x


def _rms(t, g):
    return t * lax.rsqrt(jnp.mean(t * t, axis=-1, keepdims=True) + EPS) * g


def _proj_kernel(rope, *refs):
    if rope:
        (x_ref, mod_ref, wm_ref, wmt_ref, wgt_ref, bg_ref, wa_ref, wvt_ref, wmg_ref, qg_ref, kg_ref, cos_ref, sin_ref,
         qm_o, km_o, vmt_o, somt_o, gr_o, qa_o, ka_o, vt_o, sgm_o, sga_o) = refs
    else:
        (x_ref, mod_ref, wm_ref, wmt_ref, wgt_ref, bg_ref, wa_ref, wvt_ref, wmg_ref, qg_ref, kg_ref,
         qm_o, km_o, vmt_o, somt_o, gr_o, qa_o, ka_o, vt_o, sgm_o, sga_o, kc_o, vc_o) = refs
    tm = x_ref.shape[1]
    mod = mod_ref[0]
    h = (x_ref[0] * (1.0 + mod[1:2]) + mod[0:1]).astype(BF16)

    qm_o[0] = _dot(h, wm_ref[:, 0:D_MODEL]).astype(BF16)
    km_o[0] = (_dot(h, wm_ref[:, D_MODEL:2 * D_MODEL]) * (DH_M ** -0.5)).astype(BF16)
    for hh in range(NH_M):
        vmt = _dot_nt(wmt_ref[hh * DH_M:(hh + 1) * DH_M, :], h)
        vmt_o[0, hh, 0:DH_M, :] = vmt.astype(BF16)
        vmt_o[0, hh, DH_M:VM_ROWS, :] = jnp.ones((VM_ROWS - DH_M, tm), BF16)
    somt_o[0] = jax.nn.sigmoid(_dot_nt(wmt_ref[D_MODEL:2 * D_MODEL, :], h))

    gates = _dot_nt(wgt_ref[...], h) + bg_ref[...]
    lf = _log_sigmoid(gates)
    row = lax.broadcasted_iota(jnp.int32, (tm, tm), 0)
    col = lax.broadcasted_iota(jnp.int32, (tm, tm), 1)
    tri = jnp.where(row <= col, 1.0, 0.0).astype(BF16)
    hi = lf.astype(BF16)
    r1 = lf - hi.astype(F32)
    mid = r1.astype(BF16)
    lo = (r1 - mid.astype(F32)).astype(BF16)
    cum = (_dot(hi, tri) + _dot(mid, tri) + _dot(lo, tri))[0:8]
    lf8 = lf[0:8]
    tot = cum[:, tm - 1:tm]
    rev = tot - cum + lf8
    is_fwd = lax.broadcasted_iota(jnp.int32, (8, tm), 0) < NH_M
    a = jnp.where(is_fwd, cum, rev)
    cc = (gates[8:16] - a) * LOG2E
    a = a * LOG2E
    totb = jnp.broadcast_to(tot * LOG2E, (8, tm))
    c_pre = _cummax_lanes(cc, False)
    c_suf = _cummax_lanes(cc, True)
    for hh in range(NH_M):
        rows = (a[hh:hh + 1], cc[hh:hh + 1], totb[hh:hh + 1],
                a[NH_M + hh:NH_M + hh + 1], cc[NH_M + hh:NH_M + hh + 1], totb[NH_M + hh:NH_M + hh + 1],
                c_pre[hh:hh + 1], c_suf[NH_M + hh:NH_M + hh + 1])
        for k, r in enumerate(rows):
            gr_o[0, hh, k:k + 1, :] = r

    if rope:
        cos = cos_ref[...]
        sin_s = sin_ref[...]
        lane = lax.broadcasted_iota(jnp.int32, (tm, DH_A), 1)
        first_half = (lane % (2 * N_FREQ)) < N_FREQ

        def rot(t):
            partner = jnp.where(first_half, pltpu.roll(t, DH_A - N_FREQ, 1), pltpu.roll(t, N_FREQ, 1))
            return t * cos + partner * sin_s
    else:
        rot = lambda t: t

    qg = qg_ref[...]
    kg = kg_ref[...]
    q_all = _dot(h, wa_ref[:, 0:N_Q * DH_A])
    k_all = _dot(h, wa_ref[:, N_Q * DH_A:(N_Q + N_KV) * DH_A])
    for g in range(N_Q):
        t = _rms(q_all[:, g * DH_A:(g + 1) * DH_A], qg)
        qa_o[0, :, g * DH_A:(g + 1) * DH_A] = (rot(t) * Q_SCALE).astype(BF16)
    for g in range(N_KV):
        t = _rms(k_all[:, g * DH_A:(g + 1) * DH_A], kg)
        if not rope:
            kc_o[0, :, g * DH_A:(g + 1) * DH_A] = t
        ka_o[0, :, g * DH_A:(g + 1) * DH_A] = rot(t).astype(BF16)
    if not rope:
        off = (N_Q + N_KV) * DH_A
        vc_o[0] = _dot(h, wa_ref[:, off:off + N_KV * DH_A])
    vt = _dot_nt(wvt_ref[...], h)
    for g in range(N_KV):
        vt_o[0, g, 0:DH_A, :] = vt[g * DH_A:(g + 1) * DH_A].astype(BF16)
        vt_o[0, g, DH_A:V_ROWS, :] = jnp.ones((V_ROWS - DH_A, tm), BF16)

    sgm_o[0] = jax.nn.sigmoid(_dot(h, wmg_ref[:, 0:D_MODEL]))
    sga_o[0] = jax.nn.sigmoid(_dot(h, wmg_ref[:, D_MODEL:2 * D_MODEL]))


def _projection(x, mod, wts, rope_tables):
    B, T, _ = x.shape
    tm = TOK_TILE
    nt = T // tm
    rope = rope_tables is not None
    per_batch_mod = mod.shape[0] > 1
    tok = lambda width: pl.BlockSpec((1, tm, width), lambda b, t: (b, t, 0))
    in_specs = [tok(D_MODEL),
                pl.BlockSpec((1, 6, D_MODEL), (lambda b, t: (b, 0, 0)) if per_batch_mod else (lambda b, t: (0, 0, 0)))]
    in_specs += [_resident(w.shape) for w in wts]
    args = [x, mod, *wts]
    if rope:
        in_specs += [pl.BlockSpec((tm, DH_A), lambda b, t: (t, 0))] * 2
        args += list(rope_tables)

    kv_w = N_KV * DH_A
    outs = [((B, T, D_MODEL), BF16, tok(D_MODEL)),
            ((B, T, D_MODEL), BF16, tok(D_MODEL)),
            ((B, NH_M, VM_ROWS, T), BF16,
             pl.BlockSpec((1, NH_M, VM_ROWS, tm), lambda b, t: (b, 0, 0, t))),
            ((B, D_MODEL, T), F32, pl.BlockSpec((1, D_MODEL, tm), lambda b, t: (b, 0, t))),
            ((B, NH_M, 8, T), F32, pl.BlockSpec((1, NH_M, 8, tm), lambda b, t: (b, 0, 0, t))),
            ((B, T, D_MODEL), BF16, tok(D_MODEL)),
            ((B, T, kv_w), BF16, tok(kv_w)),
            ((B, N_KV, V_ROWS, T), BF16,
             pl.BlockSpec((1, N_KV, V_ROWS, tm), lambda b, t: (b, 0, 0, t))),
            ((B, T, D_MODEL), F32, tok(D_MODEL)),
            ((B, T, D_MODEL), F32, tok(D_MODEL))]
    if not rope:
        outs += [((B, T, kv_w), F32, tok(kv_w)), ((B, T, kv_w), F32, tok(kv_w))]

    return pl.pallas_call(
        functools.partial(_proj_kernel, rope),
        grid=(B, nt),
        in_specs=in_specs,
        out_specs=[o[2] for o in outs],
        out_shape=[jax.ShapeDtypeStruct(o[0], o[1]) for o in outs],
        compiler_params=_params(("parallel", "parallel")),
        name="projection_lat" if rope else "projection_ctx",
    )(*args)


def _mlstm_kernel(has_init, emit_state, nc, nh, *refs):
    refs = list(refs)
    q_ref, k_ref, vt_ref, g_ref, somt_ref, gn_ref = refs[:6]
    refs = refs[6:]
    if has_init:
        ct0_ref, m0_ref = refs[:2]
        refs = refs[2:]
    hm_o = refs[0]
    refs = refs[1:]
    if emit_state:
        c_o, n_o, m_o = refs[:3]
        refs = refs[3:]
    acc_ref, = refs

    L = CHUNK
    row = lax.broadcasted_iota(jnp.int32, (L, L), 0)
    col = lax.broadcasted_iota(jnp.int32, (L, L), 1)
    eye = row == col
    masks = (row <= col, row >= col)

    span = lambda c: slice(c * L, (c + 1) * L)
    feat = lambda hh: slice(hh * DH_M, (hh + 1) * DH_M)
    chunk_of = lambda d, s: s if d == 0 else nc - 1 - s
    steps = [(hh, s) for hh in range(nh) for s in range(nc)]

    def scores(hh, s):
        return [_dot_nt(k_ref[0, span(chunk_of(d, s)), feat(hh)], q_ref[0, span(chunk_of(d, s)), feat(hh)])
                for d in range(2)]

    def finish_chunk(hh, c, ht):
        ht = acc_ref[c] + ht
        hn = ht * lax.rsqrt(jnp.mean(ht * ht, axis=0, keepdims=True) + EPS) * gn_ref[feat(hh), :]
        hm_o[0, span(c), feat(hh)] = (hn * somt_ref[0, feat(hh), span(c)]).T.astype(BF16)

    st_next = scores(*steps[0])
    for idx, (hh, s) in enumerate(steps):
        if s == 0:
            state = [ct0_ref[0, d, hh] if has_init else None for d in range(2)]
            m_run = [m0_ref[0, d, hh] * LOG2E if has_init else jnp.zeros((1, 1), F32) for d in range(2)]
            arrived = [False] * nc
        st_cur = st_next
        if idx + 1 < len(steps):
            st_next = scores(*steps[idx + 1])
        inter = [None, None]
        if state[0] is not None:
            inter = [_dot_nt(state[d].astype(BF16), q_ref[0, span(chunk_of(d, s)), feat(hh)]) for d in range(2)]
        for d in range(2):
            c = chunk_of(d, s)
            k = k_ref[0, span(c), feat(hh)]
            vt = vt_ref[0, hh, :, span(c)]
            g = g_ref[0, hh, :, span(c)]
            a_row = g[3 * d:3 * d + 1]
            c_row = g[3 * d + 1:3 * d + 2]
            tot = g[3 * d + 2:3 * d + 3, 0:1]
            m_prev = m_run[d]
            c_col = jnp.sum(jnp.where(eye, c_row, 0.0), axis=-1, keepdims=True)
            c_run = g[6 + d:7 + d]

            keep_state = emit_state or s + 1 < nc
            if keep_state:
                c_max = c_run[:, L - 1:L] if d == 0 else c_run[:, 0:1]
                m_new = tot + jnp.maximum(m_prev, c_max)
                wk = (jnp.exp2(tot + c_col - m_new) * k.astype(F32)).astype(BF16)
                upd = _dot(vt, wk)
                new_state = upd if state[d] is None else jnp.exp2(tot + m_prev - m_new) * state[d] + upd

            m_rel = jnp.maximum(m_prev, c_run)
            sp = (st_cur[d] * jnp.exp2(jnp.where(masks[d], c_col, -jnp.inf) - m_rel)).astype(BF16)
            numt = _dot(vt, sp)
            if inter[d] is not None:
                numt = numt + jnp.exp2(m_prev - m_rel) * inter[d]
            den = numt[DH_M:DH_M + 1]
            ht = numt[0:DH_M] * (1.0 / jnp.maximum(jnp.abs(den), jnp.exp2(-(a_row + m_rel))))
            if arrived[c]:
                finish_chunk(hh, c, ht)
            else:
                acc_ref[c] = ht
                arrived[c] = True
            if keep_state:
                state[d] = new_state
                m_run[d] = m_new

        if emit_state and s == nc - 1:
            for d in range(2):
                c_o[0, d, hh] = state[d][0:DH_M].T
                n_o[0, d, hh] = state[d][DH_M:DH_M + 1]
                m_o[0, d, hh] = m_run[d] * (1.0 / LOG2E)


def _mlstm(qm, km, vmt, gr, somt, gnorm_col, init_state, emit_state, nh):
    B, T, _ = qm.shape
    nc = T // CHUNK
    has_init = init_state is not None

    seq = pl.BlockSpec((1, T, nh * DH_M), lambda b, h: (b, 0, h))
    c_spec = pl.BlockSpec((1, 2, nh, DH_M, DH_M), lambda b, h: (b, 0, h, 0, 0))
    n_spec = pl.BlockSpec((1, 2, nh, 1, DH_M), lambda b, h: (b, 0, h, 0, 0))
    m_spec = pl.BlockSpec((1, 2, nh, 1, 1), lambda b, h: (b, 0, h, 0, 0))

    in_specs = [seq, seq,
                pl.BlockSpec((1, nh, VM_ROWS, T), lambda b, h: (b, h, 0, 0)),
                pl.BlockSpec((1, nh, 8, T), lambda b, h: (b, h, 0, 0)),
                pl.BlockSpec((1, nh * DH_M, T), lambda b, h: (b, h, 0)),
                pl.BlockSpec((nh * DH_M, 1), lambda b, h: (h, 0))]
    args = [qm, km, vmt, gr, somt, gnorm_col]
    if has_init:
        in_specs += [pl.BlockSpec((1, 2, nh, VM_ROWS, DH_M), lambda b, h: (b, 0, h, 0, 0)), m_spec]
        args += list(init_state)
    out_specs = [seq]
    out_shape = [jax.ShapeDtypeStruct((B, T, D_MODEL), BF16)]
    if emit_state:
        out_specs += [c_spec, n_spec, m_spec]
        out_shape += [jax.ShapeDtypeStruct((B, 2, NH_M, DH_M, DH_M), F32),
                      jax.ShapeDtypeStruct((B, 2, NH_M, 1, DH_M), F32),
                      jax.ShapeDtypeStruct((B, 2, NH_M, 1, 1), F32)]

    return pl.pallas_call(
        functools.partial(_mlstm_kernel, has_init, emit_state, nc, nh),
        grid=(B, NH_M // nh),
        in_specs=in_specs,
        out_specs=out_specs,
        out_shape=out_shape,
        scratch_shapes=[pltpu.VMEM((nc, DH_M, CHUNK), F32)],
        compiler_params=_params(("parallel", "parallel")),
        name="mlstm_lat" if has_init else "mlstm_ctx",
    )(*args)


def _attn_kernel(n_lat_tiles, has_ctx, nkv, *refs):
    if has_ctx:
        q_ref, k_ref, vt_ref, kc_ref, vct_ref, o_ref = refs
    else:
        q_ref, k_ref, vt_ref, o_ref = refs
    tk = k_ref.shape[1] // n_lat_tiles
    head = lambda h: slice(h * DH_A, (h + 1) * DH_A)
    tiles = [(lambda h, i=i: k_ref[0, i * tk:(i + 1) * tk, head(h)],
              lambda h, i=i: vt_ref[0, h, :, i * tk:(i + 1) * tk]) for i in range(n_lat_tiles)]
    if has_ctx:
        tiles.append((lambda h: kc_ref[0, :, head(h)], lambda h: vct_ref[0, h]))
    groups = [(h, g, r) for h in range(nkv) for r in range(q_ref.shape[1] // Q_CHAIN) for g in range(G_Q)]
    chains = [(t, i) for t in range(len(tiles)) for i in range(len(groups))]
    rows = lambda i: slice(groups[i][2] * Q_CHAIN, (groups[i][2] + 1) * Q_CHAIN)
    cols = lambda i: head(groups[i][0] * G_Q + groups[i][1])

    def scores(t, i):
        return _dot_nt(tiles[t][0](groups[i][0]), q_ref[0, rows(i), cols(i)])

    m = [None] * len(groups)
    acc = [None] * len(groups)
    pending = []
    for idx in range(len(chains) + ATTN_LOOKAHEAD):
        if idx < len(chains):
            pending.append(scores(*chains[idx]))
        if idx < ATTN_LOOKAHEAD:
            continue
        t, i = chains[idx - ATTN_LOOKAHEAD]
        st = pending.pop(0)
        m_tile = jnp.max(st, axis=0, keepdims=True)
        m_new = m_tile if t == 0 else jnp.maximum(m[i], m_tile)
        pv = _dot(tiles[t][1](groups[i][0]), jnp.exp2(st - m_new).astype(BF16))
        acc[i] = pv if t == 0 else jnp.exp2(m[i] - m_new) * acc[i] + pv
        m[i] = m_new

    for i in range(len(groups)):
        out = acc[i][0:DH_A] * (1.0 / acc[i][DH_A:DH_A + 1])
        o_ref[0, rows(i), cols(i)] = out.T.astype(BF16)


def _attention(qa, ka, vt, ctx_kv):
    B, T, _ = qa.shape
    tq = min(Q_TILE, T)
    nq = T // tq
    has_ctx = ctx_kv is not None
    n_lat_tiles = max(1, T // K_TILE)
    nkv = 1 if has_ctx else N_KV
    qspec = pl.BlockSpec((1, tq, nkv * G_Q * DH_A), lambda b, h, i: (b, i, h))
    kspec = lambda tk: pl.BlockSpec((1, tk, nkv * DH_A), lambda b, h, i: (b, 0, h))
    vspec = lambda tk: pl.BlockSpec((1, nkv, V_ROWS, tk), lambda b, h, i: (b, h, 0, 0))
    in_specs = [qspec, kspec(T), vspec(T)]
    args = [qa, ka, vt]
    if has_ctx:
        tc = ctx_kv[0].shape[1]
        in_specs += [kspec(tc), vspec(tc)]
        args += list(ctx_kv)
    return pl.pallas_call(
        functools.partial(_attn_kernel, n_lat_tiles, has_ctx, nkv),
        grid=(B, N_KV // nkv, nq),
        in_specs=in_specs,
        out_specs=qspec,
        out_shape=jax.ShapeDtypeStruct((B, T, D_MODEL), BF16),
        compiler_params=_params(("parallel", "parallel", "parallel")),
        name="attention_lat" if has_ctx else "attention_ctx",
    )(*args)


def _layer_norm(y, g, b):
    mu = jnp.mean(y, axis=-1, keepdims=True)
    yc = y - mu
    var = jnp.mean(yc * yc, axis=-1, keepdims=True)
    return yc * lax.rsqrt(var + EPS) * g + b


def _tail_kernel(x_ref, mod_ref, hm_ref, ha_ref, sgm_ref, sga_ref, wbm_ref, wba_ref, wout_ref,
                 wup_ref, wdown_ref, ln_ref, o_ref):
    mod = mod_ref[0]
    ln = ln_ref[...]
    x = x_ref[0]
    merged = sgm_ref[0] * _dot(hm_ref[0], wbm_ref[...]) + sga_ref[0] * _dot(ha_ref[0], wba_ref[...])
    mix = _dot(merged.astype(BF16), wout_ref[...])
    x1 = _layer_norm(ALPHA * x + mod[2:3] * mix, ln[0:1], ln[1:2])
    h = (x1 * (1.0 + mod[4:5]) + mod[3:4]).astype(BF16)
    ff = jnp.zeros_like(x1)
    for j in range(D_FF // D_MODEL):
        u = jnp.maximum(_dot(h, wup_ref[:, j * D_MODEL:(j + 1) * D_MODEL]), 0.0)
        ff = ff + _dot((u * u).astype(BF16), wdown_ref[j * D_MODEL:(j + 1) * D_MODEL, :])
    o_ref[0] = _layer_norm(ALPHA * x1 + mod[5:6] * ff, ln[2:3], ln[3:4])


def _tail(x, mod, hm, ha, sgm, sga, wts, name):
    per_batch_mod = mod.shape[0] > 1
    if not per_batch_mod:
        x, hm, ha, sgm, sga = (a.reshape(1, -1, D_MODEL) for a in (x, hm, ha, sgm, sga))
    B, T, _ = x.shape
    tm = TAIL_TILE
    tok = pl.BlockSpec((1, tm, D_MODEL), lambda b, t: (b, t, 0))
    in_specs = [tok,
                pl.BlockSpec((1, 6, D_MODEL), (lambda b, t: (b, 0, 0)) if per_batch_mod else (lambda b, t: (0, 0, 0))),
                tok, tok, tok, tok] + [_resident(w.shape) for w in wts]
    return pl.pallas_call(
        _tail_kernel,
        grid=(B, T // tm),
        in_specs=in_specs,
        out_specs=tok,
        out_shape=jax.ShapeDtypeStruct((B, T, D_MODEL), F32),
        compiler_params=_params(("parallel", "parallel")),
        name=name,
    )(x, mod, hm, ha, sgm, sga, *wts)


def _rope_tables(n_tokens):
    rows = n_tokens // GRID_W
    row = jnp.repeat(jnp.arange(rows), GRID_W)
    col = jnp.tile(jnp.arange(GRID_W), rows)
    inv = ROPE_BASE ** (-jnp.arange(N_FREQ, dtype=F32) / N_FREQ)
    ang = jnp.stack([row, col], -1).astype(F32)[..., None] * inv
    ang = jnp.broadcast_to(ang[:, :, None, :], (n_tokens, 2, 2, N_FREQ))
    sign = jnp.asarray([-1.0, 1.0], F32)[None, None, :, None]
    return jnp.cos(ang).reshape(n_tokens, DH_A), (jnp.sin(ang) * sign).reshape(n_tokens, DH_A)


def kernel(x_prompt, x_sample, cache_k, cache_v, state_C, state_n, state_m, c, c_ctx, w_mod, b_mod, w_in,
           b_gates, mlstm_norm_g, q_norm_g, k_norm_g, w_bm, w_ba, w_out, ln1_g, ln1_b, w_up, w_down,
           ln2_g, ln2_b):
    B, T, _ = x_prompt.shape
    Bd, Td, _ = x_sample.shape
    l = 0

    w = w_in[l]
    o_g = 4 * D_MODEL
    o_a = o_g + 4 * NH_M
    o_mg = o_a + (N_Q + 2 * N_KV) * DH_A
    gate_rows = np.array([4, 5, 6, 7, 12, 13, 14, 15, 0, 1, 2, 3, 8, 9, 10, 11])
    proj_w = (w[:, :2 * D_MODEL].astype(BF16),
              w[:, 2 * D_MODEL:o_g].T.astype(BF16),
              w[:, o_g:o_a].T[gate_rows].astype(BF16),
              b_gates[l][gate_rows].reshape(4 * NH_M, 1),
              w[:, o_a:o_mg].astype(BF16),
              w[:, o_a + (N_Q + N_KV) * DH_A:o_mg].T.astype(BF16),
              w[:, o_mg:].astype(BF16),
              q_norm_g[l].reshape(1, DH_A),
              k_norm_g[l].reshape(1, DH_A))
    tail_w = (w_bm[l].astype(BF16), w_ba[l].astype(BF16), w_out[l].astype(BF16),
              w_up[l].astype(BF16), w_down[l].astype(BF16),
              jnp.stack([ln1_g[l], ln1_b[l], ln2_g[l], ln2_b[l]]))
    gnorm = mlstm_norm_g[l].reshape(D_MODEL, 1)

    c_rows = jnp.concatenate([c_ctx[None, :], c, jnp.zeros((MOD_ROWS - 1 - Bd, D_MODEL), F32)], axis=0)
    mod = _modulation(c_rows, w_mod[l], b_mod[l]).reshape(MOD_ROWS, 6, D_MODEL)
    mod_ctx, mod_lat = mod[0:1], mod[1:1 + Bd]

    (qm, km, vmt, somt, gr, qa, ka, vt, sgm, sga, k_new, v_new) = _projection(x_prompt, mod_ctx, proj_w, None)
    hm, c_new, n_new, m_new = _mlstm(qm, km, vmt, gr, somt, gnorm, None, True, NH_M)
    ha = _attention(qa, ka, vt, None)
    y_prompt = _tail(x_prompt, mod_ctx, hm, ha, sgm, sga, tail_w, "tail_ctx").reshape(x_prompt.shape)

    (qm, km, vmt, somt, gr, qa, ka, vt, sgm, sga) = _projection(x_sample, mod_lat, proj_w, _rope_tables(Td))
    past = cache_k.shape[2]
    n_rows = jnp.broadcast_to(state_n[:, l][:, :, :, None, :], (Bd, 2, NH_M, VM_ROWS - DH_M, DH_M))
    init = (jnp.concatenate([jnp.swapaxes(state_C[:, l], -1, -2), n_rows], axis=-2),
            state_m[:, l].reshape(Bd, 2, NH_M, 1, 1))
    hm, = _mlstm(qm, km, vmt, gr, somt, gnorm, init, False, 1)
    vct = jnp.transpose(cache_v[:, l], (0, 2, 3, 1)).astype(BF16)
    vct = jnp.concatenate([vct, jnp.ones((Bd, N_KV, V_ROWS - DH_A, past), BF16)], axis=2)
    ctx_kv = (cache_k[:, l].reshape(Bd, past, N_KV * DH_A).astype(BF16), vct)
    ha = _attention(qa, ka, vt, ctx_kv)
    y_sample = _tail(x_sample, mod_lat, hm, ha, sgm, sga, tail_w, "tail_lat")

    return (y_prompt, y_sample,
            k_new.reshape(B, 1, T, N_KV, DH_A), v_new.reshape(B, 1, T, N_KV, DH_A),
            c_new.reshape(B, 1, 2, NH_M, DH_M, DH_M), n_new.reshape(B, 1, 2, NH_M, DH_M),
            m_new.reshape(B, 1, 2, NH_M))
```

```python
import functools

import jax
import jax.numpy as jnp
import numpy as np
from jax import lax
from jax.experimental import pallas as pl
from jax.experimental.pallas import tpu as pltpu

D_MODEL = 1024
NH_M = 4
DH_M = 256
N_Q = 8
N_KV = 2
G_Q = N_Q // N_KV
DH_A = 128
D_FF = 4 * D_MODEL
GRID_W = 64
N_FREQ = DH_A // 4
ROPE_BASE = 10000.0
EPS = 1e-6
DEPTH = 1
ALPHA = (2 * DEPTH) ** 0.25

CHUNK = 256
TOK_TILE = 256
TAIL_TILE = 512
Q_TILE = 512
Q_CHAIN = 256
K_TILE = 512
LOG2E = float(np.log2(np.e))
Q_SCALE = DH_A ** -0.5 * LOG2E
ATTN_LOOKAHEAD = 4
VM_ROWS = DH_M + 16
V_ROWS = DH_A + 16
MOD_ROWS = 8

V7X_VMEM_BYTES = 64 * 1024 * 1024
VMEM_LIMIT = V7X_VMEM_BYTES - 8 * 1024 * 1024

F32 = jnp.float32
BF16 = jnp.bfloat16


def _dot(a, b):
    return jnp.dot(a, b, preferred_element_type=F32)


def _dot_nt(a, b):
    return lax.dot_general(a, b, (((1,), (1,)), ((), ())), preferred_element_type=F32)


def _resident(shape):
    nd = len(shape)
    return pl.BlockSpec(shape, lambda *_: (0,) * nd, pipeline_mode=pl.Buffered(1))


def _with_streamed_weights(body, w_hbm, w_vmem, sem):
    copies = [pltpu.make_async_copy(src, dst, sem.at[i]) for i, (src, dst) in enumerate(zip(w_hbm, w_vmem))]
    first = (pl.program_id(0) == 0) & (pl.program_id(1) == 0)

    @pl.when(first)
    def _first_step():
        for copy in copies:
            copy.start()
        body(lambda i: copies[i].wait())

    @pl.when(jnp.logical_not(first))
    def _other_steps():
        body(lambda i: None)


def _params(semantics):
    return pltpu.CompilerParams(dimension_semantics=semantics, vmem_limit_bytes=VMEM_LIMIT)


def _mod_kernel(c_ref, w_ref, b_ref, o_ref):
    c = c_ref[...]
    s = c * jax.nn.sigmoid(c)
    o_ref[...] = _dot(s.astype(BF16), w_ref[...].astype(BF16)) + b_ref[...]


def _modulation(c_rows, w_mod, b_mod):
    n_out = w_mod.shape[1]
    blk = D_MODEL
    return pl.pallas_call(
        _mod_kernel,
        grid=(n_out // blk,),
        in_specs=[pl.BlockSpec((MOD_ROWS, D_MODEL), lambda j: (0, 0)),
                  pl.BlockSpec((D_MODEL, blk), lambda j: (0, j)),
                  pl.BlockSpec((1, blk), lambda j: (0, j))],
        out_specs=pl.BlockSpec((MOD_ROWS, blk), lambda j: (0, j)),
        out_shape=jax.ShapeDtypeStruct((MOD_ROWS, n_out), F32),
        compiler_params=_params(("parallel",)),
        name="modulation",
    )(c_rows, w_mod, b_mod.reshape(1, n_out))


def _log_sigmoid(x):
    return jnp.minimum(x, 0.0) - jnp.log1p(jnp.exp(-jnp.abs(x)))


def _cummax_lanes(x, reverse):
    n = x.shape[-1]
    lane = lax.broadcasted_iota(jnp.int32, x.shape, x.ndim - 1)
    step = 1
    while step < n:
        if reverse:
            shifted, valid = pltpu.roll(x, n - step, x.ndim - 1), lane < n - step
        else:
            shifted, valid = pltpu.roll(x, step, x.ndim - 1), lane >= step
        x = jnp.maximum(x, jnp.where(valid, shifted, -jnp.inf))
        step *= 2
    return x


def _rms(t, g):
    return t * lax.rsqrt(jnp.mean(t * t, axis=-1, keepdims=True) + EPS) * g


def _proj_body(rope, acts, weights, outs, ready):
    if rope:
        x_ref, mod_ref, wgt_ref, bg_ref, qg_ref, kg_ref, cos_ref, sin_ref = acts
        qm_o, km_o, vmt_o, somt_o, gr_o, qa_o, ka_o, vt_o, sgm_o, sga_o = outs
    else:
        x_ref, mod_ref, wgt_ref, bg_ref, qg_ref, kg_ref = acts
        qm_o, km_o, vmt_o, somt_o, gr_o, qa_o, ka_o, vt_o, sgm_o, sga_o, kc_o, vc_o = outs
    wm_ref, wmt_ref, wa_ref, wvt_ref, wmg_ref = weights
    tm = x_ref.shape[1]
    mod = mod_ref[0]
    h = (x_ref[0] * (1.0 + mod[1:2]) + mod[0:1]).astype(BF16)

    ready(0)
    qm_o[0] = _dot(h, wm_ref[:, 0:D_MODEL]).astype(BF16)
    km_o[0] = (_dot(h, wm_ref[:, D_MODEL:2 * D_MODEL]) * (DH_M ** -0.5)).astype(BF16)
    ready(1)
    for hh in range(NH_M):
        vmt = _dot_nt(wmt_ref[hh * DH_M:(hh + 1) * DH_M, :], h)
        vmt_o[0, hh, 0:DH_M, :] = vmt.astype(BF16)
        vmt_o[0, hh, DH_M:VM_ROWS, :] = jnp.ones((VM_ROWS - DH_M, tm), BF16)
    somt_o[0] = jax.nn.sigmoid(_dot_nt(wmt_ref[D_MODEL:2 * D_MODEL, :], h))

    gates = _dot_nt(wgt_ref[...], h) + bg_ref[...]
    lf = _log_sigmoid(gates)
    row = lax.broadcasted_iota(jnp.int32, (tm, tm), 0)
    col = lax.broadcasted_iota(jnp.int32, (tm, tm), 1)
    tri = jnp.where(row <= col, 1.0, 0.0).astype(BF16)
    hi = lf.astype(BF16)
    r1 = lf - hi.astype(F32)
    mid = r1.astype(BF16)
    lo = (r1 - mid.astype(F32)).astype(BF16)
    cum = (_dot(hi, tri) + _dot(mid, tri) + _dot(lo, tri))[0:8]
    lf8 = lf[0:8]
    tot = cum[:, tm - 1:tm]
    rev = tot - cum + lf8
    is_fwd = lax.broadcasted_iota(jnp.int32, (8, tm), 0) < NH_M
    a = jnp.where(is_fwd, cum, rev)
    cc = (gates[8:16] - a) * LOG2E
    a = a * LOG2E
    totb = jnp.broadcast_to(tot * LOG2E, (8, tm))
    c_pre = _cummax_lanes(cc, False)
    c_suf = _cummax_lanes(cc, True)
    for hh in range(NH_M):
        rows = (a[hh:hh + 1], cc[hh:hh + 1], totb[hh:hh + 1],
                a[NH_M + hh:NH_M + hh + 1], cc[NH_M + hh:NH_M + hh + 1], totb[NH_M + hh:NH_M + hh + 1],
                c_pre[hh:hh + 1], c_suf[NH_M + hh:NH_M + hh + 1])
        for k, r in enumerate(rows):
            gr_o[0, hh, k:k + 1, :] = r

    if rope:
        cos = cos_ref[...]
        sin_s = sin_ref[...]
        lane = lax.broadcasted_iota(jnp.int32, (tm, DH_A), 1)
        first_half = (lane % (2 * N_FREQ)) < N_FREQ

        def rot(t):
            partner = jnp.where(first_half, pltpu.roll(t, DH_A - N_FREQ, 1), pltpu.roll(t, N_FREQ, 1))
            return t * cos + partner * sin_s
    else:
        rot = lambda t: t

    qg = qg_ref[...]
    kg = kg_ref[...]
    ready(2)
    q_all = _dot(h, wa_ref[:, 0:N_Q * DH_A])
    k_all = _dot(h, wa_ref[:, N_Q * DH_A:(N_Q + N_KV) * DH_A])
    for g in range(N_Q):
        t = _rms(q_all[:, g * DH_A:(g + 1) * DH_A], qg)
        qa_o[0, :, g * DH_A:(g + 1) * DH_A] = (rot(t) * Q_SCALE).astype(BF16)
    for g in range(N_KV):
        t = _rms(k_all[:, g * DH_A:(g + 1) * DH_A], kg)
        if not rope:
            kc_o[0, :, g * DH_A:(g + 1) * DH_A] = t
        ka_o[0, :, g * DH_A:(g + 1) * DH_A] = rot(t).astype(BF16)
    if not rope:
        off = (N_Q + N_KV) * DH_A
        vc_o[0] = _dot(h, wa_ref[:, off:off + N_KV * DH_A])
    ready(3)
    vt = _dot_nt(wvt_ref[...], h)
    for g in range(N_KV):
        vt_o[0, g, 0:DH_A, :] = vt[g * DH_A:(g + 1) * DH_A].astype(BF16)
        vt_o[0, g, DH_A:V_ROWS, :] = jnp.ones((V_ROWS - DH_A, tm), BF16)

    ready(4)
    sgm_o[0] = jax.nn.sigmoid(_dot(h, wmg_ref[:, 0:D_MODEL]))
    sga_o[0] = jax.nn.sigmoid(_dot(h, wmg_ref[:, D_MODEL:2 * D_MODEL]))


def _proj_kernel(rope, n_w, *refs):
    n_act = 8 if rope else 6
    n_out = 10 if rope else 12
    acts, w_hbm = refs[:n_act], refs[n_act:n_act + n_w]
    outs = refs[n_act + n_w:n_act + n_w + n_out]
    w_vmem, sem = refs[n_act + n_w + n_out:n_act + 2 * n_w + n_out], refs[n_act + 2 * n_w + n_out]
    _with_streamed_weights(lambda ready: _proj_body(rope, acts, w_vmem, outs, ready), w_hbm, w_vmem, sem)


def _projection(x, mod, small, big, rope_tables):
    B, T, _ = x.shape
    tm = TOK_TILE
    nt = T // tm
    rope = rope_tables is not None
    per_batch_mod = mod.shape[0] > 1
    tok = lambda width: pl.BlockSpec((1, tm, width), lambda b, t: (b, t, 0))
    in_specs = [tok(D_MODEL),
                pl.BlockSpec((1, 6, D_MODEL), (lambda b, t: (b, 0, 0)) if per_batch_mod else (lambda b, t: (0, 0, 0)))]
    in_specs += [_resident(w.shape) for w in small]
    args = [x, mod, *small]
    if rope:
        in_specs += [pl.BlockSpec((tm, DH_A), lambda b, t: (t, 0))] * 2
        args += list(rope_tables)
    in_specs += [pl.BlockSpec(memory_space=pl.ANY)] * len(big)
    args += list(big)

    kv_w = N_KV * DH_A
    outs = [((B, T, D_MODEL), BF16, tok(D_MODEL)),
            ((B, T, D_MODEL), BF16, tok(D_MODEL)),
            ((B, NH_M, VM_ROWS, T), BF16,
             pl.BlockSpec((1, NH_M, VM_ROWS, tm), lambda b, t: (b, 0, 0, t))),
            ((B, D_MODEL, T), F32, pl.BlockSpec((1, D_MODEL, tm), lambda b, t: (b, 0, t))),
            ((B, NH_M, 8, T), F32, pl.BlockSpec((1, NH_M, 8, tm), lambda b, t: (b, 0, 0, t))),
            ((B, T, D_MODEL), BF16, tok(D_MODEL)),
            ((B, T, kv_w), BF16, tok(kv_w)),
            ((B, N_KV, V_ROWS, T), BF16,
             pl.BlockSpec((1, N_KV, V_ROWS, tm), lambda b, t: (b, 0, 0, t))),
            ((B, T, D_MODEL), F32, tok(D_MODEL)),
            ((B, T, D_MODEL), F32, tok(D_MODEL))]
    if not rope:
        outs += [((B, T, kv_w), F32, tok(kv_w)), ((B, T, kv_w), F32, tok(kv_w))]

    return pl.pallas_call(
        functools.partial(_proj_kernel, rope, len(big)),
        grid=(B, nt),
        in_specs=in_specs,
        out_specs=[o[2] for o in outs],
        out_shape=[jax.ShapeDtypeStruct(o[0], o[1]) for o in outs],
        scratch_shapes=[pltpu.VMEM(w.shape, w.dtype) for w in big] + [pltpu.SemaphoreType.DMA((len(big),))],
        compiler_params=_params(("arbitrary", "arbitrary")),
        name="projection_lat" if rope else "projection_ctx",
    )(*args)


def _mlstm_kernel(has_init, emit_state, nc, nh, *refs):
    refs = list(refs)
    q_ref, k_ref, vt_ref, g_ref, somt_ref, gn_ref = refs[:6]
    refs = refs[6:]
    if has_init:
        ct0_ref, m0_ref = refs[:2]
        refs = refs[2:]
    hm_o = refs[0]
    refs = refs[1:]
    if emit_state:
        c_o, n_o, m_o = refs[:3]
        refs = refs[3:]
    acc_ref, = refs

    L = CHUNK
    row = lax.broadcasted_iota(jnp.int32, (L, L), 0)
    col = lax.broadcasted_iota(jnp.int32, (L, L), 1)
    eye = row == col
    masks = (row <= col, row >= col)

    span = lambda c: slice(c * L, (c + 1) * L)
    feat = lambda hh: slice(hh * DH_M, (hh + 1) * DH_M)
    chunk_of = lambda d, s: s if d == 0 else nc - 1 - s
    steps = [(hh, s) for hh in range(nh) for s in range(nc)]

    def scores(hh, s):
        return [_dot_nt(k_ref[0, span(chunk_of(d, s)), feat(hh)], q_ref[0, span(chunk_of(d, s)), feat(hh)])
                for d in range(2)]

    def finish_chunk(hh, c, ht):
        ht = acc_ref[c] + ht
        hn = ht * lax.rsqrt(jnp.mean(ht * ht, axis=0, keepdims=True) + EPS) * gn_ref[feat(hh), :]
        hm_o[0, span(c), feat(hh)] = (hn * somt_ref[0, feat(hh), span(c)]).T.astype(BF16)

    st_next = scores(*steps[0])
    for idx, (hh, s) in enumerate(steps):
        if s == 0:
            state = [ct0_ref[0, d, hh] if has_init else None for d in range(2)]
            m_run = [m0_ref[0, d, hh] * LOG2E if has_init else jnp.zeros((1, 1), F32) for d in range(2)]
            arrived = [False] * nc
        st_cur = st_next
        if idx + 1 < len(steps):
            st_next = scores(*steps[idx + 1])
        inter = [None, None]
        if state[0] is not None:
            inter = [_dot_nt(state[d].astype(BF16), q_ref[0, span(chunk_of(d, s)), feat(hh)]) for d in range(2)]
        for d in range(2):
            c = chunk_of(d, s)
            k = k_ref[0, span(c), feat(hh)]
            vt = vt_ref[0, hh, :, span(c)]
            g = g_ref[0, hh, :, span(c)]
            a_row = g[3 * d:3 * d + 1]
            c_row = g[3 * d + 1:3 * d + 2]
            tot = g[3 * d + 2:3 * d + 3, 0:1]
            m_prev = m_run[d]
            c_col = jnp.sum(jnp.where(eye, c_row, 0.0), axis=-1, keepdims=True)
            c_run = g[6 + d:7 + d]

            keep_state = emit_state or s + 1 < nc
            if keep_state:
                c_max = c_run[:, L - 1:L] if d == 0 else c_run[:, 0:1]
                m_new = tot + jnp.maximum(m_prev, c_max)
                wk = (jnp.exp2(tot + c_col - m_new) * k.astype(F32)).astype(BF16)
                upd = _dot(vt, wk)
                new_state = upd if state[d] is None else jnp.exp2(tot + m_prev - m_new) * state[d] + upd

            m_rel = jnp.maximum(m_prev, c_run)
            sp = (st_cur[d] * jnp.exp2(jnp.where(masks[d], c_col, -jnp.inf) - m_rel)).astype(BF16)
            numt = _dot(vt, sp)
            if inter[d] is not None:
                numt = numt + jnp.exp2(m_prev - m_rel) * inter[d]
            den = numt[DH_M:DH_M + 1]
            ht = numt[0:DH_M] * (1.0 / jnp.maximum(jnp.abs(den), jnp.exp2(-(a_row + m_rel))))
            if arrived[c]:
                finish_chunk(hh, c, ht)
            else:
                acc_ref[c] = ht
                arrived[c] = True
            if keep_state:
                state[d] = new_state
                m_run[d] = m_new

        if emit_state and s == nc - 1:
            for d in range(2):
                c_o[0, d, hh] = state[d][0:DH_M].T
                n_o[0, d, hh] = state[d][DH_M:DH_M + 1]
                m_o[0, d, hh] = m_run[d] * (1.0 / LOG2E)


def _mlstm(qm, km, vmt, gr, somt, gnorm_col, init_state, emit_state, nh):
    B, T, _ = qm.shape
    nc = T // CHUNK
    has_init = init_state is not None

    seq = pl.BlockSpec((1, T, nh * DH_M), lambda b, h: (b, 0, h))
    c_spec = pl.BlockSpec((1, 2, nh, DH_M, DH_M), lambda b, h: (b, 0, h, 0, 0))
    n_spec = pl.BlockSpec((1, 2, nh, 1, DH_M), lambda b, h: (b, 0, h, 0, 0))
    m_spec = pl.BlockSpec((1, 2, nh, 1, 1), lambda b, h: (b, 0, h, 0, 0))

    in_specs = [seq, seq,
                pl.BlockSpec((1, nh, VM_ROWS, T), lambda b, h: (b, h, 0, 0)),
                pl.BlockSpec((1, nh, 8, T), lambda b, h: (b, h, 0, 0)),
                pl.BlockSpec((1, nh * DH_M, T), lambda b, h: (b, h, 0)),
                pl.BlockSpec((nh * DH_M, 1), lambda b, h: (h, 0))]
    args = [qm, km, vmt, gr, somt, gnorm_col]
    if has_init:
        in_specs += [pl.BlockSpec((1, 2, nh, VM_ROWS, DH_M), lambda b, h: (b, 0, h, 0, 0)), m_spec]
        args += list(init_state)
    out_specs = [seq]
    out_shape = [jax.ShapeDtypeStruct((B, T, D_MODEL), BF16)]
    if emit_state:
        out_specs += [c_spec, n_spec, m_spec]
        out_shape += [jax.ShapeDtypeStruct((B, 2, NH_M, DH_M, DH_M), F32),
                      jax.ShapeDtypeStruct((B, 2, NH_M, 1, DH_M), F32),
                      jax.ShapeDtypeStruct((B, 2, NH_M, 1, 1), F32)]

    return pl.pallas_call(
        functools.partial(_mlstm_kernel, has_init, emit_state, nc, nh),
        grid=(B, NH_M // nh),
        in_specs=in_specs,
        out_specs=out_specs,
        out_shape=out_shape,
        scratch_shapes=[pltpu.VMEM((nc, DH_M, CHUNK), F32)],
        compiler_params=_params(("parallel", "parallel")),
        name="mlstm_lat" if has_init else "mlstm_ctx",
    )(*args)


def _attn_kernel(n_lat_tiles, has_ctx, nkv, *refs):
    if has_ctx:
        q_ref, k_ref, vt_ref, kc_ref, vct_ref, o_ref = refs
    else:
        q_ref, k_ref, vt_ref, o_ref = refs
    tk = k_ref.shape[1] // n_lat_tiles
    head = lambda h: slice(h * DH_A, (h + 1) * DH_A)
    tiles = [(lambda h, i=i: k_ref[0, i * tk:(i + 1) * tk, head(h)],
              lambda h, i=i: vt_ref[0, h, :, i * tk:(i + 1) * tk]) for i in range(n_lat_tiles)]
    if has_ctx:
        tiles.append((lambda h: kc_ref[0, :, head(h)], lambda h: vct_ref[0, h]))
    groups = [(h, g, r) for h in range(nkv) for r in range(q_ref.shape[1] // Q_CHAIN) for g in range(G_Q)]
    chains = [(t, i) for t in range(len(tiles)) for i in range(len(groups))]
    rows = lambda i: slice(groups[i][2] * Q_CHAIN, (groups[i][2] + 1) * Q_CHAIN)
    cols = lambda i: head(groups[i][0] * G_Q + groups[i][1])

    def scores(t, i):
        return _dot_nt(tiles[t][0](groups[i][0]), q_ref[0, rows(i), cols(i)])

    m = [None] * len(groups)
    acc = [None] * len(groups)
    pending = []
    for idx in range(len(chains) + ATTN_LOOKAHEAD):
        if idx < len(chains):
            pending.append(scores(*chains[idx]))
        if idx < ATTN_LOOKAHEAD:
            continue
        t, i = chains[idx - ATTN_LOOKAHEAD]
        st = pending.pop(0)
        m_tile = jnp.max(st, axis=0, keepdims=True)
        m_new = m_tile if t == 0 else jnp.maximum(m[i], m_tile)
        pv = _dot(tiles[t][1](groups[i][0]), jnp.exp2(st - m_new).astype(BF16))
        acc[i] = pv if t == 0 else jnp.exp2(m[i] - m_new) * acc[i] + pv
        m[i] = m_new

    for i in range(len(groups)):
        out = acc[i][0:DH_A] * (1.0 / acc[i][DH_A:DH_A + 1])
        o_ref[0, rows(i), cols(i)] = out.T.astype(BF16)


def _attention(qa, ka, vt, ctx_kv):
    B, T, _ = qa.shape
    tq = min(Q_TILE, T)
    nq = T // tq
    has_ctx = ctx_kv is not None
    n_lat_tiles = max(1, T // K_TILE)
    nkv = 1 if has_ctx else N_KV
    qspec = pl.BlockSpec((1, tq, nkv * G_Q * DH_A), lambda b, h, i: (b, i, h))
    kspec = lambda tk: pl.BlockSpec((1, tk, nkv * DH_A), lambda b, h, i: (b, 0, h))
    vspec = lambda tk: pl.BlockSpec((1, nkv, V_ROWS, tk), lambda b, h, i: (b, h, 0, 0))
    in_specs = [qspec, kspec(T), vspec(T)]
    args = [qa, ka, vt]
    if has_ctx:
        tc = ctx_kv[0].shape[1]
        in_specs += [kspec(tc), vspec(tc)]
        args += list(ctx_kv)
    return pl.pallas_call(
        functools.partial(_attn_kernel, n_lat_tiles, has_ctx, nkv),
        grid=(B, N_KV // nkv, nq),
        in_specs=in_specs,
        out_specs=qspec,
        out_shape=jax.ShapeDtypeStruct((B, T, D_MODEL), BF16),
        compiler_params=_params(("parallel", "parallel", "parallel")),
        name="attention_lat" if has_ctx else "attention_ctx",
    )(*args)


def _layer_norm(y, g, b):
    mu = jnp.mean(y, axis=-1, keepdims=True)
    yc = y - mu
    var = jnp.mean(yc * yc, axis=-1, keepdims=True)
    return yc * lax.rsqrt(var + EPS) * g + b


def _tail_body(x_ref, mod_ref, hm_ref, ha_ref, sgm_ref, sga_ref, ln_ref, o_ref, weights, ready):
    wbm_ref, wba_ref, wout_ref, wup_ref, wdown_ref = weights
    mod = mod_ref[0]
    ln = ln_ref[...]
    x = x_ref[0]
    ready(0)
    ready(1)
    merged = sgm_ref[0] * _dot(hm_ref[0], wbm_ref[...]) + sga_ref[0] * _dot(ha_ref[0], wba_ref[...])
    ready(2)
    mix = _dot(merged.astype(BF16), wout_ref[...])
    x1 = _layer_norm(ALPHA * x + mod[2:3] * mix, ln[0:1], ln[1:2])
    h = (x1 * (1.0 + mod[4:5]) + mod[3:4]).astype(BF16)
    ready(3)
    ready(4)
    ff = jnp.zeros_like(x1)
    for j in range(D_FF // D_MODEL):
        u = jnp.maximum(_dot(h, wup_ref[:, j * D_MODEL:(j + 1) * D_MODEL]), 0.0)
        ff = ff + _dot((u * u).astype(BF16), wdown_ref[j * D_MODEL:(j + 1) * D_MODEL, :])
    o_ref[0] = _layer_norm(ALPHA * x1 + mod[5:6] * ff, ln[2:3], ln[3:4])


def _tail_kernel(n_w, *refs):
    acts, w_hbm, o_ref = refs[:7], refs[7:7 + n_w], refs[7 + n_w]
    w_vmem, sem = refs[8 + n_w:8 + 2 * n_w], refs[8 + 2 * n_w]
    _with_streamed_weights(lambda ready: _tail_body(*acts, o_ref, w_vmem, ready), w_hbm, w_vmem, sem)


def _tail(x, mod, hm, ha, sgm, sga, wts, ln, name):
    per_batch_mod = mod.shape[0] > 1
    if not per_batch_mod:
        x, hm, ha, sgm, sga = (a.reshape(1, -1, D_MODEL) for a in (x, hm, ha, sgm, sga))
    B, T, _ = x.shape
    tm = TAIL_TILE
    tok = pl.BlockSpec((1, tm, D_MODEL), lambda b, t: (b, t, 0))
    in_specs = [tok,
                pl.BlockSpec((1, 6, D_MODEL), (lambda b, t: (b, 0, 0)) if per_batch_mod else (lambda b, t: (0, 0, 0))),
                tok, tok, tok, tok, _resident(ln.shape)] + [pl.BlockSpec(memory_space=pl.ANY)] * len(wts)
    return pl.pallas_call(
        functools.partial(_tail_kernel, len(wts)),
        grid=(B, T // tm),
        in_specs=in_specs,
        out_specs=tok,
        out_shape=jax.ShapeDtypeStruct((B, T, D_MODEL), F32),
        scratch_shapes=[pltpu.VMEM(w.shape, w.dtype) for w in wts] + [pltpu.SemaphoreType.DMA((len(wts),))],
        compiler_params=_params(("arbitrary", "arbitrary")),
        name=name,
    )(x, mod, hm, ha, sgm, sga, ln, *wts)


def _rope_tables(n_tokens):
    rows = n_tokens // GRID_W
    row = jnp.repeat(jnp.arange(rows), GRID_W)
    col = jnp.tile(jnp.arange(GRID_W), rows)
    inv = ROPE_BASE ** (-jnp.arange(N_FREQ, dtype=F32) / N_FREQ)
    ang = jnp.stack([row, col], -1).astype(F32)[..., None] * inv
    ang = jnp.broadcast_to(ang[:, :, None, :], (n_tokens, 2, 2, N_FREQ))
    sign = jnp.asarray([-1.0, 1.0], F32)[None, None, :, None]
    return jnp.cos(ang).reshape(n_tokens, DH_A), (jnp.sin(ang) * sign).reshape(n_tokens, DH_A)


def kernel(x_prompt, x_sample, cache_k, cache_v, state_C, state_n, state_m, c, c_ctx, w_mod, b_mod, w_in,
           b_gates, mlstm_norm_g, q_norm_g, k_norm_g, w_bm, w_ba, w_out, ln1_g, ln1_b, w_up, w_down,
           ln2_g, ln2_b):
    B, T, _ = x_prompt.shape
    Bd, Td, _ = x_sample.shape
    l = 0

    w = w_in[l]
    o_g = 4 * D_MODEL
    o_a = o_g + 4 * NH_M
    o_mg = o_a + (N_Q + 2 * N_KV) * DH_A
    gate_rows = np.array([4, 5, 6, 7, 12, 13, 14, 15, 0, 1, 2, 3, 8, 9, 10, 11])
    proj_small = (w[:, o_g:o_a].T[gate_rows].astype(BF16),
                  b_gates[l][gate_rows].reshape(4 * NH_M, 1),
                  q_norm_g[l].reshape(1, DH_A),
                  k_norm_g[l].reshape(1, DH_A))
    proj_big = (w[:, :2 * D_MODEL].astype(BF16),
                w[:, 2 * D_MODEL:o_g].T.astype(BF16),
                w[:, o_a:o_mg].astype(BF16),
                w[:, o_a + (N_Q + N_KV) * DH_A:o_mg].T.astype(BF16),
                w[:, o_mg:].astype(BF16))
    tail_w = (w_bm[l].astype(BF16), w_ba[l].astype(BF16), w_out[l].astype(BF16),
              w_up[l].astype(BF16), w_down[l].astype(BF16))
    ln = jnp.stack([ln1_g[l], ln1_b[l], ln2_g[l], ln2_b[l]])
    gnorm = mlstm_norm_g[l].reshape(D_MODEL, 1)

    c_rows = jnp.concatenate([c_ctx[None, :], c, jnp.zeros((MOD_ROWS - 1 - Bd, D_MODEL), F32)], axis=0)
    mod = _modulation(c_rows, w_mod[l], b_mod[l]).reshape(MOD_ROWS, 6, D_MODEL)
    mod_ctx, mod_lat = mod[0:1], mod[1:1 + Bd]

    (qm, km, vmt, somt, gr, qa, ka, vt, sgm, sga, k_new, v_new) = _projection(x_prompt, mod_ctx, proj_small, proj_big, None)
    hm, c_new, n_new, m_new = _mlstm(qm, km, vmt, gr, somt, gnorm, None, True, NH_M)
    ha = _attention(qa, ka, vt, None)
    y_prompt = _tail(x_prompt, mod_ctx, hm, ha, sgm, sga, tail_w, ln, "tail_ctx").reshape(x_prompt.shape)

    (qm, km, vmt, somt, gr, qa, ka, vt, sgm, sga) = _projection(x_sample, mod_lat, proj_small, proj_big, _rope_tables(Td))
    past = cache_k.shape[2]
    n_rows = jnp.broadcast_to(state_n[:, l][:, :, :, None, :], (Bd, 2, NH_M, VM_ROWS - DH_M, DH_M))
    init = (jnp.concatenate([jnp.swapaxes(state_C[:, l], -1, -2), n_rows], axis=-2),
            state_m[:, l].reshape(Bd, 2, NH_M, 1, 1))
    hm, = _mlstm(qm, km, vmt, gr, somt, gnorm, init, False, 1)
    vct = jnp.transpose(cache_v[:, l], (0, 2, 3, 1)).astype(BF16)
    vct = jnp.concatenate([vct, jnp.ones((Bd, N_KV, V_ROWS - DH_A, past), BF16)], axis=2)
    ctx_kv = (cache_k[:, l].reshape(Bd, past, N_KV * DH_A).astype(BF16), vct)
    ha = _attention(qa, ka, vt, ctx_kv)
    y_sample = _tail(x_sample, mod_lat, hm, ha, sgm, sga, tail_w, ln, "tail_lat")

    return (y_prompt, y_sample,
            k_new.reshape(B, 1, T, N_KV, DH_A), v_new.reshape(B, 1, T, N_KV, DH_A),
            c_new.reshape(B, 1, 2, NH_M, DH_M, DH_M), n_new.reshape(B, 1, 2, NH_M, DH_M),
            m_new.reshape(B, 1, 2, NH_M))
```

```python
import functools

import jax
import jax.numpy as jnp
import numpy as np
from jax import lax
from jax.experimental import pallas as pl
from jax.experimental.pallas import tpu as pltpu

D_MODEL = 1024
NH_M = 4
DH_M = 256
N_Q = 8
N_KV = 2
G_Q = N_Q // N_KV
DH_A = 128
D_FF = 4 * D_MODEL
GRID_W = 64
N_FREQ = DH_A // 4
ROPE_BASE = 10000.0
EPS = 1e-6
DEPTH = 1
ALPHA = (2 * DEPTH) ** 0.25

CHUNK = 256
TOK_TILE = 256
TAIL_TILE = 512
Q_TILE = 512
Q_CHAIN = 256
K_TILE = 512
LOG2E = float(np.log2(np.e))
Q_SCALE = DH_A ** -0.5 * LOG2E
ATTN_LOOKAHEAD = 4
VM_ROWS = DH_M + 16
V_ROWS = DH_A + 16
MOD_ROWS = 8

V7X_VMEM_BYTES = 64 * 1024 * 1024
VMEM_LIMIT = V7X_VMEM_BYTES - 8 * 1024 * 1024

F32 = jnp.float32
BF16 = jnp.bfloat16


def _dot(a, b):
    return jnp.dot(a, b, preferred_element_type=F32)


def _dot_nt(a, b):
    return lax.dot_general(a, b, (((1,), (1,)), ((), ())), preferred_element_type=F32)


def _resident(shape):
    nd = len(shape)
    return pl.BlockSpec(shape, lambda *_: (0,) * nd, pipeline_mode=pl.Buffered(1))


def _with_streamed_weights(body, w_hbm, w_vmem, sem):
    copies = [pltpu.make_async_copy(src, dst, sem.at[i]) for i, (src, dst) in enumerate(zip(w_hbm, w_vmem))]
    first = (pl.program_id(0) == 0) & (pl.program_id(1) == 0)

    @pl.when(first)
    def _first_step():
        for copy in copies:
            copy.start()
        body(lambda i: copies[i].wait())

    @pl.when(jnp.logical_not(first))
    def _other_steps():
        body(lambda i: None)


def _params(semantics):
    return pltpu.CompilerParams(dimension_semantics=semantics, vmem_limit_bytes=VMEM_LIMIT)


def _mod_kernel(c_ref, w_ref, b_ref, o_ref):
    c = c_ref[...]
    s = c * jax.nn.sigmoid(c)
    o_ref[...] = _dot(s.astype(BF16), w_ref[...].astype(BF16)) + b_ref[...]


def _modulation(c_rows, w_mod, b_mod):
    n_out = w_mod.shape[1]
    blk = D_MODEL
    return pl.pallas_call(
        _mod_kernel,
        grid=(n_out // blk,),
        in_specs=[pl.BlockSpec((MOD_ROWS, D_MODEL), lambda j: (0, 0)),
                  pl.BlockSpec((D_MODEL, blk), lambda j: (0, j)),
                  pl.BlockSpec((1, blk), lambda j: (0, j))],
        out_specs=pl.BlockSpec((MOD_ROWS, blk), lambda j: (0, j)),
        out_shape=jax.ShapeDtypeStruct((MOD_ROWS, n_out), F32),
        compiler_params=_params(("parallel",)),
        name="modulation",
    )(c_rows, w_mod, b_mod.reshape(1, n_out))


def _log_sigmoid(x):
    return jnp.minimum(x, 0.0) - jnp.log1p(jnp.exp(-jnp.abs(x)))


def _cummax_lanes(x, reverse):
    n = x.shape[-1]
    lane = lax.broadcasted_iota(jnp.int32, x.shape, x.ndim - 1)
    step = 1
    while step < n:
        if reverse:
            shifted, valid = pltpu.roll(x, n - step, x.ndim - 1), lane < n - step
        else:
            shifted, valid = pltpu.roll(x, step, x.ndim - 1), lane >= step
        x = jnp.maximum(x, jnp.where(valid, shifted, -jnp.inf))
        step *= 2
    return x


def _rms(t, g):
    return t * lax.rsqrt(jnp.mean(t * t, axis=-1, keepdims=True) + EPS) * g


def _proj_body(rope, acts, weights, outs, ready):
    if rope:
        x_ref, mod_ref, wgt_ref, bg_ref, qg_ref, kg_ref, cos_ref, sin_ref = acts
        qm_o, km_o, vmt_o, somt_o, gr_o, qa_o, ka_o, vt_o, sgm_o, sga_o = outs
    else:
        x_ref, mod_ref, wgt_ref, bg_ref, qg_ref, kg_ref = acts
        qm_o, km_o, vmt_o, somt_o, gr_o, qa_o, ka_o, vt_o, sgm_o, sga_o, kc_o, vc_o = outs
    wm_ref, wmt_ref, wa_ref, wvt_ref, wmg_ref = weights
    tm = x_ref.shape[1]
    mod = mod_ref[0]
    h = (x_ref[0] * (1.0 + mod[1:2]) + mod[0:1]).astype(BF16)

    ready(0)
    qm_o[0] = _dot(h, wm_ref[:, 0:D_MODEL]).astype(BF16)
    km_o[0] = (_dot(h, wm_ref[:, D_MODEL:2 * D_MODEL]) * (DH_M ** -0.5)).astype(BF16)
    ready(1)
    for hh in range(NH_M):
        vmt = _dot_nt(wmt_ref[hh * DH_M:(hh + 1) * DH_M, :], h)
        vmt_o[0, hh, 0:DH_M, :] = vmt.astype(BF16)
        vmt_o[0, hh, DH_M:VM_ROWS, :] = jnp.ones((VM_ROWS - DH_M, tm), BF16)
    somt_o[0] = jax.nn.sigmoid(_dot_nt(wmt_ref[D_MODEL:2 * D_MODEL, :], h)).astype(BF16)

    gates = _dot_nt(wgt_ref[...], h) + bg_ref[...]
    lf = _log_sigmoid(gates)
    row = lax.broadcasted_iota(jnp.int32, (tm, tm), 0)
    col = lax.broadcasted_iota(jnp.int32, (tm, tm), 1)
    tri = jnp.where(row <= col, 1.0, 0.0).astype(BF16)
    hi = lf.astype(BF16)
    r1 = lf - hi.astype(F32)
    mid = r1.astype(BF16)
    lo = (r1 - mid.astype(F32)).astype(BF16)
    cum = (_dot(hi, tri) + _dot(mid, tri) + _dot(lo, tri))[0:8]
    lf8 = lf[0:8]
    tot = cum[:, tm - 1:tm]
    rev = tot - cum + lf8
    is_fwd = lax.broadcasted_iota(jnp.int32, (8, tm), 0) < NH_M
    a = jnp.where(is_fwd, cum, rev)
    cc = (gates[8:16] - a) * LOG2E
    a = a * LOG2E
    totb = jnp.broadcast_to(tot * LOG2E, (8, tm))
    c_pre = _cummax_lanes(cc, False)
    c_suf = _cummax_lanes(cc, True)
    for hh in range(NH_M):
        rows = (a[hh:hh + 1], cc[hh:hh + 1], totb[hh:hh + 1],
                a[NH_M + hh:NH_M + hh + 1], cc[NH_M + hh:NH_M + hh + 1], totb[NH_M + hh:NH_M + hh + 1],
                c_pre[hh:hh + 1], c_suf[NH_M + hh:NH_M + hh + 1])
        for k, r in enumerate(rows):
            gr_o[0, hh, k:k + 1, :] = r

    if rope:
        cos = cos_ref[...]
        sin_s = sin_ref[...]
        lane = lax.broadcasted_iota(jnp.int32, (tm, DH_A), 1)
        first_half = (lane % (2 * N_FREQ)) < N_FREQ

        def rot(t):
            partner = jnp.where(first_half, pltpu.roll(t, DH_A - N_FREQ, 1), pltpu.roll(t, N_FREQ, 1))
            return t * cos + partner * sin_s
    else:
        rot = lambda t: t

    qg = qg_ref[...]
    kg = kg_ref[...]
    ready(2)
    q_all = _dot(h, wa_ref[:, 0:N_Q * DH_A])
    k_all = _dot(h, wa_ref[:, N_Q * DH_A:(N_Q + N_KV) * DH_A])
    for g in range(N_Q):
        t = _rms(q_all[:, g * DH_A:(g + 1) * DH_A], qg)
        qa_o[0, :, g * DH_A:(g + 1) * DH_A] = (rot(t) * Q_SCALE).astype(BF16)
    for g in range(N_KV):
        t = _rms(k_all[:, g * DH_A:(g + 1) * DH_A], kg)
        if not rope:
            kc_o[0, :, g * DH_A:(g + 1) * DH_A] = t
        ka_o[0, :, g * DH_A:(g + 1) * DH_A] = rot(t).astype(BF16)
    if not rope:
        off = (N_Q + N_KV) * DH_A
        vc_o[0] = _dot(h, wa_ref[:, off:off + N_KV * DH_A])
    ready(3)
    vt = _dot_nt(wvt_ref[...], h)
    for g in range(N_KV):
        vt_o[0, g, 0:DH_A, :] = vt[g * DH_A:(g + 1) * DH_A].astype(BF16)
        vt_o[0, g, DH_A:V_ROWS, :] = jnp.ones((V_ROWS - DH_A, tm), BF16)

    ready(4)
    sgm_o[0] = jax.nn.sigmoid(_dot(h, wmg_ref[:, 0:D_MODEL])).astype(BF16)
    sga_o[0] = jax.nn.sigmoid(_dot(h, wmg_ref[:, D_MODEL:2 * D_MODEL])).astype(BF16)


def _proj_kernel(rope, n_w, *refs):
    n_act = 8 if rope else 6
    n_out = 10 if rope else 12
    acts, w_hbm = refs[:n_act], refs[n_act:n_act + n_w]
    outs = refs[n_act + n_w:n_act + n_w + n_out]
    w_vmem, sem = refs[n_act + n_w + n_out:n_act + 2 * n_w + n_out], refs[n_act + 2 * n_w + n_out]
    _with_streamed_weights(lambda ready: _proj_body(rope, acts, w_vmem, outs, ready), w_hbm, w_vmem, sem)


def _projection(x, mod, small, big, rope_tables):
    B, T, _ = x.shape
    tm = TOK_TILE
    nt = T // tm
    rope = rope_tables is not None
    per_batch_mod = mod.shape[0] > 1
    tok = lambda width: pl.BlockSpec((1, tm, width), lambda b, t: (b, t, 0))
    in_specs = [tok(D_MODEL),
                pl.BlockSpec((1, 6, D_MODEL), (lambda b, t: (b, 0, 0)) if per_batch_mod else (lambda b, t: (0, 0, 0)))]
    in_specs += [_resident(w.shape) for w in small]
    args = [x, mod, *small]
    if rope:
        in_specs += [pl.BlockSpec((tm, DH_A), lambda b, t: (t, 0))] * 2
        args += list(rope_tables)
    in_specs += [pl.BlockSpec(memory_space=pl.ANY)] * len(big)
    args += list(big)

    kv_w = N_KV * DH_A
    outs = [((B, T, D_MODEL), BF16, tok(D_MODEL)),
            ((B, T, D_MODEL), BF16, tok(D_MODEL)),
            ((B, NH_M, VM_ROWS, T), BF16,
             pl.BlockSpec((1, NH_M, VM_ROWS, tm), lambda b, t: (b, 0, 0, t))),
            ((B, D_MODEL, T), BF16, pl.BlockSpec((1, D_MODEL, tm), lambda b, t: (b, 0, t))),
            ((B, NH_M, 8, T), F32, pl.BlockSpec((1, NH_M, 8, tm), lambda b, t: (b, 0, 0, t))),
            ((B, T, D_MODEL), BF16, tok(D_MODEL)),
            ((B, T, kv_w), BF16, tok(kv_w)),
            ((B, N_KV, V_ROWS, T), BF16,
             pl.BlockSpec((1, N_KV, V_ROWS, tm), lambda b, t: (b, 0, 0, t))),
            ((B, T, D_MODEL), BF16, tok(D_MODEL)),
            ((B, T, D_MODEL), BF16, tok(D_MODEL))]
    if not rope:
        outs += [((B, T, kv_w), F32, tok(kv_w)), ((B, T, kv_w), F32, tok(kv_w))]

    return pl.pallas_call(
        functools.partial(_proj_kernel, rope, len(big)),
        grid=(B, nt),
        in_specs=in_specs,
        out_specs=[o[2] for o in outs],
        out_shape=[jax.ShapeDtypeStruct(o[0], o[1]) for o in outs],
        scratch_shapes=[pltpu.VMEM(w.shape, w.dtype) for w in big] + [pltpu.SemaphoreType.DMA((len(big),))],
        compiler_params=_params(("arbitrary", "arbitrary")),
        name="projection_lat" if rope else "projection_ctx",
    )(*args)


def _mlstm_kernel(has_init, emit_state, nc, nh, *refs):
    refs = list(refs)
    q_ref, k_ref, vt_ref, g_ref, somt_ref, gn_ref = refs[:6]
    refs = refs[6:]
    if has_init:
        ct0_ref, m0_ref = refs[:2]
        refs = refs[2:]
    hm_o = refs[0]
    refs = refs[1:]
    if emit_state:
        c_o, n_o, m_o = refs[:3]
        refs = refs[3:]
    acc_ref, = refs

    L = CHUNK
    row = lax.broadcasted_iota(jnp.int32, (L, L), 0)
    col = lax.broadcasted_iota(jnp.int32, (L, L), 1)
    eye = row == col
    masks = (row <= col, row >= col)

    span = lambda c: slice(c * L, (c + 1) * L)
    feat = lambda hh: slice(hh * DH_M, (hh + 1) * DH_M)
    chunk_of = lambda d, s: s if d == 0 else nc - 1 - s
    steps = [(hh, s) for hh in range(nh) for s in range(nc)]

    def scores(hh, s):
        return [_dot_nt(k_ref[0, span(chunk_of(d, s)), feat(hh)], q_ref[0, span(chunk_of(d, s)), feat(hh)])
                for d in range(2)]

    def finish_chunk(hh, c, ht):
        ht = acc_ref[c] + ht
        hn = ht * lax.rsqrt(jnp.mean(ht * ht, axis=0, keepdims=True) + EPS) * gn_ref[feat(hh), :]
        hm_o[0, span(c), feat(hh)] = (hn * somt_ref[0, feat(hh), span(c)]).T.astype(BF16)

    st_next = scores(*steps[0])
    for idx, (hh, s) in enumerate(steps):
        if s == 0:
            state = [ct0_ref[0, d, hh] if has_init else None for d in range(2)]
            m_run = [m0_ref[0, d, hh] * LOG2E if has_init else jnp.zeros((1, 1), F32) for d in range(2)]
            arrived = [False] * nc
        st_cur = st_next
        if idx + 1 < len(steps):
            st_next = scores(*steps[idx + 1])
        inter = [None, None]
        if state[0] is not None:
            inter = [_dot_nt(state[d].astype(BF16), q_ref[0, span(chunk_of(d, s)), feat(hh)]) for d in range(2)]
        for d in range(2):
            c = chunk_of(d, s)
            k = k_ref[0, span(c), feat(hh)]
            vt = vt_ref[0, hh, :, span(c)]
            g = g_ref[0, hh, :, span(c)]
            a_row = g[3 * d:3 * d + 1]
            c_row = g[3 * d + 1:3 * d + 2]
            tot = g[3 * d + 2:3 * d + 3, 0:1]
            m_prev = m_run[d]
            c_col = jnp.sum(jnp.where(eye, c_row, 0.0), axis=-1, keepdims=True)
            c_run = g[6 + d:7 + d]

            keep_state = emit_state or s + 1 < nc
            if keep_state:
                c_max = c_run[:, L - 1:L] if d == 0 else c_run[:, 0:1]
                m_new = tot + jnp.maximum(m_prev, c_max)
                wk = (jnp.exp2(tot + c_col - m_new) * k.astype(F32)).astype(BF16)
                upd = _dot(vt, wk)
                new_state = upd if state[d] is None else jnp.exp2(tot + m_prev - m_new) * state[d] + upd

            m_rel = jnp.maximum(m_prev, c_run)
            sp = (st_cur[d] * jnp.exp2(jnp.where(masks[d], c_col, -jnp.inf) - m_rel)).astype(BF16)
            numt = _dot(vt, sp)
            if inter[d] is not None:
                numt = numt + jnp.exp2(m_prev - m_rel) * inter[d]
            den = numt[DH_M:DH_M + 1]
            ht = numt[0:DH_M] * (1.0 / jnp.maximum(jnp.abs(den), jnp.exp2(-(a_row + m_rel))))
            if arrived[c]:
                finish_chunk(hh, c, ht)
            else:
                acc_ref[c] = ht
                arrived[c] = True
            if keep_state:
                state[d] = new_state
                m_run[d] = m_new

        if emit_state and s == nc - 1:
            for d in range(2):
                c_o[0, d, hh] = state[d][0:DH_M].T
                n_o[0, d, hh] = state[d][DH_M:DH_M + 1]
                m_o[0, d, hh] = m_run[d] * (1.0 / LOG2E)


def _mlstm(qm, km, vmt, gr, somt, gnorm_col, init_state, emit_state, nh):
    B, T, _ = qm.shape
    nc = T // CHUNK
    has_init = init_state is not None

    seq = pl.BlockSpec((1, T, nh * DH_M), lambda b, h: (b, 0, h))
    c_spec = pl.BlockSpec((1, 2, nh, DH_M, DH_M), lambda b, h: (b, 0, h, 0, 0))
    n_spec = pl.BlockSpec((1, 2, nh, 1, DH_M), lambda b, h: (b, 0, h, 0, 0))
    m_spec = pl.BlockSpec((1, 2, nh, 1, 1), lambda b, h: (b, 0, h, 0, 0))

    in_specs = [seq, seq,
                pl.BlockSpec((1, nh, VM_ROWS, T), lambda b, h: (b, h, 0, 0)),
                pl.BlockSpec((1, nh, 8, T), lambda b, h: (b, h, 0, 0)),
                pl.BlockSpec((1, nh * DH_M, T), lambda b, h: (b, h, 0)),
                pl.BlockSpec((nh * DH_M, 1), lambda b, h: (h, 0))]
    args = [qm, km, vmt, gr, somt, gnorm_col]
    if has_init:
        in_specs += [pl.BlockSpec((1, 2, nh, VM_ROWS, DH_M), lambda b, h: (b, 0, h, 0, 0)), m_spec]
        args += list(init_state)
    out_specs = [seq]
    out_shape = [jax.ShapeDtypeStruct((B, T, D_MODEL), BF16)]
    if emit_state:
        out_specs += [c_spec, n_spec, m_spec]
        out_shape += [jax.ShapeDtypeStruct((B, 2, NH_M, DH_M, DH_M), F32),
                      jax.ShapeDtypeStruct((B, 2, NH_M, 1, DH_M), F32),
                      jax.ShapeDtypeStruct((B, 2, NH_M, 1, 1), F32)]

    return pl.pallas_call(
        functools.partial(_mlstm_kernel, has_init, emit_state, nc, nh),
        grid=(B, NH_M // nh),
        in_specs=in_specs,
        out_specs=out_specs,
        out_shape=out_shape,
        scratch_shapes=[pltpu.VMEM((nc, DH_M, CHUNK), F32)],
        compiler_params=_params(("parallel", "parallel")),
        name="mlstm_lat" if has_init else "mlstm_ctx",
    )(*args)


def _attn_kernel(n_lat_tiles, has_ctx, nkv, *refs):
    if has_ctx:
        q_ref, k_ref, vt_ref, kc_ref, vct_ref, o_ref = refs
    else:
        q_ref, k_ref, vt_ref, o_ref = refs
    tk = k_ref.shape[1] // n_lat_tiles
    head = lambda h: slice(h * DH_A, (h + 1) * DH_A)
    tiles = [(lambda h, i=i: k_ref[0, i * tk:(i + 1) * tk, head(h)],
              lambda h, i=i: vt_ref[0, h, :, i * tk:(i + 1) * tk]) for i in range(n_lat_tiles)]
    if has_ctx:
        tiles.append((lambda h: kc_ref[0, :, head(h)], lambda h: vct_ref[0, h]))
    groups = [(h, g, r) for h in range(nkv) for r in range(q_ref.shape[1] // Q_CHAIN) for g in range(G_Q)]
    chains = [(t, i) for t in range(len(tiles)) for i in range(len(groups))]
    rows = lambda i: slice(groups[i][2] * Q_CHAIN, (groups[i][2] + 1) * Q_CHAIN)
    cols = lambda i: head(groups[i][0] * G_Q + groups[i][1])

    def scores(t, i):
        return _dot_nt(tiles[t][0](groups[i][0]), q_ref[0, rows(i), cols(i)])

    m = [None] * len(groups)
    acc = [None] * len(groups)
    pending = []
    for idx in range(len(chains) + ATTN_LOOKAHEAD):
        if idx < len(chains):
            pending.append(scores(*chains[idx]))
        if idx < ATTN_LOOKAHEAD:
            continue
        t, i = chains[idx - ATTN_LOOKAHEAD]
        st = pending.pop(0)
        m_tile = jnp.max(st, axis=0, keepdims=True)
        m_new = m_tile if t == 0 else jnp.maximum(m[i], m_tile)
        pv = _dot(tiles[t][1](groups[i][0]), jnp.exp2(st - m_new).astype(BF16))
        acc[i] = pv if t == 0 else jnp.exp2(m[i] - m_new) * acc[i] + pv
        m[i] = m_new

    for i in range(len(groups)):
        out = acc[i][0:DH_A] * (1.0 / acc[i][DH_A:DH_A + 1])
        o_ref[0, rows(i), cols(i)] = out.T.astype(BF16)


def _attention(qa, ka, vt, ctx_kv):
    B, T, _ = qa.shape
    tq = min(Q_TILE, T)
    nq = T // tq
    has_ctx = ctx_kv is not None
    n_lat_tiles = max(1, T // K_TILE)
    nkv = 1 if has_ctx else N_KV
    qspec = pl.BlockSpec((1, tq, nkv * G_Q * DH_A), lambda b, h, i: (b, i, h))
    kspec = lambda tk: pl.BlockSpec((1, tk, nkv * DH_A), lambda b, h, i: (b, 0, h))
    vspec = lambda tk: pl.BlockSpec((1, nkv, V_ROWS, tk), lambda b, h, i: (b, h, 0, 0))
    in_specs = [qspec, kspec(T), vspec(T)]
    args = [qa, ka, vt]
    if has_ctx:
        tc = ctx_kv[0].shape[1]
        in_specs += [kspec(tc), vspec(tc)]
        args += list(ctx_kv)
    return pl.pallas_call(
        functools.partial(_attn_kernel, n_lat_tiles, has_ctx, nkv),
        grid=(B, N_KV // nkv, nq),
        in_specs=in_specs,
        out_specs=qspec,
        out_shape=jax.ShapeDtypeStruct((B, T, D_MODEL), BF16),
        compiler_params=_params(("parallel", "parallel", "parallel")),
        name="attention_lat" if has_ctx else "attention_ctx",
    )(*args)


def _layer_norm(y, g, b):
    mu = jnp.mean(y, axis=-1, keepdims=True)
    yc = y - mu
    var = jnp.mean(yc * yc, axis=-1, keepdims=True)
    return yc * lax.rsqrt(var + EPS) * g + b


def _tail_body(x_ref, mod_ref, hm_ref, ha_ref, sgm_ref, sga_ref, ln_ref, o_ref, weights, ready):
    wbm_ref, wba_ref, wout_ref, wup_ref, wdown_ref = weights
    mod = mod_ref[0]
    ln = ln_ref[...]
    x = x_ref[0]
    ready(0)
    ready(1)
    merged = sgm_ref[0] * _dot(hm_ref[0], wbm_ref[...]) + sga_ref[0] * _dot(ha_ref[0], wba_ref[...])
    ready(2)
    mix = _dot(merged.astype(BF16), wout_ref[...])
    x1 = _layer_norm(ALPHA * x + mod[2:3] * mix, ln[0:1], ln[1:2])
    h = (x1 * (1.0 + mod[4:5]) + mod[3:4]).astype(BF16)
    ready(3)
    ready(4)
    ff = jnp.zeros_like(x1)
    for j in range(D_FF // D_MODEL):
        u = jnp.maximum(_dot(h, wup_ref[:, j * D_MODEL:(j + 1) * D_MODEL]), 0.0)
        ff = ff + _dot((u * u).astype(BF16), wdown_ref[j * D_MODEL:(j + 1) * D_MODEL, :])
    o_ref[0] = _layer_norm(ALPHA * x1 + mod[5:6] * ff, ln[2:3], ln[3:4])


def _tail_kernel(n_w, *refs):
    acts, w_hbm, o_ref = refs[:7], refs[7:7 + n_w], refs[7 + n_w]
    w_vmem, sem = refs[8 + n_w:8 + 2 * n_w], refs[8 + 2 * n_w]
    _with_streamed_weights(lambda ready: _tail_body(*acts, o_ref, w_vmem, ready), w_hbm, w_vmem, sem)


def _tail(x, mod, hm, ha, sgm, sga, wts, ln, name):
    per_batch_mod = mod.shape[0] > 1
    if not per_batch_mod:
        x, hm, ha, sgm, sga = (a.reshape(1, -1, D_MODEL) for a in (x, hm, ha, sgm, sga))
    B, T, _ = x.shape
    tm = TAIL_TILE
    tok = pl.BlockSpec((1, tm, D_MODEL), lambda b, t: (b, t, 0))
    in_specs = [tok,
                pl.BlockSpec((1, 6, D_MODEL), (lambda b, t: (b, 0, 0)) if per_batch_mod else (lambda b, t: (0, 0, 0))),
                tok, tok, tok, tok, _resident(ln.shape)] + [pl.BlockSpec(memory_space=pl.ANY)] * len(wts)
    return pl.pallas_call(
        functools.partial(_tail_kernel, len(wts)),
        grid=(B, T // tm),
        in_specs=in_specs,
        out_specs=tok,
        out_shape=jax.ShapeDtypeStruct((B, T, D_MODEL), F32),
        scratch_shapes=[pltpu.VMEM(w.shape, w.dtype) for w in wts] + [pltpu.SemaphoreType.DMA((len(wts),))],
        compiler_params=_params(("arbitrary", "arbitrary")),
        name=name,
    )(x, mod, hm, ha, sgm, sga, ln, *wts)


def _rope_tables(n_tokens):
    rows = n_tokens // GRID_W
    row = jnp.repeat(jnp.arange(rows), GRID_W)
    col = jnp.tile(jnp.arange(GRID_W), rows)
    inv = ROPE_BASE ** (-jnp.arange(N_FREQ, dtype=F32) / N_FREQ)
    ang = jnp.stack([row, col], -1).astype(F32)[..., None] * inv
    ang = jnp.broadcast_to(ang[:, :, None, :], (n_tokens, 2, 2, N_FREQ))
    sign = jnp.asarray([-1.0, 1.0], F32)[None, None, :, None]
    return jnp.cos(ang).reshape(n_tokens, DH_A), (jnp.sin(ang) * sign).reshape(n_tokens, DH_A)


def kernel(x_prompt, x_sample, cache_k, cache_v, state_C, state_n, state_m, c, c_ctx, w_mod, b_mod, w_in,
           b_gates, mlstm_norm_g, q_norm_g, k_norm_g, w_bm, w_ba, w_out, ln1_g, ln1_b, w_up, w_down,
           ln2_g, ln2_b):
    B, T, _ = x_prompt.shape
    Bd, Td, _ = x_sample.shape
    l = 0

    w = w_in[l]
    o_g = 4 * D_MODEL
    o_a = o_g + 4 * NH_M
    o_mg = o_a + (N_Q + 2 * N_KV) * DH_A
    gate_rows = np.array([4, 5, 6, 7, 12, 13, 14, 15, 0, 1, 2, 3, 8, 9, 10, 11])
    proj_small = (w[:, o_g:o_a].T[gate_rows].astype(BF16),
                  b_gates[l][gate_rows].reshape(4 * NH_M, 1),
                  q_norm_g[l].reshape(1, DH_A),
                  k_norm_g[l].reshape(1, DH_A))
    proj_big = (w[:, :2 * D_MODEL].astype(BF16),
                w[:, 2 * D_MODEL:o_g].T.astype(BF16),
                w[:, o_a:o_mg].astype(BF16),
                w[:, o_a + (N_Q + N_KV) * DH_A:o_mg].T.astype(BF16),
                w[:, o_mg:].astype(BF16))
    tail_w = (w_bm[l].astype(BF16), w_ba[l].astype(BF16), w_out[l].astype(BF16),
              w_up[l].astype(BF16), w_down[l].astype(BF16))
    ln = jnp.stack([ln1_g[l], ln1_b[l], ln2_g[l], ln2_b[l]])
    gnorm = mlstm_norm_g[l].reshape(D_MODEL, 1)

    c_rows = jnp.concatenate([c_ctx[None, :], c, jnp.zeros((MOD_ROWS - 1 - Bd, D_MODEL), F32)], axis=0)
    mod = _modulation(c_rows, w_mod[l], b_mod[l]).reshape(MOD_ROWS, 6, D_MODEL)
    mod_ctx, mod_lat = mod[0:1], mod[1:1 + Bd]

    (qm, km, vmt, somt, gr, qa, ka, vt, sgm, sga, k_new, v_new) = _projection(x_prompt, mod_ctx, proj_small, proj_big, None)
    hm, c_new, n_new, m_new = _mlstm(qm, km, vmt, gr, somt, gnorm, None, True, NH_M)
    ha = _attention(qa, ka, vt, None)
    y_prompt = _tail(x_prompt, mod_ctx, hm, ha, sgm, sga, tail_w, ln, "tail_ctx").reshape(x_prompt.shape)

    (qm, km, vmt, somt, gr, qa, ka, vt, sgm, sga) = _projection(x_sample, mod_lat, proj_small, proj_big, _rope_tables(Td))
    past = cache_k.shape[2]
    n_rows = jnp.broadcast_to(state_n[:, l][:, :, :, None, :], (Bd, 2, NH_M, VM_ROWS - DH_M, DH_M))
    init = (jnp.concatenate([jnp.swapaxes(state_C[:, l], -1, -2), n_rows], axis=-2),
            state_m[:, l].reshape(Bd, 2, NH_M, 1, 1))
    hm, = _mlstm(qm, km, vmt, gr, somt, gnorm, init, False, 1)
    vct = jnp.transpose(cache_v[:, l], (0, 2, 3, 1)).astype(BF16)
    vct = jnp.concatenate([vct, jnp.ones((Bd, N_KV, V_ROWS - DH_A, past), BF16)], axis=2)
    ctx_kv = (cache_k[:, l].reshape(Bd, past, N_KV * DH_A).astype(BF16), vct)
    ha = _attention(qa, ka, vt, ctx_kv)
    y_sample = _tail(x_sample, mod_lat, hm, ha, sgm, sga, tail_w, ln, "tail_lat")

    return (y_prompt, y_sample,
            k_new.reshape(B, 1, T, N_KV, DH_A), v_new.reshape(B, 1, T, N_KV, DH_A),
            c_new.reshape(B, 1, 2, NH_M, DH_M, DH_M), n_new.reshape(B, 1, 2, NH_M, DH_M),
            m_new.reshape(B, 1, 2, NH_M))
```

```python
import functools

import jax
import jax.numpy as jnp
import numpy as np
from jax import lax
from jax.experimental import pallas as pl
from jax.experimental.pallas import tpu as pltpu

D_MODEL = 1024
NH_M = 4
DH_M = 256
N_Q = 8
N_KV = 2
G_Q = N_Q // N_KV
DH_A = 128
D_FF = 4 * D_MODEL
GRID_W = 64
N_FREQ = DH_A // 4
ROPE_BASE = 10000.0
EPS = 1e-6
DEPTH = 1
ALPHA = (2 * DEPTH) ** 0.25

CHUNK = 256
TOK_TILE = 256
TAIL_TILE = 512
TAIL_SUB = 256
Q_TILE = 512
Q_CHAIN = 256
K_TILE = 512
LOG2E = float(np.log2(np.e))
Q_SCALE = DH_A ** -0.5 * LOG2E
ATTN_LOOKAHEAD = 4
VM_ROWS = DH_M + 16
V_ROWS = DH_A + 16
MOD_ROWS = 8

VMEM_WORK_BYTES = 20 * 1024 * 1024

F32 = jnp.float32
BF16 = jnp.bfloat16


def _dot(a, b):
    return jnp.dot(a, b, preferred_element_type=F32)


def _dot_nt(a, b):
    return lax.dot_general(a, b, (((1,), (1,)), ((), ())), preferred_element_type=F32)


def _resident(shape):
    nd = len(shape)
    return pl.BlockSpec(shape, lambda *_: (0,) * nd, pipeline_mode=pl.Buffered(1))


def _mod_spec(first_row, per_batch):
    return pl.BlockSpec((1, 6, D_MODEL), (lambda b, t: (first_row + b, 0, 0)) if per_batch
                        else (lambda b, t: (first_row, 0, 0)))


def _call(body, name, grid, semantics, in_specs, args, out_specs, out_shape, scratch=()):
    def window_bytes(spec, a):
        buffers = 2 if spec.pipeline_mode is None else spec.pipeline_mode.buffer_count
        return buffers * int(np.prod(spec.block_shape)) * jnp.dtype(a.dtype).itemsize

    outs, out_sp = (out_shape, out_specs) if isinstance(out_shape, (list, tuple)) else ([out_shape], [out_specs])
    windows = sum(map(window_bytes, in_specs, args)) + sum(map(window_bytes, out_sp, outs))
    held = sum(int(np.prod(s.shape)) * jnp.dtype(s.dtype).itemsize for s in scratch)
    return pl.pallas_call(
        body, grid=grid, in_specs=in_specs, out_specs=out_specs, out_shape=out_shape, scratch_shapes=list(scratch),
        compiler_params=pltpu.CompilerParams(dimension_semantics=semantics,
                                             vmem_limit_bytes=windows + held + VMEM_WORK_BYTES),
        name=name,
    )(*args)


def _mod_kernel(c_ref, w_ref, b_ref, o_ref):
    c = c_ref[...]
    s = c * jax.nn.sigmoid(c)
    o_ref[...] = _dot(s.astype(BF16), w_ref[...].astype(BF16)) + b_ref[...]


def _modulation(c_rows, w_mod, b_mod):
    n_out = w_mod.shape[1]
    blk = D_MODEL
    in_specs = [pl.BlockSpec((MOD_ROWS, D_MODEL), lambda j: (0, 0)),
                pl.BlockSpec((D_MODEL, blk), lambda j: (0, j)),
                pl.BlockSpec((1, blk), lambda j: (0, j))]
    return _call(_mod_kernel, "modulation", (n_out // blk,), ("parallel",),
                 in_specs, (c_rows, w_mod, b_mod.reshape(1, n_out)),
                 pl.BlockSpec((MOD_ROWS, blk), lambda j: (0, j)), jax.ShapeDtypeStruct((MOD_ROWS, n_out), F32))


def _log_sigmoid(x):
    return jnp.minimum(x, 0.0) - jnp.log1p(jnp.exp(-jnp.abs(x)))


def _cummax_lanes(x, reverse):
    n = x.shape[-1]
    lane = lax.broadcasted_iota(jnp.int32, x.shape, x.ndim - 1)
    step = 1
    while step < n:
        if reverse:
            shifted, valid = pltpu.roll(x, n - step, x.ndim - 1), lane < n - step
        else:
            shifted, valid = pltpu.roll(x, step, x.ndim - 1), lane >= step
        x = jnp.maximum(x, jnp.where(valid, shifted, -jnp.inf))
        step *= 2
    return x


def _rms(t, g):
    return t * lax.rsqrt(jnp.mean(t * t, axis=-1, keepdims=True) + EPS) * g


def _proj_kernel(rope, *refs):
    if rope:
        (x_ref, mod_ref, wgt_ref, bg_ref, qg_ref, kg_ref, cos_ref, sin_ref, wm_ref, wmt_ref, wa_ref, wvt_ref, wmg_ref,
         qm_o, km_o, vmt_o, somt_o, gr_o, qa_o, ka_o, vt_o, sgm_o, sga_o) = refs
    else:
        (x_ref, mod_ref, wgt_ref, bg_ref, qg_ref, kg_ref, wm_ref, wmt_ref, wa_ref, wvt_ref, wmg_ref,
         qm_o, km_o, vmt_o, somt_o, gr_o, qa_o, ka_o, vt_o, sgm_o, sga_o, kc_o, vc_o) = refs
    tm = x_ref.shape[1]
    mod = mod_ref[0]
    h = (x_ref[0] * (1.0 + mod[1:2]) + mod[0:1]).astype(BF16)

    gates = _dot_nt(wgt_ref[...], h) + bg_ref[...]
    lf = _log_sigmoid(gates)

    qm_o[0] = _dot(h, wm_ref[:, 0:D_MODEL]).astype(BF16)
    km_o[0] = (_dot(h, wm_ref[:, D_MODEL:2 * D_MODEL]) * (DH_M ** -0.5)).astype(BF16)
    for hh in range(NH_M):
        vmt = _dot_nt(wmt_ref[hh * DH_M:(hh + 1) * DH_M, :], h)
        vmt_o[0, hh, 0:DH_M, :] = vmt.astype(BF16)
        vmt_o[0, hh, DH_M:VM_ROWS, :] = jnp.ones((VM_ROWS - DH_M, tm), BF16)
    somt_o[0] = jax.nn.sigmoid(_dot_nt(wmt_ref[D_MODEL:2 * D_MODEL, :], h)).astype(BF16)

    if rope:
        cos = cos_ref[...]
        sin_s = sin_ref[...]
        lane = lax.broadcasted_iota(jnp.int32, (tm, DH_A), 1)
        first_half = (lane % (2 * N_FREQ)) < N_FREQ

        def rot(t):
            partner = jnp.where(first_half, pltpu.roll(t, DH_A - N_FREQ, 1), pltpu.roll(t, N_FREQ, 1))
            return t * cos + partner * sin_s
    else:
        rot = lambda t: t

    qg = qg_ref[...]
    kg = kg_ref[...]
    q_all = _dot(h, wa_ref[:, 0:N_Q * DH_A])
    k_all = _dot(h, wa_ref[:, N_Q * DH_A:(N_Q + N_KV) * DH_A])
    for g in range(N_Q):
        t = _rms(q_all[:, g * DH_A:(g + 1) * DH_A], qg)
        qa_o[0, :, g * DH_A:(g + 1) * DH_A] = (rot(t) * Q_SCALE).astype(BF16)
    for g in range(N_KV):
        t = _rms(k_all[:, g * DH_A:(g + 1) * DH_A], kg)
        if not rope:
            kc_o[0, :, g * DH_A:(g + 1) * DH_A] = t
        ka_o[0, :, g * DH_A:(g + 1) * DH_A] = rot(t).astype(BF16)
    if not rope:
        off = (N_Q + N_KV) * DH_A
        vc_o[0] = _dot(h, wa_ref[:, off:off + N_KV * DH_A])
    vt = _dot_nt(wvt_ref[...], h)
    for g in range(N_KV):
        vt_o[0, g, 0:DH_A, :] = vt[g * DH_A:(g + 1) * DH_A].astype(BF16)
        vt_o[0, g, DH_A:V_ROWS, :] = jnp.ones((V_ROWS - DH_A, tm), BF16)

    row = lax.broadcasted_iota(jnp.int32, (tm, tm), 0)
    col = lax.broadcasted_iota(jnp.int32, (tm, tm), 1)
    tri = jnp.where(row <= col, 1.0, 0.0).astype(BF16)
    hi = lf.astype(BF16)
    r1 = lf - hi.astype(F32)
    mid = r1.astype(BF16)
    lo = (r1 - mid.astype(F32)).astype(BF16)
    cum = (_dot(hi, tri) + _dot(mid, tri) + _dot(lo, tri))[0:8]
    lf8 = lf[0:8]
    tot = cum[:, tm - 1:tm]
    rev = tot - cum + lf8
    is_fwd = lax.broadcasted_iota(jnp.int32, (8, tm), 0) < NH_M
    a = jnp.where(is_fwd, cum, rev)
    cc = (gates[8:16] - a) * LOG2E
    a = a * LOG2E
    totb = jnp.broadcast_to(tot * LOG2E, (8, tm))
    c_pre = _cummax_lanes(cc, False)
    c_suf = _cummax_lanes(cc, True)
    for hh in range(NH_M):
        rows = (a[hh:hh + 1], cc[hh:hh + 1], totb[hh:hh + 1],
                a[NH_M + hh:NH_M + hh + 1], cc[NH_M + hh:NH_M + hh + 1], totb[NH_M + hh:NH_M + hh + 1],
                c_pre[hh:hh + 1], c_suf[NH_M + hh:NH_M + hh + 1])
        for k, r in enumerate(rows):
            gr_o[0, hh, k:k + 1, :] = r

    sgm_o[0] = jax.nn.sigmoid(_dot(h, wmg_ref[:, 0:D_MODEL])).astype(BF16)
    sga_o[0] = jax.nn.sigmoid(_dot(h, wmg_ref[:, D_MODEL:2 * D_MODEL])).astype(BF16)


def _projection(x, mod, mod_rows, small, big, rope_tables):
    B, T, _ = x.shape
    tm = TOK_TILE
    nt = T // tm
    rope = rope_tables is not None
    tok = lambda width: pl.BlockSpec((1, tm, width), lambda b, t: (b, t, 0))
    in_specs = [tok(D_MODEL), _mod_spec(*mod_rows)]
    in_specs += [_resident(w.shape) for w in small]
    args = [x, mod, *small]
    if rope:
        in_specs += [pl.BlockSpec((tm, DH_A), lambda b, t: (t, 0))] * 2
        args += list(rope_tables)
    in_specs += [_resident(w.shape) for w in big]
    args += list(big)

    kv_w = N_KV * DH_A
    outs = [((B, T, D_MODEL), BF16, tok(D_MODEL)),
            ((B, T, D_MODEL), BF16, tok(D_MODEL)),
            ((B, NH_M, VM_ROWS, T), BF16,
             pl.BlockSpec((1, NH_M, VM_ROWS, tm), lambda b, t: (b, 0, 0, t))),
            ((B, D_MODEL, T), BF16, pl.BlockSpec((1, D_MODEL, tm), lambda b, t: (b, 0, t))),
            ((B, NH_M, 8, T), F32, pl.BlockSpec((1, NH_M, 8, tm), lambda b, t: (b, 0, 0, t))),
            ((B, T, D_MODEL), BF16, tok(D_MODEL)),
            ((B, T, kv_w), BF16, tok(kv_w)),
            ((B, N_KV, V_ROWS, T), BF16,
             pl.BlockSpec((1, N_KV, V_ROWS, tm), lambda b, t: (b, 0, 0, t))),
            ((B, T, D_MODEL), BF16, tok(D_MODEL)),
            ((B, T, D_MODEL), BF16, tok(D_MODEL))]
    if not rope:
        outs += [((B, T, kv_w), F32, tok(kv_w)), ((B, T, kv_w), F32, tok(kv_w))]

    return _call(functools.partial(_proj_kernel, rope), "projection_lat" if rope else "projection_ctx",
                 (B, nt), ("parallel", "parallel"), in_specs, args,
                 [o[2] for o in outs], [jax.ShapeDtypeStruct(o[0], o[1]) for o in outs])


def _mlstm_kernel(has_init, emit_state, nc, nh, *refs):
    refs = list(refs)
    q_ref, k_ref, vt_ref, g_ref, somt_ref, gn_ref = refs[:6]
    refs = refs[6:]
    if has_init:
        ct0_ref, m0_ref = refs[:2]
        refs = refs[2:]
    hm_o = refs[0]
    refs = refs[1:]
    if emit_state:
        c_o, n_o, m_o = refs[:3]
        refs = refs[3:]
    acc_ref, = refs

    L = CHUNK
    row = lax.broadcasted_iota(jnp.int32, (L, L), 0)
    col = lax.broadcasted_iota(jnp.int32, (L, L), 1)
    eye = row == col
    masks = (row <= col, row >= col)

    span = lambda c: slice(c * L, (c + 1) * L)
    feat = lambda hh: slice(hh * DH_M, (hh + 1) * DH_M)
    chunk_of = lambda d, s: s if d == 0 else nc - 1 - s
    steps = [(hh, s) for hh in range(nh) for s in range(nc)]

    def scores(hh, s):
        return [_dot_nt(k_ref[0, span(chunk_of(d, s)), feat(hh)], q_ref[0, span(chunk_of(d, s)), feat(hh)])
                for d in range(2)]

    def finish_chunk(hh, c, ht):
        ht = acc_ref[c] + ht
        hn = ht * lax.rsqrt(jnp.mean(ht * ht, axis=0, keepdims=True) + EPS) * gn_ref[feat(hh), :]
        hm_o[0, span(c), feat(hh)] = (hn * somt_ref[0, feat(hh), span(c)]).T.astype(BF16)

    st_next = scores(*steps[0])
    for idx, (hh, s) in enumerate(steps):
        if s == 0:
            state = [ct0_ref[0, d, hh] if has_init else None for d in range(2)]
            m_run = [m0_ref[0, d, hh] * LOG2E if has_init else jnp.zeros((1, 1), F32) for d in range(2)]
            arrived = [False] * nc
        st_cur = st_next
        if idx + 1 < len(steps):
            st_next = scores(*steps[idx + 1])
        inter = [None, None]
        if state[0] is not None:
            inter = [_dot_nt(state[d].astype(BF16), q_ref[0, span(chunk_of(d, s)), feat(hh)]) for d in range(2)]
        for d in range(2):
            c = chunk_of(d, s)
            k = k_ref[0, span(c), feat(hh)]
            vt = vt_ref[0, hh, :, span(c)]
            g = g_ref[0, hh, :, span(c)]
            a_row = g[3 * d:3 * d + 1]
            c_row = g[3 * d + 1:3 * d + 2]
            tot = g[3 * d + 2:3 * d + 3, 0:1]
            m_prev = m_run[d]
            c_col = jnp.sum(jnp.where(eye, c_row, 0.0), axis=-1, keepdims=True)
            c_run = g[6 + d:7 + d]

            keep_state = emit_state or s + 1 < nc
            if keep_state:
                c_max = c_run[:, L - 1:L] if d == 0 else c_run[:, 0:1]
                m_new = tot + jnp.maximum(m_prev, c_max)
                wk = jnp.exp2(tot + c_col - m_new).astype(BF16) * k
                upd = _dot(vt, wk)
                new_state = upd if state[d] is None else jnp.exp2(tot + m_prev - m_new) * state[d] + upd

            m_rel = jnp.maximum(m_prev, c_run)
            sp = (st_cur[d] * jnp.exp2(jnp.where(masks[d], c_col, -jnp.inf) - m_rel)).astype(BF16)
            numt = _dot(vt, sp)
            if inter[d] is not None:
                numt = numt + jnp.exp2(m_prev - m_rel) * inter[d]
            den = numt[DH_M:DH_M + 1]
            ht = numt[0:DH_M] * (1.0 / jnp.maximum(jnp.abs(den), jnp.exp2(-(a_row + m_rel))))
            if arrived[c]:
                finish_chunk(hh, c, ht)
            else:
                acc_ref[c] = ht
                arrived[c] = True
            if keep_state:
                state[d] = new_state
                m_run[d] = m_new

        if emit_state and s == nc - 1:
            for d in range(2):
                c_o[0, d, hh] = state[d][0:DH_M].T
                n_o[0, d, hh] = state[d][DH_M:DH_M + 1]
                m_o[0, d, hh] = m_run[d] * (1.0 / LOG2E)


def _mlstm(qm, km, vmt, gr, somt, gnorm_col, init_state, emit_state, nh):
    B, T, _ = qm.shape
    nc = T // CHUNK
    has_init = init_state is not None

    seq = pl.BlockSpec((1, T, nh * DH_M), lambda b, h: (b, 0, h))
    c_spec = pl.BlockSpec((1, 2, nh, DH_M, DH_M), lambda b, h: (b, 0, h, 0, 0))
    n_spec = pl.BlockSpec((1, 2, nh, 1, DH_M), lambda b, h: (b, 0, h, 0, 0))
    m_spec = pl.BlockSpec((1, 2, nh, 1, 1), lambda b, h: (b, 0, h, 0, 0))

    in_specs = [seq, seq,
                pl.BlockSpec((1, nh, VM_ROWS, T), lambda b, h: (b, h, 0, 0)),
                pl.BlockSpec((1, nh, 8, T), lambda b, h: (b, h, 0, 0)),
                pl.BlockSpec((1, nh * DH_M, T), lambda b, h: (b, h, 0)),
                pl.BlockSpec((nh * DH_M, 1), lambda b, h: (h, 0))]
    args = [qm, km, vmt, gr, somt, gnorm_col]
    if has_init:
        in_specs += [pl.BlockSpec((1, 2, nh, VM_ROWS, DH_M), lambda b, h: (b, 0, h, 0, 0)), m_spec]
        args += list(init_state)
    out_specs = [seq]
    out_shape = [jax.ShapeDtypeStruct((B, T, D_MODEL), BF16)]
    if emit_state:
        out_specs += [c_spec, n_spec, m_spec]
        out_shape += [jax.ShapeDtypeStruct((B, 2, NH_M, DH_M, DH_M), F32),
                      jax.ShapeDtypeStruct((B, 2, NH_M, 1, DH_M), F32),
                      jax.ShapeDtypeStruct((B, 2, NH_M, 1, 1), F32)]

    return _call(functools.partial(_mlstm_kernel, has_init, emit_state, nc, nh),
                 "mlstm_lat" if has_init else "mlstm_ctx", (B, NH_M // nh), ("parallel", "parallel"),
                 in_specs, args, out_specs, out_shape, [pltpu.VMEM((nc, DH_M, CHUNK), F32)])


def _attn_kernel(n_lat_tiles, has_ctx, nkv, *refs):
    if has_ctx:
        q_ref, k_ref, vt_ref, kc_ref, vct_ref, o_ref = refs
    else:
        q_ref, k_ref, vt_ref, o_ref = refs
    tk = k_ref.shape[1] // n_lat_tiles
    head = lambda h: slice(h * DH_A, (h + 1) * DH_A)
    tiles = [(lambda h, i=i: k_ref[0, i * tk:(i + 1) * tk, head(h)],
              lambda h, i=i: vt_ref[0, h, :, i * tk:(i + 1) * tk]) for i in range(n_lat_tiles)]
    if has_ctx:
        tiles.append((lambda h: kc_ref[0, :, head(h)], lambda h: vct_ref[0, h]))
    groups = [(h, g, r) for h in range(nkv) for r in range(q_ref.shape[1] // Q_CHAIN) for g in range(G_Q)]
    chains = [(t, i) for t in range(len(tiles)) for i in range(len(groups))]
    rows = lambda i: slice(groups[i][2] * Q_CHAIN, (groups[i][2] + 1) * Q_CHAIN)
    cols = lambda i: head(groups[i][0] * G_Q + groups[i][1])

    def scores(t, i):
        return _dot_nt(tiles[t][0](groups[i][0]), q_ref[0, rows(i), cols(i)])

    m = [None] * len(groups)
    acc = [None] * len(groups)
    pending = []
    for idx in range(len(chains) + ATTN_LOOKAHEAD):
        if idx < len(chains):
            pending.append(scores(*chains[idx]))
        if idx < ATTN_LOOKAHEAD:
            continue
        t, i = chains[idx - ATTN_LOOKAHEAD]
        st = pending.pop(0)
        m_tile = jnp.max(st, axis=0, keepdims=True)
        m_new = m_tile if t == 0 else jnp.maximum(m[i], m_tile)
        pv = _dot(tiles[t][1](groups[i][0]), jnp.exp2(st - m_new).astype(BF16))
        acc[i] = pv if t == 0 else jnp.exp2(m[i] - m_new) * acc[i] + pv
        m[i] = m_new

    for i in range(len(groups)):
        out = acc[i][0:DH_A] * (1.0 / acc[i][DH_A:DH_A + 1])
        o_ref[0, rows(i), cols(i)] = out.T.astype(BF16)


def _attention(qa, ka, vt, ctx_kv):
    B, T, _ = qa.shape
    tq = min(Q_TILE, T)
    nq = T // tq
    has_ctx = ctx_kv is not None
    n_lat_tiles = max(1, T // K_TILE)
    nkv = 1 if has_ctx else N_KV
    qspec = pl.BlockSpec((1, tq, nkv * G_Q * DH_A), lambda b, h, i: (b, i, h))
    kspec = lambda tk: pl.BlockSpec((1, tk, nkv * DH_A), lambda b, h, i: (b, 0, h))
    vspec = lambda tk: pl.BlockSpec((1, nkv, V_ROWS, tk), lambda b, h, i: (b, h, 0, 0))
    in_specs = [qspec, kspec(T), vspec(T)]
    args = [qa, ka, vt]
    if has_ctx:
        tc = ctx_kv[0].shape[1]
        in_specs += [kspec(tc), vspec(tc)]
        args += list(ctx_kv)
    return _call(functools.partial(_attn_kernel, n_lat_tiles, has_ctx, nkv),
                 "attention_lat" if has_ctx else "attention_ctx", (B, N_KV // nkv, nq),
                 ("parallel", "parallel", "parallel"), in_specs, args,
                 qspec, jax.ShapeDtypeStruct((B, T, D_MODEL), BF16))


def _layer_norm(y, g, b):
    mu = jnp.mean(y, axis=-1, keepdims=True)
    yc = y - mu
    var = jnp.mean(yc * yc, axis=-1, keepdims=True)
    return yc * lax.rsqrt(var + EPS) * g + b


def _tail_kernel(x_ref, mod_ref, hm_ref, ha_ref, sgm_ref, sga_ref, ln_ref,
                 wbm_ref, wba_ref, wout_ref, wup_ref, wdown_ref, o_ref):
    mod = mod_ref[0]
    ln = ln_ref[...]
    n_sub = x_ref.shape[1] // TAIL_SUB
    rows = lambda p: slice(p * TAIL_SUB, (p + 1) * TAIL_SUB)

    def merge(p):
        merged = (sgm_ref[0, rows(p), :] * _dot(hm_ref[0, rows(p), :], wbm_ref[...])
                  + sga_ref[0, rows(p), :] * _dot(ha_ref[0, rows(p), :], wba_ref[...]))
        return ALPHA * x_ref[0, rows(p), :] + mod[2:3] * _dot(merged.astype(BF16), wout_ref[...])

    def ffn(x1):
        h = (x1 * (1.0 + mod[4:5]) + mod[3:4]).astype(BF16)
        ff = jnp.zeros_like(x1)
        for j in range(D_FF // D_MODEL):
            u = jnp.maximum(_dot(h, wup_ref[:, j * D_MODEL:(j + 1) * D_MODEL]), 0.0)
            ff = ff + _dot((u * u).astype(BF16), wdown_ref[j * D_MODEL:(j + 1) * D_MODEL, :])
        return ALPHA * x1 + mod[5:6] * ff

    y1 = [merge(p) for p in range(n_sub)]
    y2 = [ffn(_layer_norm(y1[p], ln[0:1], ln[1:2])) for p in range(n_sub)]
    for p in range(n_sub):
        o_ref[0, rows(p), :] = _layer_norm(y2[p], ln[2:3], ln[3:4])


def _tail(x, mod, mod_rows, hm, ha, sgm, sga, wts, ln, name):
    if not mod_rows[1]:
        x, hm, ha, sgm, sga = (a.reshape(1, -1, D_MODEL) for a in (x, hm, ha, sgm, sga))
    B, T, _ = x.shape
    tm = TAIL_TILE
    tok = pl.BlockSpec((1, tm, D_MODEL), lambda b, t: (b, t, 0))
    in_specs = ([tok, _mod_spec(*mod_rows), tok, tok, tok, tok, _resident(ln.shape)]
                + [_resident(w.shape) for w in wts])
    return _call(_tail_kernel, name, (B, T // tm), ("parallel", "parallel"),
                 in_specs, (x, mod, hm, ha, sgm, sga, ln, *wts), tok, jax.ShapeDtypeStruct((B, T, D_MODEL), F32))


def _rope_tables(n_tokens):
    rows = n_tokens // GRID_W
    row = np.repeat(np.arange(rows), GRID_W)
    col = np.tile(np.arange(GRID_W), rows)
    inv = ROPE_BASE ** (-np.arange(N_FREQ, dtype=np.float64) / N_FREQ)
    ang = np.stack([row, col], -1).astype(np.float64)[..., None] * inv
    ang = np.broadcast_to(ang[:, :, None, :], (n_tokens, 2, 2, N_FREQ))
    sign = np.asarray([-1.0, 1.0])[None, None, :, None]
    return (jnp.asarray(np.cos(ang).reshape(n_tokens, DH_A), F32),
            jnp.asarray((np.sin(ang) * sign).reshape(n_tokens, DH_A), F32))


def kernel(x_prompt, x_sample, cache_k, cache_v, state_C, state_n, state_m, c, c_ctx, w_mod, b_mod, w_in,
           b_gates, mlstm_norm_g, q_norm_g, k_norm_g, w_bm, w_ba, w_out, ln1_g, ln1_b, w_up, w_down,
           ln2_g, ln2_b):
    B, T, _ = x_prompt.shape
    Bd, Td, _ = x_sample.shape
    l = 0

    w = w_in[l]
    o_g = 4 * D_MODEL
    o_a = o_g + 4 * NH_M
    o_mg = o_a + (N_Q + 2 * N_KV) * DH_A
    gate_rows = np.array([4, 5, 6, 7, 12, 13, 14, 15, 0, 1, 2, 3, 8, 9, 10, 11])
    proj_small = (w[:, o_g:o_a].T[gate_rows].astype(BF16),
                  b_gates[l][gate_rows].reshape(4 * NH_M, 1),
                  q_norm_g[l].reshape(1, DH_A),
                  k_norm_g[l].reshape(1, DH_A))
    proj_big = (w[:, :2 * D_MODEL].astype(BF16),
                w[:, 2 * D_MODEL:o_g].T.astype(BF16),
                w[:, o_a:o_mg].astype(BF16),
                w[:, o_a + (N_Q + N_KV) * DH_A:o_mg].T.astype(BF16),
                w[:, o_mg:].astype(BF16))
    tail_w = (w_bm[l].astype(BF16), w_ba[l].astype(BF16), w_out[l].astype(BF16),
              w_up[l].astype(BF16), w_down[l].astype(BF16))
    ln = jnp.stack([ln1_g[l], ln1_b[l], ln2_g[l], ln2_b[l]])
    gnorm = mlstm_norm_g[l].reshape(D_MODEL, 1)

    c_rows = jnp.concatenate([c_ctx[None, :], c, jnp.zeros((MOD_ROWS - 1 - Bd, D_MODEL), F32)], axis=0)
    mod = _modulation(c_rows, w_mod[l], b_mod[l]).reshape(MOD_ROWS, 6, D_MODEL)
    rows_ctx, rows_lat = (0, False), (1, True)

    (qm, km, vmt, somt, gr, qa, ka, vt, sgm, sga, k_new, v_new) = _projection(x_prompt, mod, rows_ctx, proj_small, proj_big, None)
    hm, c_new, n_new, m_new = _mlstm(qm, km, vmt, gr, somt, gnorm, None, True, NH_M)
    ha = _attention(qa, ka, vt, None)
    y_prompt = _tail(x_prompt, mod, rows_ctx, hm, ha, sgm, sga, tail_w, ln, "tail_ctx").reshape(x_prompt.shape)

    (qm, km, vmt, somt, gr, qa, ka, vt, sgm, sga) = _projection(x_sample, mod, rows_lat, proj_small, proj_big, _rope_tables(Td))
    past = cache_k.shape[2]
    n_rows = jnp.broadcast_to(state_n[:, l][:, :, :, None, :], (Bd, 2, NH_M, VM_ROWS - DH_M, DH_M))
    init = (jnp.concatenate([jnp.swapaxes(state_C[:, l], -1, -2), n_rows], axis=-2),
            state_m[:, l].reshape(Bd, 2, NH_M, 1, 1))
    hm, = _mlstm(qm, km, vmt, gr, somt, gnorm, init, False, 1)
    vct = jnp.transpose(cache_v[:, l], (0, 2, 3, 1)).astype(BF16)
    vct = jnp.concatenate([vct, jnp.ones((Bd, N_KV, V_ROWS - DH_A, past), BF16)], axis=2)
    ctx_kv = (cache_k[:, l].reshape(Bd, past, N_KV * DH_A).astype(BF16), vct)
    ha = _attention(qa, ka, vt, ctx_kv)
    y_sample = _tail(x_sample, mod, rows_lat, hm, ha, sgm, sga, tail_w, ln, "tail_lat")

    return (y_prompt, y_sample,
            k_new.reshape(B, 1, T, N_KV, DH_A), v_new.reshape(B, 1, T, N_KV, DH_A),
            c_new.reshape(B, 1, 2, NH_M, DH_M, DH_M), n_new.reshape(B, 1, 2, NH_M, DH_M),
            m_new.reshape(B, 1, 2, NH_M))
```

```python
import functools

import jax
import jax.numpy as jnp
import numpy as np
from jax import lax
from jax.experimental import pallas as pl
from jax.experimental.pallas import tpu as pltpu

D_MODEL = 1024
NH_M = 4
DH_M = 256
N_Q = 8
N_KV = 2
G_Q = N_Q // N_KV
DH_A = 128
D_FF = 4 * D_MODEL
GRID_W = 64
N_FREQ = DH_A // 4
ROPE_BASE = 10000.0
EPS = 1e-6
DEPTH = 1
ALPHA = (2 * DEPTH) ** 0.25

CHUNK = 256
TOK_TILE = 512
TAIL_TILE = 512
TAIL_SUB = 256
Q_TILE = 512
Q_CHAIN = 256
K_TILE = 512
LOG2E = float(np.log2(np.e))
Q_SCALE = DH_A ** -0.5 * LOG2E
ATTN_LOOKAHEAD = 4
VM_ROWS = DH_M + 16
V_ROWS = DH_A + 16
MOD_ROWS = 8

VMEM_WORK_BYTES = 20 * 1024 * 1024

F32 = jnp.float32
BF16 = jnp.bfloat16


def _dot(a, b):
    return jnp.dot(a, b, preferred_element_type=F32)


def _dot_nt(a, b):
    return lax.dot_general(a, b, (((1,), (1,)), ((), ())), preferred_element_type=F32)


def _resident(shape):
    nd = len(shape)
    return pl.BlockSpec(shape, lambda *_: (0,) * nd, pipeline_mode=pl.Buffered(1))


def _mod_spec(first_row, per_batch):
    return pl.BlockSpec((1, 6, D_MODEL), (lambda b, t: (first_row + b, 0, 0)) if per_batch
                        else (lambda b, t: (first_row, 0, 0)))


def _call(body, name, grid, semantics, in_specs, args, out_specs, out_shape, scratch=()):
    def window_bytes(spec, a):
        buffers = 2 if spec.pipeline_mode is None else spec.pipeline_mode.buffer_count
        return buffers * int(np.prod(spec.block_shape)) * jnp.dtype(a.dtype).itemsize

    outs, out_sp = (out_shape, out_specs) if isinstance(out_shape, (list, tuple)) else ([out_shape], [out_specs])
    windows = sum(map(window_bytes, in_specs, args)) + sum(map(window_bytes, out_sp, outs))
    held = sum(int(np.prod(s.shape)) * jnp.dtype(s.dtype).itemsize for s in scratch)
    return pl.pallas_call(
        body, grid=grid, in_specs=in_specs, out_specs=out_specs, out_shape=out_shape, scratch_shapes=list(scratch),
        compiler_params=pltpu.CompilerParams(dimension_semantics=semantics,
                                             vmem_limit_bytes=windows + held + VMEM_WORK_BYTES),
        name=name,
    )(*args)


def _mod_kernel(c_ref, w_ref, b_ref, o_ref):
    c = c_ref[...]
    s = c * jax.nn.sigmoid(c)
    o_ref[...] = _dot(s.astype(BF16), w_ref[...].astype(BF16)) + b_ref[...]


def _modulation(c_rows, w_mod, b_mod):
    n_out = w_mod.shape[1]
    blk = D_MODEL
    in_specs = [pl.BlockSpec((MOD_ROWS, D_MODEL), lambda j: (0, 0)),
                pl.BlockSpec((D_MODEL, blk), lambda j: (0, j)),
                pl.BlockSpec((1, blk), lambda j: (0, j))]
    return _call(_mod_kernel, "modulation", (n_out // blk,), ("parallel",),
                 in_specs, (c_rows, w_mod, b_mod.reshape(1, n_out)),
                 pl.BlockSpec((MOD_ROWS, blk), lambda j: (0, j)), jax.ShapeDtypeStruct((MOD_ROWS, n_out), F32))


def _log_sigmoid(x):
    return jnp.minimum(x, 0.0) - jnp.log1p(jnp.exp(-jnp.abs(x)))


def _cummax_lanes(x, reverse):
    n = x.shape[-1]
    lane = lax.broadcasted_iota(jnp.int32, x.shape, x.ndim - 1)
    step = 1
    while step < n:
        if reverse:
            shifted, valid = pltpu.roll(x, n - step, x.ndim - 1), lane < n - step
        else:
            shifted, valid = pltpu.roll(x, step, x.ndim - 1), lane >= step
        x = jnp.maximum(x, jnp.where(valid, shifted, -jnp.inf))
        step *= 2
    return x


def _rms(t, g):
    return t * lax.rsqrt(jnp.mean(t * t, axis=-1, keepdims=True) + EPS) * g


def _proj_kernel(rope, *refs):
    x_ref, mod_ref = refs[:2]
    mod = mod_ref[0]
    for p in range(x_ref.shape[1] // CHUNK):
        _proj_subtile(rope, slice(p * CHUNK, (p + 1) * CHUNK), mod, refs)


def _proj_subtile(rope, rows, mod, refs):
    if rope:
        (x_ref, mod_ref, wgt_ref, bg_ref, qg_ref, kg_ref, cos_ref, sin_ref, wm_ref, wmt_ref, wa_ref, wvt_ref, wmg_ref,
         qm_o, km_o, vmt_o, somt_o, gr_o, qa_o, ka_o, vt_o, sgm_o, sga_o) = refs
    else:
        (x_ref, mod_ref, wgt_ref, bg_ref, qg_ref, kg_ref, wm_ref, wmt_ref, wa_ref, wvt_ref, wmg_ref,
         qm_o, km_o, vmt_o, somt_o, gr_o, qa_o, ka_o, vt_o, sgm_o, sga_o, kc_o, vc_o) = refs
    tm = CHUNK
    h = (x_ref[0, rows, :] * (1.0 + mod[1:2]) + mod[0:1]).astype(BF16)

    gates = _dot_nt(wgt_ref[...], h) + bg_ref[...]
    lf = _log_sigmoid(gates)

    qm_o[0, rows, :] = _dot(h, wm_ref[:, 0:D_MODEL]).astype(BF16)
    km_o[0, rows, :] = (_dot(h, wm_ref[:, D_MODEL:2 * D_MODEL]) * (DH_M ** -0.5)).astype(BF16)
    for hh in range(NH_M):
        vmt = _dot_nt(wmt_ref[hh * DH_M:(hh + 1) * DH_M, :], h)
        vmt_o[0, hh, 0:DH_M, rows] = vmt.astype(BF16)
        vmt_o[0, hh, DH_M:VM_ROWS, rows] = jnp.ones((VM_ROWS - DH_M, tm), BF16)
    somt_o[0, :, rows] = jax.nn.sigmoid(_dot_nt(wmt_ref[D_MODEL:2 * D_MODEL, :], h)).astype(BF16)

    if rope:
        cos = cos_ref[rows, :]
        sin_s = sin_ref[rows, :]
        lane = lax.broadcasted_iota(jnp.int32, (tm, DH_A), 1)
        first_half = (lane % (2 * N_FREQ)) < N_FREQ

        def rot(t):
            partner = jnp.where(first_half, pltpu.roll(t, DH_A - N_FREQ, 1), pltpu.roll(t, N_FREQ, 1))
            return t * cos + partner * sin_s
    else:
        rot = lambda t: t

    qg = qg_ref[...]
    kg = kg_ref[...]
    q_all = _dot(h, wa_ref[:, 0:N_Q * DH_A])
    k_all = _dot(h, wa_ref[:, N_Q * DH_A:(N_Q + N_KV) * DH_A])
    for g in range(N_Q):
        t = _rms(q_all[:, g * DH_A:(g + 1) * DH_A], qg)
        qa_o[0, rows, g * DH_A:(g + 1) * DH_A] = (rot(t) * Q_SCALE).astype(BF16)
    for g in range(N_KV):
        t = _rms(k_all[:, g * DH_A:(g + 1) * DH_A], kg)
        if not rope:
            kc_o[0, rows, g * DH_A:(g + 1) * DH_A] = t
        ka_o[0, rows, g * DH_A:(g + 1) * DH_A] = rot(t).astype(BF16)
    if not rope:
        off = (N_Q + N_KV) * DH_A
        vc_o[0, rows, :] = _dot(h, wa_ref[:, off:off + N_KV * DH_A])
    vt = _dot_nt(wvt_ref[...], h)
    for g in range(N_KV):
        vt_o[0, g, 0:DH_A, rows] = vt[g * DH_A:(g + 1) * DH_A].astype(BF16)
        vt_o[0, g, DH_A:V_ROWS, rows] = jnp.ones((V_ROWS - DH_A, tm), BF16)

    row = lax.broadcasted_iota(jnp.int32, (tm, tm), 0)
    col = lax.broadcasted_iota(jnp.int32, (tm, tm), 1)
    tri = jnp.where(row <= col, 1.0, 0.0).astype(BF16)
    hi = lf.astype(BF16)
    r1 = lf - hi.astype(F32)
    mid = r1.astype(BF16)
    lo = (r1 - mid.astype(F32)).astype(BF16)
    cum = (_dot(hi, tri) + _dot(mid, tri) + _dot(lo, tri))[0:8]
    lf8 = lf[0:8]
    tot = cum[:, tm - 1:tm]
    rev = tot - cum + lf8
    is_fwd = lax.broadcasted_iota(jnp.int32, (8, tm), 0) < NH_M
    a = jnp.where(is_fwd, cum, rev)
    cc = (gates[8:16] - a) * LOG2E
    a = a * LOG2E
    totb = jnp.broadcast_to(tot * LOG2E, (8, tm))
    c_pre = _cummax_lanes(cc, False)
    c_suf = _cummax_lanes(cc, True)
    for hh in range(NH_M):
        gate_rows = (a[hh:hh + 1], cc[hh:hh + 1], totb[hh:hh + 1],
                     a[NH_M + hh:NH_M + hh + 1], cc[NH_M + hh:NH_M + hh + 1], totb[NH_M + hh:NH_M + hh + 1],
                     c_pre[hh:hh + 1], c_suf[NH_M + hh:NH_M + hh + 1])
        for k, r in enumerate(gate_rows):
            gr_o[0, hh, k:k + 1, rows] = r

    sgm_o[0, rows, :] = jax.nn.sigmoid(_dot(h, wmg_ref[:, 0:D_MODEL])).astype(BF16)
    sga_o[0, rows, :] = jax.nn.sigmoid(_dot(h, wmg_ref[:, D_MODEL:2 * D_MODEL])).astype(BF16)


def _projection(x, mod, mod_rows, small, big, rope_tables):
    B, T, _ = x.shape
    tm = min(TOK_TILE, T)
    nt = T // tm
    rope = rope_tables is not None
    tok = lambda width: pl.BlockSpec((1, tm, width), lambda b, t: (b, t, 0))
    in_specs = [tok(D_MODEL), _mod_spec(*mod_rows)]
    in_specs += [_resident(w.shape) for w in small]
    args = [x, mod, *small]
    if rope:
        in_specs += [pl.BlockSpec((tm, DH_A), lambda b, t: (t, 0))] * 2
        args += list(rope_tables)
    in_specs += [_resident(w.shape) for w in big]
    args += list(big)

    kv_w = N_KV * DH_A
    outs = [((B, T, D_MODEL), BF16, tok(D_MODEL)),
            ((B, T, D_MODEL), BF16, tok(D_MODEL)),
            ((B, NH_M, VM_ROWS, T), BF16,
             pl.BlockSpec((1, NH_M, VM_ROWS, tm), lambda b, t: (b, 0, 0, t))),
            ((B, D_MODEL, T), BF16, pl.BlockSpec((1, D_MODEL, tm), lambda b, t: (b, 0, t))),
            ((B, NH_M, 8, T), F32, pl.BlockSpec((1, NH_M, 8, tm), lambda b, t: (b, 0, 0, t))),
            ((B, T, D_MODEL), BF16, tok(D_MODEL)),
            ((B, T, kv_w), BF16, tok(kv_w)),
            ((B, N_KV, V_ROWS, T), BF16,
             pl.BlockSpec((1, N_KV, V_ROWS, tm), lambda b, t: (b, 0, 0, t))),
            ((B, T, D_MODEL), BF16, tok(D_MODEL)),
            ((B, T, D_MODEL), BF16, tok(D_MODEL))]
    if not rope:
        outs += [((B, T, kv_w), F32, tok(kv_w)), ((B, T, kv_w), F32, tok(kv_w))]

    return _call(functools.partial(_proj_kernel, rope), "projection_lat" if rope else "projection_ctx",
                 (B, nt), ("parallel", "parallel"), in_specs, args,
                 [o[2] for o in outs], [jax.ShapeDtypeStruct(o[0], o[1]) for o in outs])


def _mlstm_kernel(has_init, emit_state, nc, nh, *refs):
    refs = list(refs)
    q_ref, k_ref, vt_ref, g_ref, somt_ref, gn_ref = refs[:6]
    refs = refs[6:]
    if has_init:
        c0_ref, n0_ref, m0_ref = refs[:3]
        refs = refs[3:]
    hm_o = refs[0]
    refs = refs[1:]
    if emit_state:
        c_o, n_o, m_o = refs[:3]
        refs = refs[3:]
    acc_ref, = refs

    L = CHUNK
    row = lax.broadcasted_iota(jnp.int32, (L, L), 0)
    col = lax.broadcasted_iota(jnp.int32, (L, L), 1)
    eye = row == col
    masks = (row <= col, row >= col)

    span = lambda c: slice(c * L, (c + 1) * L)
    feat = lambda hh: slice(hh * DH_M, (hh + 1) * DH_M)
    chunk_of = lambda d, s: s if d == 0 else nc - 1 - s
    steps = [(hh, s) for hh in range(nh) for s in range(nc)]

    def scores(hh, s):
        return [_dot_nt(k_ref[0, span(chunk_of(d, s)), feat(hh)], q_ref[0, span(chunk_of(d, s)), feat(hh)])
                for d in range(2)]

    def finish_chunk(hh, c, ht):
        ht = acc_ref[c] + ht
        hn = ht * lax.rsqrt(jnp.mean(ht * ht, axis=0, keepdims=True) + EPS) * gn_ref[feat(hh), :]
        hm_o[0, span(c), feat(hh)] = (hn * somt_ref[0, feat(hh), span(c)]).T.astype(BF16)

    st_next = scores(*steps[0])
    for idx, (hh, s) in enumerate(steps):
        if s == 0:
            state = [None, None]
            if has_init:
                state = [jnp.concatenate([c0_ref[0, d, hh].T,
                                          jnp.broadcast_to(n0_ref[0, d, hh], (VM_ROWS - DH_M, DH_M))], axis=0)
                         for d in range(2)]
            m_run = [m0_ref[0, d, hh] * LOG2E if has_init else jnp.zeros((1, 1), F32) for d in range(2)]
            arrived = [False] * nc
        st_cur = st_next
        if idx + 1 < len(steps):
            st_next = scores(*steps[idx + 1])
        inter = [None, None]
        if state[0] is not None:
            inter = [_dot_nt(state[d].astype(BF16), q_ref[0, span(chunk_of(d, s)), feat(hh)]) for d in range(2)]
        for d in range(2):
            c = chunk_of(d, s)
            k = k_ref[0, span(c), feat(hh)]
            vt = vt_ref[0, hh, :, span(c)]
            g = g_ref[0, hh, :, span(c)]
            a_row = g[3 * d:3 * d + 1]
            c_row = g[3 * d + 1:3 * d + 2]
            tot = g[3 * d + 2:3 * d + 3, 0:1]
            m_prev = m_run[d]
            c_col = jnp.sum(jnp.where(eye, c_row, 0.0), axis=-1, keepdims=True)
            c_run = g[6 + d:7 + d]

            keep_state = emit_state or s + 1 < nc
            if keep_state:
                c_max = c_run[:, L - 1:L] if d == 0 else c_run[:, 0:1]
                m_new = tot + jnp.maximum(m_prev, c_max)
                wk = jnp.exp2(tot + c_col - m_new).astype(BF16) * k
                upd = _dot(vt, wk)
                new_state = upd if state[d] is None else jnp.exp2(tot + m_prev - m_new) * state[d] + upd

            m_rel = jnp.maximum(m_prev, c_run)
            sp = (st_cur[d] * jnp.exp2(jnp.where(masks[d], c_col, -jnp.inf) - m_rel)).astype(BF16)
            numt = _dot(vt, sp)
            if inter[d] is not None:
                numt = numt + jnp.exp2(m_prev - m_rel) * inter[d]
            den = numt[DH_M:DH_M + 1]
            ht = numt[0:DH_M] * (1.0 / jnp.maximum(jnp.abs(den), jnp.exp2(-(a_row + m_rel))))
            if arrived[c]:
                finish_chunk(hh, c, ht)
            else:
                acc_ref[c] = ht
                arrived[c] = True
            if keep_state:
                state[d] = new_state
                m_run[d] = m_new

        if emit_state and s == nc - 1:
            for d in range(2):
                c_o[0, d, hh] = state[d][0:DH_M].T
                n_o[0, d, hh] = state[d][DH_M:DH_M + 1]
                m_o[0, d, hh] = m_run[d] * (1.0 / LOG2E)


def _mlstm(qm, km, vmt, gr, somt, gnorm_col, init_state, emit_state, nh):
    B, T, _ = qm.shape
    nc = T // CHUNK
    has_init = init_state is not None

    seq = pl.BlockSpec((1, T, nh * DH_M), lambda b, h: (b, 0, h))
    c_spec = pl.BlockSpec((1, 2, nh, DH_M, DH_M), lambda b, h: (b, 0, h, 0, 0))
    n_spec = pl.BlockSpec((1, 2, nh, 1, DH_M), lambda b, h: (b, 0, h, 0, 0))
    m_spec = pl.BlockSpec((1, 2, nh, 1, 1), lambda b, h: (b, 0, h, 0, 0))

    in_specs = [seq, seq,
                pl.BlockSpec((1, nh, VM_ROWS, T), lambda b, h: (b, h, 0, 0)),
                pl.BlockSpec((1, nh, 8, T), lambda b, h: (b, h, 0, 0)),
                pl.BlockSpec((1, nh * DH_M, T), lambda b, h: (b, h, 0)),
                pl.BlockSpec((nh * DH_M, 1), lambda b, h: (h, 0))]
    args = [qm, km, vmt, gr, somt, gnorm_col]
    if has_init:
        in_specs += [c_spec, n_spec, m_spec]
        args += list(init_state)
    out_specs = [seq]
    out_shape = [jax.ShapeDtypeStruct((B, T, D_MODEL), BF16)]
    if emit_state:
        out_specs += [c_spec, n_spec, m_spec]
        out_shape += [jax.ShapeDtypeStruct((B, 2, NH_M, DH_M, DH_M), F32),
                      jax.ShapeDtypeStruct((B, 2, NH_M, 1, DH_M), F32),
                      jax.ShapeDtypeStruct((B, 2, NH_M, 1, 1), F32)]

    return _call(functools.partial(_mlstm_kernel, has_init, emit_state, nc, nh),
                 "mlstm_lat" if has_init else "mlstm_ctx", (B, NH_M // nh), ("parallel", "parallel"),
                 in_specs, args, out_specs, out_shape, [pltpu.VMEM((nc, DH_M, CHUNK), F32)])


def _attn_kernel(n_lat_tiles, has_ctx, nkv, *refs):
    if has_ctx:
        q_ref, k_ref, vt_ref, kc_ref, vct_ref, o_ref = refs
    else:
        q_ref, k_ref, vt_ref, o_ref = refs
    tk = k_ref.shape[1] // n_lat_tiles
    head = lambda h: slice(h * DH_A, (h + 1) * DH_A)
    tiles = [(lambda h, i=i: k_ref[0, i * tk:(i + 1) * tk, head(h)],
              lambda h, i=i: vt_ref[0, h, :, i * tk:(i + 1) * tk]) for i in range(n_lat_tiles)]
    if has_ctx:
        tiles.append((lambda h: kc_ref[0, :, head(h)], lambda h: vct_ref[0, h]))
    groups = [(h, g, r) for h in range(nkv) for r in range(q_ref.shape[1] // Q_CHAIN) for g in range(G_Q)]
    chains = [(t, i) for t in range(len(tiles)) for i in range(len(groups))]
    rows = lambda i: slice(groups[i][2] * Q_CHAIN, (groups[i][2] + 1) * Q_CHAIN)
    cols = lambda i: head(groups[i][0] * G_Q + groups[i][1])

    def scores(t, i):
        return _dot_nt(tiles[t][0](groups[i][0]), q_ref[0, rows(i), cols(i)])

    m = [None] * len(groups)
    acc = [None] * len(groups)
    pending = []
    for idx in range(len(chains) + ATTN_LOOKAHEAD):
        if idx < len(chains):
            pending.append(scores(*chains[idx]))
        if idx < ATTN_LOOKAHEAD:
            continue
        t, i = chains[idx - ATTN_LOOKAHEAD]
        st = pending.pop(0)
        m_tile = jnp.max(st, axis=0, keepdims=True)
        m_new = m_tile if t == 0 else jnp.maximum(m[i], m_tile)
        pv = _dot(tiles[t][1](groups[i][0]), jnp.exp2(st - m_new).astype(BF16))
        acc[i] = pv if t == 0 else jnp.exp2(m[i] - m_new) * acc[i] + pv
        m[i] = m_new

    for i in range(len(groups)):
        out = acc[i][0:DH_A] * (1.0 / acc[i][DH_A:DH_A + 1])
        o_ref[0, rows(i), cols(i)] = out.T.astype(BF16)


def _attention(qa, ka, vt, ctx_kv):
    B, T, _ = qa.shape
    tq = min(Q_TILE, T)
    nq = T // tq
    has_ctx = ctx_kv is not None
    n_lat_tiles = max(1, T // K_TILE)
    nkv = 1 if has_ctx else N_KV
    qspec = pl.BlockSpec((1, tq, nkv * G_Q * DH_A), lambda b, h, i: (b, i, h))
    kspec = lambda tk: pl.BlockSpec((1, tk, nkv * DH_A), lambda b, h, i: (b, 0, h))
    vspec = lambda tk: pl.BlockSpec((1, nkv, V_ROWS, tk), lambda b, h, i: (b, h, 0, 0))
    in_specs = [qspec, kspec(T), vspec(T)]
    args = [qa, ka, vt]
    if has_ctx:
        tc = ctx_kv[0].shape[1]
        in_specs += [kspec(tc), vspec(tc)]
        args += list(ctx_kv)
    return _call(functools.partial(_attn_kernel, n_lat_tiles, has_ctx, nkv),
                 "attention_lat" if has_ctx else "attention_ctx", (B, N_KV // nkv, nq),
                 ("parallel", "parallel", "parallel"), in_specs, args,
                 qspec, jax.ShapeDtypeStruct((B, T, D_MODEL), BF16))


def _layer_norm(y, g, b):
    mu = jnp.mean(y, axis=-1, keepdims=True)
    yc = y - mu
    var = jnp.mean(yc * yc, axis=-1, keepdims=True)
    return yc * lax.rsqrt(var + EPS) * g + b


def _tail_kernel(x_ref, mod_ref, hm_ref, ha_ref, sgm_ref, sga_ref, ln_ref,
                 wbm_ref, wba_ref, wout_ref, wup_ref, wdown_ref, o_ref):
    mod = mod_ref[0]
    ln = ln_ref[...]
    n_sub = x_ref.shape[1] // TAIL_SUB
    rows = lambda p: slice(p * TAIL_SUB, (p + 1) * TAIL_SUB)

    def merge(p):
        merged = (sgm_ref[0, rows(p), :] * _dot(hm_ref[0, rows(p), :], wbm_ref[...])
                  + sga_ref[0, rows(p), :] * _dot(ha_ref[0, rows(p), :], wba_ref[...]))
        return ALPHA * x_ref[0, rows(p), :] + mod[2:3] * _dot(merged.astype(BF16), wout_ref[...])

    def ffn(x1):
        h = (x1 * (1.0 + mod[4:5]) + mod[3:4]).astype(BF16)
        ff = jnp.zeros_like(x1)
        for j in range(D_FF // D_MODEL):
            u = jnp.maximum(_dot(h, wup_ref[:, j * D_MODEL:(j + 1) * D_MODEL]), 0.0)
            ff = ff + _dot((u * u).astype(BF16), wdown_ref[j * D_MODEL:(j + 1) * D_MODEL, :])
        return ALPHA * x1 + mod[5:6] * ff

    y1 = [merge(p) for p in range(n_sub)]
    y2 = [ffn(_layer_norm(y1[p], ln[0:1], ln[1:2])) for p in range(n_sub)]
    for p in range(n_sub):
        o_ref[0, rows(p), :] = _layer_norm(y2[p], ln[2:3], ln[3:4])


def _tail(x, mod, mod_rows, hm, ha, sgm, sga, wts, ln, name):
    if not mod_rows[1]:
        x, hm, ha, sgm, sga = (a.reshape(1, -1, D_MODEL) for a in (x, hm, ha, sgm, sga))
    B, T, _ = x.shape
    tm = TAIL_TILE
    tok = pl.BlockSpec((1, tm, D_MODEL), lambda b, t: (b, t, 0))
    in_specs = ([tok, _mod_spec(*mod_rows), tok, tok, tok, tok, _resident(ln.shape)]
                + [_resident(w.shape) for w in wts])
    return _call(_tail_kernel, name, (B, T // tm), ("parallel", "parallel"),
                 in_specs, (x, mod, hm, ha, sgm, sga, ln, *wts), tok, jax.ShapeDtypeStruct((B, T, D_MODEL), F32))


def _rope_tables(n_tokens):
    rows = n_tokens // GRID_W
    row = np.repeat(np.arange(rows), GRID_W)
    col = np.tile(np.arange(GRID_W), rows)
    inv = ROPE_BASE ** (-np.arange(N_FREQ, dtype=np.float64) / N_FREQ)
    ang = np.stack([row, col], -1).astype(np.float64)[..., None] * inv
    ang = np.broadcast_to(ang[:, :, None, :], (n_tokens, 2, 2, N_FREQ))
    sign = np.asarray([-1.0, 1.0])[None, None, :, None]
    return (jnp.asarray(np.cos(ang).reshape(n_tokens, DH_A), F32),
            jnp.asarray((np.sin(ang) * sign).reshape(n_tokens, DH_A), F32))


def kernel(x_prompt, x_sample, cache_k, cache_v, state_C, state_n, state_m, c, c_ctx, w_mod, b_mod, w_in,
           b_gates, mlstm_norm_g, q_norm_g, k_norm_g, w_bm, w_ba, w_out, ln1_g, ln1_b, w_up, w_down,
           ln2_g, ln2_b):
    B, T, _ = x_prompt.shape
    Bd, Td, _ = x_sample.shape
    l = 0

    w = w_in[l]
    o_g = 4 * D_MODEL
    o_a = o_g + 4 * NH_M
    o_mg = o_a + (N_Q + 2 * N_KV) * DH_A
    gate_rows = np.array([4, 5, 6, 7, 12, 13, 14, 15, 0, 1, 2, 3, 8, 9, 10, 11])
    proj_small = (w[:, o_g:o_a].T[gate_rows].astype(BF16),
                  b_gates[l][gate_rows].reshape(4 * NH_M, 1),
                  q_norm_g[l].reshape(1, DH_A),
                  k_norm_g[l].reshape(1, DH_A))
    proj_big = (w[:, :2 * D_MODEL].astype(BF16),
                w[:, 2 * D_MODEL:o_g].T.astype(BF16),
                w[:, o_a:o_mg].astype(BF16),
                w[:, o_a + (N_Q + N_KV) * DH_A:o_mg].T.astype(BF16),
                w[:, o_mg:].astype(BF16))
    tail_w = (w_bm[l].astype(BF16), w_ba[l].astype(BF16), w_out[l].astype(BF16),
              w_up[l].astype(BF16), w_down[l].astype(BF16))
    ln = jnp.stack([ln1_g[l], ln1_b[l], ln2_g[l], ln2_b[l]])
    gnorm = mlstm_norm_g[l].reshape(D_MODEL, 1)

    c_rows = jnp.concatenate([c_ctx[None, :], c, jnp.zeros((MOD_ROWS - 1 - Bd, D_MODEL), F32)], axis=0)
    mod = _modulation(c_rows, w_mod[l], b_mod[l]).reshape(MOD_ROWS, 6, D_MODEL)
    rows_ctx, rows_lat = (0, False), (1, True)

    (qm, km, vmt, somt, gr, qa, ka, vt, sgm, sga, k_new, v_new) = _projection(x_prompt, mod, rows_ctx, proj_small, proj_big, None)
    hm, c_new, n_new, m_new = _mlstm(qm, km, vmt, gr, somt, gnorm, None, True, NH_M)
    ha = _attention(qa, ka, vt, None)
    y_prompt = _tail(x_prompt, mod, rows_ctx, hm, ha, sgm, sga, tail_w, ln, "tail_ctx").reshape(x_prompt.shape)

    (qm, km, vmt, somt, gr, qa, ka, vt, sgm, sga) = _projection(x_sample, mod, rows_lat, proj_small, proj_big, _rope_tables(Td))
    past = cache_k.shape[2]
    init = (state_C[:, l], state_n[:, l].reshape(Bd, 2, NH_M, 1, DH_M), state_m[:, l].reshape(Bd, 2, NH_M, 1, 1))
    hm, = _mlstm(qm, km, vmt, gr, somt, gnorm, init, False, 1)
    vct = jnp.transpose(cache_v[:, l], (0, 2, 3, 1)).astype(BF16)
    vct = jnp.concatenate([vct, jnp.ones((Bd, N_KV, V_ROWS - DH_A, past), BF16)], axis=2)
    ctx_kv = (cache_k[:, l].reshape(Bd, past, N_KV * DH_A).astype(BF16), vct)
    ha = _attention(qa, ka, vt, ctx_kv)
    y_sample = _tail(x_sample, mod, rows_lat, hm, ha, sgm, sga, tail_w, ln, "tail_lat")

    return (y_prompt, y_sample,
            k_new.reshape(B, 1, T, N_KV, DH_A), v_new.reshape(B, 1, T, N_KV, DH_A),
            c_new.reshape(B, 1, 2, NH_M, DH_M, DH_M), n_new.reshape(B, 1, 2, NH_M, DH_M),
            m_new.reshape(B, 1, 2, NH_M))
```

```python
import functools

import jax
import jax.numpy as jnp
import numpy as np
from jax import lax
from jax.experimental import pallas as pl
from jax.experimental.pallas import tpu as pltpu

D_MODEL = 1024
NH_M = 4
DH_M = 256
N_Q = 8
N_KV = 2
G_Q = N_Q // N_KV
DH_A = 128
D_FF = 4 * D_MODEL
GRID_W = 64
N_FREQ = DH_A // 4
ROPE_BASE = 10000.0
EPS = 1e-6
DEPTH = 1
ALPHA = (2 * DEPTH) ** 0.25

CHUNK = 256
TOK_TILE = 512
TAIL_TILE = 512
TAIL_SUB = 256
Q_TILE = 512
Q_CHAIN = 256
K_TILE = 512
LOG2E = float(np.log2(np.e))
Q_SCALE = DH_A ** -0.5 * LOG2E
ATTN_LOOKAHEAD = 4
VM_ROWS = DH_M + 16
V_ROWS = DH_A + 16
MOD_ROWS = 8

VMEM_WORK_BYTES = 20 * 1024 * 1024
VMEM_FLOOR_BYTES = 56 * 1024 * 1024

F32 = jnp.float32
BF16 = jnp.bfloat16


def _dot(a, b):
    return jnp.dot(a, b, preferred_element_type=F32)


def _dot_nt(a, b):
    return lax.dot_general(a, b, (((1,), (1,)), ((), ())), preferred_element_type=F32)


def _resident(shape):
    nd = len(shape)
    return pl.BlockSpec(shape, lambda *_: (0,) * nd, pipeline_mode=pl.Buffered(1))


def _mod_spec(first_row, per_batch):
    return pl.BlockSpec((1, 6, D_MODEL), (lambda b, t: (first_row + b, 0, 0)) if per_batch
                        else (lambda b, t: (first_row, 0, 0)))


def _call(body, name, grid, semantics, in_specs, args, out_specs, out_shape, scratch=()):
    def window_bytes(spec, a):
        buffers = 2 if spec.pipeline_mode is None else spec.pipeline_mode.buffer_count
        return buffers * int(np.prod(spec.block_shape)) * jnp.dtype(a.dtype).itemsize

    outs, out_sp = (out_shape, out_specs) if isinstance(out_shape, (list, tuple)) else ([out_shape], [out_specs])
    windows = sum(map(window_bytes, in_specs, args)) + sum(map(window_bytes, out_sp, outs))
    held = sum(int(np.prod(s.shape)) * jnp.dtype(s.dtype).itemsize for s in scratch)
    return pl.pallas_call(
        body, grid=grid, in_specs=in_specs, out_specs=out_specs, out_shape=out_shape, scratch_shapes=list(scratch),
        compiler_params=pltpu.CompilerParams(dimension_semantics=semantics,
                                             vmem_limit_bytes=max(windows + held + VMEM_WORK_BYTES, VMEM_FLOOR_BYTES)),
        name=name,
    )(*args)


def _mod_kernel(c_ref, w_ref, b_ref, o_ref):
    c = c_ref[...]
    s = c * jax.nn.sigmoid(c)
    o_ref[...] = _dot(s.astype(BF16), w_ref[...].astype(BF16)) + b_ref[...]


def _modulation(c_rows, w_mod, b_mod):
    n_out = w_mod.shape[1]
    blk = D_MODEL
    in_specs = [pl.BlockSpec((MOD_ROWS, D_MODEL), lambda j: (0, 0)),
                pl.BlockSpec((D_MODEL, blk), lambda j: (0, j)),
                pl.BlockSpec((1, blk), lambda j: (0, j))]
    return _call(_mod_kernel, "modulation", (n_out // blk,), ("parallel",),
                 in_specs, (c_rows, w_mod, b_mod.reshape(1, n_out)),
                 pl.BlockSpec((MOD_ROWS, blk), lambda j: (0, j)), jax.ShapeDtypeStruct((MOD_ROWS, n_out), F32))


def _log_sigmoid(x):
    return jnp.minimum(x, 0.0) - jnp.log1p(jnp.exp(-jnp.abs(x)))


def _cummax_lanes(x, reverse):
    n = x.shape[-1]
    lane = lax.broadcasted_iota(jnp.int32, x.shape, x.ndim - 1)
    step = 1
    while step < n:
        if reverse:
            shifted, valid = pltpu.roll(x, n - step, x.ndim - 1), lane < n - step
        else:
            shifted, valid = pltpu.roll(x, step, x.ndim - 1), lane >= step
        x = jnp.maximum(x, jnp.where(valid, shifted, -jnp.inf))
        step *= 2
    return x


def _rms(t, g):
    return t * lax.rsqrt(jnp.mean(t * t, axis=-1, keepdims=True) + EPS) * g


def _proj_kernel(rope, *refs):
    x_ref, mod_ref = refs[:2]
    mod = mod_ref[0]
    for p in range(x_ref.shape[1] // CHUNK):
        _proj_subtile(rope, slice(p * CHUNK, (p + 1) * CHUNK), mod, refs)


def _proj_subtile(rope, rows, mod, refs):
    if rope:
        (x_ref, mod_ref, wgt_ref, bg_ref, qg_ref, kg_ref, cos_ref, sin_ref, wm_ref, wmt_ref, wa_ref, wvt_ref, wmg_ref,
         qm_o, km_o, vmt_o, somt_o, gr_o, qa_o, ka_o, vt_o, sgm_o, sga_o) = refs
    else:
        (x_ref, mod_ref, wgt_ref, bg_ref, qg_ref, kg_ref, wm_ref, wmt_ref, wa_ref, wvt_ref, wmg_ref,
         qm_o, km_o, vmt_o, somt_o, gr_o, qa_o, ka_o, vt_o, sgm_o, sga_o, kc_o, vc_o) = refs
    tm = CHUNK
    h = (x_ref[0, rows, :] * (1.0 + mod[1:2]) + mod[0:1]).astype(BF16)

    gates = _dot_nt(wgt_ref[...], h) + bg_ref[...]
    lf = _log_sigmoid(gates)

    qm_o[0, rows, :] = _dot(h, wm_ref[:, 0:D_MODEL]).astype(BF16)
    km_o[0, rows, :] = (_dot(h, wm_ref[:, D_MODEL:2 * D_MODEL]) * (DH_M ** -0.5)).astype(BF16)
    for hh in range(NH_M):
        vmt = _dot_nt(wmt_ref[hh * DH_M:(hh + 1) * DH_M, :], h)
        vmt_o[0, hh, 0:DH_M, rows] = vmt.astype(BF16)
        vmt_o[0, hh, DH_M:VM_ROWS, rows] = jnp.ones((VM_ROWS - DH_M, tm), BF16)
    somt_o[0, :, rows] = jax.nn.sigmoid(_dot_nt(wmt_ref[D_MODEL:2 * D_MODEL, :], h)).astype(BF16)

    if rope:
        cos = cos_ref[rows, :]
        sin_s = sin_ref[rows, :]
        lane = lax.broadcasted_iota(jnp.int32, (tm, DH_A), 1)
        first_half = (lane % (2 * N_FREQ)) < N_FREQ

        def rot(t):
            partner = jnp.where(first_half, pltpu.roll(t, DH_A - N_FREQ, 1), pltpu.roll(t, N_FREQ, 1))
            return t * cos + partner * sin_s
    else:
        rot = lambda t: t

    qg = qg_ref[...]
    kg = kg_ref[...]
    q_all = _dot(h, wa_ref[:, 0:N_Q * DH_A])
    k_all = _dot(h, wa_ref[:, N_Q * DH_A:(N_Q + N_KV) * DH_A])
    for g in range(N_Q):
        t = _rms(q_all[:, g * DH_A:(g + 1) * DH_A], qg)
        qa_o[0, rows, g * DH_A:(g + 1) * DH_A] = (rot(t) * Q_SCALE).astype(BF16)
    for g in range(N_KV):
        t = _rms(k_all[:, g * DH_A:(g + 1) * DH_A], kg)
        if not rope:
            kc_o[0, rows, g * DH_A:(g + 1) * DH_A] = t
        ka_o[0, rows, g * DH_A:(g + 1) * DH_A] = rot(t).astype(BF16)
    if not rope:
        off = (N_Q + N_KV) * DH_A
        vc_o[0, rows, :] = _dot(h, wa_ref[:, off:off + N_KV * DH_A])
    vt = _dot_nt(wvt_ref[...], h)
    for g in range(N_KV):
        vt_o[0, g, 0:DH_A, rows] = vt[g * DH_A:(g + 1) * DH_A].astype(BF16)
        vt_o[0, g, DH_A:V_ROWS, rows] = jnp.ones((V_ROWS - DH_A, tm), BF16)

    row = lax.broadcasted_iota(jnp.int32, (tm, tm), 0)
    col = lax.broadcasted_iota(jnp.int32, (tm, tm), 1)
    tri = jnp.where(row <= col, 1.0, 0.0).astype(BF16)
    hi = lf.astype(BF16)
    r1 = lf - hi.astype(F32)
    mid = r1.astype(BF16)
    lo = (r1 - mid.astype(F32)).astype(BF16)
    cum = (_dot(hi, tri) + _dot(mid, tri) + _dot(lo, tri))[0:8]
    lf8 = lf[0:8]
    tot = cum[:, tm - 1:tm]
    rev = tot - cum + lf8
    is_fwd = lax.broadcasted_iota(jnp.int32, (8, tm), 0) < NH_M
    a = jnp.where(is_fwd, cum, rev)
    cc = (gates[8:16] - a) * LOG2E
    a = a * LOG2E
    totb = jnp.broadcast_to(tot * LOG2E, (8, tm))
    c_pre = _cummax_lanes(cc, False)
    c_suf = _cummax_lanes(cc, True)
    for hh in range(NH_M):
        gate_rows = (a[hh:hh + 1], cc[hh:hh + 1], totb[hh:hh + 1],
                     a[NH_M + hh:NH_M + hh + 1], cc[NH_M + hh:NH_M + hh + 1], totb[NH_M + hh:NH_M + hh + 1],
                     c_pre[hh:hh + 1], c_suf[NH_M + hh:NH_M + hh + 1])
        for k, r in enumerate(gate_rows):
            gr_o[0, hh, k:k + 1, rows] = r

    sgm_o[0, rows, :] = jax.nn.sigmoid(_dot(h, wmg_ref[:, 0:D_MODEL])).astype(BF16)
    sga_o[0, rows, :] = jax.nn.sigmoid(_dot(h, wmg_ref[:, D_MODEL:2 * D_MODEL])).astype(BF16)


def _projection(x, mod, mod_rows, small, big, rope_tables):
    B, T, _ = x.shape
    tm = min(TOK_TILE, T)
    nt = T // tm
    rope = rope_tables is not None
    tok = lambda width: pl.BlockSpec((1, tm, width), lambda b, t: (b, t, 0))
    in_specs = [tok(D_MODEL), _mod_spec(*mod_rows)]
    in_specs += [_resident(w.shape) for w in small]
    args = [x, mod, *small]
    if rope:
        in_specs += [pl.BlockSpec((tm, DH_A), lambda b, t: (t, 0))] * 2
        args += list(rope_tables)
    in_specs += [_resident(w.shape) for w in big]
    args += list(big)

    kv_w = N_KV * DH_A
    outs = [((B, T, D_MODEL), BF16, tok(D_MODEL)),
            ((B, T, D_MODEL), BF16, tok(D_MODEL)),
            ((B, NH_M, VM_ROWS, T), BF16,
             pl.BlockSpec((1, NH_M, VM_ROWS, tm), lambda b, t: (b, 0, 0, t))),
            ((B, D_MODEL, T), BF16, pl.BlockSpec((1, D_MODEL, tm), lambda b, t: (b, 0, t))),
            ((B, NH_M, 8, T), F32, pl.BlockSpec((1, NH_M, 8, tm), lambda b, t: (b, 0, 0, t))),
            ((B, T, D_MODEL), BF16, tok(D_MODEL)),
            ((B, T, kv_w), BF16, tok(kv_w)),
            ((B, N_KV, V_ROWS, T), BF16,
             pl.BlockSpec((1, N_KV, V_ROWS, tm), lambda b, t: (b, 0, 0, t))),
            ((B, T, D_MODEL), BF16, tok(D_MODEL)),
            ((B, T, D_MODEL), BF16, tok(D_MODEL))]
    if not rope:
        outs += [((B, T, kv_w), F32, tok(kv_w)), ((B, T, kv_w), F32, tok(kv_w))]

    return _call(functools.partial(_proj_kernel, rope), "projection_lat" if rope else "projection_ctx",
                 (B, nt), ("parallel", "parallel"), in_specs, args,
                 [o[2] for o in outs], [jax.ShapeDtypeStruct(o[0], o[1]) for o in outs])


def _mlstm_kernel(has_init, emit_state, nc, nh, *refs):
    refs = list(refs)
    q_ref, k_ref, vt_ref, g_ref, somt_ref, gn_ref = refs[:6]
    refs = refs[6:]
    if has_init:
        c0_ref, n0_ref, m0_ref = refs[:3]
        refs = refs[3:]
    hm_o = refs[0]
    refs = refs[1:]
    if emit_state:
        c_o, n_o, m_o = refs[:3]
        refs = refs[3:]
    acc_ref, = refs

    L = CHUNK
    row = lax.broadcasted_iota(jnp.int32, (L, L), 0)
    col = lax.broadcasted_iota(jnp.int32, (L, L), 1)
    eye = row == col
    masks = (row <= col, row >= col)

    span = lambda c: slice(c * L, (c + 1) * L)
    feat = lambda hh: slice(hh * DH_M, (hh + 1) * DH_M)
    chunk_of = lambda d, s: s if d == 0 else nc - 1 - s
    steps = [(hh, s) for hh in range(nh) for s in range(nc)]

    def scores(hh, s):
        return [_dot_nt(k_ref[0, span(chunk_of(d, s)), feat(hh)], q_ref[0, span(chunk_of(d, s)), feat(hh)])
                for d in range(2)]

    def finish_chunk(hh, c, ht):
        ht = acc_ref[c] + ht
        hn = ht * lax.rsqrt(jnp.mean(ht * ht, axis=0, keepdims=True) + EPS) * gn_ref[feat(hh), :]
        hm_o[0, span(c), feat(hh)] = (hn * somt_ref[0, feat(hh), span(c)]).T.astype(BF16)

    st_next = scores(*steps[0])
    for idx, (hh, s) in enumerate(steps):
        if s == 0:
            state = [None, None]
            if has_init:
                state = [jnp.concatenate([c0_ref[0, d, hh].T,
                                          jnp.broadcast_to(n0_ref[0, d, hh], (VM_ROWS - DH_M, DH_M))], axis=0)
                         for d in range(2)]
            m_run = [m0_ref[0, d, hh] * LOG2E if has_init else jnp.zeros((1, 1), F32) for d in range(2)]
            arrived = [False] * nc
        st_cur = st_next
        if idx + 1 < len(steps):
            st_next = scores(*steps[idx + 1])
        inter = [None, None]
        if state[0] is not None:
            inter = [_dot_nt(state[d].astype(BF16), q_ref[0, span(chunk_of(d, s)), feat(hh)]) for d in range(2)]
        for d in range(2):
            c = chunk_of(d, s)
            k = k_ref[0, span(c), feat(hh)]
            vt = vt_ref[0, hh, :, span(c)]
            g = g_ref[0, hh, :, span(c)]
            a_row = g[3 * d:3 * d + 1]
            c_row = g[3 * d + 1:3 * d + 2]
            tot = g[3 * d + 2:3 * d + 3, 0:1]
            m_prev = m_run[d]
            c_col = jnp.sum(jnp.where(eye, c_row, 0.0), axis=-1, keepdims=True)
            c_run = g[6 + d:7 + d]

            keep_state = emit_state or s + 1 < nc
            if keep_state:
                c_max = c_run[:, L - 1:L] if d == 0 else c_run[:, 0:1]
                m_new = tot + jnp.maximum(m_prev, c_max)
                wk = jnp.exp2(tot + c_col - m_new).astype(BF16) * k
                upd = _dot(vt, wk)
                new_state = upd if state[d] is None else jnp.exp2(tot + m_prev - m_new) * state[d] + upd

            m_rel = jnp.maximum(m_prev, c_run)
            sp = (st_cur[d] * jnp.exp2(jnp.where(masks[d], c_col, -jnp.inf) - m_rel)).astype(BF16)
            numt = _dot(vt, sp)
            if inter[d] is not None:
                numt = numt + jnp.exp2(m_prev - m_rel) * inter[d]
            den = numt[DH_M:DH_M + 1]
            ht = numt[0:DH_M] * (1.0 / jnp.maximum(jnp.abs(den), jnp.exp2(-(a_row + m_rel))))
            if arrived[c]:
                finish_chunk(hh, c, ht)
            else:
                acc_ref[c] = ht
                arrived[c] = True
            if keep_state:
                state[d] = new_state
                m_run[d] = m_new

        if emit_state and s == nc - 1:
            for d in range(2):
                c_o[0, d, hh] = state[d][0:DH_M].T
                n_o[0, d, hh] = state[d][DH_M:DH_M + 1]
                m_o[0, d, hh] = m_run[d] * (1.0 / LOG2E)


def _mlstm(qm, km, vmt, gr, somt, gnorm_col, init_state, emit_state, nh):
    B, T, _ = qm.shape
    nc = T // CHUNK
    has_init = init_state is not None

    seq = pl.BlockSpec((1, T, nh * DH_M), lambda b, h: (b, 0, h))
    c_spec = pl.BlockSpec((1, 2, nh, DH_M, DH_M), lambda b, h: (b, 0, h, 0, 0))
    n_spec = pl.BlockSpec((1, 2, nh, 1, DH_M), lambda b, h: (b, 0, h, 0, 0))
    m_spec = pl.BlockSpec((1, 2, nh, 1, 1), lambda b, h: (b, 0, h, 0, 0))

    in_specs = [seq, seq,
                pl.BlockSpec((1, nh, VM_ROWS, T), lambda b, h: (b, h, 0, 0)),
                pl.BlockSpec((1, nh, 8, T), lambda b, h: (b, h, 0, 0)),
                pl.BlockSpec((1, nh * DH_M, T), lambda b, h: (b, h, 0)),
                pl.BlockSpec((nh * DH_M, 1), lambda b, h: (h, 0))]
    args = [qm, km, vmt, gr, somt, gnorm_col]
    if has_init:
        in_specs += [c_spec, n_spec, m_spec]
        args += list(init_state)
    out_specs = [seq]
    out_shape = [jax.ShapeDtypeStruct((B, T, D_MODEL), BF16)]
    if emit_state:
        out_specs += [c_spec, n_spec, m_spec]
        out_shape += [jax.ShapeDtypeStruct((B, 2, NH_M, DH_M, DH_M), F32),
                      jax.ShapeDtypeStruct((B, 2, NH_M, 1, DH_M), F32),
                      jax.ShapeDtypeStruct((B, 2, NH_M, 1, 1), F32)]

    return _call(functools.partial(_mlstm_kernel, has_init, emit_state, nc, nh),
                 "mlstm_lat" if has_init else "mlstm_ctx", (B, NH_M // nh), ("parallel", "parallel"),
                 in_specs, args, out_specs, out_shape, [pltpu.VMEM((nc, DH_M, CHUNK), F32)])


def _attn_kernel(n_lat_tiles, has_ctx, nkv, *refs):
    if has_ctx:
        q_ref, k_ref, vt_ref, kc_ref, vct_ref, o_ref = refs
    else:
        q_ref, k_ref, vt_ref, o_ref = refs
    tk = k_ref.shape[1] // n_lat_tiles
    head = lambda h: slice(h * DH_A, (h + 1) * DH_A)
    tiles = [(lambda h, i=i: k_ref[0, i * tk:(i + 1) * tk, head(h)],
              lambda h, i=i: vt_ref[0, h, :, i * tk:(i + 1) * tk]) for i in range(n_lat_tiles)]
    if has_ctx:
        tiles.append((lambda h: kc_ref[0, :, head(h)], lambda h: vct_ref[0, h]))
    groups = [(h, g, r) for h in range(nkv) for r in range(q_ref.shape[1] // Q_CHAIN) for g in range(G_Q)]
    chains = [(t, i) for t in range(len(tiles)) for i in range(len(groups))]
    rows = lambda i: slice(groups[i][2] * Q_CHAIN, (groups[i][2] + 1) * Q_CHAIN)
    cols = lambda i: head(groups[i][0] * G_Q + groups[i][1])

    def scores(t, i):
        return _dot_nt(tiles[t][0](groups[i][0]), q_ref[0, rows(i), cols(i)])

    m = [None] * len(groups)
    acc = [None] * len(groups)
    pending = []
    for idx in range(len(chains) + ATTN_LOOKAHEAD):
        if idx < len(chains):
            pending.append(scores(*chains[idx]))
        if idx < ATTN_LOOKAHEAD:
            continue
        t, i = chains[idx - ATTN_LOOKAHEAD]
        st = pending.pop(0)
        m_tile = jnp.max(st, axis=0, keepdims=True)
        m_new = m_tile if t == 0 else jnp.maximum(m[i], m_tile)
        pv = _dot(tiles[t][1](groups[i][0]), jnp.exp2(st - m_new).astype(BF16))
        acc[i] = pv if t == 0 else jnp.exp2(m[i] - m_new) * acc[i] + pv
        m[i] = m_new

    for i in range(len(groups)):
        out = acc[i][0:DH_A] * (1.0 / acc[i][DH_A:DH_A + 1])
        o_ref[0, rows(i), cols(i)] = out.T.astype(BF16)


def _attention(qa, ka, vt, ctx_kv):
    B, T, _ = qa.shape
    tq = min(Q_TILE, T)
    nq = T // tq
    has_ctx = ctx_kv is not None
    n_lat_tiles = max(1, T // K_TILE)
    nkv = 1 if has_ctx else N_KV
    qspec = pl.BlockSpec((1, tq, nkv * G_Q * DH_A), lambda b, h, i: (b, i, h))
    kspec = lambda tk: pl.BlockSpec((1, tk, nkv * DH_A), lambda b, h, i: (b, 0, h))
    vspec = lambda tk: pl.BlockSpec((1, nkv, V_ROWS, tk), lambda b, h, i: (b, h, 0, 0))
    in_specs = [qspec, kspec(T), vspec(T)]
    args = [qa, ka, vt]
    if has_ctx:
        tc = ctx_kv[0].shape[1]
        in_specs += [kspec(tc), vspec(tc)]
        args += list(ctx_kv)
    return _call(functools.partial(_attn_kernel, n_lat_tiles, has_ctx, nkv),
                 "attention_lat" if has_ctx else "attention_ctx", (B, N_KV // nkv, nq),
                 ("parallel", "parallel", "parallel"), in_specs, args,
                 qspec, jax.ShapeDtypeStruct((B, T, D_MODEL), BF16))


def _layer_norm(y, g, b):
    mu = jnp.mean(y, axis=-1, keepdims=True)
    yc = y - mu
    var = jnp.mean(yc * yc, axis=-1, keepdims=True)
    return yc * lax.rsqrt(var + EPS) * g + b


def _tail_kernel(x_ref, mod_ref, hm_ref, ha_ref, sgm_ref, sga_ref, ln_ref,
                 wbm_ref, wba_ref, wout_ref, wup_ref, wdown_ref, o_ref):
    mod = mod_ref[0]
    ln = ln_ref[...]
    n_sub = x_ref.shape[1] // TAIL_SUB
    rows = lambda p: slice(p * TAIL_SUB, (p + 1) * TAIL_SUB)

    def merge(p):
        merged = (sgm_ref[0, rows(p), :] * _dot(hm_ref[0, rows(p), :], wbm_ref[...])
                  + sga_ref[0, rows(p), :] * _dot(ha_ref[0, rows(p), :], wba_ref[...]))
        return ALPHA * x_ref[0, rows(p), :] + mod[2:3] * _dot(merged.astype(BF16), wout_ref[...])

    def ffn(x1):
        h = (x1 * (1.0 + mod[4:5]) + mod[3:4]).astype(BF16)
        ff = jnp.zeros_like(x1)
        for j in range(D_FF // D_MODEL):
            u = jnp.maximum(_dot(h, wup_ref[:, j * D_MODEL:(j + 1) * D_MODEL]), 0.0)
            ff = ff + _dot((u * u).astype(BF16), wdown_ref[j * D_MODEL:(j + 1) * D_MODEL, :])
        return ALPHA * x1 + mod[5:6] * ff

    y1 = [merge(p) for p in range(n_sub)]
    y2 = [ffn(_layer_norm(y1[p], ln[0:1], ln[1:2])) for p in range(n_sub)]
    for p in range(n_sub):
        o_ref[0, rows(p), :] = _layer_norm(y2[p], ln[2:3], ln[3:4])


def _tail(x, mod, mod_rows, hm, ha, sgm, sga, wts, ln, name):
    if not mod_rows[1]:
        x, hm, ha, sgm, sga = (a.reshape(1, -1, D_MODEL) for a in (x, hm, ha, sgm, sga))
    B, T, _ = x.shape
    tm = TAIL_TILE
    tok = pl.BlockSpec((1, tm, D_MODEL), lambda b, t: (b, t, 0))
    in_specs = ([tok, _mod_spec(*mod_rows), tok, tok, tok, tok, _resident(ln.shape)]
                + [_resident(w.shape) for w in wts])
    return _call(_tail_kernel, name, (B, T // tm), ("parallel", "parallel"),
                 in_specs, (x, mod, hm, ha, sgm, sga, ln, *wts), tok, jax.ShapeDtypeStruct((B, T, D_MODEL), F32))


def _rope_tables(n_tokens):
    rows = n_tokens // GRID_W
    row = np.repeat(np.arange(rows), GRID_W)
    col = np.tile(np.arange(GRID_W), rows)
    inv = ROPE_BASE ** (-np.arange(N_FREQ, dtype=np.float64) / N_FREQ)
    ang = np.stack([row, col], -1).astype(np.float64)[..., None] * inv
    ang = np.broadcast_to(ang[:, :, None, :], (n_tokens, 2, 2, N_FREQ))
    sign = np.asarray([-1.0, 1.0])[None, None, :, None]
    return (jnp.asarray(np.cos(ang).reshape(n_tokens, DH_A), F32),
            jnp.asarray((np.sin(ang) * sign).reshape(n_tokens, DH_A), F32))


def kernel(x_prompt, x_sample, cache_k, cache_v, state_C, state_n, state_m, c, c_ctx, w_mod, b_mod, w_in,
           b_gates, mlstm_norm_g, q_norm_g, k_norm_g, w_bm, w_ba, w_out, ln1_g, ln1_b, w_up, w_down,
           ln2_g, ln2_b):
    B, T, _ = x_prompt.shape
    Bd, Td, _ = x_sample.shape
    l = 0

    w = w_in[l]
    o_g = 4 * D_MODEL
    o_a = o_g + 4 * NH_M
    o_mg = o_a + (N_Q + 2 * N_KV) * DH_A
    gate_rows = np.array([4, 5, 6, 7, 12, 13, 14, 15, 0, 1, 2, 3, 8, 9, 10, 11])
    proj_small = (w[:, o_g:o_a].T[gate_rows].astype(BF16),
                  b_gates[l][gate_rows].reshape(4 * NH_M, 1),
                  q_norm_g[l].reshape(1, DH_A),
                  k_norm_g[l].reshape(1, DH_A))
    proj_big = (w[:, :2 * D_MODEL].astype(BF16),
                w[:, 2 * D_MODEL:o_g].T.astype(BF16),
                w[:, o_a:o_mg].astype(BF16),
                w[:, o_a + (N_Q + N_KV) * DH_A:o_mg].T.astype(BF16),
                w[:, o_mg:].astype(BF16))
    tail_w = (w_bm[l].astype(BF16), w_ba[l].astype(BF16), w_out[l].astype(BF16),
              w_up[l].astype(BF16), w_down[l].astype(BF16))
    ln = jnp.stack([ln1_g[l], ln1_b[l], ln2_g[l], ln2_b[l]])
    gnorm = mlstm_norm_g[l].reshape(D_MODEL, 1)

    c_rows = jnp.concatenate([c_ctx[None, :], c, jnp.zeros((MOD_ROWS - 1 - Bd, D_MODEL), F32)], axis=0)
    mod = _modulation(c_rows, w_mod[l], b_mod[l]).reshape(MOD_ROWS, 6, D_MODEL)
    rows_ctx, rows_lat = (0, False), (1, True)

    (qm, km, vmt, somt, gr, qa, ka, vt, sgm, sga, k_new, v_new) = _projection(x_prompt, mod, rows_ctx, proj_small, proj_big, None)
    hm, c_new, n_new, m_new = _mlstm(qm, km, vmt, gr, somt, gnorm, None, True, NH_M)
    ha = _attention(qa, ka, vt, None)
    y_prompt = _tail(x_prompt, mod, rows_ctx, hm, ha, sgm, sga, tail_w, ln, "tail_ctx").reshape(x_prompt.shape)

    (qm, km, vmt, somt, gr, qa, ka, vt, sgm, sga) = _projection(x_sample, mod, rows_lat, proj_small, proj_big, _rope_tables(Td))
    past = cache_k.shape[2]
    init = (state_C[:, l], state_n[:, l].reshape(Bd, 2, NH_M, 1, DH_M), state_m[:, l].reshape(Bd, 2, NH_M, 1, 1))
    hm, = _mlstm(qm, km, vmt, gr, somt, gnorm, init, False, 1)
    vct = jnp.transpose(cache_v[:, l], (0, 2, 3, 1)).astype(BF16)
    vct = jnp.concatenate([vct, jnp.ones((Bd, N_KV, V_ROWS - DH_A, past), BF16)], axis=2)
    ctx_kv = (cache_k[:, l].reshape(Bd, past, N_KV * DH_A).astype(BF16), vct)
    ha = _attention(qa, ka, vt, ctx_kv)
    y_sample = _tail(x_sample, mod, rows_lat, hm, ha, sgm, sga, tail_w, ln, "tail_lat")

    return (y_prompt, y_sample,
            k_new.reshape(B, 1, T, N_KV, DH_A), v_new.reshape(B, 1, T, N_KV, DH_A),
            c_new.reshape(B, 1, 2, NH_M, DH_M, DH_M), n_new.reshape(B, 1, 2, NH_M, DH_M),
            m_new.reshape(B, 1, 2, NH_M))
```

```python
import functools

import jax
import jax.numpy as jnp
import numpy as np
from jax import lax
from jax.experimental import pallas as pl
from jax.experimental.pallas import tpu as pltpu

D_MODEL = 1024
NH_M = 4
DH_M = 256
N_Q = 8
N_KV = 2
G_Q = N_Q // N_KV
DH_A = 128
D_FF = 4 * D_MODEL
GRID_W = 64
N_FREQ = DH_A // 4
ROPE_BASE = 10000.0
EPS = 1e-6
DEPTH = 1
ALPHA = (2 * DEPTH) ** 0.25

CHUNK = 256
TOK_TILE = 512
TAIL_TILE = 512
TAIL_SUB = 256
Q_TILE = 1024
Q_CHAIN = 256
K_TILE = 512
LOG2E = float(np.log2(np.e))
Q_SCALE = DH_A ** -0.5 * LOG2E
ATTN_LOOKAHEAD = 4
VM_ROWS = DH_M + 16
V_ROWS = DH_A + 16
MOD_ROWS = 8

VMEM_WORK_BYTES = 20 * 1024 * 1024
VMEM_FLOOR_BYTES = 56 * 1024 * 1024

F32 = jnp.float32
BF16 = jnp.bfloat16


def _dot(a, b):
    return jnp.dot(a, b, preferred_element_type=F32)


def _dot_nt(a, b):
    return lax.dot_general(a, b, (((1,), (1,)), ((), ())), preferred_element_type=F32)


def _resident(shape):
    nd = len(shape)
    return pl.BlockSpec(shape, lambda *_: (0,) * nd, pipeline_mode=pl.Buffered(1))


def _mod_spec(first_row, per_batch):
    return pl.BlockSpec((1, 6, D_MODEL), (lambda b, t: (first_row + b, 0, 0)) if per_batch
                        else (lambda b, t: (first_row, 0, 0)))


def _call(body, name, grid, semantics, in_specs, args, out_specs, out_shape, scratch=(), let_prefetch=False):
    def window_bytes(spec, a):
        buffers = 2 if spec.pipeline_mode is None else spec.pipeline_mode.buffer_count
        return buffers * int(np.prod(spec.block_shape)) * jnp.dtype(a.dtype).itemsize

    outs, out_sp = (out_shape, out_specs) if isinstance(out_shape, (list, tuple)) else ([out_shape], [out_specs])
    windows = sum(map(window_bytes, in_specs, args)) + sum(map(window_bytes, out_sp, outs))
    held = sum(int(np.prod(s.shape)) * jnp.dtype(s.dtype).itemsize for s in scratch)
    limit = windows + held + VMEM_WORK_BYTES
    if not let_prefetch:
        limit = max(limit, VMEM_FLOOR_BYTES)
    return pl.pallas_call(
        body, grid=grid, in_specs=in_specs, out_specs=out_specs, out_shape=out_shape, scratch_shapes=list(scratch),
        compiler_params=pltpu.CompilerParams(dimension_semantics=semantics, vmem_limit_bytes=limit),
        name=name,
    )(*args)


def _mod_kernel(c_ref, w_ref, b_ref, o_ref):
    c = c_ref[...]
    s = c * jax.nn.sigmoid(c)
    o_ref[...] = _dot(s.astype(BF16), w_ref[...].astype(BF16)) + b_ref[...]


def _modulation(c_rows, w_mod, b_mod):
    n_out = w_mod.shape[1]
    blk = D_MODEL
    in_specs = [pl.BlockSpec((MOD_ROWS, D_MODEL), lambda j: (0, 0)),
                pl.BlockSpec((D_MODEL, blk), lambda j: (0, j)),
                pl.BlockSpec((1, blk), lambda j: (0, j))]
    return _call(_mod_kernel, "modulation", (n_out // blk,), ("parallel",),
                 in_specs, (c_rows, w_mod, b_mod.reshape(1, n_out)),
                 pl.BlockSpec((MOD_ROWS, blk), lambda j: (0, j)), jax.ShapeDtypeStruct((MOD_ROWS, n_out), F32))


def _log_sigmoid(x):
    return jnp.minimum(x, 0.0) - jnp.log1p(jnp.exp(-jnp.abs(x)))


def _cummax_lanes(x, reverse):
    n = x.shape[-1]
    lane = lax.broadcasted_iota(jnp.int32, x.shape, x.ndim - 1)
    step = 1
    while step < n:
        if reverse:
            shifted, valid = pltpu.roll(x, n - step, x.ndim - 1), lane < n - step
        else:
            shifted, valid = pltpu.roll(x, step, x.ndim - 1), lane >= step
        x = jnp.maximum(x, jnp.where(valid, shifted, -jnp.inf))
        step *= 2
    return x


def _rms(t, g):
    return t * lax.rsqrt(jnp.mean(t * t, axis=-1, keepdims=True) + EPS) * g


def _proj_kernel(rope, *refs):
    x_ref, mod_ref = refs[:2]
    mod = mod_ref[0]
    for p in range(x_ref.shape[1] // CHUNK):
        _proj_subtile(rope, slice(p * CHUNK, (p + 1) * CHUNK), mod, refs)


def _proj_subtile(rope, rows, mod, refs):
    if rope:
        (x_ref, mod_ref, wgt_ref, bg_ref, qg_ref, kg_ref, cos_ref, sin_ref, wm_ref, wmt_ref, wa_ref, wvt_ref, wmg_ref,
         qm_o, km_o, vmt_o, somt_o, gr_o, qa_o, ka_o, vt_o, sgm_o, sga_o) = refs
    else:
        (x_ref, mod_ref, wgt_ref, bg_ref, qg_ref, kg_ref, wm_ref, wmt_ref, wa_ref, wvt_ref, wmg_ref,
         qm_o, km_o, vmt_o, somt_o, gr_o, qa_o, ka_o, vt_o, sgm_o, sga_o, kc_o, vc_o) = refs
    tm = CHUNK
    h = (x_ref[0, rows, :] * (1.0 + mod[1:2]) + mod[0:1]).astype(BF16)

    gates = _dot_nt(wgt_ref[...], h) + bg_ref[...]
    lf = _log_sigmoid(gates)

    qm_o[0, rows, :] = _dot(h, wm_ref[:, 0:D_MODEL]).astype(BF16)
    km_o[0, rows, :] = (_dot(h, wm_ref[:, D_MODEL:2 * D_MODEL]) * (DH_M ** -0.5)).astype(BF16)
    for hh in range(NH_M):
        vmt = _dot_nt(wmt_ref[hh * DH_M:(hh + 1) * DH_M, :], h)
        vmt_o[0, hh, 0:DH_M, rows] = vmt.astype(BF16)
        vmt_o[0, hh, DH_M:VM_ROWS, rows] = jnp.ones((VM_ROWS - DH_M, tm), BF16)
    somt_o[0, :, rows] = jax.nn.sigmoid(_dot_nt(wmt_ref[D_MODEL:2 * D_MODEL, :], h)).astype(BF16)

    if rope:
        cos = cos_ref[rows, :]
        sin_s = sin_ref[rows, :]
        lane = lax.broadcasted_iota(jnp.int32, (tm, DH_A), 1)
        first_half = (lane % (2 * N_FREQ)) < N_FREQ

        def rot(t):
            partner = jnp.where(first_half, pltpu.roll(t, DH_A - N_FREQ, 1), pltpu.roll(t, N_FREQ, 1))
            return t * cos + partner * sin_s
    else:
        rot = lambda t: t

    qg = qg_ref[...]
    kg = kg_ref[...]
    q_all = _dot(h, wa_ref[:, 0:N_Q * DH_A])
    k_all = _dot(h, wa_ref[:, N_Q * DH_A:(N_Q + N_KV) * DH_A])
    for g in range(N_Q):
        t = _rms(q_all[:, g * DH_A:(g + 1) * DH_A], qg)
        qa_o[0, rows, g * DH_A:(g + 1) * DH_A] = (rot(t) * Q_SCALE).astype(BF16)
    for g in range(N_KV):
        t = _rms(k_all[:, g * DH_A:(g + 1) * DH_A], kg)
        if not rope:
            kc_o[0, rows, g * DH_A:(g + 1) * DH_A] = t
        ka_o[0, rows, g * DH_A:(g + 1) * DH_A] = rot(t).astype(BF16)
    if not rope:
        off = (N_Q + N_KV) * DH_A
        vc_o[0, rows, :] = _dot(h, wa_ref[:, off:off + N_KV * DH_A])
    vt = _dot_nt(wvt_ref[...], h)
    for g in range(N_KV):
        vt_o[0, g, 0:DH_A, rows] = vt[g * DH_A:(g + 1) * DH_A].astype(BF16)
        vt_o[0, g, DH_A:V_ROWS, rows] = jnp.ones((V_ROWS - DH_A, tm), BF16)

    row = lax.broadcasted_iota(jnp.int32, (tm, tm), 0)
    col = lax.broadcasted_iota(jnp.int32, (tm, tm), 1)
    tri = jnp.where(row <= col, 1.0, 0.0).astype(BF16)
    hi = lf.astype(BF16)
    r1 = lf - hi.astype(F32)
    mid = r1.astype(BF16)
    lo = (r1 - mid.astype(F32)).astype(BF16)
    cum = (_dot(hi, tri) + _dot(mid, tri) + _dot(lo, tri))[0:8]
    lf8 = lf[0:8]
    tot = cum[:, tm - 1:tm]
    rev = tot - cum + lf8
    is_fwd = lax.broadcasted_iota(jnp.int32, (8, tm), 0) < NH_M
    a = jnp.where(is_fwd, cum, rev)
    cc = (gates[8:16] - a) * LOG2E
    a = a * LOG2E
    totb = jnp.broadcast_to(tot * LOG2E, (8, tm))
    c_pre = _cummax_lanes(cc, False)
    c_suf = _cummax_lanes(cc, True)
    for hh in range(NH_M):
        gate_rows = (a[hh:hh + 1], cc[hh:hh + 1], totb[hh:hh + 1],
                     a[NH_M + hh:NH_M + hh + 1], cc[NH_M + hh:NH_M + hh + 1], totb[NH_M + hh:NH_M + hh + 1],
                     c_pre[hh:hh + 1], c_suf[NH_M + hh:NH_M + hh + 1])
        for k, r in enumerate(gate_rows):
            gr_o[0, hh, k:k + 1, rows] = r

    sgm_o[0, rows, :] = jax.nn.sigmoid(_dot(h, wmg_ref[:, 0:D_MODEL])).astype(BF16)
    sga_o[0, rows, :] = jax.nn.sigmoid(_dot(h, wmg_ref[:, D_MODEL:2 * D_MODEL])).astype(BF16)


def _projection(x, mod, mod_rows, small, big, rope_tables):
    B, T, _ = x.shape
    tm = min(TOK_TILE, T)
    nt = T // tm
    rope = rope_tables is not None
    tok = lambda width: pl.BlockSpec((1, tm, width), lambda b, t: (b, t, 0))
    in_specs = [tok(D_MODEL), _mod_spec(*mod_rows)]
    in_specs += [_resident(w.shape) for w in small]
    args = [x, mod, *small]
    if rope:
        in_specs += [pl.BlockSpec((tm, DH_A), lambda b, t: (t, 0))] * 2
        args += list(rope_tables)
    in_specs += [_resident(w.shape) for w in big]
    args += list(big)

    kv_w = N_KV * DH_A
    outs = [((B, T, D_MODEL), BF16, tok(D_MODEL)),
            ((B, T, D_MODEL), BF16, tok(D_MODEL)),
            ((B, NH_M, VM_ROWS, T), BF16,
             pl.BlockSpec((1, NH_M, VM_ROWS, tm), lambda b, t: (b, 0, 0, t))),
            ((B, D_MODEL, T), BF16, pl.BlockSpec((1, D_MODEL, tm), lambda b, t: (b, 0, t))),
            ((B, NH_M, 8, T), F32, pl.BlockSpec((1, NH_M, 8, tm), lambda b, t: (b, 0, 0, t))),
            ((B, T, D_MODEL), BF16, tok(D_MODEL)),
            ((B, T, kv_w), BF16, tok(kv_w)),
            ((B, N_KV, V_ROWS, T), BF16,
             pl.BlockSpec((1, N_KV, V_ROWS, tm), lambda b, t: (b, 0, 0, t))),
            ((B, T, D_MODEL), BF16, tok(D_MODEL)),
            ((B, T, D_MODEL), BF16, tok(D_MODEL))]
    if not rope:
        outs += [((B, T, kv_w), F32, tok(kv_w)), ((B, T, kv_w), F32, tok(kv_w))]

    return _call(functools.partial(_proj_kernel, rope), "projection_lat" if rope else "projection_ctx",
                 (B, nt), ("parallel", "parallel"), in_specs, args,
                 [o[2] for o in outs], [jax.ShapeDtypeStruct(o[0], o[1]) for o in outs])


def _mlstm_kernel(has_init, emit_state, nc, nh, *refs):
    refs = list(refs)
    q_ref, k_ref, vt_ref, g_ref, somt_ref, gn_ref = refs[:6]
    refs = refs[6:]
    if has_init:
        c0_ref, n0_ref, m0_ref = refs[:3]
        refs = refs[3:]
    hm_o = refs[0]
    refs = refs[1:]
    if emit_state:
        c_o, n_o, m_o = refs[:3]
        refs = refs[3:]
    acc_ref, = refs

    L = CHUNK
    row = lax.broadcasted_iota(jnp.int32, (L, L), 0)
    col = lax.broadcasted_iota(jnp.int32, (L, L), 1)
    eye = row == col
    masks = (row <= col, row >= col)

    span = lambda c: slice(c * L, (c + 1) * L)
    feat = lambda hh: slice(hh * DH_M, (hh + 1) * DH_M)
    chunk_of = lambda d, s: s if d == 0 else nc - 1 - s
    steps = [(hh, s) for hh in range(nh) for s in range(nc)]

    def scores(hh, s):
        return [_dot_nt(k_ref[0, span(chunk_of(d, s)), feat(hh)], q_ref[0, span(chunk_of(d, s)), feat(hh)])
                for d in range(2)]

    def finish_chunk(hh, c, ht):
        ht = acc_ref[c] + ht
        hn = ht * lax.rsqrt(jnp.mean(ht * ht, axis=0, keepdims=True) + EPS) * gn_ref[feat(hh), :]
        hm_o[0, span(c), feat(hh)] = (hn * somt_ref[0, feat(hh), span(c)]).T.astype(BF16)

    st_next = scores(*steps[0])
    for idx, (hh, s) in enumerate(steps):
        if s == 0:
            state = [None, None]
            if has_init:
                state = [jnp.concatenate([c0_ref[0, d, hh].T,
                                          jnp.broadcast_to(n0_ref[0, d, hh], (VM_ROWS - DH_M, DH_M))], axis=0)
                         for d in range(2)]
            m_run = [m0_ref[0, d, hh] * LOG2E if has_init else jnp.zeros((1, 1), F32) for d in range(2)]
            arrived = [False] * nc
        st_cur = st_next
        if idx + 1 < len(steps):
            st_next = scores(*steps[idx + 1])
        inter = [None, None]
        if state[0] is not None:
            inter = [_dot_nt(state[d].astype(BF16), q_ref[0, span(chunk_of(d, s)), feat(hh)]) for d in range(2)]
        for d in range(2):
            c = chunk_of(d, s)
            k = k_ref[0, span(c), feat(hh)]
            vt = vt_ref[0, hh, :, span(c)]
            g = g_ref[0, hh, :, span(c)]
            a_row = g[3 * d:3 * d + 1]
            c_row = g[3 * d + 1:3 * d + 2]
            tot = g[3 * d + 2:3 * d + 3, 0:1]
            m_prev = m_run[d]
            c_col = jnp.sum(jnp.where(eye, c_row, 0.0), axis=-1, keepdims=True)
            c_run = g[6 + d:7 + d]

            keep_state = emit_state or s + 1 < nc
            if keep_state:
                c_max = c_run[:, L - 1:L] if d == 0 else c_run[:, 0:1]
                m_new = tot + jnp.maximum(m_prev, c_max)
                wk = jnp.exp2(tot + c_col - m_new).astype(BF16) * k
                upd = _dot(vt, wk)
                new_state = upd if state[d] is None else jnp.exp2(tot + m_prev - m_new) * state[d] + upd

            m_rel = jnp.maximum(m_prev, c_run)
            sp = (st_cur[d] * jnp.exp2(jnp.where(masks[d], c_col, -jnp.inf) - m_rel)).astype(BF16)
            numt = _dot(vt, sp)
            if inter[d] is not None:
                numt = numt + jnp.exp2(m_prev - m_rel) * inter[d]
            den = numt[DH_M:DH_M + 1]
            ht = numt[0:DH_M] * (1.0 / jnp.maximum(jnp.abs(den), jnp.exp2(-(a_row + m_rel))))
            if arrived[c]:
                finish_chunk(hh, c, ht)
            else:
                acc_ref[c] = ht
                arrived[c] = True
            if keep_state:
                state[d] = new_state
                m_run[d] = m_new

        if emit_state and s == nc - 1:
            for d in range(2):
                c_o[0, d, hh] = state[d][0:DH_M].T
                n_o[0, d, hh] = state[d][DH_M:DH_M + 1]
                m_o[0, d, hh] = m_run[d] * (1.0 / LOG2E)


def _mlstm(qm, km, vmt, gr, somt, gnorm_col, init_state, emit_state, nh):
    B, T, _ = qm.shape
    nc = T // CHUNK
    has_init = init_state is not None

    seq = pl.BlockSpec((1, T, nh * DH_M), lambda b, h: (b, 0, h))
    c_spec = pl.BlockSpec((1, 2, nh, DH_M, DH_M), lambda b, h: (b, 0, h, 0, 0))
    n_spec = pl.BlockSpec((1, 2, nh, 1, DH_M), lambda b, h: (b, 0, h, 0, 0))
    m_spec = pl.BlockSpec((1, 2, nh, 1, 1), lambda b, h: (b, 0, h, 0, 0))

    in_specs = [seq, seq,
                pl.BlockSpec((1, nh, VM_ROWS, T), lambda b, h: (b, h, 0, 0)),
                pl.BlockSpec((1, nh, 8, T), lambda b, h: (b, h, 0, 0)),
                pl.BlockSpec((1, nh * DH_M, T), lambda b, h: (b, h, 0)),
                pl.BlockSpec((nh * DH_M, 1), lambda b, h: (h, 0))]
    args = [qm, km, vmt, gr, somt, gnorm_col]
    if has_init:
        in_specs += [c_spec, n_spec, m_spec]
        args += list(init_state)
    out_specs = [seq]
    out_shape = [jax.ShapeDtypeStruct((B, T, D_MODEL), BF16)]
    if emit_state:
        out_specs += [c_spec, n_spec, m_spec]
        out_shape += [jax.ShapeDtypeStruct((B, 2, NH_M, DH_M, DH_M), F32),
                      jax.ShapeDtypeStruct((B, 2, NH_M, 1, DH_M), F32),
                      jax.ShapeDtypeStruct((B, 2, NH_M, 1, 1), F32)]

    return _call(functools.partial(_mlstm_kernel, has_init, emit_state, nc, nh),
                 "mlstm_lat" if has_init else "mlstm_ctx", (B, NH_M // nh), ("parallel", "parallel"),
                 in_specs, args, out_specs, out_shape, [pltpu.VMEM((nc, DH_M, CHUNK), F32)],
                 let_prefetch=not has_init)


def _attn_kernel(n_lat_tiles, has_ctx, nkv, *refs):
    if has_ctx:
        q_ref, k_ref, vt_ref, kc_ref, vct_ref, o_ref = refs
    else:
        q_ref, k_ref, vt_ref, o_ref = refs
    tk = k_ref.shape[1] // n_lat_tiles
    head = lambda h: slice(h * DH_A, (h + 1) * DH_A)
    tiles = [(lambda h, i=i: k_ref[0, i * tk:(i + 1) * tk, head(h)],
              lambda h, i=i: vt_ref[0, h, :, i * tk:(i + 1) * tk]) for i in range(n_lat_tiles)]
    if has_ctx:
        tiles.append((lambda h: kc_ref[0, :, head(h)], lambda h: vct_ref[0, h]))
    groups = [(h, g, r) for h in range(nkv) for r in range(q_ref.shape[1] // Q_CHAIN) for g in range(G_Q)]
    chains = [(t, i) for t in range(len(tiles)) for i in range(len(groups))]
    rows = lambda i: slice(groups[i][2] * Q_CHAIN, (groups[i][2] + 1) * Q_CHAIN)
    cols = lambda i: head(groups[i][0] * G_Q + groups[i][1])

    def scores(t, i):
        return _dot_nt(tiles[t][0](groups[i][0]), q_ref[0, rows(i), cols(i)])

    m = [None] * len(groups)
    acc = [None] * len(groups)
    pending = []
    for idx in range(len(chains) + ATTN_LOOKAHEAD):
        if idx < len(chains):
            pending.append(scores(*chains[idx]))
        if idx < ATTN_LOOKAHEAD:
            continue
        t, i = chains[idx - ATTN_LOOKAHEAD]
        st = pending.pop(0)
        m_tile = jnp.max(st, axis=0, keepdims=True)
        m_new = m_tile if t == 0 else jnp.maximum(m[i], m_tile)
        pv = _dot(tiles[t][1](groups[i][0]), jnp.exp2(st - m_new).astype(BF16))
        acc[i] = pv if t == 0 else jnp.exp2(m[i] - m_new) * acc[i] + pv
        m[i] = m_new

    for i in range(len(groups)):
        out = acc[i][0:DH_A] * (1.0 / acc[i][DH_A:DH_A + 1])
        o_ref[0, rows(i), cols(i)] = out.T.astype(BF16)


def _attention(qa, ka, vt, ctx_kv):
    B, T, _ = qa.shape
    tq = min(Q_TILE, T)
    nq = T // tq
    has_ctx = ctx_kv is not None
    n_lat_tiles = max(1, T // K_TILE)
    nkv = 1 if has_ctx else N_KV
    qspec = pl.BlockSpec((1, tq, nkv * G_Q * DH_A), lambda b, h, i: (b, i, h))
    kspec = lambda tk: pl.BlockSpec((1, tk, nkv * DH_A), lambda b, h, i: (b, 0, h))
    vspec = lambda tk: pl.BlockSpec((1, nkv, V_ROWS, tk), lambda b, h, i: (b, h, 0, 0))
    in_specs = [qspec, kspec(T), vspec(T)]
    args = [qa, ka, vt]
    if has_ctx:
        tc = ctx_kv[0].shape[1]
        in_specs += [kspec(tc), vspec(tc)]
        args += list(ctx_kv)
    return _call(functools.partial(_attn_kernel, n_lat_tiles, has_ctx, nkv),
                 "attention_lat" if has_ctx else "attention_ctx", (B, N_KV // nkv, nq),
                 ("parallel", "parallel", "parallel"), in_specs, args,
                 qspec, jax.ShapeDtypeStruct((B, T, D_MODEL), BF16), let_prefetch=not has_ctx)


def _layer_norm(y, g, b):
    mu = jnp.mean(y, axis=-1, keepdims=True)
    yc = y - mu
    var = jnp.mean(yc * yc, axis=-1, keepdims=True)
    return yc * lax.rsqrt(var + EPS) * g + b


def _tail_kernel(x_ref, mod_ref, hm_ref, ha_ref, sgm_ref, sga_ref, ln_ref,
                 wbm_ref, wba_ref, wout_ref, wup_ref, wdown_ref, o_ref):
    mod = mod_ref[0]
    ln = ln_ref[...]
    n_sub = x_ref.shape[1] // TAIL_SUB
    rows = lambda p: slice(p * TAIL_SUB, (p + 1) * TAIL_SUB)

    def merge(p):
        merged = (sgm_ref[0, rows(p), :] * _dot(hm_ref[0, rows(p), :], wbm_ref[...])
                  + sga_ref[0, rows(p), :] * _dot(ha_ref[0, rows(p), :], wba_ref[...]))
        return ALPHA * x_ref[0, rows(p), :] + mod[2:3] * _dot(merged.astype(BF16), wout_ref[...])

    def ffn(x1):
        h = (x1 * (1.0 + mod[4:5]) + mod[3:4]).astype(BF16)
        ff = jnp.zeros_like(x1)
        for j in range(D_FF // D_MODEL):
            u = jnp.maximum(_dot(h, wup_ref[:, j * D_MODEL:(j + 1) * D_MODEL]), 0.0)
            ff = ff + _dot((u * u).astype(BF16), wdown_ref[j * D_MODEL:(j + 1) * D_MODEL, :])
        return ALPHA * x1 + mod[5:6] * ff

    y1 = [merge(p) for p in range(n_sub)]
    y2 = [ffn(_layer_norm(y1[p], ln[0:1], ln[1:2])) for p in range(n_sub)]
    for p in range(n_sub):
        o_ref[0, rows(p), :] = _layer_norm(y2[p], ln[2:3], ln[3:4])


def _tail(x, mod, mod_rows, hm, ha, sgm, sga, wts, ln, name):
    if not mod_rows[1]:
        x, hm, ha, sgm, sga = (a.reshape(1, -1, D_MODEL) for a in (x, hm, ha, sgm, sga))
    B, T, _ = x.shape
    tm = TAIL_TILE
    tok = pl.BlockSpec((1, tm, D_MODEL), lambda b, t: (b, t, 0))
    in_specs = ([tok, _mod_spec(*mod_rows), tok, tok, tok, tok, _resident(ln.shape)]
                + [_resident(w.shape) for w in wts])
    return _call(_tail_kernel, name, (B, T // tm), ("parallel", "parallel"),
                 in_specs, (x, mod, hm, ha, sgm, sga, ln, *wts), tok, jax.ShapeDtypeStruct((B, T, D_MODEL), F32))


def _rope_tables(n_tokens):
    rows = n_tokens // GRID_W
    row = np.repeat(np.arange(rows), GRID_W)
    col = np.tile(np.arange(GRID_W), rows)
    inv = ROPE_BASE ** (-np.arange(N_FREQ, dtype=np.float64) / N_FREQ)
    ang = np.stack([row, col], -1).astype(np.float64)[..., None] * inv
    ang = np.broadcast_to(ang[:, :, None, :], (n_tokens, 2, 2, N_FREQ))
    sign = np.asarray([-1.0, 1.0])[None, None, :, None]
    return (jnp.asarray(np.cos(ang).reshape(n_tokens, DH_A), F32),
            jnp.asarray((np.sin(ang) * sign).reshape(n_tokens, DH_A), F32))


def kernel(x_prompt, x_sample, cache_k, cache_v, state_C, state_n, state_m, c, c_ctx, w_mod, b_mod, w_in,
           b_gates, mlstm_norm_g, q_norm_g, k_norm_g, w_bm, w_ba, w_out, ln1_g, ln1_b, w_up, w_down,
           ln2_g, ln2_b):
    B, T, _ = x_prompt.shape
    Bd, Td, _ = x_sample.shape
    l = 0

    w = w_in[l]
    o_g = 4 * D_MODEL
    o_a = o_g + 4 * NH_M
    o_mg = o_a + (N_Q + 2 * N_KV) * DH_A
    gate_rows = np.array([4, 5, 6, 7, 12, 13, 14, 15, 0, 1, 2, 3, 8, 9, 10, 11])
    proj_small = (w[:, o_g:o_a].T[gate_rows].astype(BF16),
                  b_gates[l][gate_rows].reshape(4 * NH_M, 1),
                  q_norm_g[l].reshape(1, DH_A),
                  k_norm_g[l].reshape(1, DH_A))
    proj_big = (w[:, :2 * D_MODEL].astype(BF16),
                w[:, 2 * D_MODEL:o_g].T.astype(BF16),
                w[:, o_a:o_mg].astype(BF16),
                w[:, o_a + (N_Q + N_KV) * DH_A:o_mg].T.astype(BF16),
                w[:, o_mg:].astype(BF16))
    tail_w = (w_bm[l].astype(BF16), w_ba[l].astype(BF16), w_out[l].astype(BF16),
              w_up[l].astype(BF16), w_down[l].astype(BF16))
    ln = jnp.stack([ln1_g[l], ln1_b[l], ln2_g[l], ln2_b[l]])
    gnorm = mlstm_norm_g[l].reshape(D_MODEL, 1)

    c_rows = jnp.concatenate([c_ctx[None, :], c, jnp.zeros((MOD_ROWS - 1 - Bd, D_MODEL), F32)], axis=0)
    mod = _modulation(c_rows, w_mod[l], b_mod[l]).reshape(MOD_ROWS, 6, D_MODEL)
    rows_ctx, rows_lat = (0, False), (1, True)

    (qm, km, vmt, somt, gr, qa, ka, vt, sgm, sga, k_new, v_new) = _projection(x_prompt, mod, rows_ctx, proj_small, proj_big, None)
    hm, c_new, n_new, m_new = _mlstm(qm, km, vmt, gr, somt, gnorm, None, True, NH_M)
    ha = _attention(qa, ka, vt, None)
    y_prompt = _tail(x_prompt, mod, rows_ctx, hm, ha, sgm, sga, tail_w, ln, "tail_ctx").reshape(x_prompt.shape)

    (qm, km, vmt, somt, gr, qa, ka, vt, sgm, sga) = _projection(x_sample, mod, rows_lat, proj_small, proj_big, _rope_tables(Td))
    past = cache_k.shape[2]
    init = (state_C[:, l], state_n[:, l].reshape(Bd, 2, NH_M, 1, DH_M), state_m[:, l].reshape(Bd, 2, NH_M, 1, 1))
    hm, = _mlstm(qm, km, vmt, gr, somt, gnorm, init, False, 1)
    vct = jnp.transpose(cache_v[:, l], (0, 2, 3, 1)).astype(BF16)
    vct = jnp.concatenate([vct, jnp.ones((Bd, N_KV, V_ROWS - DH_A, past), BF16)], axis=2)
    ctx_kv = (cache_k[:, l].reshape(Bd, past, N_KV * DH_A).astype(BF16), vct)
    ha = _attention(qa, ka, vt, ctx_kv)
    y_sample = _tail(x_sample, mod, rows_lat, hm, ha, sgm, sga, tail_w, ln, "tail_lat")

    return (y_prompt, y_sample,
            k_new.reshape(B, 1, T, N_KV, DH_A), v_new.reshape(B, 1, T, N_KV, DH_A),
            c_new.reshape(B, 1, 2, NH_M, DH_M, DH_M), n_new.reshape(B, 1, 2, NH_M, DH_M),
            m_new.reshape(B, 1, 2, NH_M))
```

```python
import functools

import jax
import jax.numpy as jnp
import numpy as np
from jax import lax
from jax.experimental import pallas as pl
from jax.experimental.pallas import tpu as pltpu

D_MODEL = 1024
NH_M = 4
DH_M = 256
N_Q = 8
N_KV = 2
G_Q = N_Q // N_KV
DH_A = 128
D_FF = 4 * D_MODEL
GRID_W = 64
N_FREQ = DH_A // 4
ROPE_BASE = 10000.0
EPS = 1e-6
DEPTH = 1
ALPHA = (2 * DEPTH) ** 0.25

CHUNK = 256
TOK_TILE = 512
TAIL_TILE = 512
TAIL_SUB = 256
Q_TILE = 1024
Q_CHAIN = 256
K_TILE = 512
LOG2E = float(np.log2(np.e))
Q_SCALE = DH_A ** -0.5 * LOG2E
ATTN_LOOKAHEAD = 4
VM_ROWS = DH_M + 16
V_ROWS = DH_A + 16
MOD_ROWS = 8

VMEM_WORK_BYTES = 20 * 1024 * 1024
VMEM_FLOOR_BYTES = 56 * 1024 * 1024

F32 = jnp.float32
BF16 = jnp.bfloat16


def _dot(a, b):
    return jnp.dot(a, b, preferred_element_type=F32)


def _dot_nt(a, b):
    return lax.dot_general(a, b, (((1,), (1,)), ((), ())), preferred_element_type=F32)


def _resident(shape):
    nd = len(shape)
    return pl.BlockSpec(shape, lambda *_: (0,) * nd, pipeline_mode=pl.Buffered(1))


def _mod_spec(first_row, per_batch):
    return pl.BlockSpec((1, 6, D_MODEL), (lambda b, t: (first_row + b, 0, 0)) if per_batch
                        else (lambda b, t: (first_row, 0, 0)))


def _call(body, name, grid, semantics, in_specs, args, out_specs, out_shape, scratch=()):
    def window_bytes(spec, a):
        buffers = 2 if spec.pipeline_mode is None else spec.pipeline_mode.buffer_count
        return buffers * int(np.prod(spec.block_shape)) * jnp.dtype(a.dtype).itemsize

    outs, out_sp = (out_shape, out_specs) if isinstance(out_shape, (list, tuple)) else ([out_shape], [out_specs])
    windows = sum(map(window_bytes, in_specs, args)) + sum(map(window_bytes, out_sp, outs))
    held = sum(int(np.prod(s.shape)) * jnp.dtype(s.dtype).itemsize for s in scratch)
    limit = max(windows + held + VMEM_WORK_BYTES, VMEM_FLOOR_BYTES)
    return pl.pallas_call(
        body, grid=grid, in_specs=in_specs, out_specs=out_specs, out_shape=out_shape, scratch_shapes=list(scratch),
        compiler_params=pltpu.CompilerParams(dimension_semantics=semantics, vmem_limit_bytes=limit),
        name=name,
    )(*args)


def _mod_kernel(c_ref, w_ref, b_ref, o_ref):
    c = c_ref[...]
    s = c * jax.nn.sigmoid(c)
    o_ref[...] = _dot(s.astype(BF16), w_ref[...].astype(BF16)) + b_ref[...]


def _modulation(c_rows, w_mod, b_mod):
    n_out = w_mod.shape[1]
    blk = D_MODEL
    in_specs = [pl.BlockSpec((MOD_ROWS, D_MODEL), lambda j: (0, 0)),
                pl.BlockSpec((D_MODEL, blk), lambda j: (0, j)),
                pl.BlockSpec((1, blk), lambda j: (0, j))]
    return _call(_mod_kernel, "modulation", (n_out // blk,), ("parallel",),
                 in_specs, (c_rows, w_mod, b_mod.reshape(1, n_out)),
                 pl.BlockSpec((MOD_ROWS, blk), lambda j: (0, j)), jax.ShapeDtypeStruct((MOD_ROWS, n_out), F32))


def _log_sigmoid(x):
    return jnp.minimum(x, 0.0) - jnp.log1p(jnp.exp(-jnp.abs(x)))


def _cummax_lanes(x, reverse):
    n = x.shape[-1]
    lane = lax.broadcasted_iota(jnp.int32, x.shape, x.ndim - 1)
    step = 1
    while step < n:
        if reverse:
            shifted, valid = pltpu.roll(x, n - step, x.ndim - 1), lane < n - step
        else:
            shifted, valid = pltpu.roll(x, step, x.ndim - 1), lane >= step
        x = jnp.maximum(x, jnp.where(valid, shifted, -jnp.inf))
        step *= 2
    return x


def _rms(t, g):
    return t * lax.rsqrt(jnp.mean(t * t, axis=-1, keepdims=True) + EPS) * g


def _proj_kernel(*refs):
    x_ref, mod_ref = refs[:2]
    mod = mod_ref[0]
    for p in range(x_ref.shape[1] // CHUNK):
        _proj_subtile(True, slice(p * CHUNK, (p + 1) * CHUNK), mod, refs)


def _context_front_kernel(*refs):
    (x_ref, mod_ref, wgt_ref, bg_ref, qg_ref, kg_ref, gn_ref, wm_ref, wmt_ref, wa_ref, wvt_ref, wmg_ref,
     sgm_o, sga_o, kc_o, vc_o, hm_o, c_o, n_o, m_o, ha_o,
     qm_s, km_s, vmt_s, somt_s, gr_s, qa_s, ka_s, vt_s, acc_s) = refs
    proj_refs = (x_ref, mod_ref, wgt_ref, bg_ref, qg_ref, kg_ref, wm_ref, wmt_ref, wa_ref, wvt_ref, wmg_ref,
                 qm_s, km_s, vmt_s, somt_s, gr_s, qa_s, ka_s, vt_s, sgm_o, sga_o, kc_o, vc_o)
    _proj_subtile(False, slice(0, CHUNK), mod_ref[0], proj_refs)
    _mlstm_kernel(False, True, 1, NH_M, qm_s, km_s, vmt_s, gr_s, somt_s, gn_ref, hm_o, c_o, n_o, m_o, acc_s)
    _attn_kernel(1, False, N_KV, qa_s, ka_s, vt_s, ha_o)


def _proj_subtile(rope, rows, mod, refs):
    if rope:
        (x_ref, mod_ref, wgt_ref, bg_ref, qg_ref, kg_ref, cos_ref, sin_ref, wm_ref, wmt_ref, wa_ref, wvt_ref, wmg_ref,
         qm_o, km_o, vmt_o, somt_o, gr_o, qa_o, ka_o, vt_o, sgm_o, sga_o) = refs
    else:
        (x_ref, mod_ref, wgt_ref, bg_ref, qg_ref, kg_ref, wm_ref, wmt_ref, wa_ref, wvt_ref, wmg_ref,
         qm_o, km_o, vmt_o, somt_o, gr_o, qa_o, ka_o, vt_o, sgm_o, sga_o, kc_o, vc_o) = refs
    tm = CHUNK
    h = (x_ref[0, rows, :] * (1.0 + mod[1:2]) + mod[0:1]).astype(BF16)

    gates = _dot_nt(wgt_ref[...], h) + bg_ref[...]
    lf = _log_sigmoid(gates)

    qm_o[0, rows, :] = _dot(h, wm_ref[:, 0:D_MODEL]).astype(BF16)
    km_o[0, rows, :] = (_dot(h, wm_ref[:, D_MODEL:2 * D_MODEL]) * (DH_M ** -0.5)).astype(BF16)
    for hh in range(NH_M):
        vmt = _dot_nt(wmt_ref[hh * DH_M:(hh + 1) * DH_M, :], h)
        vmt_o[0, hh, 0:DH_M, rows] = vmt.astype(BF16)
        vmt_o[0, hh, DH_M:VM_ROWS, rows] = jnp.ones((VM_ROWS - DH_M, tm), BF16)
    somt_o[0, :, rows] = jax.nn.sigmoid(_dot_nt(wmt_ref[D_MODEL:2 * D_MODEL, :], h)).astype(BF16)

    if rope:
        cos = cos_ref[rows, :]
        sin_s = sin_ref[rows, :]
        lane = lax.broadcasted_iota(jnp.int32, (tm, DH_A), 1)
        first_half = (lane % (2 * N_FREQ)) < N_FREQ

        def rot(t):
            partner = jnp.where(first_half, pltpu.roll(t, DH_A - N_FREQ, 1), pltpu.roll(t, N_FREQ, 1))
            return t * cos + partner * sin_s
    else:
        rot = lambda t: t

    qg = qg_ref[...]
    kg = kg_ref[...]
    q_all = _dot(h, wa_ref[:, 0:N_Q * DH_A])
    k_all = _dot(h, wa_ref[:, N_Q * DH_A:(N_Q + N_KV) * DH_A])
    for g in range(N_Q):
        t = _rms(q_all[:, g * DH_A:(g + 1) * DH_A], qg)
        qa_o[0, rows, g * DH_A:(g + 1) * DH_A] = (rot(t) * Q_SCALE).astype(BF16)
    for g in range(N_KV):
        t = _rms(k_all[:, g * DH_A:(g + 1) * DH_A], kg)
        if not rope:
            kc_o[0, rows, g * DH_A:(g + 1) * DH_A] = t
        ka_o[0, rows, g * DH_A:(g + 1) * DH_A] = rot(t).astype(BF16)
    if not rope:
        off = (N_Q + N_KV) * DH_A
        vc_o[0, rows, :] = _dot(h, wa_ref[:, off:off + N_KV * DH_A])
    vt = _dot_nt(wvt_ref[...], h)
    for g in range(N_KV):
        vt_o[0, g, 0:DH_A, rows] = vt[g * DH_A:(g + 1) * DH_A].astype(BF16)
        vt_o[0, g, DH_A:V_ROWS, rows] = jnp.ones((V_ROWS - DH_A, tm), BF16)

    row = lax.broadcasted_iota(jnp.int32, (tm, tm), 0)
    col = lax.broadcasted_iota(jnp.int32, (tm, tm), 1)
    tri = jnp.where(row <= col, 1.0, 0.0).astype(BF16)
    hi = lf.astype(BF16)
    r1 = lf - hi.astype(F32)
    mid = r1.astype(BF16)
    lo = (r1 - mid.astype(F32)).astype(BF16)
    cum = (_dot(hi, tri) + _dot(mid, tri) + _dot(lo, tri))[0:8]
    lf8 = lf[0:8]
    tot = cum[:, tm - 1:tm]
    rev = tot - cum + lf8
    is_fwd = lax.broadcasted_iota(jnp.int32, (8, tm), 0) < NH_M
    a = jnp.where(is_fwd, cum, rev)
    cc = (gates[8:16] - a) * LOG2E
    a = a * LOG2E
    totb = jnp.broadcast_to(tot * LOG2E, (8, tm))
    c_pre = _cummax_lanes(cc, False)
    c_suf = _cummax_lanes(cc, True)
    for hh in range(NH_M):
        gate_rows = (a[hh:hh + 1], cc[hh:hh + 1], totb[hh:hh + 1],
                     a[NH_M + hh:NH_M + hh + 1], cc[NH_M + hh:NH_M + hh + 1], totb[NH_M + hh:NH_M + hh + 1],
                     c_pre[hh:hh + 1], c_suf[NH_M + hh:NH_M + hh + 1])
        for k, r in enumerate(gate_rows):
            gr_o[0, hh, k:k + 1, rows] = r

    sgm_o[0, rows, :] = jax.nn.sigmoid(_dot(h, wmg_ref[:, 0:D_MODEL])).astype(BF16)
    sga_o[0, rows, :] = jax.nn.sigmoid(_dot(h, wmg_ref[:, D_MODEL:2 * D_MODEL])).astype(BF16)


def _projection(x, mod, mod_rows, small, big, rope_tables):
    B, T, _ = x.shape
    tm = min(TOK_TILE, T)
    nt = T // tm
    tok = lambda width: pl.BlockSpec((1, tm, width), lambda b, t: (b, t, 0))
    in_specs = [tok(D_MODEL), _mod_spec(*mod_rows)]
    in_specs += [_resident(w.shape) for w in small]
    in_specs += [pl.BlockSpec((tm, DH_A), lambda b, t: (t, 0))] * 2
    in_specs += [_resident(w.shape) for w in big]
    args = [x, mod, *small, *rope_tables, *big]

    kv_w = N_KV * DH_A
    outs = [((B, T, D_MODEL), BF16, tok(D_MODEL)),
            ((B, T, D_MODEL), BF16, tok(D_MODEL)),
            ((B, NH_M, VM_ROWS, T), BF16,
             pl.BlockSpec((1, NH_M, VM_ROWS, tm), lambda b, t: (b, 0, 0, t))),
            ((B, D_MODEL, T), BF16, pl.BlockSpec((1, D_MODEL, tm), lambda b, t: (b, 0, t))),
            ((B, NH_M, 8, T), F32, pl.BlockSpec((1, NH_M, 8, tm), lambda b, t: (b, 0, 0, t))),
            ((B, T, D_MODEL), BF16, tok(D_MODEL)),
            ((B, T, kv_w), BF16, tok(kv_w)),
            ((B, N_KV, V_ROWS, T), BF16,
             pl.BlockSpec((1, N_KV, V_ROWS, tm), lambda b, t: (b, 0, 0, t))),
            ((B, T, D_MODEL), BF16, tok(D_MODEL)),
            ((B, T, D_MODEL), BF16, tok(D_MODEL))]
    return _call(_proj_kernel, "projection_lat", (B, nt), ("parallel", "parallel"), in_specs, args,
                 [o[2] for o in outs], [jax.ShapeDtypeStruct(o[0], o[1]) for o in outs])


def _context_front(x, mod, mod_rows, small, big, gnorm_col):
    B, T, _ = x.shape
    assert T == CHUNK, "the fused context front handles one chunk per batch row"
    kv_w = N_KV * DH_A
    tok = lambda width: pl.BlockSpec((1, T, width), lambda b, t: (b, 0, 0))
    state = lambda *tail: pl.BlockSpec((1, 2, NH_M) + tail, lambda b, t: (b, 0, 0, 0, 0))
    in_specs = ([tok(D_MODEL), _mod_spec(*mod_rows)] + [_resident(w.shape) for w in small]
                + [_resident(gnorm_col.shape)] + [_resident(w.shape) for w in big])
    outs = [((B, T, D_MODEL), BF16, tok(D_MODEL)), ((B, T, D_MODEL), BF16, tok(D_MODEL)),
            ((B, T, kv_w), F32, tok(kv_w)), ((B, T, kv_w), F32, tok(kv_w)),
            ((B, T, D_MODEL), BF16, tok(D_MODEL)),
            ((B, 2, NH_M, DH_M, DH_M), F32, state(DH_M, DH_M)),
            ((B, 2, NH_M, 1, DH_M), F32, state(1, DH_M)),
            ((B, 2, NH_M, 1, 1), F32, state(1, 1)),
            ((B, T, D_MODEL), BF16, tok(D_MODEL))]
    scratch = [pltpu.VMEM((1, T, D_MODEL), BF16), pltpu.VMEM((1, T, D_MODEL), BF16),
               pltpu.VMEM((1, NH_M, VM_ROWS, T), BF16), pltpu.VMEM((1, D_MODEL, T), BF16),
               pltpu.VMEM((1, NH_M, 8, T), F32),
               pltpu.VMEM((1, T, D_MODEL), BF16), pltpu.VMEM((1, T, kv_w), BF16),
               pltpu.VMEM((1, N_KV, V_ROWS, T), BF16),
               pltpu.VMEM((1, DH_M, CHUNK), F32)]
    return _call(_context_front_kernel, "context_front", (B, 1), ("parallel", "arbitrary"), in_specs,
                 [x, mod, *small, gnorm_col, *big],
                 [o[2] for o in outs], [jax.ShapeDtypeStruct(o[0], o[1]) for o in outs], scratch)


def _mlstm_kernel(has_init, emit_state, nc, nh, *refs):
    refs = list(refs)
    q_ref, k_ref, vt_ref, g_ref, somt_ref, gn_ref = refs[:6]
    refs = refs[6:]
    if has_init:
        c0_ref, n0_ref, m0_ref = refs[:3]
        refs = refs[3:]
    hm_o = refs[0]
    refs = refs[1:]
    if emit_state:
        c_o, n_o, m_o = refs[:3]
        refs = refs[3:]
    acc_ref, = refs

    L = CHUNK
    row = lax.broadcasted_iota(jnp.int32, (L, L), 0)
    col = lax.broadcasted_iota(jnp.int32, (L, L), 1)
    eye = row == col
    masks = (row <= col, row >= col)

    span = lambda c: slice(c * L, (c + 1) * L)
    feat = lambda hh: slice(hh * DH_M, (hh + 1) * DH_M)
    chunk_of = lambda d, s: s if d == 0 else nc - 1 - s
    steps = [(hh, s) for hh in range(nh) for s in range(nc)]

    def scores(hh, s):
        return [_dot_nt(k_ref[0, span(chunk_of(d, s)), feat(hh)], q_ref[0, span(chunk_of(d, s)), feat(hh)])
                for d in range(2)]

    def finish_chunk(hh, c, ht):
        ht = acc_ref[c] + ht
        hn = ht * lax.rsqrt(jnp.mean(ht * ht, axis=0, keepdims=True) + EPS) * gn_ref[feat(hh), :]
        hm_o[0, span(c), feat(hh)] = (hn * somt_ref[0, feat(hh), span(c)]).T.astype(BF16)

    st_next = scores(*steps[0])
    for idx, (hh, s) in enumerate(steps):
        if s == 0:
            state = [None, None]
            if has_init:
                state = [jnp.concatenate([c0_ref[0, d, hh].T,
                                          jnp.broadcast_to(n0_ref[0, d, hh], (VM_ROWS - DH_M, DH_M))], axis=0)
                         for d in range(2)]
            m_run = [m0_ref[0, d, hh] * LOG2E if has_init else jnp.zeros((1, 1), F32) for d in range(2)]
            arrived = [False] * nc
        st_cur = st_next
        if idx + 1 < len(steps):
            st_next = scores(*steps[idx + 1])
        inter = [None, None]
        if state[0] is not None:
            inter = [_dot_nt(state[d].astype(BF16), q_ref[0, span(chunk_of(d, s)), feat(hh)]) for d in range(2)]
        for d in range(2):
            c = chunk_of(d, s)
            k = k_ref[0, span(c), feat(hh)]
            vt = vt_ref[0, hh, :, span(c)]
            g = g_ref[0, hh, :, span(c)]
            a_row = g[3 * d:3 * d + 1]
            c_row = g[3 * d + 1:3 * d + 2]
            tot = g[3 * d + 2:3 * d + 3, 0:1]
            m_prev = m_run[d]
            c_col = jnp.sum(jnp.where(eye, c_row, 0.0), axis=-1, keepdims=True)
            c_run = g[6 + d:7 + d]

            keep_state = emit_state or s + 1 < nc
            if keep_state:
                c_max = c_run[:, L - 1:L] if d == 0 else c_run[:, 0:1]
                m_new = tot + jnp.maximum(m_prev, c_max)
                wk = jnp.exp2(tot + c_col - m_new).astype(BF16) * k
                upd = _dot(vt, wk)
                new_state = upd if state[d] is None else jnp.exp2(tot + m_prev - m_new) * state[d] + upd

            m_rel = jnp.maximum(m_prev, c_run)
            sp = (st_cur[d] * jnp.exp2(jnp.where(masks[d], c_col, -jnp.inf) - m_rel)).astype(BF16)
            numt = _dot(vt, sp)
            if inter[d] is not None:
                numt = numt + jnp.exp2(m_prev - m_rel) * inter[d]
            den = numt[DH_M:DH_M + 1]
            ht = numt[0:DH_M] * (1.0 / jnp.maximum(jnp.abs(den), jnp.exp2(-(a_row + m_rel))))
            if arrived[c]:
                finish_chunk(hh, c, ht)
            else:
                acc_ref[c] = ht
                arrived[c] = True
            if keep_state:
                state[d] = new_state
                m_run[d] = m_new

        if emit_state and s == nc - 1:
            for d in range(2):
                c_o[0, d, hh] = state[d][0:DH_M].T
                n_o[0, d, hh] = state[d][DH_M:DH_M + 1]
                m_o[0, d, hh] = m_run[d] * (1.0 / LOG2E)


def _mlstm(qm, km, vmt, gr, somt, gnorm_col, init_state, emit_state, nh):
    B, T, _ = qm.shape
    nc = T // CHUNK
    has_init = init_state is not None

    seq = pl.BlockSpec((1, T, nh * DH_M), lambda b, h: (b, 0, h))
    c_spec = pl.BlockSpec((1, 2, nh, DH_M, DH_M), lambda b, h: (b, 0, h, 0, 0))
    n_spec = pl.BlockSpec((1, 2, nh, 1, DH_M), lambda b, h: (b, 0, h, 0, 0))
    m_spec = pl.BlockSpec((1, 2, nh, 1, 1), lambda b, h: (b, 0, h, 0, 0))

    in_specs = [seq, seq,
                pl.BlockSpec((1, nh, VM_ROWS, T), lambda b, h: (b, h, 0, 0)),
                pl.BlockSpec((1, nh, 8, T), lambda b, h: (b, h, 0, 0)),
                pl.BlockSpec((1, nh * DH_M, T), lambda b, h: (b, h, 0)),
                pl.BlockSpec((nh * DH_M, 1), lambda b, h: (h, 0))]
    args = [qm, km, vmt, gr, somt, gnorm_col]
    if has_init:
        in_specs += [c_spec, n_spec, m_spec]
        args += list(init_state)
    out_specs = [seq]
    out_shape = [jax.ShapeDtypeStruct((B, T, D_MODEL), BF16)]
    if emit_state:
        out_specs += [c_spec, n_spec, m_spec]
        out_shape += [jax.ShapeDtypeStruct((B, 2, NH_M, DH_M, DH_M), F32),
                      jax.ShapeDtypeStruct((B, 2, NH_M, 1, DH_M), F32),
                      jax.ShapeDtypeStruct((B, 2, NH_M, 1, 1), F32)]

    return _call(functools.partial(_mlstm_kernel, has_init, emit_state, nc, nh),
                 "mlstm_lat" if has_init else "mlstm_ctx", (B, NH_M // nh), ("parallel", "parallel"),
                 in_specs, args, out_specs, out_shape, [pltpu.VMEM((nc, DH_M, CHUNK), F32)])


def _attn_kernel(n_lat_tiles, has_ctx, nkv, *refs):
    if has_ctx:
        q_ref, k_ref, vt_ref, kc_ref, vct_ref, o_ref = refs
    else:
        q_ref, k_ref, vt_ref, o_ref = refs
    tk = k_ref.shape[1] // n_lat_tiles
    head = lambda h: slice(h * DH_A, (h + 1) * DH_A)
    tiles = [(lambda h, i=i: k_ref[0, i * tk:(i + 1) * tk, head(h)],
              lambda h, i=i: vt_ref[0, h, :, i * tk:(i + 1) * tk]) for i in range(n_lat_tiles)]
    if has_ctx:
        tiles.append((lambda h: kc_ref[0, :, head(h)], lambda h: vct_ref[0, h]))
    groups = [(h, g, r) for h in range(nkv) for r in range(q_ref.shape[1] // Q_CHAIN) for g in range(G_Q)]
    chains = [(t, i) for t in range(len(tiles)) for i in range(len(groups))]
    rows = lambda i: slice(groups[i][2] * Q_CHAIN, (groups[i][2] + 1) * Q_CHAIN)
    cols = lambda i: head(groups[i][0] * G_Q + groups[i][1])

    def scores(t, i):
        return _dot_nt(tiles[t][0](groups[i][0]), q_ref[0, rows(i), cols(i)])

    m = [None] * len(groups)
    acc = [None] * len(groups)
    pending = []
    for idx in range(len(chains) + ATTN_LOOKAHEAD):
        if idx < len(chains):
            pending.append(scores(*chains[idx]))
        if idx < ATTN_LOOKAHEAD:
            continue
        t, i = chains[idx - ATTN_LOOKAHEAD]
        st = pending.pop(0)
        m_tile = jnp.max(st, axis=0, keepdims=True)
        m_new = m_tile if t == 0 else jnp.maximum(m[i], m_tile)
        pv = _dot(tiles[t][1](groups[i][0]), jnp.exp2(st - m_new).astype(BF16))
        acc[i] = pv if t == 0 else jnp.exp2(m[i] - m_new) * acc[i] + pv
        m[i] = m_new

    for i in range(len(groups)):
        out = acc[i][0:DH_A] * (1.0 / acc[i][DH_A:DH_A + 1])
        o_ref[0, rows(i), cols(i)] = out.T.astype(BF16)


def _attention(qa, ka, vt, ctx_kv):
    B, T, _ = qa.shape
    tq = min(Q_TILE, T)
    nq = T // tq
    has_ctx = ctx_kv is not None
    n_lat_tiles = max(1, T // K_TILE)
    nkv = 1 if has_ctx else N_KV
    qspec = pl.BlockSpec((1, tq, nkv * G_Q * DH_A), lambda b, h, i: (b, i, h))
    kspec = lambda tk: pl.BlockSpec((1, tk, nkv * DH_A), lambda b, h, i: (b, 0, h))
    vspec = lambda tk: pl.BlockSpec((1, nkv, V_ROWS, tk), lambda b, h, i: (b, h, 0, 0))
    in_specs = [qspec, kspec(T), vspec(T)]
    args = [qa, ka, vt]
    if has_ctx:
        tc = ctx_kv[0].shape[1]
        in_specs += [kspec(tc), vspec(tc)]
        args += list(ctx_kv)
    return _call(functools.partial(_attn_kernel, n_lat_tiles, has_ctx, nkv),
                 "attention_lat" if has_ctx else "attention_ctx", (B, N_KV // nkv, nq),
                 ("parallel", "parallel", "parallel"), in_specs, args,
                 qspec, jax.ShapeDtypeStruct((B, T, D_MODEL), BF16))


def _layer_norm(y, g, b):
    mu = jnp.mean(y, axis=-1, keepdims=True)
    yc = y - mu
    var = jnp.mean(yc * yc, axis=-1, keepdims=True)
    return yc * lax.rsqrt(var + EPS) * g + b


def _tail_kernel(x_ref, mod_ref, hm_ref, ha_ref, sgm_ref, sga_ref, ln_ref,
                 wbm_ref, wba_ref, wout_ref, wup_ref, wdown_ref, o_ref):
    mod = mod_ref[0]
    ln = ln_ref[...]
    n_sub = x_ref.shape[1] // TAIL_SUB
    rows = lambda p: slice(p * TAIL_SUB, (p + 1) * TAIL_SUB)

    def merge(p):
        merged = (sgm_ref[0, rows(p), :] * _dot(hm_ref[0, rows(p), :], wbm_ref[...])
                  + sga_ref[0, rows(p), :] * _dot(ha_ref[0, rows(p), :], wba_ref[...]))
        return ALPHA * x_ref[0, rows(p), :] + mod[2:3] * _dot(merged.astype(BF16), wout_ref[...])

    def ffn(x1):
        h = (x1 * (1.0 + mod[4:5]) + mod[3:4]).astype(BF16)
        ff = jnp.zeros_like(x1)
        for j in range(D_FF // D_MODEL):
            u = jnp.maximum(_dot(h, wup_ref[:, j * D_MODEL:(j + 1) * D_MODEL]), 0.0)
            ff = ff + _dot((u * u).astype(BF16), wdown_ref[j * D_MODEL:(j + 1) * D_MODEL, :])
        return ALPHA * x1 + mod[5:6] * ff

    y1 = [merge(p) for p in range(n_sub)]
    y2 = [ffn(_layer_norm(y1[p], ln[0:1], ln[1:2])) for p in range(n_sub)]
    for p in range(n_sub):
        o_ref[0, rows(p), :] = _layer_norm(y2[p], ln[2:3], ln[3:4])


def _tail(x, mod, mod_rows, hm, ha, sgm, sga, wts, ln, name):
    if not mod_rows[1]:
        x, hm, ha, sgm, sga = (a.reshape(1, -1, D_MODEL) for a in (x, hm, ha, sgm, sga))
    B, T, _ = x.shape
    tm = TAIL_TILE
    tok = pl.BlockSpec((1, tm, D_MODEL), lambda b, t: (b, t, 0))
    in_specs = ([tok, _mod_spec(*mod_rows), tok, tok, tok, tok, _resident(ln.shape)]
                + [_resident(w.shape) for w in wts])
    return _call(_tail_kernel, name, (B, T // tm), ("parallel", "parallel"),
                 in_specs, (x, mod, hm, ha, sgm, sga, ln, *wts), tok, jax.ShapeDtypeStruct((B, T, D_MODEL), F32))


def _rope_tables(n_tokens):
    rows = n_tokens // GRID_W
    row = np.repeat(np.arange(rows), GRID_W)
    col = np.tile(np.arange(GRID_W), rows)
    inv = ROPE_BASE ** (-np.arange(N_FREQ, dtype=np.float64) / N_FREQ)
    ang = np.stack([row, col], -1).astype(np.float64)[..., None] * inv
    ang = np.broadcast_to(ang[:, :, None, :], (n_tokens, 2, 2, N_FREQ))
    sign = np.asarray([-1.0, 1.0])[None, None, :, None]
    return (jnp.asarray(np.cos(ang).reshape(n_tokens, DH_A), F32),
            jnp.asarray((np.sin(ang) * sign).reshape(n_tokens, DH_A), F32))


def kernel(x_prompt, x_sample, cache_k, cache_v, state_C, state_n, state_m, c, c_ctx, w_mod, b_mod, w_in,
           b_gates, mlstm_norm_g, q_norm_g, k_norm_g, w_bm, w_ba, w_out, ln1_g, ln1_b, w_up, w_down,
           ln2_g, ln2_b):
    B, T, _ = x_prompt.shape
    Bd, Td, _ = x_sample.shape
    l = 0

    w = w_in[l]
    o_g = 4 * D_MODEL
    o_a = o_g + 4 * NH_M
    o_mg = o_a + (N_Q + 2 * N_KV) * DH_A
    gate_rows = np.array([4, 5, 6, 7, 12, 13, 14, 15, 0, 1, 2, 3, 8, 9, 10, 11])
    proj_small = (w[:, o_g:o_a].T[gate_rows].astype(BF16),
                  b_gates[l][gate_rows].reshape(4 * NH_M, 1),
                  q_norm_g[l].reshape(1, DH_A),
                  k_norm_g[l].reshape(1, DH_A))
    proj_big = (w[:, :2 * D_MODEL].astype(BF16),
                w[:, 2 * D_MODEL:o_g].T.astype(BF16),
                w[:, o_a:o_mg].astype(BF16),
                w[:, o_a + (N_Q + N_KV) * DH_A:o_mg].T.astype(BF16),
                w[:, o_mg:].astype(BF16))
    tail_w = (w_bm[l].astype(BF16), w_ba[l].astype(BF16), w_out[l].astype(BF16),
              w_up[l].astype(BF16), w_down[l].astype(BF16))
    ln = jnp.stack([ln1_g[l], ln1_b[l], ln2_g[l], ln2_b[l]])
    gnorm = mlstm_norm_g[l].reshape(D_MODEL, 1)

    c_rows = jnp.concatenate([c_ctx[None, :], c, jnp.zeros((MOD_ROWS - 1 - Bd, D_MODEL), F32)], axis=0)
    mod = _modulation(c_rows, w_mod[l], b_mod[l]).reshape(MOD_ROWS, 6, D_MODEL)
    rows_ctx, rows_lat = (0, False), (1, True)

    (sgm, sga, k_new, v_new, hm, c_new, n_new, m_new, ha) = _context_front(x_prompt, mod, rows_ctx, proj_small,
                                                                          proj_big, gnorm)
    y_prompt = _tail(x_prompt, mod, rows_ctx, hm, ha, sgm, sga, tail_w, ln, "tail_ctx").reshape(x_prompt.shape)

    (qm, km, vmt, somt, gr, qa, ka, vt, sgm, sga) = _projection(x_sample, mod, rows_lat, proj_small, proj_big, _rope_tables(Td))
    past = cache_k.shape[2]
    init = (state_C[:, l], state_n[:, l].reshape(Bd, 2, NH_M, 1, DH_M), state_m[:, l].reshape(Bd, 2, NH_M, 1, 1))
    hm, = _mlstm(qm, km, vmt, gr, somt, gnorm, init, False, 1)
    vct = jnp.transpose(cache_v[:, l], (0, 2, 3, 1)).astype(BF16)
    vct = jnp.concatenate([vct, jnp.ones((Bd, N_KV, V_ROWS - DH_A, past), BF16)], axis=2)
    ctx_kv = (cache_k[:, l].reshape(Bd, past, N_KV * DH_A).astype(BF16), vct)
    ha = _attention(qa, ka, vt, ctx_kv)
    y_sample = _tail(x_sample, mod, rows_lat, hm, ha, sgm, sga, tail_w, ln, "tail_lat")

    return (y_prompt, y_sample,
            k_new.reshape(B, 1, T, N_KV, DH_A), v_new.reshape(B, 1, T, N_KV, DH_A),
            c_new.reshape(B, 1, 2, NH_M, DH_M, DH_M), n_new.reshape(B, 1, 2, NH_M, DH_M),
            m_new.reshape(B, 1, 2, NH_M))
```

```python
import functools

import jax
import jax.numpy as jnp
import numpy as np
from jax import lax
from jax.experimental import pallas as pl
from jax.experimental.pallas import tpu as pltpu

D_MODEL = 1024
NH_M = 4
DH_M = 256
N_Q = 8
N_KV = 2
G_Q = N_Q // N_KV
DH_A = 128
D_FF = 4 * D_MODEL
GRID_W = 64
N_FREQ = DH_A // 4
ROPE_BASE = 10000.0
EPS = 1e-6
DEPTH = 1
ALPHA = (2 * DEPTH) ** 0.25

CHUNK = 256
TOK_TILE = 512
TAIL_TILE = 512
TAIL_SUB = 256
Q_TILE = 1024
Q_CHAIN = 256
K_TILE = 512
LOG2E = float(np.log2(np.e))
Q_SCALE = DH_A ** -0.5 * LOG2E
ATTN_LOOKAHEAD = 4
VM_ROWS = DH_M + 16
V_ROWS = DH_A + 16
MOD_ROWS = 8

VMEM_WORK_BYTES = 20 * 1024 * 1024
VMEM_FLOOR_BYTES = 56 * 1024 * 1024

F32 = jnp.float32
BF16 = jnp.bfloat16


def _dot(a, b):
    return jnp.dot(a, b, preferred_element_type=F32)


def _dot_nt(a, b):
    return lax.dot_general(a, b, (((1,), (1,)), ((), ())), preferred_element_type=F32)


def _resident(shape):
    nd = len(shape)
    return pl.BlockSpec(shape, lambda *_: (0,) * nd, pipeline_mode=pl.Buffered(1))


def _mod_spec(first_row, per_batch):
    return pl.BlockSpec((1, 6, D_MODEL), (lambda b, t: (first_row + b, 0, 0)) if per_batch
                        else (lambda b, t: (first_row, 0, 0)))


def _call(body, name, grid, semantics, in_specs, args, out_specs, out_shape, scratch=()):
    def window_bytes(spec, a):
        buffers = 2 if spec.pipeline_mode is None else spec.pipeline_mode.buffer_count
        return buffers * int(np.prod(spec.block_shape)) * jnp.dtype(a.dtype).itemsize

    outs, out_sp = (out_shape, out_specs) if isinstance(out_shape, (list, tuple)) else ([out_shape], [out_specs])
    windows = sum(map(window_bytes, in_specs, args)) + sum(map(window_bytes, out_sp, outs))
    held = sum(int(np.prod(s.shape)) * jnp.dtype(s.dtype).itemsize for s in scratch)
    limit = max(windows + held + VMEM_WORK_BYTES, VMEM_FLOOR_BYTES)
    return pl.pallas_call(
        body, grid=grid, in_specs=in_specs, out_specs=out_specs, out_shape=out_shape, scratch_shapes=list(scratch),
        compiler_params=pltpu.CompilerParams(dimension_semantics=semantics, vmem_limit_bytes=limit),
        name=name,
    )(*args)


def _mod_kernel(c_ref, w_ref, b_ref, o_ref):
    c = c_ref[...]
    s = c * jax.nn.sigmoid(c)
    o_ref[...] = _dot(s.astype(BF16), w_ref[...].astype(BF16)) + b_ref[...]


def _modulation(c_rows, w_mod, b_mod):
    n_out = w_mod.shape[1]
    blk = D_MODEL
    in_specs = [pl.BlockSpec((MOD_ROWS, D_MODEL), lambda j: (0, 0)),
                pl.BlockSpec((D_MODEL, blk), lambda j: (0, j)),
                pl.BlockSpec((1, blk), lambda j: (0, j))]
    return _call(_mod_kernel, "modulation", (n_out // blk,), ("parallel",),
                 in_specs, (c_rows, w_mod, b_mod.reshape(1, n_out)),
                 pl.BlockSpec((MOD_ROWS, blk), lambda j: (0, j)), jax.ShapeDtypeStruct((MOD_ROWS, n_out), F32))


def _log_sigmoid(x):
    return jnp.minimum(x, 0.0) - jnp.log1p(jnp.exp(-jnp.abs(x)))


def _cummax_lanes(x, reverse):
    n = x.shape[-1]
    lane = lax.broadcasted_iota(jnp.int32, x.shape, x.ndim - 1)
    step = 1
    while step < n:
        if reverse:
            shifted, valid = pltpu.roll(x, n - step, x.ndim - 1), lane < n - step
        else:
            shifted, valid = pltpu.roll(x, step, x.ndim - 1), lane >= step
        x = jnp.maximum(x, jnp.where(valid, shifted, -jnp.inf))
        step *= 2
    return x


def _rms(t, g):
    return t * lax.rsqrt(jnp.mean(t * t, axis=-1, keepdims=True) + EPS) * g


def _proj_kernel(*refs):
    x_ref, mod_ref = refs[:2]
    mod = mod_ref[0]
    for p in range(x_ref.shape[1] // CHUNK):
        _proj_subtile(True, slice(p * CHUNK, (p + 1) * CHUNK), mod, refs)


def _context_front_kernel(*refs):
    (x_ref, mod_ref, wgt_ref, bg_ref, qg_ref, kg_ref, gn_ref, wm_ref, wmt_ref, wa_ref, wvt_ref, wmg_ref,
     sgm_o, sga_o, kc_o, vc_o, hm_o, c_o, n_o, m_o, ha_o,
     qm_s, km_s, vmt_s, somt_s, gr_s, qa_s, ka_s, vt_s, acc_s) = refs
    proj_refs = (x_ref, mod_ref, wgt_ref, bg_ref, qg_ref, kg_ref, wm_ref, wmt_ref, wa_ref, wvt_ref, wmg_ref,
                 qm_s, km_s, vmt_s, somt_s, gr_s, qa_s, ka_s, vt_s, sgm_o, sga_o, kc_o, vc_o)
    _proj_subtile(False, slice(0, CHUNK), mod_ref[0], proj_refs)
    _mlstm_kernel(False, True, 1, NH_M, qm_s, km_s, vmt_s, gr_s, somt_s, gn_ref, hm_o, c_o, n_o, m_o, acc_s)
    _attn_kernel(1, False, N_KV, qa_s, ka_s, vt_s, ha_o)


def _proj_subtile(rope, rows, mod, refs):
    if rope:
        (x_ref, mod_ref, wgt_ref, bg_ref, qg_ref, kg_ref, cos_ref, sin_ref, wm_ref, wmt_ref, wa_ref, wvt_ref, wmg_ref,
         qm_o, km_o, vmt_o, somt_o, gr_o, qa_o, ka_o, vt_o, sgm_o, sga_o) = refs
    else:
        (x_ref, mod_ref, wgt_ref, bg_ref, qg_ref, kg_ref, wm_ref, wmt_ref, wa_ref, wvt_ref, wmg_ref,
         qm_o, km_o, vmt_o, somt_o, gr_o, qa_o, ka_o, vt_o, sgm_o, sga_o, kc_o, vc_o) = refs
    tm = CHUNK
    h = (x_ref[0, rows, :] * (1.0 + mod[1:2]) + mod[0:1]).astype(BF16)

    gates = _dot_nt(wgt_ref[...], h) + bg_ref[...]
    lf = _log_sigmoid(gates)

    qm_o[0, rows, :] = _dot(h, wm_ref[:, 0:D_MODEL]).astype(BF16)
    km_o[0, rows, :] = (_dot(h, wm_ref[:, D_MODEL:2 * D_MODEL]) * (DH_M ** -0.5)).astype(BF16)
    for hh in range(NH_M):
        vmt = _dot_nt(wmt_ref[hh * DH_M:(hh + 1) * DH_M, :], h)
        vmt_o[0, hh, 0:DH_M, rows] = vmt.astype(BF16)
        vmt_o[0, hh, DH_M:VM_ROWS, rows] = jnp.ones((VM_ROWS - DH_M, tm), BF16)
    somt_o[0, :, rows] = jax.nn.sigmoid(_dot_nt(wmt_ref[D_MODEL:2 * D_MODEL, :], h)).astype(BF16)

    if rope:
        cos = cos_ref[rows, :]
        sin_s = sin_ref[rows, :]
        lane = lax.broadcasted_iota(jnp.int32, (tm, DH_A), 1)
        first_half = (lane % (2 * N_FREQ)) < N_FREQ

        def rot(t):
            partner = jnp.where(first_half, pltpu.roll(t, DH_A - N_FREQ, 1), pltpu.roll(t, N_FREQ, 1))
            return t * cos + partner * sin_s
    else:
        rot = lambda t: t

    qg = qg_ref[...]
    kg = kg_ref[...]
    head_rows = lambda g: pl.ds(rows.start * N_KV + g, tm, stride=N_KV)
    q_all = _dot(h, wa_ref[:, 0:N_Q * DH_A])
    k_all = _dot(h, wa_ref[:, N_Q * DH_A:(N_Q + N_KV) * DH_A])
    for g in range(N_Q):
        t = _rms(q_all[:, g * DH_A:(g + 1) * DH_A], qg)
        qa_o[0, rows, g * DH_A:(g + 1) * DH_A] = (rot(t) * Q_SCALE).astype(BF16)
    for g in range(N_KV):
        t = _rms(k_all[:, g * DH_A:(g + 1) * DH_A], kg)
        if not rope:
            kc_o[0, head_rows(g), :] = t
        ka_o[0, rows, g * DH_A:(g + 1) * DH_A] = rot(t).astype(BF16)
    if not rope:
        off = (N_Q + N_KV) * DH_A
        v_all = _dot(h, wa_ref[:, off:off + N_KV * DH_A])
        for g in range(N_KV):
            vc_o[0, head_rows(g), :] = v_all[:, g * DH_A:(g + 1) * DH_A]
    vt = _dot_nt(wvt_ref[...], h)
    for g in range(N_KV):
        vt_o[0, g, 0:DH_A, rows] = vt[g * DH_A:(g + 1) * DH_A].astype(BF16)
        vt_o[0, g, DH_A:V_ROWS, rows] = jnp.ones((V_ROWS - DH_A, tm), BF16)

    row = lax.broadcasted_iota(jnp.int32, (tm, tm), 0)
    col = lax.broadcasted_iota(jnp.int32, (tm, tm), 1)
    tri = jnp.where(row <= col, 1.0, 0.0).astype(BF16)
    hi = lf.astype(BF16)
    r1 = lf - hi.astype(F32)
    mid = r1.astype(BF16)
    lo = (r1 - mid.astype(F32)).astype(BF16)
    cum = (_dot(hi, tri) + _dot(mid, tri) + _dot(lo, tri))[0:8]
    lf8 = lf[0:8]
    tot = cum[:, tm - 1:tm]
    rev = tot - cum + lf8
    is_fwd = lax.broadcasted_iota(jnp.int32, (8, tm), 0) < NH_M
    a = jnp.where(is_fwd, cum, rev)
    cc = (gates[8:16] - a) * LOG2E
    a = a * LOG2E
    totb = jnp.broadcast_to(tot * LOG2E, (8, tm))
    c_pre = _cummax_lanes(cc, False)
    c_suf = _cummax_lanes(cc, True)
    for hh in range(NH_M):
        gate_rows = (a[hh:hh + 1], cc[hh:hh + 1], totb[hh:hh + 1],
                     a[NH_M + hh:NH_M + hh + 1], cc[NH_M + hh:NH_M + hh + 1], totb[NH_M + hh:NH_M + hh + 1],
                     c_pre[hh:hh + 1], c_suf[NH_M + hh:NH_M + hh + 1])
        for k, r in enumerate(gate_rows):
            gr_o[0, hh, k:k + 1, rows] = r

    sgm_o[0, rows, :] = jax.nn.sigmoid(_dot(h, wmg_ref[:, 0:D_MODEL])).astype(BF16)
    sga_o[0, rows, :] = jax.nn.sigmoid(_dot(h, wmg_ref[:, D_MODEL:2 * D_MODEL])).astype(BF16)


def _projection(x, mod, mod_rows, small, big, rope_tables):
    B, T, _ = x.shape
    tm = min(TOK_TILE, T)
    nt = T // tm
    tok = lambda width: pl.BlockSpec((1, tm, width), lambda b, t: (b, t, 0))
    in_specs = [tok(D_MODEL), _mod_spec(*mod_rows)]
    in_specs += [_resident(w.shape) for w in small]
    in_specs += [pl.BlockSpec((tm, DH_A), lambda b, t: (t, 0))] * 2
    in_specs += [_resident(w.shape) for w in big]
    args = [x, mod, *small, *rope_tables, *big]

    kv_w = N_KV * DH_A
    outs = [((B, T, D_MODEL), BF16, tok(D_MODEL)),
            ((B, T, D_MODEL), BF16, tok(D_MODEL)),
            ((B, NH_M, VM_ROWS, T), BF16,
             pl.BlockSpec((1, NH_M, VM_ROWS, tm), lambda b, t: (b, 0, 0, t))),
            ((B, D_MODEL, T), BF16, pl.BlockSpec((1, D_MODEL, tm), lambda b, t: (b, 0, t))),
            ((B, NH_M, 8, T), F32, pl.BlockSpec((1, NH_M, 8, tm), lambda b, t: (b, 0, 0, t))),
            ((B, T, D_MODEL), BF16, tok(D_MODEL)),
            ((B, T, kv_w), BF16, tok(kv_w)),
            ((B, N_KV, V_ROWS, T), BF16,
             pl.BlockSpec((1, N_KV, V_ROWS, tm), lambda b, t: (b, 0, 0, t))),
            ((B, T, D_MODEL), BF16, tok(D_MODEL)),
            ((B, T, D_MODEL), BF16, tok(D_MODEL))]
    return _call(_proj_kernel, "projection_lat", (B, nt), ("parallel", "parallel"), in_specs, args,
                 [o[2] for o in outs], [jax.ShapeDtypeStruct(o[0], o[1]) for o in outs])


def _context_front(x, mod, mod_rows, small, big, gnorm_col):
    B, T, _ = x.shape
    assert T == CHUNK, "the fused context front handles one chunk per batch row"
    kv_w = N_KV * DH_A
    tok = lambda width: pl.BlockSpec((1, T, width), lambda b, t: (b, 0, 0))
    state = lambda *tail: pl.BlockSpec((1, 2, NH_M) + tail, lambda b, t: (b, 0, 0, 0, 0))
    cache = pl.BlockSpec((1, T * N_KV, DH_A), lambda b, t: (b, 0, 0))
    in_specs = ([tok(D_MODEL), _mod_spec(*mod_rows)] + [_resident(w.shape) for w in small]
                + [_resident(gnorm_col.shape)] + [_resident(w.shape) for w in big])
    outs = [((B, T, D_MODEL), BF16, tok(D_MODEL)), ((B, T, D_MODEL), BF16, tok(D_MODEL)),
            ((B, T * N_KV, DH_A), F32, cache), ((B, T * N_KV, DH_A), F32, cache),
            ((B, T, D_MODEL), BF16, tok(D_MODEL)),
            ((B, 2, NH_M, DH_M, DH_M), F32, state(DH_M, DH_M)),
            ((B, 2, NH_M, 1, DH_M), F32, state(1, DH_M)),
            ((B, 2, NH_M, 1, 1), F32, state(1, 1)),
            ((B, T, D_MODEL), BF16, tok(D_MODEL))]
    scratch = [pltpu.VMEM((1, T, D_MODEL), BF16), pltpu.VMEM((1, T, D_MODEL), BF16),
               pltpu.VMEM((1, NH_M, VM_ROWS, T), BF16), pltpu.VMEM((1, D_MODEL, T), BF16),
               pltpu.VMEM((1, NH_M, 8, T), F32),
               pltpu.VMEM((1, T, D_MODEL), BF16), pltpu.VMEM((1, T, kv_w), BF16),
               pltpu.VMEM((1, N_KV, V_ROWS, T), BF16),
               pltpu.VMEM((1, DH_M, CHUNK), F32)]
    return _call(_context_front_kernel, "context_front", (B, 1), ("parallel", "arbitrary"), in_specs,
                 [x, mod, *small, gnorm_col, *big],
                 [o[2] for o in outs], [jax.ShapeDtypeStruct(o[0], o[1]) for o in outs], scratch)


def _mlstm_kernel(has_init, emit_state, nc, nh, *refs):
    refs = list(refs)
    q_ref, k_ref, vt_ref, g_ref, somt_ref, gn_ref = refs[:6]
    refs = refs[6:]
    if has_init:
        c0_ref, n0_ref, m0_ref = refs[:3]
        refs = refs[3:]
    hm_o = refs[0]
    refs = refs[1:]
    if emit_state:
        c_o, n_o, m_o = refs[:3]
        refs = refs[3:]
    acc_ref, = refs

    L = CHUNK
    row = lax.broadcasted_iota(jnp.int32, (L, L), 0)
    col = lax.broadcasted_iota(jnp.int32, (L, L), 1)
    eye = row == col
    masks = (row <= col, row >= col)

    span = lambda c: slice(c * L, (c + 1) * L)
    feat = lambda hh: slice(hh * DH_M, (hh + 1) * DH_M)
    chunk_of = lambda d, s: s if d == 0 else nc - 1 - s
    steps = [(hh, s) for hh in range(nh) for s in range(nc)]

    def scores(hh, s):
        return [_dot_nt(k_ref[0, span(chunk_of(d, s)), feat(hh)], q_ref[0, span(chunk_of(d, s)), feat(hh)])
                for d in range(2)]

    def finish_chunk(hh, c, ht):
        ht = acc_ref[c] + ht
        hn = ht * lax.rsqrt(jnp.mean(ht * ht, axis=0, keepdims=True) + EPS) * gn_ref[feat(hh), :]
        hm_o[0, span(c), feat(hh)] = (hn * somt_ref[0, feat(hh), span(c)]).T.astype(BF16)

    st_next = scores(*steps[0])
    for idx, (hh, s) in enumerate(steps):
        if s == 0:
            state = [None, None]
            if has_init:
                state = [jnp.concatenate([c0_ref[0, d, hh].T,
                                          jnp.broadcast_to(n0_ref[0, d, hh], (VM_ROWS - DH_M, DH_M))], axis=0)
                         for d in range(2)]
            m_run = [m0_ref[0, d, hh] * LOG2E if has_init else jnp.zeros((1, 1), F32) for d in range(2)]
            arrived = [False] * nc
        st_cur = st_next
        if idx + 1 < len(steps):
            st_next = scores(*steps[idx + 1])
        inter = [None, None]
        if state[0] is not None:
            inter = [_dot_nt(state[d].astype(BF16), q_ref[0, span(chunk_of(d, s)), feat(hh)]) for d in range(2)]
        for d in range(2):
            c = chunk_of(d, s)
            k = k_ref[0, span(c), feat(hh)]
            vt = vt_ref[0, hh, :, span(c)]
            g = g_ref[0, hh, :, span(c)]
            a_row = g[3 * d:3 * d + 1]
            c_row = g[3 * d + 1:3 * d + 2]
            tot = g[3 * d + 2:3 * d + 3, 0:1]
            m_prev = m_run[d]
            c_col = jnp.sum(jnp.where(eye, c_row, 0.0), axis=-1, keepdims=True)
            c_run = g[6 + d:7 + d]

            keep_state = emit_state or s + 1 < nc
            if keep_state:
                c_max = c_run[:, L - 1:L] if d == 0 else c_run[:, 0:1]
                m_new = tot + jnp.maximum(m_prev, c_max)
                wk = jnp.exp2(tot + c_col - m_new).astype(BF16) * k
                upd = _dot(vt, wk)
                new_state = upd if state[d] is None else jnp.exp2(tot + m_prev - m_new) * state[d] + upd

            m_rel = jnp.maximum(m_prev, c_run)
            sp = (st_cur[d] * jnp.exp2(jnp.where(masks[d], c_col, -jnp.inf) - m_rel)).astype(BF16)
            numt = _dot(vt, sp)
            if inter[d] is not None:
                numt = numt + jnp.exp2(m_prev - m_rel) * inter[d]
            den = numt[DH_M:DH_M + 1]
            ht = numt[0:DH_M] * (1.0 / jnp.maximum(jnp.abs(den), jnp.exp2(-(a_row + m_rel))))
            if arrived[c]:
                finish_chunk(hh, c, ht)
            else:
                acc_ref[c] = ht
                arrived[c] = True
            if keep_state:
                state[d] = new_state
                m_run[d] = m_new

        if emit_state and s == nc - 1:
            for d in range(2):
                c_o[0, d, hh] = state[d][0:DH_M].T
                n_o[0, d, hh] = state[d][DH_M:DH_M + 1]
                m_o[0, d, hh] = m_run[d] * (1.0 / LOG2E)


def _mlstm(qm, km, vmt, gr, somt, gnorm_col, init_state, emit_state, nh):
    B, T, _ = qm.shape
    nc = T // CHUNK
    has_init = init_state is not None

    seq = pl.BlockSpec((1, T, nh * DH_M), lambda b, h: (b, 0, h))
    c_spec = pl.BlockSpec((1, 2, nh, DH_M, DH_M), lambda b, h: (b, 0, h, 0, 0))
    n_spec = pl.BlockSpec((1, 2, nh, 1, DH_M), lambda b, h: (b, 0, h, 0, 0))
    m_spec = pl.BlockSpec((1, 2, nh, 1, 1), lambda b, h: (b, 0, h, 0, 0))

    in_specs = [seq, seq,
                pl.BlockSpec((1, nh, VM_ROWS, T), lambda b, h: (b, h, 0, 0)),
                pl.BlockSpec((1, nh, 8, T), lambda b, h: (b, h, 0, 0)),
                pl.BlockSpec((1, nh * DH_M, T), lambda b, h: (b, h, 0)),
                pl.BlockSpec((nh * DH_M, 1), lambda b, h: (h, 0))]
    args = [qm, km, vmt, gr, somt, gnorm_col]
    if has_init:
        in_specs += [c_spec, n_spec, m_spec]
        args += list(init_state)
    out_specs = [seq]
    out_shape = [jax.ShapeDtypeStruct((B, T, D_MODEL), BF16)]
    if emit_state:
        out_specs += [c_spec, n_spec, m_spec]
        out_shape += [jax.ShapeDtypeStruct((B, 2, NH_M, DH_M, DH_M), F32),
                      jax.ShapeDtypeStruct((B, 2, NH_M, 1, DH_M), F32),
                      jax.ShapeDtypeStruct((B, 2, NH_M, 1, 1), F32)]

    return _call(functools.partial(_mlstm_kernel, has_init, emit_state, nc, nh),
                 "mlstm_lat" if has_init else "mlstm_ctx", (B, NH_M // nh), ("parallel", "parallel"),
                 in_specs, args, out_specs, out_shape, [pltpu.VMEM((nc, DH_M, CHUNK), F32)])


def _attn_kernel(n_lat_tiles, has_ctx, nkv, *refs):
    if has_ctx:
        q_ref, k_ref, vt_ref, kc_ref, vct_ref, o_ref = refs
    else:
        q_ref, k_ref, vt_ref, o_ref = refs
    tk = k_ref.shape[1] // n_lat_tiles
    head = lambda h: slice(h * DH_A, (h + 1) * DH_A)
    tiles = [(lambda h, i=i: k_ref[0, i * tk:(i + 1) * tk, head(h)],
              lambda h, i=i: vt_ref[0, h, :, i * tk:(i + 1) * tk]) for i in range(n_lat_tiles)]
    if has_ctx:
        tiles.append((lambda h: kc_ref[0, :, head(h)], lambda h: vct_ref[0, h]))
    groups = [(h, g, r) for h in range(nkv) for r in range(q_ref.shape[1] // Q_CHAIN) for g in range(G_Q)]
    chains = [(t, i) for t in range(len(tiles)) for i in range(len(groups))]
    rows = lambda i: slice(groups[i][2] * Q_CHAIN, (groups[i][2] + 1) * Q_CHAIN)
    cols = lambda i: head(groups[i][0] * G_Q + groups[i][1])

    def scores(t, i):
        return _dot_nt(tiles[t][0](groups[i][0]), q_ref[0, rows(i), cols(i)])

    m = [None] * len(groups)
    acc = [None] * len(groups)
    pending = []
    for idx in range(len(chains) + ATTN_LOOKAHEAD):
        if idx < len(chains):
            pending.append(scores(*chains[idx]))
        if idx < ATTN_LOOKAHEAD:
            continue
        t, i = chains[idx - ATTN_LOOKAHEAD]
        st = pending.pop(0)
        m_tile = jnp.max(st, axis=0, keepdims=True)
        m_new = m_tile if t == 0 else jnp.maximum(m[i], m_tile)
        pv = _dot(tiles[t][1](groups[i][0]), jnp.exp2(st - m_new).astype(BF16))
        acc[i] = pv if t == 0 else jnp.exp2(m[i] - m_new) * acc[i] + pv
        m[i] = m_new

    for i in range(len(groups)):
        out = acc[i][0:DH_A] * (1.0 / acc[i][DH_A:DH_A + 1])
        o_ref[0, rows(i), cols(i)] = out.T.astype(BF16)


def _attention(qa, ka, vt, ctx_kv):
    B, T, _ = qa.shape
    tq = min(Q_TILE, T)
    nq = T // tq
    has_ctx = ctx_kv is not None
    n_lat_tiles = max(1, T // K_TILE)
    nkv = 1 if has_ctx else N_KV
    qspec = pl.BlockSpec((1, tq, nkv * G_Q * DH_A), lambda b, h, i: (b, i, h))
    kspec = lambda tk: pl.BlockSpec((1, tk, nkv * DH_A), lambda b, h, i: (b, 0, h))
    vspec = lambda tk: pl.BlockSpec((1, nkv, V_ROWS, tk), lambda b, h, i: (b, h, 0, 0))
    in_specs = [qspec, kspec(T), vspec(T)]
    args = [qa, ka, vt]
    if has_ctx:
        tc = ctx_kv[0].shape[1]
        in_specs += [kspec(tc), vspec(tc)]
        args += list(ctx_kv)
    return _call(functools.partial(_attn_kernel, n_lat_tiles, has_ctx, nkv),
                 "attention_lat" if has_ctx else "attention_ctx", (B, N_KV // nkv, nq),
                 ("parallel", "parallel", "parallel"), in_specs, args,
                 qspec, jax.ShapeDtypeStruct((B, T, D_MODEL), BF16))


def _layer_norm(y, g, b):
    mu = jnp.mean(y, axis=-1, keepdims=True)
    yc = y - mu
    var = jnp.mean(yc * yc, axis=-1, keepdims=True)
    return yc * lax.rsqrt(var + EPS) * g + b


def _tail_kernel(x_ref, mod_ref, hm_ref, ha_ref, sgm_ref, sga_ref, ln_ref,
                 wbm_ref, wba_ref, wout_ref, wup_ref, wdown_ref, o_ref):
    mod = mod_ref[0]
    ln = ln_ref[...]
    n_sub = x_ref.shape[1] // TAIL_SUB
    rows = lambda p: slice(p * TAIL_SUB, (p + 1) * TAIL_SUB)

    def merge(p):
        merged = (sgm_ref[0, rows(p), :] * _dot(hm_ref[0, rows(p), :], wbm_ref[...])
                  + sga_ref[0, rows(p), :] * _dot(ha_ref[0, rows(p), :], wba_ref[...]))
        return ALPHA * x_ref[0, rows(p), :] + mod[2:3] * _dot(merged.astype(BF16), wout_ref[...])

    def ffn(x1):
        h = (x1 * (1.0 + mod[4:5]) + mod[3:4]).astype(BF16)
        ff = jnp.zeros_like(x1)
        for j in range(D_FF // D_MODEL):
            u = jnp.maximum(_dot(h, wup_ref[:, j * D_MODEL:(j + 1) * D_MODEL]), 0.0)
            ff = ff + _dot((u * u).astype(BF16), wdown_ref[j * D_MODEL:(j + 1) * D_MODEL, :])
        return ALPHA * x1 + mod[5:6] * ff

    y1 = [merge(p) for p in range(n_sub)]
    y2 = [ffn(_layer_norm(y1[p], ln[0:1], ln[1:2])) for p in range(n_sub)]
    for p in range(n_sub):
        o_ref[0, rows(p), :] = _layer_norm(y2[p], ln[2:3], ln[3:4])


def _tail(x, mod, mod_rows, hm, ha, sgm, sga, wts, ln, name):
    if not mod_rows[1]:
        x, hm, ha, sgm, sga = (a.reshape(1, -1, D_MODEL) for a in (x, hm, ha, sgm, sga))
    B, T, _ = x.shape
    tm = TAIL_TILE
    tok = pl.BlockSpec((1, tm, D_MODEL), lambda b, t: (b, t, 0))
    in_specs = ([tok, _mod_spec(*mod_rows), tok, tok, tok, tok, _resident(ln.shape)]
                + [_resident(w.shape) for w in wts])
    return _call(_tail_kernel, name, (B, T // tm), ("parallel", "parallel"),
                 in_specs, (x, mod, hm, ha, sgm, sga, ln, *wts), tok, jax.ShapeDtypeStruct((B, T, D_MODEL), F32))


def _rope_tables(n_tokens):
    rows = n_tokens // GRID_W
    row = np.repeat(np.arange(rows), GRID_W)
    col = np.tile(np.arange(GRID_W), rows)
    inv = ROPE_BASE ** (-np.arange(N_FREQ, dtype=np.float64) / N_FREQ)
    ang = np.stack([row, col], -1).astype(np.float64)[..., None] * inv
    ang = np.broadcast_to(ang[:, :, None, :], (n_tokens, 2, 2, N_FREQ))
    sign = np.asarray([-1.0, 1.0])[None, None, :, None]
    return (jnp.asarray(np.cos(ang).reshape(n_tokens, DH_A), F32),
            jnp.asarray((np.sin(ang) * sign).reshape(n_tokens, DH_A), F32))


def kernel(x_prompt, x_sample, cache_k, cache_v, state_C, state_n, state_m, c, c_ctx, w_mod, b_mod, w_in,
           b_gates, mlstm_norm_g, q_norm_g, k_norm_g, w_bm, w_ba, w_out, ln1_g, ln1_b, w_up, w_down,
           ln2_g, ln2_b):
    B, T, _ = x_prompt.shape
    Bd, Td, _ = x_sample.shape
    l = 0

    w = w_in[l]
    o_g = 4 * D_MODEL
    o_a = o_g + 4 * NH_M
    o_mg = o_a + (N_Q + 2 * N_KV) * DH_A
    gate_rows = np.array([4, 5, 6, 7, 12, 13, 14, 15, 0, 1, 2, 3, 8, 9, 10, 11])
    proj_small = (w[:, o_g:o_a].T[gate_rows].astype(BF16),
                  b_gates[l][gate_rows].reshape(4 * NH_M, 1),
                  q_norm_g[l].reshape(1, DH_A),
                  k_norm_g[l].reshape(1, DH_A))
    proj_big = (w[:, :2 * D_MODEL].astype(BF16),
                w[:, 2 * D_MODEL:o_g].T.astype(BF16),
                w[:, o_a:o_mg].astype(BF16),
                w[:, o_a + (N_Q + N_KV) * DH_A:o_mg].T.astype(BF16),
                w[:, o_mg:].astype(BF16))
    tail_w = (w_bm[l].astype(BF16), w_ba[l].astype(BF16), w_out[l].astype(BF16),
              w_up[l].astype(BF16), w_down[l].astype(BF16))
    ln = jnp.stack([ln1_g[l], ln1_b[l], ln2_g[l], ln2_b[l]])
    gnorm = mlstm_norm_g[l].reshape(D_MODEL, 1)

    c_rows = jnp.concatenate([c_ctx[None, :], c, jnp.zeros((MOD_ROWS - 1 - Bd, D_MODEL), F32)], axis=0)
    mod = _modulation(c_rows, w_mod[l], b_mod[l]).reshape(MOD_ROWS, 6, D_MODEL)
    rows_ctx, rows_lat = (0, False), (1, True)

    (sgm, sga, k_new, v_new, hm, c_new, n_new, m_new, ha) = _context_front(x_prompt, mod, rows_ctx, proj_small,
                                                                          proj_big, gnorm)
    y_prompt = _tail(x_prompt, mod, rows_ctx, hm, ha, sgm, sga, tail_w, ln, "tail_ctx").reshape(x_prompt.shape)

    (qm, km, vmt, somt, gr, qa, ka, vt, sgm, sga) = _projection(x_sample, mod, rows_lat, proj_small, proj_big, _rope_tables(Td))
    past = cache_k.shape[2]
    init = (state_C[:, l], state_n[:, l].reshape(Bd, 2, NH_M, 1, DH_M), state_m[:, l].reshape(Bd, 2, NH_M, 1, 1))
    hm, = _mlstm(qm, km, vmt, gr, somt, gnorm, init, False, 1)
    vct = jnp.transpose(cache_v[:, l], (0, 2, 3, 1)).astype(BF16)
    vct = jnp.concatenate([vct, jnp.ones((Bd, N_KV, V_ROWS - DH_A, past), BF16)], axis=2)
    ctx_kv = (cache_k[:, l].reshape(Bd, past, N_KV * DH_A).astype(BF16), vct)
    ha = _attention(qa, ka, vt, ctx_kv)
    y_sample = _tail(x_sample, mod, rows_lat, hm, ha, sgm, sga, tail_w, ln, "tail_lat")

    return (y_prompt, y_sample,
            k_new.reshape(B, 1, T, N_KV, DH_A), v_new.reshape(B, 1, T, N_KV, DH_A),
            c_new.reshape(B, 1, 2, NH_M, DH_M, DH_M), n_new.reshape(B, 1, 2, NH_M, DH_M),
            m_new.reshape(B, 1, 2, NH_M))
```

```python
import functools

import jax
import jax.numpy as jnp
import numpy as np
from jax import lax
from jax.experimental import pallas as pl
from jax.experimental.pallas import tpu as pltpu

D_MODEL = 1024
NH_M = 4
DH_M = 256
N_Q = 8
N_KV = 2
G_Q = N_Q // N_KV
DH_A = 128
D_FF = 4 * D_MODEL
GRID_W = 64
N_FREQ = DH_A // 4
ROPE_BASE = 10000.0
EPS = 1e-6
DEPTH = 1
ALPHA = (2 * DEPTH) ** 0.25

CHUNK = 256
TOK_TILE = 512
TAIL_TILE = 512
TAIL_SUB = 256
Q_TILE = 1024
Q_CHAIN = 256
K_TILE = 512
LOG2E = float(np.log2(np.e))
Q_SCALE = DH_A ** -0.5 * LOG2E
ATTN_LOOKAHEAD = 4
VM_ROWS = DH_M + 16
V_ROWS = DH_A + 16
MOD_ROWS = 8

VMEM_WORK_BYTES = 20 * 1024 * 1024
VMEM_FLOOR_BYTES = 56 * 1024 * 1024

F32 = jnp.float32
BF16 = jnp.bfloat16


def _dot(a, b):
    return jnp.dot(a, b, preferred_element_type=F32)


def _dot_nt(a, b):
    return lax.dot_general(a, b, (((1,), (1,)), ((), ())), preferred_element_type=F32)


def _resident(shape):
    nd = len(shape)
    return pl.BlockSpec(shape, lambda *_: (0,) * nd, pipeline_mode=pl.Buffered(1))


def _mod_spec(first_row, per_batch):
    return pl.BlockSpec((1, 6, D_MODEL), (lambda b, t: (first_row + b, 0, 0)) if per_batch
                        else (lambda b, t: (first_row, 0, 0)))


def _call(body, name, grid, semantics, in_specs, args, out_specs, out_shape, scratch=()):
    def window_bytes(spec, a):
        buffers = 2 if spec.pipeline_mode is None else spec.pipeline_mode.buffer_count
        return buffers * int(np.prod(spec.block_shape)) * jnp.dtype(a.dtype).itemsize

    outs, out_sp = (out_shape, out_specs) if isinstance(out_shape, (list, tuple)) else ([out_shape], [out_specs])
    windows = sum(map(window_bytes, in_specs, args)) + sum(map(window_bytes, out_sp, outs))
    held = sum(int(np.prod(s.shape)) * jnp.dtype(s.dtype).itemsize for s in scratch)
    limit = max(windows + held + VMEM_WORK_BYTES, VMEM_FLOOR_BYTES)
    return pl.pallas_call(
        body, grid=grid, in_specs=in_specs, out_specs=out_specs, out_shape=out_shape, scratch_shapes=list(scratch),
        compiler_params=pltpu.CompilerParams(dimension_semantics=semantics, vmem_limit_bytes=limit),
        name=name,
    )(*args)


def _mod_kernel(c_ref, w_top_ref, w_bot_ref, b_ref, o_ref):
    c = c_ref[...]
    s = (c * jax.nn.sigmoid(c)).astype(BF16)
    half = w_top_ref.shape[0]
    o_ref[...] = (_dot(s[:, :half], w_top_ref[...].astype(BF16)) + _dot(s[:, half:], w_bot_ref[...].astype(BF16))
                  + b_ref[...])


def _modulation(c_rows, w_mod, b_mod):
    n_out = w_mod.shape[1]
    blk = D_MODEL
    half = D_MODEL // 2
    in_specs = [pl.BlockSpec((MOD_ROWS, D_MODEL), lambda j: (0, 0)),
                pl.BlockSpec((half, blk), lambda j: (0, j)),
                pl.BlockSpec((half, blk), lambda j: (1, j)),
                pl.BlockSpec((1, blk), lambda j: (0, j))]
    return _call(_mod_kernel, "modulation", (n_out // blk,), ("parallel",),
                 in_specs, (c_rows, w_mod, w_mod, b_mod.reshape(1, n_out)),
                 pl.BlockSpec((MOD_ROWS, blk), lambda j: (0, j)), jax.ShapeDtypeStruct((MOD_ROWS, n_out), F32))


def _log_sigmoid(x):
    return jnp.minimum(x, 0.0) - jnp.log1p(jnp.exp(-jnp.abs(x)))


def _cummax_lanes(x, reverse):
    n = x.shape[-1]
    lane = lax.broadcasted_iota(jnp.int32, x.shape, x.ndim - 1)
    step = 1
    while step < n:
        if reverse:
            shifted, valid = pltpu.roll(x, n - step, x.ndim - 1), lane < n - step
        else:
            shifted, valid = pltpu.roll(x, step, x.ndim - 1), lane >= step
        x = jnp.maximum(x, jnp.where(valid, shifted, -jnp.inf))
        step *= 2
    return x


def _rms(t, g):
    return t * lax.rsqrt(jnp.mean(t * t, axis=-1, keepdims=True) + EPS) * g


def _proj_kernel(*refs):
    x_ref, mod_ref = refs[:2]
    mod = mod_ref[0]
    for p in range(x_ref.shape[1] // CHUNK):
        _proj_subtile(True, slice(p * CHUNK, (p + 1) * CHUNK), mod, refs)


def _context_front_kernel(*refs):
    (x_ref, mod_ref, wgt_ref, bg_ref, qg_ref, kg_ref, gn_ref, wm_ref, wmt_ref, wa_ref, wvt_ref, wmg_ref,
     sgm_o, sga_o, kc_o, vc_o, hm_o, c_o, n_o, m_o, ha_o,
     qm_s, km_s, vmt_s, somt_s, gr_s, qa_s, ka_s, vt_s, acc_s) = refs
    proj_refs = (x_ref, mod_ref, wgt_ref, bg_ref, qg_ref, kg_ref, wm_ref, wmt_ref, wa_ref, wvt_ref, wmg_ref,
                 qm_s, km_s, vmt_s, somt_s, gr_s, qa_s, ka_s, vt_s, sgm_o, sga_o, kc_o, vc_o)
    _proj_subtile(False, slice(0, CHUNK), mod_ref[0], proj_refs)
    _mlstm_kernel(False, True, 1, NH_M, qm_s, km_s, vmt_s, gr_s, somt_s, gn_ref, hm_o, c_o, n_o, m_o, acc_s)
    _attn_kernel(1, False, N_KV, qa_s, ka_s, vt_s, ha_o)


def _proj_subtile(rope, rows, mod, refs):
    if rope:
        (x_ref, mod_ref, wgt_ref, bg_ref, qg_ref, kg_ref, cos_ref, sin_ref, wm_ref, wmt_ref, wa_ref, wvt_ref, wmg_ref,
         qm_o, km_o, vmt_o, somt_o, gr_o, qa_o, ka_o, vt_o, sgm_o, sga_o) = refs
    else:
        (x_ref, mod_ref, wgt_ref, bg_ref, qg_ref, kg_ref, wm_ref, wmt_ref, wa_ref, wvt_ref, wmg_ref,
         qm_o, km_o, vmt_o, somt_o, gr_o, qa_o, ka_o, vt_o, sgm_o, sga_o, kc_o, vc_o) = refs
    tm = CHUNK
    h = (x_ref[0, rows, :] * (1.0 + mod[1:2]) + mod[0:1]).astype(BF16)

    gates = _dot_nt(wgt_ref[...], h) + bg_ref[...]
    lf = _log_sigmoid(gates)

    qm_o[0, rows, :] = _dot(h, wm_ref[:, 0:D_MODEL]).astype(BF16)
    km_o[0, rows, :] = (_dot(h, wm_ref[:, D_MODEL:2 * D_MODEL]) * (DH_M ** -0.5)).astype(BF16)
    for hh in range(NH_M):
        vmt = _dot_nt(wmt_ref[hh * DH_M:(hh + 1) * DH_M, :], h)
        vmt_o[0, hh, 0:DH_M, rows] = vmt.astype(BF16)
        vmt_o[0, hh, DH_M:VM_ROWS, rows] = jnp.ones((VM_ROWS - DH_M, tm), BF16)
    somt_o[0, :, rows] = jax.nn.sigmoid(_dot_nt(wmt_ref[D_MODEL:2 * D_MODEL, :], h)).astype(BF16)

    if rope:
        cos = cos_ref[rows, :]
        sin_s = sin_ref[rows, :]
        lane = lax.broadcasted_iota(jnp.int32, (tm, DH_A), 1)
        first_half = (lane % (2 * N_FREQ)) < N_FREQ

        def rot(t):
            partner = jnp.where(first_half, pltpu.roll(t, DH_A - N_FREQ, 1), pltpu.roll(t, N_FREQ, 1))
            return t * cos + partner * sin_s
    else:
        rot = lambda t: t

    qg = qg_ref[...]
    kg = kg_ref[...]
    head_rows = lambda g: pl.ds(rows.start * N_KV + g, tm, stride=N_KV)
    q_all = _dot(h, wa_ref[:, 0:N_Q * DH_A])
    k_all = _dot(h, wa_ref[:, N_Q * DH_A:(N_Q + N_KV) * DH_A])
    for g in range(N_Q):
        t = _rms(q_all[:, g * DH_A:(g + 1) * DH_A], qg)
        qa_o[0, rows, g * DH_A:(g + 1) * DH_A] = (rot(t) * Q_SCALE).astype(BF16)
    for g in range(N_KV):
        t = _rms(k_all[:, g * DH_A:(g + 1) * DH_A], kg)
        if not rope:
            kc_o[0, head_rows(g), :] = t
        ka_o[0, rows, g * DH_A:(g + 1) * DH_A] = rot(t).astype(BF16)
    if not rope:
        off = (N_Q + N_KV) * DH_A
        v_all = _dot(h, wa_ref[:, off:off + N_KV * DH_A])
        for g in range(N_KV):
            vc_o[0, head_rows(g), :] = v_all[:, g * DH_A:(g + 1) * DH_A]
    vt = _dot_nt(wvt_ref[...], h)
    for g in range(N_KV):
        vt_o[0, g, 0:DH_A, rows] = vt[g * DH_A:(g + 1) * DH_A].astype(BF16)
        vt_o[0, g, DH_A:V_ROWS, rows] = jnp.ones((V_ROWS - DH_A, tm), BF16)

    row = lax.broadcasted_iota(jnp.int32, (tm, tm), 0)
    col = lax.broadcasted_iota(jnp.int32, (tm, tm), 1)
    tri = jnp.where(row <= col, 1.0, 0.0).astype(BF16)
    hi = lf.astype(BF16)
    r1 = lf - hi.astype(F32)
    mid = r1.astype(BF16)
    lo = (r1 - mid.astype(F32)).astype(BF16)
    cum = (_dot(hi, tri) + _dot(mid, tri) + _dot(lo, tri))[0:8]
    lf8 = lf[0:8]
    tot = cum[:, tm - 1:tm]
    rev = tot - cum + lf8
    is_fwd = lax.broadcasted_iota(jnp.int32, (8, tm), 0) < NH_M
    a = jnp.where(is_fwd, cum, rev)
    cc = (gates[8:16] - a) * LOG2E
    a = a * LOG2E
    totb = jnp.broadcast_to(tot * LOG2E, (8, tm))
    c_pre = _cummax_lanes(cc, False)
    c_suf = _cummax_lanes(cc, True)
    for hh in range(NH_M):
        gate_rows = (a[hh:hh + 1], cc[hh:hh + 1], totb[hh:hh + 1],
                     a[NH_M + hh:NH_M + hh + 1], cc[NH_M + hh:NH_M + hh + 1], totb[NH_M + hh:NH_M + hh + 1],
                     c_pre[hh:hh + 1], c_suf[NH_M + hh:NH_M + hh + 1])
        for k, r in enumerate(gate_rows):
            gr_o[0, hh, k:k + 1, rows] = r

    sgm_o[0, rows, :] = jax.nn.sigmoid(_dot(h, wmg_ref[:, 0:D_MODEL])).astype(BF16)
    sga_o[0, rows, :] = jax.nn.sigmoid(_dot(h, wmg_ref[:, D_MODEL:2 * D_MODEL])).astype(BF16)


def _projection(x, mod, mod_rows, small, big, rope_tables):
    B, T, _ = x.shape
    tm = min(TOK_TILE, T)
    nt = T // tm
    tok = lambda width: pl.BlockSpec((1, tm, width), lambda b, t: (b, t, 0))
    in_specs = [tok(D_MODEL), _mod_spec(*mod_rows)]
    in_specs += [_resident(w.shape) for w in small]
    in_specs += [pl.BlockSpec((tm, DH_A), lambda b, t: (t, 0))] * 2
    in_specs += [_resident(w.shape) for w in big]
    args = [x, mod, *small, *rope_tables, *big]

    kv_w = N_KV * DH_A
    outs = [((B, T, D_MODEL), BF16, tok(D_MODEL)),
            ((B, T, D_MODEL), BF16, tok(D_MODEL)),
            ((B, NH_M, VM_ROWS, T), BF16,
             pl.BlockSpec((1, NH_M, VM_ROWS, tm), lambda b, t: (b, 0, 0, t))),
            ((B, D_MODEL, T), BF16, pl.BlockSpec((1, D_MODEL, tm), lambda b, t: (b, 0, t))),
            ((B, NH_M, 8, T), F32, pl.BlockSpec((1, NH_M, 8, tm), lambda b, t: (b, 0, 0, t))),
            ((B, T, D_MODEL), BF16, tok(D_MODEL)),
            ((B, T, kv_w), BF16, tok(kv_w)),
            ((B, N_KV, V_ROWS, T), BF16,
             pl.BlockSpec((1, N_KV, V_ROWS, tm), lambda b, t: (b, 0, 0, t))),
            ((B, T, D_MODEL), BF16, tok(D_MODEL)),
            ((B, T, D_MODEL), BF16, tok(D_MODEL))]
    return _call(_proj_kernel, "projection_lat", (B, nt), ("parallel", "parallel"), in_specs, args,
                 [o[2] for o in outs], [jax.ShapeDtypeStruct(o[0], o[1]) for o in outs])


def _context_front(x, mod, mod_rows, small, big, gnorm_col):
    B, T, _ = x.shape
    assert T == CHUNK, "the fused context front handles one chunk per batch row"
    kv_w = N_KV * DH_A
    tok = lambda width: pl.BlockSpec((1, T, width), lambda b, t: (b, 0, 0))
    state = lambda *tail: pl.BlockSpec((1, 2, NH_M) + tail, lambda b, t: (b, 0, 0, 0, 0))
    cache = pl.BlockSpec((1, T * N_KV, DH_A), lambda b, t: (b, 0, 0))
    in_specs = ([tok(D_MODEL), _mod_spec(*mod_rows)] + [_resident(w.shape) for w in small]
                + [_resident(gnorm_col.shape)] + [_resident(w.shape) for w in big])
    outs = [((B, T, D_MODEL), BF16, tok(D_MODEL)), ((B, T, D_MODEL), BF16, tok(D_MODEL)),
            ((B, T * N_KV, DH_A), F32, cache), ((B, T * N_KV, DH_A), F32, cache),
            ((B, T, D_MODEL), BF16, tok(D_MODEL)),
            ((B, 2, NH_M, DH_M, DH_M), F32, state(DH_M, DH_M)),
            ((B, 2, NH_M, 1, DH_M), F32, state(1, DH_M)),
            ((B, 2, NH_M, 1, 1), F32, state(1, 1)),
            ((B, T, D_MODEL), BF16, tok(D_MODEL))]
    scratch = [pltpu.VMEM((1, T, D_MODEL), BF16), pltpu.VMEM((1, T, D_MODEL), BF16),
               pltpu.VMEM((1, NH_M, VM_ROWS, T), BF16), pltpu.VMEM((1, D_MODEL, T), BF16),
               pltpu.VMEM((1, NH_M, 8, T), F32),
               pltpu.VMEM((1, T, D_MODEL), BF16), pltpu.VMEM((1, T, kv_w), BF16),
               pltpu.VMEM((1, N_KV, V_ROWS, T), BF16),
               pltpu.VMEM((NH_M, DH_M, CHUNK), F32)]
    return _call(_context_front_kernel, "context_front", (B, 1), ("parallel", "arbitrary"), in_specs,
                 [x, mod, *small, gnorm_col, *big],
                 [o[2] for o in outs], [jax.ShapeDtypeStruct(o[0], o[1]) for o in outs], scratch)


def _mlstm_kernel(has_init, emit_state, nc, nh, *refs):
    refs = list(refs)
    q_ref, k_ref, vt_ref, g_ref, somt_ref, gn_ref = refs[:6]
    refs = refs[6:]
    if has_init:
        c0_ref, n0_ref, m0_ref = refs[:3]
        refs = refs[3:]
    hm_o = refs[0]
    refs = refs[1:]
    if emit_state:
        c_o, n_o, m_o = refs[:3]
        refs = refs[3:]
    acc_ref, = refs

    L = CHUNK
    row = lax.broadcasted_iota(jnp.int32, (L, L), 0)
    col = lax.broadcasted_iota(jnp.int32, (L, L), 1)
    eye = row == col
    masks = (row <= col, row >= col)

    span = lambda c: slice(c * L, (c + 1) * L)
    feat = lambda hh: slice(hh * DH_M, (hh + 1) * DH_M)
    chunk_of = lambda d, s: s if d == 0 else nc - 1 - s
    steps = [(hh, s) for s in range(nc) for hh in range(nh)]

    def scores(hh, s):
        return [_dot_nt(k_ref[0, span(chunk_of(d, s)), feat(hh)], q_ref[0, span(chunk_of(d, s)), feat(hh)])
                for d in range(2)]

    def finish_chunk(hh, c, ht):
        ht = acc_ref[hh * nc + c] + ht
        hn = ht * lax.rsqrt(jnp.mean(ht * ht, axis=0, keepdims=True) + EPS) * gn_ref[feat(hh), :]
        hm_o[0, span(c), feat(hh)] = (hn * somt_ref[0, feat(hh), span(c)]).T.astype(BF16)

    states, m_runs, arriveds = {}, {}, {}
    for hh, s in steps:
        if s == 0:
            states[hh] = [None, None]
            if has_init:
                states[hh] = [jnp.concatenate([c0_ref[0, d, hh].T,
                                               jnp.broadcast_to(n0_ref[0, d, hh], (VM_ROWS - DH_M, DH_M))], axis=0)
                              for d in range(2)]
            m_runs[hh] = [m0_ref[0, d, hh] * LOG2E if has_init else jnp.zeros((1, 1), F32) for d in range(2)]
            arriveds[hh] = [False] * nc
        state, m_run, arrived = states[hh], m_runs[hh], arriveds[hh]
        st_cur = scores(hh, s)
        inter = [None, None]
        if state[0] is not None:
            inter = [_dot_nt(state[d].astype(BF16), q_ref[0, span(chunk_of(d, s)), feat(hh)]) for d in range(2)]
        for d in range(2):
            c = chunk_of(d, s)
            k = k_ref[0, span(c), feat(hh)]
            vt = vt_ref[0, hh, :, span(c)]
            g = g_ref[0, hh, :, span(c)]
            a_row = g[3 * d:3 * d + 1]
            c_row = g[3 * d + 1:3 * d + 2]
            tot = g[3 * d + 2:3 * d + 3, 0:1]
            m_prev = m_run[d]
            c_col = jnp.sum(jnp.where(eye, c_row, 0.0), axis=-1, keepdims=True)
            c_run = g[6 + d:7 + d]

            keep_state = emit_state or s + 1 < nc
            if keep_state:
                c_max = c_run[:, L - 1:L] if d == 0 else c_run[:, 0:1]
                m_new = tot + jnp.maximum(m_prev, c_max)
                wk = jnp.exp2(tot + c_col - m_new).astype(BF16) * k
                upd = _dot(vt, wk)
                new_state = upd if state[d] is None else jnp.exp2(tot + m_prev - m_new) * state[d] + upd

            m_rel = jnp.maximum(m_prev, c_run)
            sp = (st_cur[d] * jnp.exp2(jnp.where(masks[d], c_col, -jnp.inf) - m_rel)).astype(BF16)
            numt = _dot(vt, sp)
            if inter[d] is not None:
                numt = numt + jnp.exp2(m_prev - m_rel) * inter[d]
            den = numt[DH_M:DH_M + 1]
            ht = numt[0:DH_M] * (1.0 / jnp.maximum(jnp.abs(den), jnp.exp2(-(a_row + m_rel))))
            if arrived[c]:
                finish_chunk(hh, c, ht)
            else:
                acc_ref[hh * nc + c] = ht
                arrived[c] = True
            if keep_state:
                state[d] = new_state
                m_run[d] = m_new

        if emit_state and s == nc - 1:
            for d in range(2):
                c_o[0, d, hh] = state[d][0:DH_M].T
                n_o[0, d, hh] = state[d][DH_M:DH_M + 1]
                m_o[0, d, hh] = m_run[d] * (1.0 / LOG2E)


def _mlstm(qm, km, vmt, gr, somt, gnorm_col, init_state, emit_state, nh):
    B, T, _ = qm.shape
    nc = T // CHUNK
    has_init = init_state is not None

    seq = pl.BlockSpec((1, T, nh * DH_M), lambda b, h: (b, 0, h))
    c_spec = pl.BlockSpec((1, 2, nh, DH_M, DH_M), lambda b, h: (b, 0, h, 0, 0))
    n_spec = pl.BlockSpec((1, 2, nh, 1, DH_M), lambda b, h: (b, 0, h, 0, 0))
    m_spec = pl.BlockSpec((1, 2, nh, 1, 1), lambda b, h: (b, 0, h, 0, 0))

    in_specs = [seq, seq,
                pl.BlockSpec((1, nh, VM_ROWS, T), lambda b, h: (b, h, 0, 0)),
                pl.BlockSpec((1, nh, 8, T), lambda b, h: (b, h, 0, 0)),
                pl.BlockSpec((1, nh * DH_M, T), lambda b, h: (b, h, 0)),
                pl.BlockSpec((nh * DH_M, 1), lambda b, h: (h, 0))]
    args = [qm, km, vmt, gr, somt, gnorm_col]
    if has_init:
        in_specs += [c_spec, n_spec, m_spec]
        args += list(init_state)
    out_specs = [seq]
    out_shape = [jax.ShapeDtypeStruct((B, T, D_MODEL), BF16)]
    if emit_state:
        out_specs += [c_spec, n_spec, m_spec]
        out_shape += [jax.ShapeDtypeStruct((B, 2, NH_M, DH_M, DH_M), F32),
                      jax.ShapeDtypeStruct((B, 2, NH_M, 1, DH_M), F32),
                      jax.ShapeDtypeStruct((B, 2, NH_M, 1, 1), F32)]

    return _call(functools.partial(_mlstm_kernel, has_init, emit_state, nc, nh),
                 "mlstm_lat" if has_init else "mlstm_ctx", (B, NH_M // nh), ("parallel", "parallel"),
                 in_specs, args, out_specs, out_shape, [pltpu.VMEM((nh * nc, DH_M, CHUNK), F32)])


def _attn_kernel(n_lat_tiles, has_ctx, nkv, *refs):
    if has_ctx:
        q_ref, k_ref, vt_ref, kc_ref, vct_ref, o_ref = refs
    else:
        q_ref, k_ref, vt_ref, o_ref = refs
    tk = k_ref.shape[1] // n_lat_tiles
    head = lambda h: slice(h * DH_A, (h + 1) * DH_A)
    tiles = [(lambda h, i=i: k_ref[0, i * tk:(i + 1) * tk, head(h)],
              lambda h, i=i: vt_ref[0, h, :, i * tk:(i + 1) * tk]) for i in range(n_lat_tiles)]
    if has_ctx:
        tiles.append((lambda h: kc_ref[0, :, head(h)], lambda h: vct_ref[0, h]))
    groups = [(h, g, r) for h in range(nkv) for r in range(q_ref.shape[1] // Q_CHAIN) for g in range(G_Q)]
    chains = [(t, i) for t in range(len(tiles)) for i in range(len(groups))]
    rows = lambda i: slice(groups[i][2] * Q_CHAIN, (groups[i][2] + 1) * Q_CHAIN)
    cols = lambda i: head(groups[i][0] * G_Q + groups[i][1])

    def scores(t, i):
        return _dot_nt(tiles[t][0](groups[i][0]), q_ref[0, rows(i), cols(i)])

    m = [None] * len(groups)
    acc = [None] * len(groups)
    pending = []
    for idx in range(len(chains) + ATTN_LOOKAHEAD):
        if idx < len(chains):
            pending.append(scores(*chains[idx]))
        if idx < ATTN_LOOKAHEAD:
            continue
        t, i = chains[idx - ATTN_LOOKAHEAD]
        st = pending.pop(0)
        m_tile = jnp.max(st, axis=0, keepdims=True)
        m_new = m_tile if t == 0 else jnp.maximum(m[i], m_tile)
        pv = _dot(tiles[t][1](groups[i][0]), jnp.exp2(st - m_new).astype(BF16))
        acc[i] = pv if t == 0 else jnp.exp2(m[i] - m_new) * acc[i] + pv
        m[i] = m_new

    for i in range(len(groups)):
        out = acc[i][0:DH_A] * (1.0 / acc[i][DH_A:DH_A + 1])
        o_ref[0, rows(i), cols(i)] = out.T.astype(BF16)


def _attention(qa, ka, vt, ctx_kv):
    B, T, _ = qa.shape
    tq = min(Q_TILE, T)
    nq = T // tq
    has_ctx = ctx_kv is not None
    n_lat_tiles = max(1, T // K_TILE)
    nkv = 1 if has_ctx else N_KV
    qspec = pl.BlockSpec((1, tq, nkv * G_Q * DH_A), lambda b, h, i: (b, i, h))
    kspec = lambda tk: pl.BlockSpec((1, tk, nkv * DH_A), lambda b, h, i: (b, 0, h))
    vspec = lambda tk: pl.BlockSpec((1, nkv, V_ROWS, tk), lambda b, h, i: (b, h, 0, 0))
    in_specs = [qspec, kspec(T), vspec(T)]
    args = [qa, ka, vt]
    if has_ctx:
        tc = ctx_kv[0].shape[1]
        in_specs += [kspec(tc), vspec(tc)]
        args += list(ctx_kv)
    return _call(functools.partial(_attn_kernel, n_lat_tiles, has_ctx, nkv),
                 "attention_lat" if has_ctx else "attention_ctx", (B, N_KV // nkv, nq),
                 ("parallel", "parallel", "parallel"), in_specs, args,
                 qspec, jax.ShapeDtypeStruct((B, T, D_MODEL), BF16))


def _layer_norm(y, g, b):
    mu = jnp.mean(y, axis=-1, keepdims=True)
    yc = y - mu
    var = jnp.mean(yc * yc, axis=-1, keepdims=True)
    return yc * lax.rsqrt(var + EPS) * g + b


def _tail_kernel(x_ref, mod_ref, hm_ref, ha_ref, sgm_ref, sga_ref, ln_ref,
                 wbm_ref, wba_ref, wout_ref, wup_ref, wdown_ref, o_ref):
    mod = mod_ref[0]
    ln = ln_ref[...]
    n_sub = x_ref.shape[1] // TAIL_SUB
    rows = lambda p: slice(p * TAIL_SUB, (p + 1) * TAIL_SUB)

    def merge(p):
        merged = (sgm_ref[0, rows(p), :] * _dot(hm_ref[0, rows(p), :], wbm_ref[...])
                  + sga_ref[0, rows(p), :] * _dot(ha_ref[0, rows(p), :], wba_ref[...]))
        return ALPHA * x_ref[0, rows(p), :] + mod[2:3] * _dot(merged.astype(BF16), wout_ref[...])

    def ffn(x1):
        h = (x1 * (1.0 + mod[4:5]) + mod[3:4]).astype(BF16)
        ff = jnp.zeros_like(x1)
        for j in range(D_FF // D_MODEL):
            u = jnp.maximum(_dot(h, wup_ref[:, j * D_MODEL:(j + 1) * D_MODEL]), 0.0)
            ff = ff + _dot((u * u).astype(BF16), wdown_ref[j * D_MODEL:(j + 1) * D_MODEL, :])
        return ALPHA * x1 + mod[5:6] * ff

    y1 = [merge(p) for p in range(n_sub)]
    y2 = [ffn(_layer_norm(y1[p], ln[0:1], ln[1:2])) for p in range(n_sub)]
    for p in range(n_sub):
        o_ref[0, rows(p), :] = _layer_norm(y2[p], ln[2:3], ln[3:4])


def _tail(x, mod, mod_rows, hm, ha, sgm, sga, wts, ln, name):
    if not mod_rows[1]:
        x, hm, ha, sgm, sga = (a.reshape(1, -1, D_MODEL) for a in (x, hm, ha, sgm, sga))
    B, T, _ = x.shape
    tm = TAIL_TILE
    tok = pl.BlockSpec((1, tm, D_MODEL), lambda b, t: (b, t, 0))
    in_specs = ([tok, _mod_spec(*mod_rows), tok, tok, tok, tok, _resident(ln.shape)]
                + [_resident(w.shape) for w in wts])
    return _call(_tail_kernel, name, (B, T // tm), ("parallel", "parallel"),
                 in_specs, (x, mod, hm, ha, sgm, sga, ln, *wts), tok, jax.ShapeDtypeStruct((B, T, D_MODEL), F32))


def _rope_tables(n_tokens):
    rows = n_tokens // GRID_W
    row = np.repeat(np.arange(rows), GRID_W)
    col = np.tile(np.arange(GRID_W), rows)
    inv = ROPE_BASE ** (-np.arange(N_FREQ, dtype=np.float64) / N_FREQ)
    ang = np.stack([row, col], -1).astype(np.float64)[..., None] * inv
    ang = np.broadcast_to(ang[:, :, None, :], (n_tokens, 2, 2, N_FREQ))
    sign = np.asarray([-1.0, 1.0])[None, None, :, None]
    return (jnp.asarray(np.cos(ang).reshape(n_tokens, DH_A), F32),
            jnp.asarray((np.sin(ang) * sign).reshape(n_tokens, DH_A), F32))


def kernel(x_prompt, x_sample, cache_k, cache_v, state_C, state_n, state_m, c, c_ctx, w_mod, b_mod, w_in,
           b_gates, mlstm_norm_g, q_norm_g, k_norm_g, w_bm, w_ba, w_out, ln1_g, ln1_b, w_up, w_down,
           ln2_g, ln2_b):
    B, T, _ = x_prompt.shape
    Bd, Td, _ = x_sample.shape
    l = 0

    w = w_in[l]
    o_g = 4 * D_MODEL
    o_a = o_g + 4 * NH_M
    o_mg = o_a + (N_Q + 2 * N_KV) * DH_A
    gate_rows = np.array([4, 5, 6, 7, 12, 13, 14, 15, 0, 1, 2, 3, 8, 9, 10, 11])
    proj_small = (w[:, o_g:o_a].T[gate_rows].astype(BF16),
                  b_gates[l][gate_rows].reshape(4 * NH_M, 1),
                  q_norm_g[l].reshape(1, DH_A),
                  k_norm_g[l].reshape(1, DH_A))
    proj_big = (w[:, :2 * D_MODEL].astype(BF16),
                w[:, 2 * D_MODEL:o_g].T.astype(BF16),
                w[:, o_a:o_mg].astype(BF16),
                w[:, o_a + (N_Q + N_KV) * DH_A:o_mg].T.astype(BF16),
                w[:, o_mg:].astype(BF16))
    tail_w = (w_bm[l].astype(BF16), w_ba[l].astype(BF16), w_out[l].astype(BF16),
              w_up[l].astype(BF16), w_down[l].astype(BF16))
    ln = jnp.stack([ln1_g[l], ln1_b[l], ln2_g[l], ln2_b[l]])
    gnorm = mlstm_norm_g[l].reshape(D_MODEL, 1)

    c_rows = jnp.concatenate([c_ctx[None, :], c, jnp.zeros((MOD_ROWS - 1 - Bd, D_MODEL), F32)], axis=0)
    mod = _modulation(c_rows, w_mod[l], b_mod[l]).reshape(MOD_ROWS, 6, D_MODEL)
    rows_ctx, rows_lat = (0, False), (1, True)

    (sgm, sga, k_new, v_new, hm, c_new, n_new, m_new, ha) = _context_front(x_prompt, mod, rows_ctx, proj_small,
                                                                          proj_big, gnorm)
    y_prompt = _tail(x_prompt, mod, rows_ctx, hm, ha, sgm, sga, tail_w, ln, "tail_ctx").reshape(x_prompt.shape)

    (qm, km, vmt, somt, gr, qa, ka, vt, sgm, sga) = _projection(x_sample, mod, rows_lat, proj_small, proj_big, _rope_tables(Td))
    past = cache_k.shape[2]
    init = (state_C[:, l], state_n[:, l].reshape(Bd, 2, NH_M, 1, DH_M), state_m[:, l].reshape(Bd, 2, NH_M, 1, 1))
    hm, = _mlstm(qm, km, vmt, gr, somt, gnorm, init, False, 1)
    vct = jnp.transpose(cache_v[:, l], (0, 2, 3, 1)).astype(BF16)
    vct = jnp.concatenate([vct, jnp.ones((Bd, N_KV, V_ROWS - DH_A, past), BF16)], axis=2)
    ctx_kv = (cache_k[:, l].reshape(Bd, past, N_KV * DH_A).astype(BF16), vct)
    ha = _attention(qa, ka, vt, ctx_kv)
    y_sample = _tail(x_sample, mod, rows_lat, hm, ha, sgm, sga, tail_w, ln, "tail_lat")

    return (y_prompt, y_sample,
            k_new.reshape(B, 1, T, N_KV, DH_A), v_new.reshape(B, 1, T, N_KV, DH_A),
            c_new.reshape(B, 1, 2, NH_M, DH_M, DH_M), n_new.reshape(B, 1, 2, NH_M, DH_M),
            m_new.reshape(B, 1, 2, NH_M))
```

```python
import functools

import jax
import jax.numpy as jnp
import numpy as np
from jax import lax
from jax.experimental import pallas as pl
from jax.experimental.pallas import tpu as pltpu

D_MODEL = 1024
NH_M = 4
DH_M = 256
N_Q = 8
N_KV = 2
G_Q = N_Q // N_KV
DH_A = 128
D_FF = 4 * D_MODEL
GRID_W = 64
N_FREQ = DH_A // 4
ROPE_BASE = 10000.0
EPS = 1e-6
DEPTH = 1
ALPHA = (2 * DEPTH) ** 0.25

CHUNK = 256
TOK_TILE = 512
TAIL_TILE = 512
TAIL_SUB = 256
Q_TILE = 1024
Q_CHAIN = 256
K_TILE = 512
LOG2E = float(np.log2(np.e))
Q_SCALE = DH_A ** -0.5 * LOG2E
ATTN_LOOKAHEAD = 4
VM_ROWS = DH_M + 16
V_ROWS = DH_A + 16
MOD_ROWS = 8

VMEM_WORK_BYTES = 20 * 1024 * 1024
VMEM_FLOOR_BYTES = 56 * 1024 * 1024

F32 = jnp.float32
BF16 = jnp.bfloat16


def _dot(a, b):
    return jnp.dot(a, b, preferred_element_type=F32)


def _dot_nt(a, b):
    return lax.dot_general(a, b, (((1,), (1,)), ((), ())), preferred_element_type=F32)


def _resident(shape):
    nd = len(shape)
    return pl.BlockSpec(shape, lambda *_: (0,) * nd, pipeline_mode=pl.Buffered(1))


def _mod_spec(first_row, per_batch):
    return pl.BlockSpec((1, 6, D_MODEL), (lambda b, t: (first_row + b, 0, 0)) if per_batch
                        else (lambda b, t: (first_row, 0, 0)))


def _call(body, name, grid, semantics, in_specs, args, out_specs, out_shape, scratch=()):
    def window_bytes(spec, a):
        buffers = 2 if spec.pipeline_mode is None else spec.pipeline_mode.buffer_count
        return buffers * int(np.prod(spec.block_shape)) * jnp.dtype(a.dtype).itemsize

    outs, out_sp = (out_shape, out_specs) if isinstance(out_shape, (list, tuple)) else ([out_shape], [out_specs])
    windows = sum(map(window_bytes, in_specs, args)) + sum(map(window_bytes, out_sp, outs))
    held = sum(int(np.prod(s.shape)) * jnp.dtype(s.dtype).itemsize for s in scratch)
    limit = max(windows + held + VMEM_WORK_BYTES, VMEM_FLOOR_BYTES)
    return pl.pallas_call(
        body, grid=grid, in_specs=in_specs, out_specs=out_specs, out_shape=out_shape, scratch_shapes=list(scratch),
        compiler_params=pltpu.CompilerParams(dimension_semantics=semantics, vmem_limit_bytes=limit),
        name=name,
    )(*args)


def _mod_kernel(c_ref, w_ref, b_ref, o_ref):
    c = c_ref[...]
    s = c * jax.nn.sigmoid(c)
    o_ref[...] = _dot(s.astype(BF16), w_ref[...].astype(BF16)) + b_ref[...]


def _modulation(c_rows, w_mod, b_mod):
    n_out = w_mod.shape[1]
    blk = D_MODEL
    in_specs = [pl.BlockSpec((MOD_ROWS, D_MODEL), lambda j: (0, 0)),
                pl.BlockSpec((D_MODEL, blk), lambda j: (0, j)),
                pl.BlockSpec((1, blk), lambda j: (0, j))]
    return _call(_mod_kernel, "modulation", (n_out // blk,), ("parallel",),
                 in_specs, (c_rows, w_mod, b_mod.reshape(1, n_out)),
                 pl.BlockSpec((MOD_ROWS, blk), lambda j: (0, j)), jax.ShapeDtypeStruct((MOD_ROWS, n_out), F32))


def _log_sigmoid(x):
    return jnp.minimum(x, 0.0) - jnp.log1p(jnp.exp(-jnp.abs(x)))


def _cummax_lanes(x, reverse):
    n = x.shape[-1]
    lane = lax.broadcasted_iota(jnp.int32, x.shape, x.ndim - 1)
    step = 1
    while step < n:
        if reverse:
            shifted, valid = pltpu.roll(x, n - step, x.ndim - 1), lane < n - step
        else:
            shifted, valid = pltpu.roll(x, step, x.ndim - 1), lane >= step
        x = jnp.maximum(x, jnp.where(valid, shifted, -jnp.inf))
        step *= 2
    return x


def _rms(t, g):
    return t * lax.rsqrt(jnp.mean(t * t, axis=-1, keepdims=True) + EPS) * g


def _proj_kernel(*refs):
    x_ref, mod_ref = refs[:2]
    mod = mod_ref[0]
    for p in range(x_ref.shape[1] // CHUNK):
        _proj_subtile(True, slice(p * CHUNK, (p + 1) * CHUNK), mod, refs)


def _context_front_kernel(*refs):
    (x_ref, mod_ref, wgt_ref, bg_ref, qg_ref, kg_ref, gn_ref, wm_ref, wmt_ref, wa_ref, wvt_ref, wmg_ref,
     sgm_o, sga_o, kc_o, vc_o, hm_o, c_o, n_o, m_o, ha_o,
     qm_s, km_s, vmt_s, somt_s, gr_s, qa_s, ka_s, vt_s, acc_s) = refs
    proj_refs = (x_ref, mod_ref, wgt_ref, bg_ref, qg_ref, kg_ref, wm_ref, wmt_ref, wa_ref, wvt_ref, wmg_ref,
                 qm_s, km_s, vmt_s, somt_s, gr_s, qa_s, ka_s, vt_s, sgm_o, sga_o, kc_o, vc_o)
    _proj_subtile(False, slice(0, CHUNK), mod_ref[0], proj_refs)
    _mlstm_kernel(False, True, 1, NH_M, qm_s, km_s, vmt_s, gr_s, somt_s, gn_ref, hm_o, c_o, n_o, m_o, acc_s)
    _attn_kernel(1, False, N_KV, qa_s, ka_s, vt_s, ha_o)


def _proj_subtile(rope, rows, mod, refs):
    if rope:
        (x_ref, mod_ref, wgt_ref, bg_ref, qg_ref, kg_ref, cos_ref, sin_ref, wm_ref, wmt_ref, wa_ref, wvt_ref, wmg_ref,
         qm_o, km_o, vmt_o, somt_o, gr_o, qa_o, ka_o, vt_o, sgm_o, sga_o) = refs
    else:
        (x_ref, mod_ref, wgt_ref, bg_ref, qg_ref, kg_ref, wm_ref, wmt_ref, wa_ref, wvt_ref, wmg_ref,
         qm_o, km_o, vmt_o, somt_o, gr_o, qa_o, ka_o, vt_o, sgm_o, sga_o, kc_o, vc_o) = refs
    tm = CHUNK
    h = (x_ref[0, rows, :] * (1.0 + mod[1:2]) + mod[0:1]).astype(BF16)

    gates = _dot_nt(wgt_ref[...], h) + bg_ref[...]
    lf = _log_sigmoid(gates)

    qm_o[0, rows, :] = _dot(h, wm_ref[:, 0:D_MODEL]).astype(BF16)
    km_o[0, rows, :] = (_dot(h, wm_ref[:, D_MODEL:2 * D_MODEL]) * (DH_M ** -0.5)).astype(BF16)
    for hh in range(NH_M):
        vmt = _dot_nt(wmt_ref[hh * DH_M:(hh + 1) * DH_M, :], h)
        vmt_o[0, hh, 0:DH_M, rows] = vmt.astype(BF16)
        vmt_o[0, hh, DH_M:VM_ROWS, rows] = jnp.ones((VM_ROWS - DH_M, tm), BF16)
    somt_o[0, :, rows] = jax.nn.sigmoid(_dot_nt(wmt_ref[D_MODEL:2 * D_MODEL, :], h)).astype(BF16)

    if rope:
        cos = cos_ref[rows, :]
        sin_s = sin_ref[rows, :]
        lane = lax.broadcasted_iota(jnp.int32, (tm, DH_A), 1)
        first_half = (lane % (2 * N_FREQ)) < N_FREQ

        def rot(t):
            partner = jnp.where(first_half, pltpu.roll(t, DH_A - N_FREQ, 1), pltpu.roll(t, N_FREQ, 1))
            return t * cos + partner * sin_s
    else:
        rot = lambda t: t

    qg = qg_ref[...]
    kg = kg_ref[...]
    head_rows = lambda g: pl.ds(rows.start * N_KV + g, tm, stride=N_KV)
    q_all = _dot(h, wa_ref[:, 0:N_Q * DH_A])
    k_all = _dot(h, wa_ref[:, N_Q * DH_A:(N_Q + N_KV) * DH_A])
    for g in range(N_Q):
        t = _rms(q_all[:, g * DH_A:(g + 1) * DH_A], qg)
        qa_o[0, rows, g * DH_A:(g + 1) * DH_A] = (rot(t) * Q_SCALE).astype(BF16)
    for g in range(N_KV):
        t = _rms(k_all[:, g * DH_A:(g + 1) * DH_A], kg)
        if not rope:
            kc_o[0, head_rows(g), :] = t
        ka_o[0, rows, g * DH_A:(g + 1) * DH_A] = rot(t).astype(BF16)
    if not rope:
        off = (N_Q + N_KV) * DH_A
        v_all = _dot(h, wa_ref[:, off:off + N_KV * DH_A])
        for g in range(N_KV):
            vc_o[0, head_rows(g), :] = v_all[:, g * DH_A:(g + 1) * DH_A]
    vt = _dot_nt(wvt_ref[...], h)
    for g in range(N_KV):
        vt_o[0, g, 0:DH_A, rows] = vt[g * DH_A:(g + 1) * DH_A].astype(BF16)
        vt_o[0, g, DH_A:V_ROWS, rows] = jnp.ones((V_ROWS - DH_A, tm), BF16)

    row = lax.broadcasted_iota(jnp.int32, (tm, tm), 0)
    col = lax.broadcasted_iota(jnp.int32, (tm, tm), 1)
    tri = jnp.where(row <= col, 1.0, 0.0).astype(BF16)
    hi = lf.astype(BF16)
    r1 = lf - hi.astype(F32)
    mid = r1.astype(BF16)
    lo = (r1 - mid.astype(F32)).astype(BF16)
    cum = (_dot(hi, tri) + _dot(mid, tri) + _dot(lo, tri))[0:8]
    lf8 = lf[0:8]
    tot = cum[:, tm - 1:tm]
    rev = tot - cum + lf8
    is_fwd = lax.broadcasted_iota(jnp.int32, (8, tm), 0) < NH_M
    a = jnp.where(is_fwd, cum, rev)
    cc = (gates[8:16] - a) * LOG2E
    a = a * LOG2E
    totb = jnp.broadcast_to(tot * LOG2E, (8, tm))
    c_pre = _cummax_lanes(cc, False)
    c_suf = _cummax_lanes(cc, True)
    for hh in range(NH_M):
        gate_rows = (a[hh:hh + 1], cc[hh:hh + 1], totb[hh:hh + 1],
                     a[NH_M + hh:NH_M + hh + 1], cc[NH_M + hh:NH_M + hh + 1], totb[NH_M + hh:NH_M + hh + 1],
                     c_pre[hh:hh + 1], c_suf[NH_M + hh:NH_M + hh + 1])
        for k, r in enumerate(gate_rows):
            gr_o[0, hh, k:k + 1, rows] = r

    sgm_o[0, rows, :] = jax.nn.sigmoid(_dot(h, wmg_ref[:, 0:D_MODEL])).astype(BF16)
    sga_o[0, rows, :] = jax.nn.sigmoid(_dot(h, wmg_ref[:, D_MODEL:2 * D_MODEL])).astype(BF16)


def _projection(x, mod, mod_rows, small, big, rope_tables):
    B, T, _ = x.shape
    tm = min(TOK_TILE, T)
    nt = T // tm
    tok = lambda width: pl.BlockSpec((1, tm, width), lambda b, t: (b, t, 0))
    in_specs = [tok(D_MODEL), _mod_spec(*mod_rows)]
    in_specs += [_resident(w.shape) for w in small]
    in_specs += [pl.BlockSpec((tm, DH_A), lambda b, t: (t, 0))] * 2
    in_specs += [_resident(w.shape) for w in big]
    args = [x, mod, *small, *rope_tables, *big]

    kv_w = N_KV * DH_A
    outs = [((B, T, D_MODEL), BF16, tok(D_MODEL)),
            ((B, T, D_MODEL), BF16, tok(D_MODEL)),
            ((B, NH_M, VM_ROWS, T), BF16,
             pl.BlockSpec((1, NH_M, VM_ROWS, tm), lambda b, t: (b, 0, 0, t))),
            ((B, D_MODEL, T), BF16, pl.BlockSpec((1, D_MODEL, tm), lambda b, t: (b, 0, t))),
            ((B, NH_M, 8, T), F32, pl.BlockSpec((1, NH_M, 8, tm), lambda b, t: (b, 0, 0, t))),
            ((B, T, D_MODEL), BF16, tok(D_MODEL)),
            ((B, T, kv_w), BF16, tok(kv_w)),
            ((B, N_KV, V_ROWS, T), BF16,
             pl.BlockSpec((1, N_KV, V_ROWS, tm), lambda b, t: (b, 0, 0, t))),
            ((B, T, D_MODEL), BF16, tok(D_MODEL)),
            ((B, T, D_MODEL), BF16, tok(D_MODEL))]
    return _call(_proj_kernel, "projection_lat", (B, nt), ("parallel", "parallel"), in_specs, args,
                 [o[2] for o in outs], [jax.ShapeDtypeStruct(o[0], o[1]) for o in outs])


def _context_front(x, mod, mod_rows, small, big, gnorm_col):
    B, T, _ = x.shape
    assert T == CHUNK, "the fused context front handles one chunk per batch row"
    kv_w = N_KV * DH_A
    tok = lambda width: pl.BlockSpec((1, T, width), lambda b, t: (b, 0, 0))
    state = lambda *tail: pl.BlockSpec((1, 2, NH_M) + tail, lambda b, t: (b, 0, 0, 0, 0))
    cache = pl.BlockSpec((1, T * N_KV, DH_A), lambda b, t: (b, 0, 0))
    in_specs = ([tok(D_MODEL), _mod_spec(*mod_rows)] + [_resident(w.shape) for w in small]
                + [_resident(gnorm_col.shape)] + [_resident(w.shape) for w in big])
    outs = [((B, T, D_MODEL), BF16, tok(D_MODEL)), ((B, T, D_MODEL), BF16, tok(D_MODEL)),
            ((B, T * N_KV, DH_A), F32, cache), ((B, T * N_KV, DH_A), F32, cache),
            ((B, T, D_MODEL), BF16, tok(D_MODEL)),
            ((B, 2, NH_M, DH_M, DH_M), F32, state(DH_M, DH_M)),
            ((B, 2, NH_M, 1, DH_M), F32, state(1, DH_M)),
            ((B, 2, NH_M, 1, 1), F32, state(1, 1)),
            ((B, T, D_MODEL), BF16, tok(D_MODEL))]
    scratch = [pltpu.VMEM((1, T, D_MODEL), BF16), pltpu.VMEM((1, T, D_MODEL), BF16),
               pltpu.VMEM((1, NH_M, VM_ROWS, T), BF16), pltpu.VMEM((1, D_MODEL, T), BF16),
               pltpu.VMEM((1, NH_M, 8, T), F32),
               pltpu.VMEM((1, T, D_MODEL), BF16), pltpu.VMEM((1, T, kv_w), BF16),
               pltpu.VMEM((1, N_KV, V_ROWS, T), BF16),
               pltpu.VMEM((NH_M, DH_M, CHUNK), F32)]
    return _call(_context_front_kernel, "context_front", (B, 1), ("parallel", "arbitrary"), in_specs,
                 [x, mod, *small, gnorm_col, *big],
                 [o[2] for o in outs], [jax.ShapeDtypeStruct(o[0], o[1]) for o in outs], scratch)


def _mlstm_kernel(has_init, emit_state, nc, nh, *refs):
    refs = list(refs)
    q_ref, k_ref, vt_ref, g_ref, somt_ref, gn_ref = refs[:6]
    refs = refs[6:]
    if has_init:
        c0_ref, n0_ref, m0_ref = refs[:3]
        refs = refs[3:]
    hm_o = refs[0]
    refs = refs[1:]
    if emit_state:
        c_o, n_o, m_o = refs[:3]
        refs = refs[3:]
    acc_ref, = refs

    L = CHUNK
    row = lax.broadcasted_iota(jnp.int32, (L, L), 0)
    col = lax.broadcasted_iota(jnp.int32, (L, L), 1)
    eye = row == col
    masks = (row <= col, row >= col)

    span = lambda c: slice(c * L, (c + 1) * L)
    feat = lambda hh: slice(hh * DH_M, (hh + 1) * DH_M)
    chunk_of = lambda d, s: s if d == 0 else nc - 1 - s
    steps = [(hh, s) for s in range(nc) for hh in range(nh)]

    def scores(hh, s):
        return [_dot_nt(k_ref[0, span(chunk_of(d, s)), feat(hh)], q_ref[0, span(chunk_of(d, s)), feat(hh)])
                for d in range(2)]

    def finish_chunk(hh, c, ht):
        ht = acc_ref[hh * nc + c] + ht
        hn = ht * lax.rsqrt(jnp.mean(ht * ht, axis=0, keepdims=True) + EPS) * gn_ref[feat(hh), :]
        hm_o[0, span(c), feat(hh)] = (hn * somt_ref[0, feat(hh), span(c)]).T.astype(BF16)

    states, m_runs, arriveds = {}, {}, {}
    for hh, s in steps:
        if s == 0:
            states[hh] = [None, None]
            if has_init:
                states[hh] = [jnp.concatenate([c0_ref[0, d, hh].T,
                                               jnp.broadcast_to(n0_ref[0, d, hh], (VM_ROWS - DH_M, DH_M))], axis=0)
                              for d in range(2)]
            m_runs[hh] = [m0_ref[0, d, hh] * LOG2E if has_init else jnp.zeros((1, 1), F32) for d in range(2)]
            arriveds[hh] = [False] * nc
        state, m_run, arrived = states[hh], m_runs[hh], arriveds[hh]
        st_cur = scores(hh, s)
        inter = [None, None]
        if state[0] is not None:
            inter = [_dot_nt(state[d].astype(BF16), q_ref[0, span(chunk_of(d, s)), feat(hh)]) for d in range(2)]
        for d in range(2):
            c = chunk_of(d, s)
            k = k_ref[0, span(c), feat(hh)]
            vt = vt_ref[0, hh, :, span(c)]
            g = g_ref[0, hh, :, span(c)]
            a_row = g[3 * d:3 * d + 1]
            c_row = g[3 * d + 1:3 * d + 2]
            tot = g[3 * d + 2:3 * d + 3, 0:1]
            m_prev = m_run[d]
            c_col = jnp.sum(jnp.where(eye, c_row, 0.0), axis=-1, keepdims=True)
            c_run = g[6 + d:7 + d]

            keep_state = emit_state or s + 1 < nc
            if keep_state:
                c_max = c_run[:, L - 1:L] if d == 0 else c_run[:, 0:1]
                m_new = tot + jnp.maximum(m_prev, c_max)
                wk = jnp.exp2(tot + c_col - m_new).astype(BF16) * k
                upd = _dot(vt, wk)
                new_state = upd if state[d] is None else jnp.exp2(tot + m_prev - m_new) * state[d] + upd

            m_rel = jnp.maximum(m_prev, c_run)
            sp = (st_cur[d] * jnp.exp2(jnp.where(masks[d], c_col, -jnp.inf) - m_rel)).astype(BF16)
            numt = _dot(vt, sp)
            if inter[d] is not None:
                numt = numt + jnp.exp2(m_prev - m_rel) * inter[d]
            den = numt[DH_M:DH_M + 1]
            ht = numt[0:DH_M] * (1.0 / jnp.maximum(jnp.abs(den), jnp.exp2(-(a_row + m_rel))))
            if arrived[c]:
                finish_chunk(hh, c, ht)
            else:
                acc_ref[hh * nc + c] = ht
                arrived[c] = True
            if keep_state:
                state[d] = new_state
                m_run[d] = m_new

        if emit_state and s == nc - 1:
            for d in range(2):
                c_o[0, d, hh] = state[d][0:DH_M].T
                n_o[0, d, hh] = state[d][DH_M:DH_M + 1]
                m_o[0, d, hh] = m_run[d] * (1.0 / LOG2E)


def _mlstm(qm, km, vmt, gr, somt, gnorm_col, init_state, emit_state, nh):
    B, T, _ = qm.shape
    nc = T // CHUNK
    has_init = init_state is not None

    seq = pl.BlockSpec((1, T, nh * DH_M), lambda b, h: (b, 0, h))
    c_spec = pl.BlockSpec((1, 2, nh, DH_M, DH_M), lambda b, h: (b, 0, h, 0, 0))
    n_spec = pl.BlockSpec((1, 2, nh, 1, DH_M), lambda b, h: (b, 0, h, 0, 0))
    m_spec = pl.BlockSpec((1, 2, nh, 1, 1), lambda b, h: (b, 0, h, 0, 0))

    in_specs = [seq, seq,
                pl.BlockSpec((1, nh, VM_ROWS, T), lambda b, h: (b, h, 0, 0)),
                pl.BlockSpec((1, nh, 8, T), lambda b, h: (b, h, 0, 0)),
                pl.BlockSpec((1, nh * DH_M, T), lambda b, h: (b, h, 0)),
                pl.BlockSpec((nh * DH_M, 1), lambda b, h: (h, 0))]
    args = [qm, km, vmt, gr, somt, gnorm_col]
    if has_init:
        in_specs += [c_spec, n_spec, m_spec]
        args += list(init_state)
    out_specs = [seq]
    out_shape = [jax.ShapeDtypeStruct((B, T, D_MODEL), BF16)]
    if emit_state:
        out_specs += [c_spec, n_spec, m_spec]
        out_shape += [jax.ShapeDtypeStruct((B, 2, NH_M, DH_M, DH_M), F32),
                      jax.ShapeDtypeStruct((B, 2, NH_M, 1, DH_M), F32),
                      jax.ShapeDtypeStruct((B, 2, NH_M, 1, 1), F32)]

    return _call(functools.partial(_mlstm_kernel, has_init, emit_state, nc, nh),
                 "mlstm_lat" if has_init else "mlstm_ctx", (B, NH_M // nh), ("parallel", "parallel"),
                 in_specs, args, out_specs, out_shape, [pltpu.VMEM((nh * nc, DH_M, CHUNK), F32)])


def _attn_kernel(n_lat_tiles, has_ctx, nkv, *refs):
    if has_ctx:
        q_ref, k_ref, vt_ref, kc_ref, vct_ref, o_ref = refs
    else:
        q_ref, k_ref, vt_ref, o_ref = refs
    tk = k_ref.shape[1] // n_lat_tiles
    head = lambda h: slice(h * DH_A, (h + 1) * DH_A)
    tiles = [(lambda h, i=i: k_ref[0, i * tk:(i + 1) * tk, head(h)],
              lambda h, i=i: vt_ref[0, h, :, i * tk:(i + 1) * tk]) for i in range(n_lat_tiles)]
    if has_ctx:
        tiles.append((lambda h: kc_ref[0, :, head(h)], lambda h: vct_ref[0, h]))
    groups = [(h, g, r) for h in range(nkv) for r in range(q_ref.shape[1] // Q_CHAIN) for g in range(G_Q)]
    chains = [(t, i) for t in range(len(tiles)) for i in range(len(groups))]
    rows = lambda i: slice(groups[i][2] * Q_CHAIN, (groups[i][2] + 1) * Q_CHAIN)
    cols = lambda i: head(groups[i][0] * G_Q + groups[i][1])

    def scores(t, i):
        return _dot_nt(tiles[t][0](groups[i][0]), q_ref[0, rows(i), cols(i)])

    m = [None] * len(groups)
    acc = [None] * len(groups)
    pending = []
    for idx in range(len(chains) + ATTN_LOOKAHEAD):
        if idx < len(chains):
            pending.append(scores(*chains[idx]))
        if idx < ATTN_LOOKAHEAD:
            continue
        t, i = chains[idx - ATTN_LOOKAHEAD]
        st = pending.pop(0)
        m_tile = jnp.max(st, axis=0, keepdims=True)
        m_new = m_tile if t == 0 else jnp.maximum(m[i], m_tile)
        pv = _dot(tiles[t][1](groups[i][0]), jnp.exp2(st - m_new).astype(BF16))
        acc[i] = pv if t == 0 else jnp.exp2(m[i] - m_new) * acc[i] + pv
        m[i] = m_new

    for i in range(len(groups)):
        out = acc[i][0:DH_A] * (1.0 / acc[i][DH_A:DH_A + 1])
        o_ref[0, rows(i), cols(i)] = out.T.astype(BF16)


def _attention(qa, ka, vt, ctx_kv):
    B, T, _ = qa.shape
    tq = min(Q_TILE, T)
    nq = T // tq
    has_ctx = ctx_kv is not None
    n_lat_tiles = max(1, T // K_TILE)
    nkv = 1 if has_ctx else N_KV
    qspec = pl.BlockSpec((1, tq, nkv * G_Q * DH_A), lambda b, h, i: (b, i, h))
    kspec = lambda tk: pl.BlockSpec((1, tk, nkv * DH_A), lambda b, h, i: (b, 0, h))
    vspec = lambda tk: pl.BlockSpec((1, nkv, V_ROWS, tk), lambda b, h, i: (b, h, 0, 0))
    in_specs = [qspec, kspec(T), vspec(T)]
    args = [qa, ka, vt]
    if has_ctx:
        tc = ctx_kv[0].shape[1]
        in_specs += [kspec(tc), vspec(tc)]
        args += list(ctx_kv)
    return _call(functools.partial(_attn_kernel, n_lat_tiles, has_ctx, nkv),
                 "attention_lat" if has_ctx else "attention_ctx", (B, N_KV // nkv, nq),
                 ("parallel", "parallel", "parallel"), in_specs, args,
                 qspec, jax.ShapeDtypeStruct((B, T, D_MODEL), BF16))


def _layer_norm(y, g, b):
    mu = jnp.mean(y, axis=-1, keepdims=True)
    yc = y - mu
    var = jnp.mean(yc * yc, axis=-1, keepdims=True)
    return yc * lax.rsqrt(var + EPS) * g + b


def _tail_kernel(x_ref, mod_ref, hm_ref, ha_ref, sgm_ref, sga_ref, ln_ref,
                 wbm_ref, wba_ref, wout_ref, wup_ref, wdown_ref, o_ref):
    mod = mod_ref[0]
    ln = ln_ref[...]
    n_sub = x_ref.shape[1] // TAIL_SUB
    rows = lambda p: slice(p * TAIL_SUB, (p + 1) * TAIL_SUB)

    def merge(p):
        merged = (sgm_ref[0, rows(p), :] * _dot(hm_ref[0, rows(p), :], wbm_ref[...])
                  + sga_ref[0, rows(p), :] * _dot(ha_ref[0, rows(p), :], wba_ref[...]))
        return ALPHA * x_ref[0, rows(p), :] + mod[2:3] * _dot(merged.astype(BF16), wout_ref[...])

    def ffn(x1):
        h = (x1 * (1.0 + mod[4:5]) + mod[3:4]).astype(BF16)
        ff = jnp.zeros_like(x1)
        for j in range(D_FF // D_MODEL):
            u = jnp.maximum(_dot(h, wup_ref[:, j * D_MODEL:(j + 1) * D_MODEL]), 0.0)
            ff = ff + _dot((u * u).astype(BF16), wdown_ref[j * D_MODEL:(j + 1) * D_MODEL, :])
        return ALPHA * x1 + mod[5:6] * ff

    y1 = [merge(p) for p in range(n_sub)]
    y2 = [ffn(_layer_norm(y1[p], ln[0:1], ln[1:2])) for p in range(n_sub)]
    for p in range(n_sub):
        o_ref[0, rows(p), :] = _layer_norm(y2[p], ln[2:3], ln[3:4])


def _tail(x, mod, mod_rows, hm, ha, sgm, sga, wts, ln, name):
    if not mod_rows[1]:
        x, hm, ha, sgm, sga = (a.reshape(1, -1, D_MODEL) for a in (x, hm, ha, sgm, sga))
    B, T, _ = x.shape
    tm = TAIL_TILE
    tok = pl.BlockSpec((1, tm, D_MODEL), lambda b, t: (b, t, 0))
    in_specs = ([tok, _mod_spec(*mod_rows), tok, tok, tok, tok, _resident(ln.shape)]
                + [_resident(w.shape) for w in wts])
    return _call(_tail_kernel, name, (B, T // tm), ("parallel", "parallel"),
                 in_specs, (x, mod, hm, ha, sgm, sga, ln, *wts), tok, jax.ShapeDtypeStruct((B, T, D_MODEL), F32))


def _rope_tables(n_tokens):
    rows = n_tokens // GRID_W
    row = np.repeat(np.arange(rows), GRID_W)
    col = np.tile(np.arange(GRID_W), rows)
    inv = ROPE_BASE ** (-np.arange(N_FREQ, dtype=np.float64) / N_FREQ)
    ang = np.stack([row, col], -1).astype(np.float64)[..., None] * inv
    ang = np.broadcast_to(ang[:, :, None, :], (n_tokens, 2, 2, N_FREQ))
    sign = np.asarray([-1.0, 1.0])[None, None, :, None]
    return (jnp.asarray(np.cos(ang).reshape(n_tokens, DH_A), F32),
            jnp.asarray((np.sin(ang) * sign).reshape(n_tokens, DH_A), F32))


def kernel(x_prompt, x_sample, cache_k, cache_v, state_C, state_n, state_m, c, c_ctx, w_mod, b_mod, w_in,
           b_gates, mlstm_norm_g, q_norm_g, k_norm_g, w_bm, w_ba, w_out, ln1_g, ln1_b, w_up, w_down,
           ln2_g, ln2_b):
    B, T, _ = x_prompt.shape
    Bd, Td, _ = x_sample.shape
    l = 0

    w = w_in[l]
    o_g = 4 * D_MODEL
    o_a = o_g + 4 * NH_M
    o_mg = o_a + (N_Q + 2 * N_KV) * DH_A
    gate_rows = np.array([4, 5, 6, 7, 12, 13, 14, 15, 0, 1, 2, 3, 8, 9, 10, 11])
    proj_small = (w[:, o_g:o_a].T[gate_rows].astype(BF16),
                  b_gates[l][gate_rows].reshape(4 * NH_M, 1),
                  q_norm_g[l].reshape(1, DH_A),
                  k_norm_g[l].reshape(1, DH_A))
    proj_big = (w[:, :2 * D_MODEL].astype(BF16),
                w[:, 2 * D_MODEL:o_g].T.astype(BF16),
                w[:, o_a:o_mg].astype(BF16),
                w[:, o_a + (N_Q + N_KV) * DH_A:o_mg].T.astype(BF16),
                w[:, o_mg:].astype(BF16))
    tail_w = (w_bm[l].astype(BF16), w_ba[l].astype(BF16), w_out[l].astype(BF16),
              w_up[l].astype(BF16), w_down[l].astype(BF16))
    ln = jnp.stack([ln1_g[l], ln1_b[l], ln2_g[l], ln2_b[l]])
    gnorm = mlstm_norm_g[l].reshape(D_MODEL, 1)

    c_rows = jnp.concatenate([c_ctx[None, :], c, jnp.zeros((MOD_ROWS - 1 - Bd, D_MODEL), F32)], axis=0)
    mod = _modulation(c_rows, w_mod[l], b_mod[l]).reshape(MOD_ROWS, 6, D_MODEL)
    rows_ctx, rows_lat = (0, False), (1, True)

    (sgm, sga, k_new, v_new, hm, c_new, n_new, m_new, ha) = _context_front(x_prompt, mod, rows_ctx, proj_small,
                                                                          proj_big, gnorm)
    y_prompt = _tail(x_prompt, mod, rows_ctx, hm, ha, sgm, sga, tail_w, ln, "tail_ctx").reshape(x_prompt.shape)

    (qm, km, vmt, somt, gr, qa, ka, vt, sgm, sga) = _projection(x_sample, mod, rows_lat, proj_small, proj_big, _rope_tables(Td))
    past = cache_k.shape[2]
    init = (state_C[:, l], state_n[:, l].reshape(Bd, 2, NH_M, 1, DH_M), state_m[:, l].reshape(Bd, 2, NH_M, 1, 1))
    hm, = _mlstm(qm, km, vmt, gr, somt, gnorm, init, False, 1)
    vct = jnp.transpose(cache_v[:, l], (0, 2, 3, 1)).astype(BF16)
    vct = jnp.concatenate([vct, jnp.ones((Bd, N_KV, V_ROWS - DH_A, past), BF16)], axis=2)
    ctx_kv = (cache_k[:, l].reshape(Bd, past, N_KV * DH_A).astype(BF16), vct)
    ha = _attention(qa, ka, vt, ctx_kv)
    y_sample = _tail(x_sample, mod, rows_lat, hm, ha, sgm, sga, tail_w, ln, "tail_lat")

    return (y_prompt, y_sample,
            k_new.reshape(B, 1, T, N_KV, DH_A), v_new.reshape(B, 1, T, N_KV, DH_A),
            c_new.reshape(B, 1, 2, NH_M, DH_M, DH_M), n_new.reshape(B, 1, 2, NH_M, DH_M),
            m_new.reshape(B, 1, 2, NH_M))
```

```python
import functools

import jax
import jax.numpy as jnp
import numpy as np
from jax import lax
from jax.experimental import pallas as pl
from jax.experimental.pallas import tpu as pltpu

D_MODEL = 1024
NH_M = 4
DH_M = 256
N_Q = 8
N_KV = 2
G_Q = N_Q // N_KV
DH_A = 128
D_FF = 4 * D_MODEL
GRID_W = 64
N_FREQ = DH_A // 4
ROPE_BASE = 10000.0
EPS = 1e-6
DEPTH = 1
ALPHA = (2 * DEPTH) ** 0.25

CHUNK = 256
TOK_TILE = 512
TAIL_TILE = 512
TAIL_SUB = 256
Q_TILE = 1024
Q_CHAIN = 256
K_TILE = 512
LOG2E = float(np.log2(np.e))
Q_SCALE = DH_A ** -0.5 * LOG2E
ATTN_LOOKAHEAD = 4
VM_ROWS = DH_M + 16
V_ROWS = DH_A + 16
MOD_ROWS = 8

VMEM_WORK_BYTES = 20 * 1024 * 1024
VMEM_FLOOR_BYTES = 56 * 1024 * 1024

F32 = jnp.float32
BF16 = jnp.bfloat16


def _dot(a, b):
    return jnp.dot(a, b, preferred_element_type=F32)


def _dot_nt(a, b):
    return lax.dot_general(a, b, (((1,), (1,)), ((), ())), preferred_element_type=F32)


def _resident(shape):
    nd = len(shape)
    return pl.BlockSpec(shape, lambda *_: (0,) * nd, pipeline_mode=pl.Buffered(1))


def _mod_spec(first_row, per_batch):
    return pl.BlockSpec((1, 6, D_MODEL), (lambda b, t: (first_row + b, 0, 0)) if per_batch
                        else (lambda b, t: (first_row, 0, 0)))


def _call(body, name, grid, semantics, in_specs, args, out_specs, out_shape, scratch=()):
    def window_bytes(spec, a):
        buffers = 2 if spec.pipeline_mode is None else spec.pipeline_mode.buffer_count
        return buffers * int(np.prod(spec.block_shape)) * jnp.dtype(a.dtype).itemsize

    outs, out_sp = (out_shape, out_specs) if isinstance(out_shape, (list, tuple)) else ([out_shape], [out_specs])
    windows = sum(map(window_bytes, in_specs, args)) + sum(map(window_bytes, out_sp, outs))
    held = sum(int(np.prod(s.shape)) * jnp.dtype(s.dtype).itemsize for s in scratch)
    limit = max(windows + held + VMEM_WORK_BYTES, VMEM_FLOOR_BYTES)
    return pl.pallas_call(
        body, grid=grid, in_specs=in_specs, out_specs=out_specs, out_shape=out_shape, scratch_shapes=list(scratch),
        compiler_params=pltpu.CompilerParams(dimension_semantics=semantics, vmem_limit_bytes=limit),
        name=name,
    )(*args)


def _mod_kernel(c_ref, w_ref, b_ref, o_ref):
    c = c_ref[...]
    s = c * jax.nn.sigmoid(c)
    o_ref[...] = _dot(s.astype(BF16), w_ref[...].astype(BF16)) + b_ref[...]


def _modulation(c_rows, w_mod, b_mod):
    n_out = w_mod.shape[1]
    blk = D_MODEL
    in_specs = [pl.BlockSpec((MOD_ROWS, D_MODEL), lambda j: (0, 0)),
                pl.BlockSpec((D_MODEL, blk), lambda j: (0, j)),
                pl.BlockSpec((1, blk), lambda j: (0, j))]
    return _call(_mod_kernel, "modulation", (n_out // blk,), ("parallel",),
                 in_specs, (c_rows, w_mod, b_mod.reshape(1, n_out)),
                 pl.BlockSpec((MOD_ROWS, blk), lambda j: (0, j)), jax.ShapeDtypeStruct((MOD_ROWS, n_out), F32))


def _log_sigmoid(x):
    return jnp.minimum(x, 0.0) - jnp.log1p(jnp.exp(-jnp.abs(x)))


def _cummax_lanes(x, reverse):
    n = x.shape[-1]
    lane = lax.broadcasted_iota(jnp.int32, x.shape, x.ndim - 1)
    step = 1
    while step < n:
        if reverse:
            shifted, valid = pltpu.roll(x, n - step, x.ndim - 1), lane < n - step
        else:
            shifted, valid = pltpu.roll(x, step, x.ndim - 1), lane >= step
        x = jnp.maximum(x, jnp.where(valid, shifted, -jnp.inf))
        step *= 2
    return x


def _rms(t, g):
    return t * lax.rsqrt(jnp.mean(t * t, axis=-1, keepdims=True) + EPS) * g


def _proj_kernel(*refs):
    x_ref, mod_ref = refs[:2]
    mod = mod_ref[0]
    for p in range(x_ref.shape[1] // CHUNK):
        _proj_subtile(True, slice(p * CHUNK, (p + 1) * CHUNK), mod, refs)


def _context_front_kernel(*refs):
    (x_ref, mod_ref, wgt_ref, bg_ref, qg_ref, kg_ref, gn_ref, wm_ref, wmt_ref, wa_ref, wvt_ref, wmg_ref,
     sgm_o, sga_o, kc_o, vc_o, hm_o, c_o, n_o, m_o, ha_o,
     qm_s, km_s, vmt_s, somt_s, gr_s, qa_s, ka_s, vt_s, acc_s) = refs
    proj_refs = (x_ref, mod_ref, wgt_ref, bg_ref, qg_ref, kg_ref, wm_ref, wmt_ref, wa_ref, wvt_ref, wmg_ref,
                 qm_s, km_s, vmt_s, somt_s, gr_s, qa_s, ka_s, vt_s, sgm_o, sga_o, kc_o, vc_o)
    _proj_subtile(False, slice(0, CHUNK), mod_ref[0], proj_refs)
    _mlstm_kernel(False, True, 1, NH_M, qm_s, km_s, vmt_s, gr_s, somt_s, gn_ref, hm_o, c_o, n_o, m_o, acc_s)
    _attn_kernel(1, False, N_KV, qa_s, ka_s, vt_s, ha_o)


def _proj_subtile(rope, rows, mod, refs):
    if rope:
        (x_ref, mod_ref, wgt_ref, bg_ref, qg_ref, kg_ref, cos_ref, sin_ref, wm_ref, wmt_ref, wa_ref, wvt_ref, wmg_ref,
         qm_o, km_o, vmt_o, somt_o, gr_o, qa_o, ka_o, vt_o, sgm_o, sga_o) = refs
    else:
        (x_ref, mod_ref, wgt_ref, bg_ref, qg_ref, kg_ref, wm_ref, wmt_ref, wa_ref, wvt_ref, wmg_ref,
         qm_o, km_o, vmt_o, somt_o, gr_o, qa_o, ka_o, vt_o, sgm_o, sga_o, kc_o, vc_o) = refs
    tm = CHUNK
    h = (x_ref[0, rows, :] * (1.0 + mod[1:2]) + mod[0:1]).astype(BF16)

    gates = _dot_nt(wgt_ref[...], h) + bg_ref[...]
    lf = _log_sigmoid(gates)

    qm_o[0, rows, :] = _dot(h, wm_ref[:, 0:D_MODEL]).astype(BF16)
    km_o[0, rows, :] = (_dot(h, wm_ref[:, D_MODEL:2 * D_MODEL]) * (DH_M ** -0.5)).astype(BF16)
    for hh in range(NH_M):
        vmt = _dot_nt(wmt_ref[hh * DH_M:(hh + 1) * DH_M, :], h)
        vmt_o[0, hh, 0:DH_M, rows] = vmt.astype(BF16)
        vmt_o[0, hh, DH_M:VM_ROWS, rows] = jnp.ones((VM_ROWS - DH_M, tm), BF16)
    somt_o[0, :, rows] = jax.nn.sigmoid(_dot_nt(wmt_ref[D_MODEL:2 * D_MODEL, :], h)).astype(BF16)

    if rope:
        cos = cos_ref[rows, :]
        sin_s = sin_ref[rows, :]
        lane = lax.broadcasted_iota(jnp.int32, (tm, DH_A), 1)
        first_half = (lane % (2 * N_FREQ)) < N_FREQ

        def rot(t):
            partner = jnp.where(first_half, pltpu.roll(t, DH_A - N_FREQ, 1), pltpu.roll(t, N_FREQ, 1))
            return t * cos + partner * sin_s
    else:
        rot = lambda t: t

    qg = qg_ref[...]
    kg = kg_ref[...]
    head_rows = lambda g: pl.ds(rows.start * N_KV + g, tm, stride=N_KV)
    q_all = _dot(h, wa_ref[:, 0:N_Q * DH_A])
    k_all = _dot(h, wa_ref[:, N_Q * DH_A:(N_Q + N_KV) * DH_A])
    for g in range(N_Q):
        t = _rms(q_all[:, g * DH_A:(g + 1) * DH_A], qg)
        qa_o[0, rows, g * DH_A:(g + 1) * DH_A] = (rot(t) * Q_SCALE).astype(BF16)
    for g in range(N_KV):
        t = _rms(k_all[:, g * DH_A:(g + 1) * DH_A], kg)
        if not rope:
            kc_o[0, head_rows(g), :] = t
        ka_o[0, rows, g * DH_A:(g + 1) * DH_A] = rot(t).astype(BF16)
    if not rope:
        off = (N_Q + N_KV) * DH_A
        v_all = _dot(h, wa_ref[:, off:off + N_KV * DH_A])
        for g in range(N_KV):
            vc_o[0, head_rows(g), :] = v_all[:, g * DH_A:(g + 1) * DH_A]
    vt = _dot_nt(wvt_ref[...], h)
    for g in range(N_KV):
        vt_o[0, g, 0:DH_A, rows] = vt[g * DH_A:(g + 1) * DH_A].astype(BF16)
        vt_o[0, g, DH_A:V_ROWS, rows] = jnp.ones((V_ROWS - DH_A, tm), BF16)

    row = lax.broadcasted_iota(jnp.int32, (tm, tm), 0)
    col = lax.broadcasted_iota(jnp.int32, (tm, tm), 1)
    tri = jnp.where(row <= col, 1.0, 0.0).astype(BF16)
    hi = lf.astype(BF16)
    r1 = lf - hi.astype(F32)
    mid = r1.astype(BF16)
    lo = (r1 - mid.astype(F32)).astype(BF16)
    cum = (_dot(hi, tri) + _dot(mid, tri) + _dot(lo, tri))[0:8]
    lf8 = lf[0:8]
    tot = cum[:, tm - 1:tm]
    rev = tot - cum + lf8
    is_fwd = lax.broadcasted_iota(jnp.int32, (8, tm), 0) < NH_M
    a = jnp.where(is_fwd, cum, rev)
    cc = (gates[8:16] - a) * LOG2E
    a = a * LOG2E
    totb = jnp.broadcast_to(tot * LOG2E, (8, tm))
    c_pre = _cummax_lanes(cc, False)
    c_suf = _cummax_lanes(cc, True)
    for hh in range(NH_M):
        gate_rows = (a[hh:hh + 1], cc[hh:hh + 1], totb[hh:hh + 1],
                     a[NH_M + hh:NH_M + hh + 1], cc[NH_M + hh:NH_M + hh + 1], totb[NH_M + hh:NH_M + hh + 1],
                     c_pre[hh:hh + 1], c_suf[NH_M + hh:NH_M + hh + 1])
        for k, r in enumerate(gate_rows):
            gr_o[0, hh, k:k + 1, rows] = r

    sgm_o[0, rows, :] = jax.nn.sigmoid(_dot(h, wmg_ref[:, 0:D_MODEL])).astype(BF16)
    sga_o[0, rows, :] = jax.nn.sigmoid(_dot(h, wmg_ref[:, D_MODEL:2 * D_MODEL])).astype(BF16)


def _projection(x, mod, mod_rows, small, big, rope_tables):
    B, T, _ = x.shape
    tm = min(TOK_TILE, T)
    nt = T // tm
    tok = lambda width: pl.BlockSpec((1, tm, width), lambda b, t: (b, t, 0))
    in_specs = [tok(D_MODEL), _mod_spec(*mod_rows)]
    in_specs += [_resident(w.shape) for w in small]
    in_specs += [pl.BlockSpec((tm, DH_A), lambda b, t: (t, 0))] * 2
    in_specs += [_resident(w.shape) for w in big]
    args = [x, mod, *small, *rope_tables, *big]

    kv_w = N_KV * DH_A
    outs = [((B, T, D_MODEL), BF16, tok(D_MODEL)),
            ((B, T, D_MODEL), BF16, tok(D_MODEL)),
            ((B, NH_M, VM_ROWS, T), BF16,
             pl.BlockSpec((1, NH_M, VM_ROWS, tm), lambda b, t: (b, 0, 0, t))),
            ((B, D_MODEL, T), BF16, pl.BlockSpec((1, D_MODEL, tm), lambda b, t: (b, 0, t))),
            ((B, NH_M, 8, T), F32, pl.BlockSpec((1, NH_M, 8, tm), lambda b, t: (b, 0, 0, t))),
            ((B, T, D_MODEL), BF16, tok(D_MODEL)),
            ((B, T, kv_w), BF16, tok(kv_w)),
            ((B, N_KV, V_ROWS, T), BF16,
             pl.BlockSpec((1, N_KV, V_ROWS, tm), lambda b, t: (b, 0, 0, t))),
            ((B, T, D_MODEL), BF16, tok(D_MODEL)),
            ((B, T, D_MODEL), BF16, tok(D_MODEL))]
    return _call(_proj_kernel, "projection_lat", (B, nt), ("parallel", "parallel"), in_specs, args,
                 [o[2] for o in outs], [jax.ShapeDtypeStruct(o[0], o[1]) for o in outs])


def _context_front(x, mod, mod_rows, small, big, gnorm_col):
    B, T, _ = x.shape
    assert T == CHUNK, "the fused context front handles one chunk per batch row"
    kv_w = N_KV * DH_A
    tok = lambda width: pl.BlockSpec((1, T, width), lambda b, t: (b, 0, 0))
    state = lambda *tail: pl.BlockSpec((1, 2, NH_M) + tail, lambda b, t: (b, 0, 0, 0, 0))
    cache = pl.BlockSpec((1, T * N_KV, DH_A), lambda b, t: (b, 0, 0))
    in_specs = ([tok(D_MODEL), _mod_spec(*mod_rows)] + [_resident(w.shape) for w in small]
                + [_resident(gnorm_col.shape)] + [_resident(w.shape) for w in big])
    outs = [((B, T, D_MODEL), BF16, tok(D_MODEL)), ((B, T, D_MODEL), BF16, tok(D_MODEL)),
            ((B, T * N_KV, DH_A), F32, cache), ((B, T * N_KV, DH_A), F32, cache),
            ((B, T, D_MODEL), BF16, tok(D_MODEL)),
            ((B, 2, NH_M, DH_M, DH_M), F32, state(DH_M, DH_M)),
            ((B, 2, NH_M, 1, DH_M), F32, state(1, DH_M)),
            ((B, 2, NH_M, 1, 1), F32, state(1, 1)),
            ((B, T, D_MODEL), BF16, tok(D_MODEL))]
    scratch = [pltpu.VMEM((1, T, D_MODEL), BF16), pltpu.VMEM((1, T, D_MODEL), BF16),
               pltpu.VMEM((1, NH_M, VM_ROWS, T), BF16), pltpu.VMEM((1, D_MODEL, T), BF16),
               pltpu.VMEM((1, NH_M, 8, T), F32),
               pltpu.VMEM((1, T, D_MODEL), BF16), pltpu.VMEM((1, T, kv_w), BF16),
               pltpu.VMEM((1, N_KV, V_ROWS, T), BF16),
               pltpu.VMEM((NH_M, DH_M, CHUNK), F32)]
    return _call(_context_front_kernel, "context_front", (B, 1), ("parallel", "arbitrary"), in_specs,
                 [x, mod, *small, gnorm_col, *big],
                 [o[2] for o in outs], [jax.ShapeDtypeStruct(o[0], o[1]) for o in outs], scratch)


def _mlstm_kernel(has_init, emit_state, nc, nh, *refs):
    refs = list(refs)
    q_ref, k_ref, vt_ref, g_ref, somt_ref, gn_ref = refs[:6]
    refs = refs[6:]
    if has_init:
        c0_ref, n0_ref, m0_ref = refs[:3]
        refs = refs[3:]
    hm_o = refs[0]
    refs = refs[1:]
    if emit_state:
        c_o, n_o, m_o = refs[:3]
        refs = refs[3:]
    acc_ref, = refs

    L = CHUNK
    row = lax.broadcasted_iota(jnp.int32, (L, L), 0)
    col = lax.broadcasted_iota(jnp.int32, (L, L), 1)
    eye = row == col
    masks = (row <= col, row >= col)

    span = lambda c: slice(c * L, (c + 1) * L)
    feat = lambda hh: slice(hh * DH_M, (hh + 1) * DH_M)
    chunk_of = lambda d, s: s if d == 0 else nc - 1 - s
    steps = [(hh, s) for s in range(nc) for hh in range(nh)]

    def scores(hh, s):
        return [_dot_nt(k_ref[0, span(chunk_of(d, s)), feat(hh)], q_ref[0, span(chunk_of(d, s)), feat(hh)])
                for d in range(2)]

    def finish_chunk(hh, c, ht):
        ht = acc_ref[hh * nc + c] + ht
        hn = ht * lax.rsqrt(jnp.mean(ht * ht, axis=0, keepdims=True) + EPS) * gn_ref[feat(hh), :]
        hm_o[0, span(c), feat(hh)] = (hn * somt_ref[0, feat(hh), span(c)]).T.astype(BF16)

    states, m_runs, arriveds = {}, {}, {}
    for hh, s in steps:
        if s == 0:
            states[hh] = [None, None]
            if has_init:
                states[hh] = [jnp.concatenate([c0_ref[0, d, hh].T,
                                               jnp.broadcast_to(n0_ref[0, d, hh], (VM_ROWS - DH_M, DH_M))], axis=0)
                              for d in range(2)]
            m_runs[hh] = [m0_ref[0, d, hh] * LOG2E if has_init else jnp.zeros((1, 1), F32) for d in range(2)]
            arriveds[hh] = [False] * nc
        state, m_run, arrived = states[hh], m_runs[hh], arriveds[hh]
        st_cur = scores(hh, s)
        inter = [None, None]
        if state[0] is not None:
            inter = [_dot_nt(state[d].astype(BF16), q_ref[0, span(chunk_of(d, s)), feat(hh)]) for d in range(2)]
        for d in range(2):
            c = chunk_of(d, s)
            k = k_ref[0, span(c), feat(hh)]
            vt = vt_ref[0, hh, :, span(c)]
            g = g_ref[0, hh, :, span(c)]
            a_row = g[3 * d:3 * d + 1]
            c_row = g[3 * d + 1:3 * d + 2]
            tot = g[3 * d + 2:3 * d + 3, 0:1]
            m_prev = m_run[d]
            c_col = jnp.sum(jnp.where(eye, c_row, 0.0), axis=-1, keepdims=True)
            c_run = g[6 + d:7 + d]

            keep_state = emit_state or s + 1 < nc
            if keep_state:
                c_max = c_run[:, L - 1:L] if d == 0 else c_run[:, 0:1]
                m_new = tot + jnp.maximum(m_prev, c_max)
                wk = jnp.exp2(tot + c_col - m_new).astype(BF16) * k
                upd = _dot(vt, wk)
                new_state = upd if state[d] is None else jnp.exp2(tot + m_prev - m_new) * state[d] + upd

            m_rel = jnp.maximum(m_prev, c_run)
            sp = (st_cur[d] * jnp.exp2(jnp.where(masks[d], c_col, -jnp.inf) - m_rel)).astype(BF16)
            numt = _dot(vt, sp)
            if inter[d] is not None:
                numt = numt + jnp.exp2(m_prev - m_rel) * inter[d]
            den = numt[DH_M:DH_M + 1]
            ht = numt[0:DH_M] * (1.0 / jnp.maximum(jnp.abs(den), jnp.exp2(-(a_row + m_rel))))
            if arrived[c]:
                finish_chunk(hh, c, ht)
            else:
                acc_ref[hh * nc + c] = ht
                arrived[c] = True
            if keep_state:
                state[d] = new_state
                m_run[d] = m_new

        if emit_state and s == nc - 1:
            for d in range(2):
                c_o[0, d, hh] = state[d][0:DH_M].T
                n_o[0, d, hh] = state[d][DH_M:DH_M + 1]
                m_o[0, d, hh] = m_run[d] * (1.0 / LOG2E)


def _mlstm(qm, km, vmt, gr, somt, gnorm_col, init_state, emit_state, nh):
    B, T, _ = qm.shape
    nc = T // CHUNK
    has_init = init_state is not None

    seq = pl.BlockSpec((1, T, nh * DH_M), lambda b, h: (b, 0, h))
    c_spec = pl.BlockSpec((1, 2, nh, DH_M, DH_M), lambda b, h: (b, 0, h, 0, 0))
    n_spec = pl.BlockSpec((1, 2, nh, 1, DH_M), lambda b, h: (b, 0, h, 0, 0))
    m_spec = pl.BlockSpec((1, 2, nh, 1, 1), lambda b, h: (b, 0, h, 0, 0))

    in_specs = [seq, seq,
                pl.BlockSpec((1, nh, VM_ROWS, T), lambda b, h: (b, h, 0, 0)),
                pl.BlockSpec((1, nh, 8, T), lambda b, h: (b, h, 0, 0)),
                pl.BlockSpec((1, nh * DH_M, T), lambda b, h: (b, h, 0)),
                pl.BlockSpec((nh * DH_M, 1), lambda b, h: (h, 0))]
    args = [qm, km, vmt, gr, somt, gnorm_col]
    if has_init:
        in_specs += [c_spec, n_spec, m_spec]
        args += list(init_state)
    out_specs = [seq]
    out_shape = [jax.ShapeDtypeStruct((B, T, D_MODEL), BF16)]
    if emit_state:
        out_specs += [c_spec, n_spec, m_spec]
        out_shape += [jax.ShapeDtypeStruct((B, 2, NH_M, DH_M, DH_M), F32),
                      jax.ShapeDtypeStruct((B, 2, NH_M, 1, DH_M), F32),
                      jax.ShapeDtypeStruct((B, 2, NH_M, 1, 1), F32)]

    return _call(functools.partial(_mlstm_kernel, has_init, emit_state, nc, nh),
                 "mlstm_lat" if has_init else "mlstm_ctx", (B, NH_M // nh), ("parallel", "parallel"),
                 in_specs, args, out_specs, out_shape, [pltpu.VMEM((nh * nc, DH_M, CHUNK), F32)])


def _attn_kernel(n_lat_tiles, has_ctx, nkv, *refs):
    if has_ctx:
        q_ref, k_ref, vt_ref, kc_ref, vct_ref, o_ref = refs
    else:
        q_ref, k_ref, vt_ref, o_ref = refs
    tk = k_ref.shape[1] // n_lat_tiles
    head = lambda h: slice(h * DH_A, (h + 1) * DH_A)
    tiles = [(lambda h, i=i: k_ref[0, i * tk:(i + 1) * tk, head(h)],
              lambda h, i=i: vt_ref[0, h, :, i * tk:(i + 1) * tk]) for i in range(n_lat_tiles)]
    if has_ctx:
        tiles.append((lambda h: kc_ref[0, :, head(h)], lambda h: vct_ref[0, h]))
    groups = [(h, g, r) for h in range(nkv) for r in range(q_ref.shape[1] // Q_CHAIN) for g in range(G_Q)]
    chains = [(t, i) for t in range(len(tiles)) for i in range(len(groups))]
    rows = lambda i: slice(groups[i][2] * Q_CHAIN, (groups[i][2] + 1) * Q_CHAIN)
    cols = lambda i: head(groups[i][0] * G_Q + groups[i][1])

    def scores(t, i):
        return _dot_nt(tiles[t][0](groups[i][0]), q_ref[0, rows(i), cols(i)])

    m = [None] * len(groups)
    acc = [None] * len(groups)
    pending = []
    for idx in range(len(chains) + ATTN_LOOKAHEAD):
        if idx < len(chains):
            pending.append(scores(*chains[idx]))
        if idx < ATTN_LOOKAHEAD:
            continue
        t, i = chains[idx - ATTN_LOOKAHEAD]
        st = pending.pop(0)
        m_tile = jnp.max(st, axis=0, keepdims=True)
        m_new = m_tile if t == 0 else jnp.maximum(m[i], m_tile)
        pv = _dot(tiles[t][1](groups[i][0]), jnp.exp2(st - m_new).astype(BF16))
        acc[i] = pv if t == 0 else jnp.exp2(m[i] - m_new) * acc[i] + pv
        m[i] = m_new

    for i in range(len(groups)):
        out = acc[i][0:DH_A] * (1.0 / acc[i][DH_A:DH_A + 1])
        o_ref[0, rows(i), cols(i)] = out.T.astype(BF16)


def _attn_cast_kernel(n_lat_tiles, n_cast, *refs):
    attn_in, w_f32 = refs[:5], refs[5:5 + n_cast]
    o_ref, w_bf16 = refs[5 + n_cast], refs[6 + n_cast:]
    _attn_kernel(n_lat_tiles, True, 1, *attn_in, o_ref)
    for src, dst in zip(w_f32, w_bf16):
        dst[...] = src[...].astype(BF16)


def _attention(qa, ka, vt, ctx_kv, cast_weights):
    B, T, _ = qa.shape
    tq = min(Q_TILE, T)
    nq = T // tq
    n_steps = B * N_KV * nq
    n_lat_tiles = max(1, T // K_TILE)
    qspec = pl.BlockSpec((1, tq, G_Q * DH_A), lambda b, h, i: (b, i, h))
    kspec = lambda tk: pl.BlockSpec((1, tk, DH_A), lambda b, h, i: (b, 0, h))
    vspec = lambda tk: pl.BlockSpec((1, 1, V_ROWS, tk), lambda b, h, i: (b, h, 0, 0))
    slab = lambda w: pl.BlockSpec((w.shape[0] // n_steps, w.shape[1]), lambda b, h, i: ((b * N_KV + h) * nq + i, 0))
    tc = ctx_kv[0].shape[1]
    in_specs = [qspec, kspec(T), vspec(T), kspec(tc), vspec(tc)] + [slab(w) for w in cast_weights]
    out_specs = [qspec] + [slab(w) for w in cast_weights]
    out_shape = [jax.ShapeDtypeStruct((B, T, D_MODEL), BF16)] + [jax.ShapeDtypeStruct(w.shape, BF16) for w in cast_weights]
    return _call(functools.partial(_attn_cast_kernel, n_lat_tiles, len(cast_weights)), "attention_lat",
                 (B, N_KV, nq), ("parallel", "parallel", "parallel"), in_specs,
                 [qa, ka, vt, *ctx_kv, *cast_weights], out_specs, out_shape)


def _layer_norm(y, g, b):
    mu = jnp.mean(y, axis=-1, keepdims=True)
    yc = y - mu
    var = jnp.mean(yc * yc, axis=-1, keepdims=True)
    return yc * lax.rsqrt(var + EPS) * g + b


def _tail_kernel(x_ref, mod_ref, hm_ref, ha_ref, sgm_ref, sga_ref, ln_ref,
                 wbm_ref, wba_ref, wout_ref, wup_ref, wdown_ref, o_ref):
    mod = mod_ref[0]
    ln = ln_ref[...]
    n_sub = x_ref.shape[1] // TAIL_SUB
    rows = lambda p: slice(p * TAIL_SUB, (p + 1) * TAIL_SUB)

    def merge(p):
        merged = (sgm_ref[0, rows(p), :] * _dot(hm_ref[0, rows(p), :], wbm_ref[...])
                  + sga_ref[0, rows(p), :] * _dot(ha_ref[0, rows(p), :], wba_ref[...]))
        return ALPHA * x_ref[0, rows(p), :] + mod[2:3] * _dot(merged.astype(BF16), wout_ref[...])

    def ffn(x1):
        h = (x1 * (1.0 + mod[4:5]) + mod[3:4]).astype(BF16)
        ff = jnp.zeros_like(x1)
        for j in range(D_FF // D_MODEL):
            u = jnp.maximum(_dot(h, wup_ref[:, j * D_MODEL:(j + 1) * D_MODEL]), 0.0)
            ff = ff + _dot((u * u).astype(BF16), wdown_ref[j * D_MODEL:(j + 1) * D_MODEL, :])
        return ALPHA * x1 + mod[5:6] * ff

    y1 = [merge(p) for p in range(n_sub)]
    y2 = [ffn(_layer_norm(y1[p], ln[0:1], ln[1:2])) for p in range(n_sub)]
    for p in range(n_sub):
        o_ref[0, rows(p), :] = _layer_norm(y2[p], ln[2:3], ln[3:4])


def _tail(x, mod, mod_rows, hm, ha, sgm, sga, wts, ln, name):
    if not mod_rows[1]:
        x, hm, ha, sgm, sga = (a.reshape(1, -1, D_MODEL) for a in (x, hm, ha, sgm, sga))
    B, T, _ = x.shape
    tm = TAIL_TILE
    tok = pl.BlockSpec((1, tm, D_MODEL), lambda b, t: (b, t, 0))
    in_specs = ([tok, _mod_spec(*mod_rows), tok, tok, tok, tok, _resident(ln.shape)]
                + [_resident(w.shape) for w in wts])
    return _call(_tail_kernel, name, (B, T // tm), ("parallel", "parallel"),
                 in_specs, (x, mod, hm, ha, sgm, sga, ln, *wts), tok, jax.ShapeDtypeStruct((B, T, D_MODEL), F32))


def _rope_tables(n_tokens):
    rows = n_tokens // GRID_W
    row = np.repeat(np.arange(rows), GRID_W)
    col = np.tile(np.arange(GRID_W), rows)
    inv = ROPE_BASE ** (-np.arange(N_FREQ, dtype=np.float64) / N_FREQ)
    ang = np.stack([row, col], -1).astype(np.float64)[..., None] * inv
    ang = np.broadcast_to(ang[:, :, None, :], (n_tokens, 2, 2, N_FREQ))
    sign = np.asarray([-1.0, 1.0])[None, None, :, None]
    return (jnp.asarray(np.cos(ang).reshape(n_tokens, DH_A), F32),
            jnp.asarray((np.sin(ang) * sign).reshape(n_tokens, DH_A), F32))


def kernel(x_prompt, x_sample, cache_k, cache_v, state_C, state_n, state_m, c, c_ctx, w_mod, b_mod, w_in,
           b_gates, mlstm_norm_g, q_norm_g, k_norm_g, w_bm, w_ba, w_out, ln1_g, ln1_b, w_up, w_down,
           ln2_g, ln2_b):
    B, T, _ = x_prompt.shape
    Bd, Td, _ = x_sample.shape
    l = 0

    w = w_in[l]
    o_g = 4 * D_MODEL
    o_a = o_g + 4 * NH_M
    o_mg = o_a + (N_Q + 2 * N_KV) * DH_A
    gate_rows = np.array([4, 5, 6, 7, 12, 13, 14, 15, 0, 1, 2, 3, 8, 9, 10, 11])
    proj_small = (w[:, o_g:o_a].T[gate_rows].astype(BF16),
                  b_gates[l][gate_rows].reshape(4 * NH_M, 1),
                  q_norm_g[l].reshape(1, DH_A),
                  k_norm_g[l].reshape(1, DH_A))
    proj_big = (w[:, :2 * D_MODEL].astype(BF16),
                w[:, 2 * D_MODEL:o_g].T.astype(BF16),
                w[:, o_a:o_mg].astype(BF16),
                w[:, o_a + (N_Q + N_KV) * DH_A:o_mg].T.astype(BF16),
                w[:, o_mg:].astype(BF16))
    tail_w_f32 = (w_bm[l], w_ba[l], w_out[l], w_up[l], w_down[l])
    ln = jnp.stack([ln1_g[l], ln1_b[l], ln2_g[l], ln2_b[l]])
    gnorm = mlstm_norm_g[l].reshape(D_MODEL, 1)

    c_rows = jnp.concatenate([c_ctx[None, :], c, jnp.zeros((MOD_ROWS - 1 - Bd, D_MODEL), F32)], axis=0)
    mod = _modulation(c_rows, w_mod[l], b_mod[l]).reshape(MOD_ROWS, 6, D_MODEL)
    rows_ctx, rows_lat = (0, False), (1, True)

    ctx = _context_front(x_prompt, mod, rows_ctx, proj_small, proj_big, gnorm)
    sgm_c, sga_c, k_new, v_new, hm_c, c_new, n_new, m_new, ha_c = ctx

    (qm, km, vmt, somt, gr, qa, ka, vt, sgm, sga) = _projection(x_sample, mod, rows_lat, proj_small, proj_big, _rope_tables(Td))
    past = cache_k.shape[2]
    init = (state_C[:, l], state_n[:, l].reshape(Bd, 2, NH_M, 1, DH_M), state_m[:, l].reshape(Bd, 2, NH_M, 1, 1))
    hm, = _mlstm(qm, km, vmt, gr, somt, gnorm, init, False, 1)
    vct = jnp.transpose(cache_v[:, l], (0, 2, 3, 1)).astype(BF16)
    vct = jnp.concatenate([vct, jnp.ones((Bd, N_KV, V_ROWS - DH_A, past), BF16)], axis=2)
    ctx_kv = (cache_k[:, l].reshape(Bd, past, N_KV * DH_A).astype(BF16), vct)
    ha, *tail_w = _attention(qa, ka, vt, ctx_kv, tail_w_f32)
    y_sample = _tail(x_sample, mod, rows_lat, hm, ha, sgm, sga, tail_w, ln, "tail_lat")
    y_prompt = _tail(x_prompt, mod, rows_ctx, hm_c, ha_c, sgm_c, sga_c, tail_w, ln, "tail_ctx").reshape(x_prompt.shape)

    return (y_prompt, y_sample,
            k_new.reshape(B, 1, T, N_KV, DH_A), v_new.reshape(B, 1, T, N_KV, DH_A),
            c_new.reshape(B, 1, 2, NH_M, DH_M, DH_M), n_new.reshape(B, 1, 2, NH_M, DH_M),
            m_new.reshape(B, 1, 2, NH_M))
```

```python
import functools

import jax
import jax.numpy as jnp
import numpy as np
from jax import lax
from jax.experimental import pallas as pl
from jax.experimental.pallas import tpu as pltpu

D_MODEL = 1024
NH_M = 4
DH_M = 256
N_Q = 8
N_KV = 2
G_Q = N_Q // N_KV
DH_A = 128
D_FF = 4 * D_MODEL
GRID_W = 64
N_FREQ = DH_A // 4
ROPE_BASE = 10000.0
EPS = 1e-6
DEPTH = 1
ALPHA = (2 * DEPTH) ** 0.25

CHUNK = 256
TOK_TILE = 512
TAIL_TILE = 512
TAIL_SUB = 256
Q_TILE = 1024
Q_CHAIN = 256
K_TILE = 512
LOG2E = float(np.log2(np.e))
Q_SCALE = DH_A ** -0.5 * LOG2E
ATTN_LOOKAHEAD = 4
V_ROWS = DH_A + 16
VM_ROWS = DH_M + 16
MOD_ROWS = 8

VMEM_WORK_BYTES = 20 * 1024 * 1024
VMEM_FLOOR_BYTES = 56 * 1024 * 1024

F32 = jnp.float32
BF16 = jnp.bfloat16


def _dot(a, b):
    return jnp.dot(a, b, preferred_element_type=F32)


def _dot_nt(a, b):
    return lax.dot_general(a, b, (((1,), (1,)), ((), ())), preferred_element_type=F32)


def _resident(shape):
    nd = len(shape)
    return pl.BlockSpec(shape, lambda *_: (0,) * nd, pipeline_mode=pl.Buffered(1))


def _mod_spec(first_row, per_batch):
    return pl.BlockSpec((1, 6, D_MODEL), (lambda b, t: (first_row + b, 0, 0)) if per_batch
                        else (lambda b, t: (first_row, 0, 0)))


def _call(body, name, grid, semantics, in_specs, args, out_specs, out_shape, scratch=()):
    def window_bytes(spec, a):
        buffers = 2 if spec.pipeline_mode is None else spec.pipeline_mode.buffer_count
        return buffers * int(np.prod(spec.block_shape)) * jnp.dtype(a.dtype).itemsize

    outs, out_sp = (out_shape, out_specs) if isinstance(out_shape, (list, tuple)) else ([out_shape], [out_specs])
    windows = sum(map(window_bytes, in_specs, args)) + sum(map(window_bytes, out_sp, outs))
    held = sum(int(np.prod(s.shape)) * jnp.dtype(s.dtype).itemsize for s in scratch)
    limit = max(windows + held + VMEM_WORK_BYTES, VMEM_FLOOR_BYTES)
    return pl.pallas_call(
        body, grid=grid, in_specs=in_specs, out_specs=out_specs, out_shape=out_shape, scratch_shapes=list(scratch),
        compiler_params=pltpu.CompilerParams(dimension_semantics=semantics, vmem_limit_bytes=limit),
        name=name,
    )(*args)


def _mod_kernel(c_ref, w_ref, b_ref, o_ref):
    c = c_ref[...]
    s = c * jax.nn.sigmoid(c)
    o_ref[...] = _dot(s.astype(BF16), w_ref[...].astype(BF16)) + b_ref[...]


def _modulation(c_rows, w_mod, b_mod):
    n_out = w_mod.shape[1]
    blk = D_MODEL
    in_specs = [pl.BlockSpec((MOD_ROWS, D_MODEL), lambda j: (0, 0)),
                pl.BlockSpec((D_MODEL, blk), lambda j: (0, j)),
                pl.BlockSpec((1, blk), lambda j: (0, j))]
    return _call(_mod_kernel, "modulation", (n_out // blk,), ("parallel",),
                 in_specs, (c_rows, w_mod, b_mod.reshape(1, n_out)),
                 pl.BlockSpec((MOD_ROWS, blk), lambda j: (0, j)), jax.ShapeDtypeStruct((MOD_ROWS, n_out), F32))


def _log_sigmoid(x):
    return jnp.minimum(x, 0.0) - jnp.log1p(jnp.exp(-jnp.abs(x)))


def _cummax_lanes(x, reverse):
    n = x.shape[-1]
    lane = lax.broadcasted_iota(jnp.int32, x.shape, x.ndim - 1)
    step = 1
    while step < n:
        if reverse:
            shifted, valid = pltpu.roll(x, n - step, x.ndim - 1), lane < n - step
        else:
            shifted, valid = pltpu.roll(x, step, x.ndim - 1), lane >= step
        x = jnp.maximum(x, jnp.where(valid, shifted, -jnp.inf))
        step *= 2
    return x


def _rms(t, g):
    return t * lax.rsqrt(jnp.mean(t * t, axis=-1, keepdims=True) + EPS) * g


def _proj_kernel(*refs):
    x_ref, mod_ref = refs[:2]
    mod = mod_ref[0]
    for p in range(x_ref.shape[1] // CHUNK):
        _proj_subtile(True, slice(p * CHUNK, (p + 1) * CHUNK), mod, refs)


def _context_front_kernel(*refs):
    (x_ref, mod_ref, wgt_ref, bg_ref, qg_ref, kg_ref, gn_ref, wm_ref, wmt_ref, wa_ref, wvt_ref, wmg_ref,
     sgm_o, sga_o, kc_o, vc_o, hm_o, c_o, n_o, m_o, ha_o,
     qm_s, km_s, vmt_s, somt_s, gr_s, qa_s, ka_s, vt_s, acc_s) = refs
    proj_refs = (x_ref, mod_ref, wgt_ref, bg_ref, qg_ref, kg_ref, wm_ref, wmt_ref, wa_ref, wvt_ref, wmg_ref,
                 qm_s, km_s, vmt_s, somt_s, gr_s, qa_s, ka_s, vt_s, sgm_o, sga_o, kc_o, vc_o)
    _proj_subtile(False, slice(0, CHUNK), mod_ref[0], proj_refs)
    _mlstm_kernel(False, True, 1, NH_M, qm_s, km_s, vmt_s, gr_s, somt_s, gn_ref, hm_o, c_o, n_o, m_o, acc_s)
    _attn_kernel(1, False, N_KV, qa_s, ka_s, vt_s, ha_o)


def _proj_subtile(rope, rows, mod, refs):
    if rope:
        (x_ref, mod_ref, wgt_ref, bg_ref, qg_ref, kg_ref, cos_ref, sin_ref, wm_ref, wmt_ref, wa_ref, wvt_ref, wmg_ref,
         qm_o, km_o, vmt_o, somt_o, gr_o, qa_o, ka_o, vt_o, sgm_o, sga_o) = refs
    else:
        (x_ref, mod_ref, wgt_ref, bg_ref, qg_ref, kg_ref, wm_ref, wmt_ref, wa_ref, wvt_ref, wmg_ref,
         qm_o, km_o, vmt_o, somt_o, gr_o, qa_o, ka_o, vt_o, sgm_o, sga_o, kc_o, vc_o) = refs
    tm = CHUNK
    h = (x_ref[0, rows, :] * (1.0 + mod[1:2]) + mod[0:1]).astype(BF16)

    gates = _dot_nt(wgt_ref[...], h) + bg_ref[...]
    lf = _log_sigmoid(gates)

    qm_o[0, rows, :] = _dot(h, wm_ref[:, 0:D_MODEL]).astype(BF16)
    km_o[0, rows, :] = (_dot(h, wm_ref[:, D_MODEL:2 * D_MODEL]) * (DH_M ** -0.5)).astype(BF16)
    for hh in range(NH_M):
        vmt = _dot_nt(wmt_ref[hh * DH_M:(hh + 1) * DH_M, :], h)
        vmt_o[0, hh, 0:DH_M, rows] = vmt.astype(BF16)
        vmt_o[0, hh, DH_M:VM_ROWS, rows] = jnp.ones((VM_ROWS - DH_M, tm), BF16)
    somt_o[0, :, rows] = jax.nn.sigmoid(_dot_nt(wmt_ref[D_MODEL:2 * D_MODEL, :], h)).astype(BF16)

    if rope:
        cos = cos_ref[rows, :]
        sin_s = sin_ref[rows, :]
        lane = lax.broadcasted_iota(jnp.int32, (tm, DH_A), 1)
        first_half = (lane % (2 * N_FREQ)) < N_FREQ

        def rot(t):
            partner = jnp.where(first_half, pltpu.roll(t, DH_A - N_FREQ, 1), pltpu.roll(t, N_FREQ, 1))
            return t * cos + partner * sin_s
    else:
        rot = lambda t: t

    qg = qg_ref[...]
    kg = kg_ref[...]
    head_rows = lambda g: pl.ds(rows.start * N_KV + g, tm, stride=N_KV)
    q_all = _dot(h, wa_ref[:, 0:N_Q * DH_A])
    k_all = _dot(h, wa_ref[:, N_Q * DH_A:(N_Q + N_KV) * DH_A])
    for g in range(N_Q):
        t = _rms(q_all[:, g * DH_A:(g + 1) * DH_A], qg)
        qa_o[0, rows, g * DH_A:(g + 1) * DH_A] = (rot(t) * Q_SCALE).astype(BF16)
    for g in range(N_KV):
        t = _rms(k_all[:, g * DH_A:(g + 1) * DH_A], kg)
        if not rope:
            kc_o[0, head_rows(g), :] = t
        ka_o[0, rows, g * DH_A:(g + 1) * DH_A] = rot(t).astype(BF16)
    if not rope:
        off = (N_Q + N_KV) * DH_A
        v_all = _dot(h, wa_ref[:, off:off + N_KV * DH_A])
        for g in range(N_KV):
            vc_o[0, head_rows(g), :] = v_all[:, g * DH_A:(g + 1) * DH_A]
    vt = _dot_nt(wvt_ref[...], h)
    for g in range(N_KV):
        vt_o[0, g, 0:DH_A, rows] = vt[g * DH_A:(g + 1) * DH_A].astype(BF16)
        vt_o[0, g, DH_A:V_ROWS, rows] = jnp.ones((V_ROWS - DH_A, tm), BF16)

    row = lax.broadcasted_iota(jnp.int32, (tm, tm), 0)
    col = lax.broadcasted_iota(jnp.int32, (tm, tm), 1)
    tri = jnp.where(row <= col, 1.0, 0.0).astype(BF16)
    hi = lf.astype(BF16)
    r1 = lf - hi.astype(F32)
    mid = r1.astype(BF16)
    lo = (r1 - mid.astype(F32)).astype(BF16)
    cum = (_dot(hi, tri) + _dot(mid, tri) + _dot(lo, tri))[0:8]
    lf8 = lf[0:8]
    tot = cum[:, tm - 1:tm]
    rev = tot - cum + lf8
    is_fwd = lax.broadcasted_iota(jnp.int32, (8, tm), 0) < NH_M
    a = jnp.where(is_fwd, cum, rev)
    cc = (gates[8:16] - a) * LOG2E
    a = a * LOG2E
    totb = jnp.broadcast_to(tot * LOG2E, (8, tm))
    c_pre = _cummax_lanes(cc, False)
    c_suf = _cummax_lanes(cc, True)
    for hh in range(NH_M):
        gate_rows = (a[hh:hh + 1], cc[hh:hh + 1], totb[hh:hh + 1],
                     a[NH_M + hh:NH_M + hh + 1], cc[NH_M + hh:NH_M + hh + 1], totb[NH_M + hh:NH_M + hh + 1],
                     c_pre[hh:hh + 1], c_suf[NH_M + hh:NH_M + hh + 1])
        for k, r in enumerate(gate_rows):
            gr_o[0, hh, k:k + 1, rows] = r

    sgm_o[0, rows, :] = jax.nn.sigmoid(_dot(h, wmg_ref[:, 0:D_MODEL])).astype(BF16)
    sga_o[0, rows, :] = jax.nn.sigmoid(_dot(h, wmg_ref[:, D_MODEL:2 * D_MODEL])).astype(BF16)


def _projection(x, mod, mod_rows, small, big, rope_tables):
    B, T, _ = x.shape
    tm = min(TOK_TILE, T)
    nt = T // tm
    tok = lambda width: pl.BlockSpec((1, tm, width), lambda b, t: (b, t, 0))
    in_specs = [tok(D_MODEL), _mod_spec(*mod_rows)]
    in_specs += [_resident(w.shape) for w in small]
    in_specs += [pl.BlockSpec((tm, DH_A), lambda b, t: (t, 0))] * 2
    in_specs += [_resident(w.shape) for w in big]
    args = [x, mod, *small, *rope_tables, *big]

    kv_w = N_KV * DH_A
    outs = [((B, T, D_MODEL), BF16, tok(D_MODEL)),
            ((B, T, D_MODEL), BF16, tok(D_MODEL)),
            ((B, NH_M, VM_ROWS, T), BF16,
             pl.BlockSpec((1, NH_M, VM_ROWS, tm), lambda b, t: (b, 0, 0, t))),
            ((B, D_MODEL, T), BF16, pl.BlockSpec((1, D_MODEL, tm), lambda b, t: (b, 0, t))),
            ((B, NH_M, 8, T), F32, pl.BlockSpec((1, NH_M, 8, tm), lambda b, t: (b, 0, 0, t))),
            ((B, T, D_MODEL), BF16, tok(D_MODEL)),
            ((B, T, kv_w), BF16, tok(kv_w)),
            ((B, N_KV, V_ROWS, T), BF16,
             pl.BlockSpec((1, N_KV, V_ROWS, tm), lambda b, t: (b, 0, 0, t))),
            ((B, T, D_MODEL), BF16, tok(D_MODEL)),
            ((B, T, D_MODEL), BF16, tok(D_MODEL))]
    return _call(_proj_kernel, "projection_lat", (B, nt), ("parallel", "parallel"), in_specs, args,
                 [o[2] for o in outs], [jax.ShapeDtypeStruct(o[0], o[1]) for o in outs])


def _context_front(x, mod, mod_rows, small, big, gnorm_col):
    B, T, _ = x.shape
    assert T == CHUNK, "the fused context front handles one chunk per batch row"
    kv_w = N_KV * DH_A
    tok = lambda width: pl.BlockSpec((1, T, width), lambda b, t: (b, 0, 0))
    state = lambda *tail: pl.BlockSpec((1, 2, NH_M) + tail, lambda b, t: (b, 0, 0, 0, 0))
    cache = pl.BlockSpec((1, T * N_KV, DH_A), lambda b, t: (b, 0, 0))
    in_specs = ([tok(D_MODEL), _mod_spec(*mod_rows)] + [_resident(w.shape) for w in small]
                + [_resident(gnorm_col.shape)] + [_resident(w.shape) for w in big])
    outs = [((B, T, D_MODEL), BF16, tok(D_MODEL)), ((B, T, D_MODEL), BF16, tok(D_MODEL)),
            ((B, T * N_KV, DH_A), F32, cache), ((B, T * N_KV, DH_A), F32, cache),
            ((B, T, D_MODEL), BF16, tok(D_MODEL)),
            ((B, 2, NH_M, DH_M, DH_M), F32, state(DH_M, DH_M)),
            ((B, 2, NH_M, 1, DH_M), F32, state(1, DH_M)),
            ((B, 2, NH_M, 1, 1), F32, state(1, 1)),
            ((B, T, D_MODEL), BF16, tok(D_MODEL))]
    scratch = [pltpu.VMEM((1, T, D_MODEL), BF16), pltpu.VMEM((1, T, D_MODEL), BF16),
               pltpu.VMEM((1, NH_M, VM_ROWS, T), BF16), pltpu.VMEM((1, D_MODEL, T), BF16),
               pltpu.VMEM((1, NH_M, 8, T), F32),
               pltpu.VMEM((1, T, D_MODEL), BF16), pltpu.VMEM((1, T, kv_w), BF16),
               pltpu.VMEM((1, N_KV, V_ROWS, T), BF16),
               pltpu.VMEM((NH_M, DH_M, CHUNK), F32)]
    return _call(_context_front_kernel, "context_front", (B, 1), ("parallel", "arbitrary"), in_specs,
                 [x, mod, *small, gnorm_col, *big],
                 [o[2] for o in outs], [jax.ShapeDtypeStruct(o[0], o[1]) for o in outs], scratch)


def _mlstm_kernel(has_init, emit_state, nc, nh, *refs):
    refs = list(refs)
    q_ref, k_ref, vt_ref, g_ref, somt_ref, gn_ref = refs[:6]
    refs = refs[6:]
    if has_init:
        c0_ref, n0_ref, m0_ref = refs[:3]
        refs = refs[3:]
    hm_o = refs[0]
    refs = refs[1:]
    if emit_state:
        c_o, n_o, m_o = refs[:3]
        refs = refs[3:]
    acc_ref, = refs

    L = CHUNK
    row = lax.broadcasted_iota(jnp.int32, (L, L), 0)
    col = lax.broadcasted_iota(jnp.int32, (L, L), 1)
    eye = row == col
    masks = (row <= col, row >= col)

    span = lambda c: slice(c * L, (c + 1) * L)
    feat = lambda hh: slice(hh * DH_M, (hh + 1) * DH_M)
    chunk_of = lambda d, s: s if d == 0 else nc - 1 - s
    steps = [(hh, s) for s in range(nc) for hh in range(nh)]

    def scores(hh, s):
        return [_dot_nt(k_ref[0, span(chunk_of(d, s)), feat(hh)], q_ref[0, span(chunk_of(d, s)), feat(hh)])
                for d in range(2)]

    def finish_chunk(hh, c, ht):
        ht = acc_ref[hh * nc + c] + ht
        hn = ht * lax.rsqrt(jnp.mean(ht * ht, axis=0, keepdims=True) + EPS) * gn_ref[feat(hh), :]
        hm_o[0, span(c), feat(hh)] = (hn * somt_ref[0, feat(hh), span(c)]).T.astype(BF16)

    states, m_runs, arriveds = {}, {}, {}
    for hh, s in steps:
        if s == 0:
            states[hh] = [None, None]
            if has_init:
                states[hh] = [jnp.concatenate([c0_ref[0, d, hh].T,
                                               jnp.broadcast_to(n0_ref[0, d, hh], (VM_ROWS - DH_M, DH_M))], axis=0)
                              for d in range(2)]
            m_runs[hh] = [m0_ref[0, d, hh] * LOG2E if has_init else jnp.zeros((1, 1), F32) for d in range(2)]
            arriveds[hh] = [False] * nc
        state, m_run, arrived = states[hh], m_runs[hh], arriveds[hh]
        st_cur = scores(hh, s)
        inter = [None, None]
        if state[0] is not None:
            inter = [_dot_nt(state[d].astype(BF16), q_ref[0, span(chunk_of(d, s)), feat(hh)]) for d in range(2)]
        for d in range(2):
            c = chunk_of(d, s)
            k = k_ref[0, span(c), feat(hh)]
            vt = vt_ref[0, hh, :, span(c)]
            g = g_ref[0, hh, :, span(c)]
            a_row = g[3 * d:3 * d + 1]
            c_row = g[3 * d + 1:3 * d + 2]
            tot = g[3 * d + 2:3 * d + 3, 0:1]
            m_prev = m_run[d]
            c_col = jnp.sum(jnp.where(eye, c_row, 0.0), axis=-1, keepdims=True)
            c_run = g[6 + d:7 + d]

            keep_state = emit_state or s + 1 < nc
            if keep_state:
                c_max = c_run[:, L - 1:L] if d == 0 else c_run[:, 0:1]
                m_new = tot + jnp.maximum(m_prev, c_max)
                wk = jnp.exp2(tot + c_col - m_new).astype(BF16) * k
                upd = _dot(vt, wk)
                new_state = upd if state[d] is None else jnp.exp2(tot + m_prev - m_new) * state[d] + upd

            m_rel = jnp.maximum(m_prev, c_run)
            sp = (st_cur[d] * jnp.exp2(jnp.where(masks[d], c_col, -jnp.inf) - m_rel)).astype(BF16)
            numt = _dot(vt, sp)
            if inter[d] is not None:
                numt = numt + jnp.exp2(m_prev - m_rel) * inter[d]
            den = numt[DH_M:DH_M + 1]
            ht = numt[0:DH_M] * (1.0 / jnp.maximum(jnp.abs(den), jnp.exp2(-(a_row + m_rel))))
            if arrived[c]:
                finish_chunk(hh, c, ht)
            else:
                acc_ref[hh * nc + c] = ht
                arrived[c] = True
            if keep_state:
                state[d] = new_state
                m_run[d] = m_new

        if emit_state and s == nc - 1:
            for d in range(2):
                c_o[0, d, hh] = state[d][0:DH_M].T
                n_o[0, d, hh] = state[d][DH_M:DH_M + 1]
                m_o[0, d, hh] = m_run[d] * (1.0 / LOG2E)


def _mlstm(qm, km, vmt, gr, somt, gnorm_col, init_state, emit_state, nh):
    B, T, _ = qm.shape
    nc = T // CHUNK
    has_init = init_state is not None

    seq = pl.BlockSpec((1, T, nh * DH_M), lambda b, h: (b, 0, h))
    c_spec = pl.BlockSpec((1, 2, nh, DH_M, DH_M), lambda b, h: (b, 0, h, 0, 0))
    n_spec = pl.BlockSpec((1, 2, nh, 1, DH_M), lambda b, h: (b, 0, h, 0, 0))
    m_spec = pl.BlockSpec((1, 2, nh, 1, 1), lambda b, h: (b, 0, h, 0, 0))

    in_specs = [seq, seq,
                pl.BlockSpec((1, nh, VM_ROWS, T), lambda b, h: (b, h, 0, 0)),
                pl.BlockSpec((1, nh, 8, T), lambda b, h: (b, h, 0, 0)),
                pl.BlockSpec((1, nh * DH_M, T), lambda b, h: (b, h, 0)),
                pl.BlockSpec((nh * DH_M, 1), lambda b, h: (h, 0))]
    args = [qm, km, vmt, gr, somt, gnorm_col]
    if has_init:
        in_specs += [c_spec, n_spec, m_spec]
        args += list(init_state)
    out_specs = [seq]
    out_shape = [jax.ShapeDtypeStruct((B, T, D_MODEL), BF16)]
    if emit_state:
        out_specs += [c_spec, n_spec, m_spec]
        out_shape += [jax.ShapeDtypeStruct((B, 2, NH_M, DH_M, DH_M), F32),
                      jax.ShapeDtypeStruct((B, 2, NH_M, 1, DH_M), F32),
                      jax.ShapeDtypeStruct((B, 2, NH_M, 1, 1), F32)]

    return _call(functools.partial(_mlstm_kernel, has_init, emit_state, nc, nh),
                 "mlstm_lat" if has_init else "mlstm_ctx", (B, NH_M // nh), ("parallel", "parallel"),
                 in_specs, args, out_specs, out_shape, [pltpu.VMEM((nh * nc, DH_M, CHUNK), F32)])


def _attn_kernel(n_lat_tiles, has_ctx, nkv, *refs):
    if has_ctx:
        q_ref, k_ref, vt_ref, kc_ref, vct_ref, o_ref = refs
    else:
        q_ref, k_ref, vt_ref, o_ref = refs
    tk = k_ref.shape[1] // n_lat_tiles
    head = lambda h: slice(h * DH_A, (h + 1) * DH_A)
    tiles = [(lambda h, i=i: k_ref[0, i * tk:(i + 1) * tk, head(h)],
              lambda h, i=i: vt_ref[0, h, :, i * tk:(i + 1) * tk]) for i in range(n_lat_tiles)]
    if has_ctx:
        tiles.append((lambda h: kc_ref[0, :, head(h)], lambda h: vct_ref[0, h]))
    groups = [(h, g, r) for h in range(nkv) for r in range(q_ref.shape[1] // Q_CHAIN) for g in range(G_Q)]
    chains = [(t, i) for t in range(len(tiles)) for i in range(len(groups))]
    rows = lambda i: slice(groups[i][2] * Q_CHAIN, (groups[i][2] + 1) * Q_CHAIN)
    cols = lambda i: head(groups[i][0] * G_Q + groups[i][1])

    def scores(t, i):
        return _dot_nt(tiles[t][0](groups[i][0]), q_ref[0, rows(i), cols(i)])

    m = [None] * len(groups)
    acc = [None] * len(groups)
    pending = []
    for idx in range(len(chains) + ATTN_LOOKAHEAD):
        if idx < len(chains):
            pending.append(scores(*chains[idx]))
        if idx < ATTN_LOOKAHEAD:
            continue
        t, i = chains[idx - ATTN_LOOKAHEAD]
        st = pending.pop(0)
        m_tile = jnp.max(st, axis=0, keepdims=True)
        m_new = m_tile if t == 0 else jnp.maximum(m[i], m_tile)
        pv = _dot(tiles[t][1](groups[i][0]), jnp.exp2(st - m_new).astype(BF16))
        acc[i] = pv if t == 0 else jnp.exp2(m[i] - m_new) * acc[i] + pv
        m[i] = m_new

    for i in range(len(groups)):
        out = acc[i][0:DH_A] * (1.0 / acc[i][DH_A:DH_A + 1])
        o_ref[0, rows(i), cols(i)] = out.T.astype(BF16)


def _attn_cast_kernel(n_lat_tiles, n_cast, *refs):
    attn_in, w_f32 = refs[:5], refs[5:5 + n_cast]
    o_ref, w_bf16 = refs[5 + n_cast], refs[6 + n_cast:]
    _attn_kernel(n_lat_tiles, True, 1, *attn_in, o_ref)
    for src, dst in zip(w_f32, w_bf16):
        dst[...] = src[...].astype(BF16)


def _attention(qa, ka, vt, ctx_kv, cast_weights):
    B, T, _ = qa.shape
    tq = min(Q_TILE, T)
    nq = T // tq
    n_steps = B * N_KV * nq
    n_lat_tiles = max(1, T // K_TILE)
    qspec = pl.BlockSpec((1, tq, G_Q * DH_A), lambda b, h, i: (b, i, h))
    kspec = lambda tk: pl.BlockSpec((1, tk, DH_A), lambda b, h, i: (b, 0, h))
    vspec = lambda tk: pl.BlockSpec((1, 1, V_ROWS, tk), lambda b, h, i: (b, h, 0, 0))
    slab = lambda w: pl.BlockSpec((w.shape[0] // n_steps, w.shape[1]), lambda b, h, i: ((b * N_KV + h) * nq + i, 0))
    tc = ctx_kv[0].shape[1]
    in_specs = [qspec, kspec(T), vspec(T), kspec(tc), vspec(tc)] + [slab(w) for w in cast_weights]
    out_specs = [qspec] + [slab(w) for w in cast_weights]
    out_shape = [jax.ShapeDtypeStruct((B, T, D_MODEL), BF16)] + [jax.ShapeDtypeStruct(w.shape, BF16) for w in cast_weights]
    return _call(functools.partial(_attn_cast_kernel, n_lat_tiles, len(cast_weights)), "attention_lat",
                 (B, N_KV, nq), ("parallel", "parallel", "parallel"), in_specs,
                 [qa, ka, vt, *ctx_kv, *cast_weights], out_specs, out_shape)


def _layer_norm(y, g, b):
    mu = jnp.mean(y, axis=-1, keepdims=True)
    yc = y - mu
    var = jnp.mean(yc * yc, axis=-1, keepdims=True)
    return yc * lax.rsqrt(var + EPS) * g + b


def _tail_kernel(x_ref, mod_ref, hm_ref, ha_ref, sgm_ref, sga_ref, ln_ref,
                 wbm_ref, wba_ref, wout_ref, wup_ref, wdown_ref, o_ref):
    mod = mod_ref[0]
    ln = ln_ref[...]
    n_sub = x_ref.shape[1] // TAIL_SUB
    rows = lambda p: slice(p * TAIL_SUB, (p + 1) * TAIL_SUB)

    def merge(p):
        merged = (sgm_ref[0, rows(p), :] * _dot(hm_ref[0, rows(p), :], wbm_ref[...])
                  + sga_ref[0, rows(p), :] * _dot(ha_ref[0, rows(p), :], wba_ref[...]))
        return ALPHA * x_ref[0, rows(p), :] + mod[2:3] * _dot(merged.astype(BF16), wout_ref[...])

    def ffn(x1):
        h = (x1 * (1.0 + mod[4:5]) + mod[3:4]).astype(BF16)
        ff = jnp.zeros_like(x1)
        for j in range(D_FF // D_MODEL):
            u = jnp.maximum(_dot(h, wup_ref[:, j * D_MODEL:(j + 1) * D_MODEL]), 0.0)
            ff = ff + _dot((u * u).astype(BF16), wdown_ref[j * D_MODEL:(j + 1) * D_MODEL, :])
        return ALPHA * x1 + mod[5:6] * ff

    y1 = [merge(p) for p in range(n_sub)]
    y2 = [ffn(_layer_norm(y1[p], ln[0:1], ln[1:2])) for p in range(n_sub)]
    for p in range(n_sub):
        o_ref[0, rows(p), :] = _layer_norm(y2[p], ln[2:3], ln[3:4])


def _tail(x, mod, mod_rows, hm, ha, sgm, sga, wts, ln, name):
    if not mod_rows[1]:
        x, hm, ha, sgm, sga = (a.reshape(1, -1, D_MODEL) for a in (x, hm, ha, sgm, sga))
    B, T, _ = x.shape
    tm = TAIL_TILE
    tok = pl.BlockSpec((1, tm, D_MODEL), lambda b, t: (b, t, 0))
    in_specs = ([tok, _mod_spec(*mod_rows), tok, tok, tok, tok, _resident(ln.shape)]
                + [_resident(w.shape) for w in wts])
    return _call(_tail_kernel, name, (B, T // tm), ("parallel", "parallel"),
                 in_specs, (x, mod, hm, ha, sgm, sga, ln, *wts), tok, jax.ShapeDtypeStruct((B, T, D_MODEL), F32))


def _rope_tables(n_tokens):
    rows = n_tokens // GRID_W
    row = np.repeat(np.arange(rows), GRID_W)
    col = np.tile(np.arange(GRID_W), rows)
    inv = ROPE_BASE ** (-np.arange(N_FREQ, dtype=np.float64) / N_FREQ)
    ang = np.stack([row, col], -1).astype(np.float64)[..., None] * inv
    ang = np.broadcast_to(ang[:, :, None, :], (n_tokens, 2, 2, N_FREQ))
    sign = np.asarray([-1.0, 1.0])[None, None, :, None]
    return (jnp.asarray(np.cos(ang).reshape(n_tokens, DH_A), F32),
            jnp.asarray((np.sin(ang) * sign).reshape(n_tokens, DH_A), F32))


def kernel(x_prompt, x_sample, cache_k, cache_v, state_C, state_n, state_m, c, c_ctx, w_mod, b_mod, w_in,
           b_gates, mlstm_norm_g, q_norm_g, k_norm_g, w_bm, w_ba, w_out, ln1_g, ln1_b, w_up, w_down,
           ln2_g, ln2_b):
    B, T, _ = x_prompt.shape
    Bd, Td, _ = x_sample.shape
    l = 0

    w = w_in[l]
    o_g = 4 * D_MODEL
    o_a = o_g + 4 * NH_M
    o_mg = o_a + (N_Q + 2 * N_KV) * DH_A
    gate_rows = np.array([4, 5, 6, 7, 12, 13, 14, 15, 0, 1, 2, 3, 8, 9, 10, 11])
    proj_small = (w[:, o_g:o_a].T[gate_rows].astype(BF16),
                  b_gates[l][gate_rows].reshape(4 * NH_M, 1),
                  q_norm_g[l].reshape(1, DH_A),
                  k_norm_g[l].reshape(1, DH_A))
    proj_big = (w[:, :2 * D_MODEL].astype(BF16),
                w[:, 2 * D_MODEL:o_g].T.astype(BF16),
                w[:, o_a:o_mg].astype(BF16),
                w[:, o_a + (N_Q + N_KV) * DH_A:o_mg].T.astype(BF16),
                w[:, o_mg:].astype(BF16))
    tail_w_f32 = (w_bm[l], w_ba[l], w_out[l], w_up[l], w_down[l])
    ln = jnp.stack([ln1_g[l], ln1_b[l], ln2_g[l], ln2_b[l]])
    gnorm = mlstm_norm_g[l].reshape(D_MODEL, 1)

    c_rows = jnp.concatenate([c_ctx[None, :], c, jnp.zeros((MOD_ROWS - 1 - Bd, D_MODEL), F32)], axis=0)
    mod = _modulation(c_rows, w_mod[l], b_mod[l]).reshape(MOD_ROWS, 6, D_MODEL)
    rows_ctx, rows_lat = (0, False), (1, True)

    ctx = _context_front(x_prompt, mod, rows_ctx, proj_small, proj_big, gnorm)
    sgm_c, sga_c, k_new, v_new, hm_c, c_new, n_new, m_new, ha_c = ctx

    (qm, km, vmt, somt, gr, qa, ka, vt, sgm, sga) = _projection(x_sample, mod, rows_lat, proj_small, proj_big, _rope_tables(Td))
    past = cache_k.shape[2]
    init = (state_C[:, l], state_n[:, l].reshape(Bd, 2, NH_M, 1, DH_M), state_m[:, l].reshape(Bd, 2, NH_M, 1, 1))
    hm, = _mlstm(qm, km, vmt, gr, somt, gnorm, init, False, 1)
    vct = jnp.transpose(cache_v[:, l], (0, 2, 3, 1)).astype(BF16)
    vct = jnp.concatenate([vct, jnp.ones((Bd, N_KV, V_ROWS - DH_A, past), BF16)], axis=2)
    ctx_kv = (cache_k[:, l].reshape(Bd, past, N_KV * DH_A).astype(BF16), vct)
    ha, *tail_w = _attention(qa, ka, vt, ctx_kv, tail_w_f32)
    y_sample = _tail(x_sample, mod, rows_lat, hm, ha, sgm, sga, tail_w, ln, "tail_lat")
    y_prompt = _tail(x_prompt, mod, rows_ctx, hm_c, ha_c, sgm_c, sga_c, tail_w, ln, "tail_ctx").reshape(x_prompt.shape)

    return (y_prompt, y_sample,
            k_new.reshape(B, 1, T, N_KV, DH_A), v_new.reshape(B, 1, T, N_KV, DH_A),
            c_new.reshape(B, 1, 2, NH_M, DH_M, DH_M), n_new.reshape(B, 1, 2, NH_M, DH_M),
            m_new.reshape(B, 1, 2, NH_M))
```

```python
import functools

import jax
import jax.numpy as jnp
import numpy as np
from jax import lax
from jax.experimental import pallas as pl
from jax.experimental.pallas import tpu as pltpu

D_MODEL = 1024
NH_M = 4
DH_M = 256
N_Q = 8
N_KV = 2
G_Q = N_Q // N_KV
DH_A = 128
D_FF = 4 * D_MODEL
GRID_W = 64
N_FREQ = DH_A // 4
ROPE_BASE = 10000.0
EPS = 1e-6
DEPTH = 1
ALPHA = (2 * DEPTH) ** 0.25

CHUNK = 256
TOK_TILE = 512
TAIL_TILE = 512
TAIL_SUB = 256
Q_TILE = 1024
Q_CHAIN = 256
K_TILE = 512
LOG2E = float(np.log2(np.e))
Q_SCALE = DH_A ** -0.5 * LOG2E
ATTN_LOOKAHEAD = 4
V_ROWS = DH_A + 16
VM_ROWS = DH_M + 16
MOD_ROWS = 8

VMEM_WORK_BYTES = 20 * 1024 * 1024
VMEM_FLOOR_BYTES = 56 * 1024 * 1024

F32 = jnp.float32
BF16 = jnp.bfloat16


def _dot(a, b):
    return jnp.dot(a, b, preferred_element_type=F32)


def _dot_nt(a, b):
    return lax.dot_general(a, b, (((1,), (1,)), ((), ())), preferred_element_type=F32)


def _resident(shape):
    nd = len(shape)
    return pl.BlockSpec(shape, lambda *_: (0,) * nd, pipeline_mode=pl.Buffered(1))


def _mod_spec(first_row, per_batch):
    return pl.BlockSpec((1, 6, D_MODEL), (lambda b, t: (first_row + b, 0, 0)) if per_batch
                        else (lambda b, t: (first_row, 0, 0)))


def _call(body, name, grid, semantics, in_specs, args, out_specs, out_shape, scratch=()):
    def window_bytes(spec, a):
        buffers = 2 if spec.pipeline_mode is None else spec.pipeline_mode.buffer_count
        return buffers * int(np.prod(spec.block_shape)) * jnp.dtype(a.dtype).itemsize

    outs, out_sp = (out_shape, out_specs) if isinstance(out_shape, (list, tuple)) else ([out_shape], [out_specs])
    windows = sum(map(window_bytes, in_specs, args)) + sum(map(window_bytes, out_sp, outs))
    held = sum(int(np.prod(s.shape)) * jnp.dtype(s.dtype).itemsize for s in scratch)
    limit = max(windows + held + VMEM_WORK_BYTES, VMEM_FLOOR_BYTES)
    return pl.pallas_call(
        body, grid=grid, in_specs=in_specs, out_specs=out_specs, out_shape=out_shape, scratch_shapes=list(scratch),
        compiler_params=pltpu.CompilerParams(dimension_semantics=semantics, vmem_limit_bytes=limit),
        name=name,
    )(*args)


MOD_BUFFERS = 3


def _mod_kernel(c_ref, b_ref, w_hbm, o_ref, buf, sem):
    blk = buf.shape[2]
    n_blk = w_hbm.shape[1] // blk
    slab = lambda j: pltpu.make_async_copy(w_hbm.at[:, pl.ds(j * blk, blk)], buf.at[j % MOD_BUFFERS],
                                           sem.at[j % MOD_BUFFERS])
    for j in range(min(MOD_BUFFERS, n_blk)):
        slab(j).start()
    c = c_ref[...]
    s = (c * jax.nn.sigmoid(c)).astype(BF16)
    for j in range(n_blk):
        slab(j).wait()
        cols = slice(j * blk, (j + 1) * blk)
        o_ref[:, cols] = _dot(s, buf[j % MOD_BUFFERS].astype(BF16)) + b_ref[:, cols]
        if j + MOD_BUFFERS < n_blk:
            slab(j + MOD_BUFFERS).start()


def _modulation(c_rows, w_mod, b_mod):
    n_out = w_mod.shape[1]
    blk = D_MODEL
    whole = lambda shape: pl.BlockSpec(shape, lambda: (0,) * len(shape))
    in_specs = [whole((MOD_ROWS, D_MODEL)), whole((1, n_out)), pl.BlockSpec(memory_space=pl.ANY)]
    return pl.pallas_call(
        _mod_kernel, grid=(), in_specs=in_specs, out_specs=whole((MOD_ROWS, n_out)),
        out_shape=jax.ShapeDtypeStruct((MOD_ROWS, n_out), F32),
        scratch_shapes=[pltpu.VMEM((MOD_BUFFERS, D_MODEL, blk), F32), pltpu.SemaphoreType.DMA((MOD_BUFFERS,))],
        compiler_params=pltpu.CompilerParams(vmem_limit_bytes=VMEM_FLOOR_BYTES),
        name="modulation",
    )(c_rows, b_mod.reshape(1, n_out), w_mod)


def _log_sigmoid(x):
    return jnp.minimum(x, 0.0) - jnp.log1p(jnp.exp(-jnp.abs(x)))


def _cummax_lanes(x, reverse):
    n = x.shape[-1]
    lane = lax.broadcasted_iota(jnp.int32, x.shape, x.ndim - 1)
    step = 1
    while step < n:
        if reverse:
            shifted, valid = pltpu.roll(x, n - step, x.ndim - 1), lane < n - step
        else:
            shifted, valid = pltpu.roll(x, step, x.ndim - 1), lane >= step
        x = jnp.maximum(x, jnp.where(valid, shifted, -jnp.inf))
        step *= 2
    return x


def _rms(t, g):
    return t * lax.rsqrt(jnp.mean(t * t, axis=-1, keepdims=True) + EPS) * g


def _proj_kernel(*refs):
    x_ref, mod_ref = refs[:2]
    mod = mod_ref[0]
    for p in range(x_ref.shape[1] // CHUNK):
        _proj_subtile(True, slice(p * CHUNK, (p + 1) * CHUNK), mod, refs)


def _context_front_kernel(*refs):
    (x_ref, mod_ref, wgt_ref, bg_ref, qg_ref, kg_ref, gn_ref, wm_ref, wmt_ref, wa_ref, wvt_ref, wmg_ref,
     sgm_o, sga_o, kc_o, vc_o, hm_o, c_o, n_o, m_o, ha_o,
     qm_s, km_s, vmt_s, somt_s, gr_s, qa_s, ka_s, vt_s, acc_s) = refs
    proj_refs = (x_ref, mod_ref, wgt_ref, bg_ref, qg_ref, kg_ref, wm_ref, wmt_ref, wa_ref, wvt_ref, wmg_ref,
                 qm_s, km_s, vmt_s, somt_s, gr_s, qa_s, ka_s, vt_s, sgm_o, sga_o, kc_o, vc_o)
    _proj_subtile(False, slice(0, CHUNK), mod_ref[0], proj_refs)
    _mlstm_kernel(False, True, 1, NH_M, qm_s, km_s, vmt_s, gr_s, somt_s, gn_ref, hm_o, c_o, n_o, m_o, acc_s)
    _attn_kernel(1, False, N_KV, qa_s, ka_s, vt_s, ha_o)


def _proj_subtile(rope, rows, mod, refs):
    if rope:
        (x_ref, mod_ref, wgt_ref, bg_ref, qg_ref, kg_ref, cos_ref, sin_ref, wm_ref, wmt_ref, wa_ref, wvt_ref, wmg_ref,
         qm_o, km_o, vmt_o, somt_o, gr_o, qa_o, ka_o, vt_o, sgm_o, sga_o) = refs
    else:
        (x_ref, mod_ref, wgt_ref, bg_ref, qg_ref, kg_ref, wm_ref, wmt_ref, wa_ref, wvt_ref, wmg_ref,
         qm_o, km_o, vmt_o, somt_o, gr_o, qa_o, ka_o, vt_o, sgm_o, sga_o, kc_o, vc_o) = refs
    tm = CHUNK
    h = (x_ref[0, rows, :] * (1.0 + mod[1:2]) + mod[0:1]).astype(BF16)

    gates = _dot_nt(wgt_ref[...], h) + bg_ref[...]
    lf = _log_sigmoid(gates)

    qm_o[0, rows, :] = _dot(h, wm_ref[:, 0:D_MODEL]).astype(BF16)
    km_o[0, rows, :] = (_dot(h, wm_ref[:, D_MODEL:2 * D_MODEL]) * (DH_M ** -0.5)).astype(BF16)
    for hh in range(NH_M):
        vmt = _dot_nt(wmt_ref[hh * DH_M:(hh + 1) * DH_M, :], h)
        vmt_o[0, hh, 0:DH_M, rows] = vmt.astype(BF16)
        vmt_o[0, hh, DH_M:VM_ROWS, rows] = jnp.ones((VM_ROWS - DH_M, tm), BF16)
    somt_o[0, :, rows] = jax.nn.sigmoid(_dot_nt(wmt_ref[D_MODEL:2 * D_MODEL, :], h)).astype(BF16)

    if rope:
        cos = cos_ref[rows, :]
        sin_s = sin_ref[rows, :]
        lane = lax.broadcasted_iota(jnp.int32, (tm, DH_A), 1)
        first_half = (lane % (2 * N_FREQ)) < N_FREQ

        def rot(t):
            partner = jnp.where(first_half, pltpu.roll(t, DH_A - N_FREQ, 1), pltpu.roll(t, N_FREQ, 1))
            return t * cos + partner * sin_s
    else:
        rot = lambda t: t

    qg = qg_ref[...]
    kg = kg_ref[...]
    head_rows = lambda g: pl.ds(rows.start * N_KV + g, tm, stride=N_KV)
    q_all = _dot(h, wa_ref[:, 0:N_Q * DH_A])
    k_all = _dot(h, wa_ref[:, N_Q * DH_A:(N_Q + N_KV) * DH_A])
    for g in range(N_Q):
        t = _rms(q_all[:, g * DH_A:(g + 1) * DH_A], qg)
        qa_o[0, rows, g * DH_A:(g + 1) * DH_A] = (rot(t) * Q_SCALE).astype(BF16)
    for g in range(N_KV):
        t = _rms(k_all[:, g * DH_A:(g + 1) * DH_A], kg)
        if not rope:
            kc_o[0, head_rows(g), :] = t
        ka_o[0, rows, g * DH_A:(g + 1) * DH_A] = rot(t).astype(BF16)
    if not rope:
        off = (N_Q + N_KV) * DH_A
        v_all = _dot(h, wa_ref[:, off:off + N_KV * DH_A])
        for g in range(N_KV):
            vc_o[0, head_rows(g), :] = v_all[:, g * DH_A:(g + 1) * DH_A]
    vt = _dot_nt(wvt_ref[...], h)
    for g in range(N_KV):
        vt_o[0, g, 0:DH_A, rows] = vt[g * DH_A:(g + 1) * DH_A].astype(BF16)
        vt_o[0, g, DH_A:V_ROWS, rows] = jnp.ones((V_ROWS - DH_A, tm), BF16)

    row = lax.broadcasted_iota(jnp.int32, (tm, tm), 0)
    col = lax.broadcasted_iota(jnp.int32, (tm, tm), 1)
    tri = jnp.where(row <= col, 1.0, 0.0).astype(BF16)
    hi = lf.astype(BF16)
    r1 = lf - hi.astype(F32)
    mid = r1.astype(BF16)
    lo = (r1 - mid.astype(F32)).astype(BF16)
    cum = (_dot(hi, tri) + _dot(mid, tri) + _dot(lo, tri))[0:8]
    lf8 = lf[0:8]
    tot = cum[:, tm - 1:tm]
    rev = tot - cum + lf8
    is_fwd = lax.broadcasted_iota(jnp.int32, (8, tm), 0) < NH_M
    a = jnp.where(is_fwd, cum, rev)
    cc = (gates[8:16] - a) * LOG2E
    a = a * LOG2E
    totb = jnp.broadcast_to(tot * LOG2E, (8, tm))
    c_pre = _cummax_lanes(cc, False)
    c_suf = _cummax_lanes(cc, True)
    for hh in range(NH_M):
        gate_rows = (a[hh:hh + 1], cc[hh:hh + 1], totb[hh:hh + 1],
                     a[NH_M + hh:NH_M + hh + 1], cc[NH_M + hh:NH_M + hh + 1], totb[NH_M + hh:NH_M + hh + 1],
                     c_pre[hh:hh + 1], c_suf[NH_M + hh:NH_M + hh + 1])
        for k, r in enumerate(gate_rows):
            gr_o[0, hh, k:k + 1, rows] = r

    sgm_o[0, rows, :] = jax.nn.sigmoid(_dot(h, wmg_ref[:, 0:D_MODEL])).astype(BF16)
    sga_o[0, rows, :] = jax.nn.sigmoid(_dot(h, wmg_ref[:, D_MODEL:2 * D_MODEL])).astype(BF16)


def _projection(x, mod, mod_rows, small, big, rope_tables):
    B, T, _ = x.shape
    tm = min(TOK_TILE, T)
    nt = T // tm
    tok = lambda width: pl.BlockSpec((1, tm, width), lambda b, t: (b, t, 0))
    in_specs = [tok(D_MODEL), _mod_spec(*mod_rows)]
    in_specs += [_resident(w.shape) for w in small]
    in_specs += [pl.BlockSpec((tm, DH_A), lambda b, t: (t, 0))] * 2
    in_specs += [_resident(w.shape) for w in big]
    args = [x, mod, *small, *rope_tables, *big]

    kv_w = N_KV * DH_A
    outs = [((B, T, D_MODEL), BF16, tok(D_MODEL)),
            ((B, T, D_MODEL), BF16, tok(D_MODEL)),
            ((B, NH_M, VM_ROWS, T), BF16,
             pl.BlockSpec((1, NH_M, VM_ROWS, tm), lambda b, t: (b, 0, 0, t))),
            ((B, D_MODEL, T), BF16, pl.BlockSpec((1, D_MODEL, tm), lambda b, t: (b, 0, t))),
            ((B, NH_M, 8, T), F32, pl.BlockSpec((1, NH_M, 8, tm), lambda b, t: (b, 0, 0, t))),
            ((B, T, D_MODEL), BF16, tok(D_MODEL)),
            ((B, T, kv_w), BF16, tok(kv_w)),
            ((B, N_KV, V_ROWS, T), BF16,
             pl.BlockSpec((1, N_KV, V_ROWS, tm), lambda b, t: (b, 0, 0, t))),
            ((B, T, D_MODEL), BF16, tok(D_MODEL)),
            ((B, T, D_MODEL), BF16, tok(D_MODEL))]
    return _call(_proj_kernel, "projection_lat", (B, nt), ("parallel", "parallel"), in_specs, args,
                 [o[2] for o in outs], [jax.ShapeDtypeStruct(o[0], o[1]) for o in outs])


def _context_front(x, mod, mod_rows, small, big, gnorm_col):
    B, T, _ = x.shape
    assert T == CHUNK, "the fused context front handles one chunk per batch row"
    kv_w = N_KV * DH_A
    tok = lambda width: pl.BlockSpec((1, T, width), lambda b, t: (b, 0, 0))
    state = lambda *tail: pl.BlockSpec((1, 2, NH_M) + tail, lambda b, t: (b, 0, 0, 0, 0))
    cache = pl.BlockSpec((1, T * N_KV, DH_A), lambda b, t: (b, 0, 0))
    in_specs = ([tok(D_MODEL), _mod_spec(*mod_rows)] + [_resident(w.shape) for w in small]
                + [_resident(gnorm_col.shape)] + [_resident(w.shape) for w in big])
    outs = [((B, T, D_MODEL), BF16, tok(D_MODEL)), ((B, T, D_MODEL), BF16, tok(D_MODEL)),
            ((B, T * N_KV, DH_A), F32, cache), ((B, T * N_KV, DH_A), F32, cache),
            ((B, T, D_MODEL), BF16, tok(D_MODEL)),
            ((B, 2, NH_M, DH_M, DH_M), F32, state(DH_M, DH_M)),
            ((B, 2, NH_M, 1, DH_M), F32, state(1, DH_M)),
            ((B, 2, NH_M, 1, 1), F32, state(1, 1)),
            ((B, T, D_MODEL), BF16, tok(D_MODEL))]
    scratch = [pltpu.VMEM((1, T, D_MODEL), BF16), pltpu.VMEM((1, T, D_MODEL), BF16),
               pltpu.VMEM((1, NH_M, VM_ROWS, T), BF16), pltpu.VMEM((1, D_MODEL, T), BF16),
               pltpu.VMEM((1, NH_M, 8, T), F32),
               pltpu.VMEM((1, T, D_MODEL), BF16), pltpu.VMEM((1, T, kv_w), BF16),
               pltpu.VMEM((1, N_KV, V_ROWS, T), BF16),
               pltpu.VMEM((NH_M, DH_M, CHUNK), F32)]
    return _call(_context_front_kernel, "context_front", (B, 1), ("parallel", "arbitrary"), in_specs,
                 [x, mod, *small, gnorm_col, *big],
                 [o[2] for o in outs], [jax.ShapeDtypeStruct(o[0], o[1]) for o in outs], scratch)


def _mlstm_kernel(has_init, emit_state, nc, nh, *refs):
    refs = list(refs)
    q_ref, k_ref, vt_ref, g_ref, somt_ref, gn_ref = refs[:6]
    refs = refs[6:]
    if has_init:
        c0_ref, n0_ref, m0_ref = refs[:3]
        refs = refs[3:]
    hm_o = refs[0]
    refs = refs[1:]
    if emit_state:
        c_o, n_o, m_o = refs[:3]
        refs = refs[3:]
    acc_ref, = refs

    L = CHUNK
    row = lax.broadcasted_iota(jnp.int32, (L, L), 0)
    col = lax.broadcasted_iota(jnp.int32, (L, L), 1)
    eye = row == col
    masks = (row <= col, row >= col)

    span = lambda c: slice(c * L, (c + 1) * L)
    feat = lambda hh: slice(hh * DH_M, (hh + 1) * DH_M)
    chunk_of = lambda d, s: s if d == 0 else nc - 1 - s
    steps = [(hh, s) for s in range(nc) for hh in range(nh)]

    def scores(hh, s):
        return [_dot_nt(k_ref[0, span(chunk_of(d, s)), feat(hh)], q_ref[0, span(chunk_of(d, s)), feat(hh)])
                for d in range(2)]

    def finish_chunk(hh, c, ht):
        ht = acc_ref[hh * nc + c] + ht
        hn = ht * lax.rsqrt(jnp.mean(ht * ht, axis=0, keepdims=True) + EPS) * gn_ref[feat(hh), :]
        hm_o[0, span(c), feat(hh)] = (hn * somt_ref[0, feat(hh), span(c)]).T.astype(BF16)

    states, m_runs, arriveds = {}, {}, {}
    for hh, s in steps:
        if s == 0:
            states[hh] = [None, None]
            if has_init:
                states[hh] = [jnp.concatenate([c0_ref[0, d, hh].T,
                                               jnp.broadcast_to(n0_ref[0, d, hh], (VM_ROWS - DH_M, DH_M))], axis=0)
                              for d in range(2)]
            m_runs[hh] = [m0_ref[0, d, hh] * LOG2E if has_init else jnp.zeros((1, 1), F32) for d in range(2)]
            arriveds[hh] = [False] * nc
        state, m_run, arrived = states[hh], m_runs[hh], arriveds[hh]
        st_cur = scores(hh, s)
        inter = [None, None]
        if state[0] is not None:
            inter = [_dot_nt(state[d].astype(BF16), q_ref[0, span(chunk_of(d, s)), feat(hh)]) for d in range(2)]
        for d in range(2):
            c = chunk_of(d, s)
            k = k_ref[0, span(c), feat(hh)]
            vt = vt_ref[0, hh, :, span(c)]
            g = g_ref[0, hh, :, span(c)]
            a_row = g[3 * d:3 * d + 1]
            c_row = g[3 * d + 1:3 * d + 2]
            tot = g[3 * d + 2:3 * d + 3, 0:1]
            m_prev = m_run[d]
            c_col = jnp.sum(jnp.where(eye, c_row, 0.0), axis=-1, keepdims=True)
            c_run = g[6 + d:7 + d]

            keep_state = emit_state or s + 1 < nc
            if keep_state:
                c_max = c_run[:, L - 1:L] if d == 0 else c_run[:, 0:1]
                m_new = tot + jnp.maximum(m_prev, c_max)
                wk = jnp.exp2(tot + c_col - m_new).astype(BF16) * k
                upd = _dot(vt, wk)
                new_state = upd if state[d] is None else jnp.exp2(tot + m_prev - m_new) * state[d] + upd

            m_rel = jnp.maximum(m_prev, c_run)
            sp = (st_cur[d] * jnp.exp2(jnp.where(masks[d], c_col, -jnp.inf) - m_rel)).astype(BF16)
            numt = _dot(vt, sp)
            if inter[d] is not None:
                numt = numt + jnp.exp2(m_prev - m_rel) * inter[d]
            den = numt[DH_M:DH_M + 1]
            ht = numt[0:DH_M] * (1.0 / jnp.maximum(jnp.abs(den), jnp.exp2(-(a_row + m_rel))))
            if arrived[c]:
                finish_chunk(hh, c, ht)
            else:
                acc_ref[hh * nc + c] = ht
                arrived[c] = True
            if keep_state:
                state[d] = new_state
                m_run[d] = m_new

        if emit_state and s == nc - 1:
            for d in range(2):
                c_o[0, d, hh] = state[d][0:DH_M].T
                n_o[0, d, hh] = state[d][DH_M:DH_M + 1]
                m_o[0, d, hh] = m_run[d] * (1.0 / LOG2E)


def _mlstm(qm, km, vmt, gr, somt, gnorm_col, init_state, emit_state, nh):
    B, T, _ = qm.shape
    nc = T // CHUNK
    has_init = init_state is not None

    seq = pl.BlockSpec((1, T, nh * DH_M), lambda b, h: (b, 0, h))
    c_spec = pl.BlockSpec((1, 2, nh, DH_M, DH_M), lambda b, h: (b, 0, h, 0, 0))
    n_spec = pl.BlockSpec((1, 2, nh, 1, DH_M), lambda b, h: (b, 0, h, 0, 0))
    m_spec = pl.BlockSpec((1, 2, nh, 1, 1), lambda b, h: (b, 0, h, 0, 0))

    in_specs = [seq, seq,
                pl.BlockSpec((1, nh, VM_ROWS, T), lambda b, h: (b, h, 0, 0)),
                pl.BlockSpec((1, nh, 8, T), lambda b, h: (b, h, 0, 0)),
                pl.BlockSpec((1, nh * DH_M, T), lambda b, h: (b, h, 0)),
                pl.BlockSpec((nh * DH_M, 1), lambda b, h: (h, 0))]
    args = [qm, km, vmt, gr, somt, gnorm_col]
    if has_init:
        in_specs += [c_spec, n_spec, m_spec]
        args += list(init_state)
    out_specs = [seq]
    out_shape = [jax.ShapeDtypeStruct((B, T, D_MODEL), BF16)]
    if emit_state:
        out_specs += [c_spec, n_spec, m_spec]
        out_shape += [jax.ShapeDtypeStruct((B, 2, NH_M, DH_M, DH_M), F32),
                      jax.ShapeDtypeStruct((B, 2, NH_M, 1, DH_M), F32),
                      jax.ShapeDtypeStruct((B, 2, NH_M, 1, 1), F32)]

    return _call(functools.partial(_mlstm_kernel, has_init, emit_state, nc, nh),
                 "mlstm_lat" if has_init else "mlstm_ctx", (B, NH_M // nh), ("parallel", "parallel"),
                 in_specs, args, out_specs, out_shape, [pltpu.VMEM((nh * nc, DH_M, CHUNK), F32)])


def _attn_kernel(n_lat_tiles, has_ctx, nkv, *refs):
    if has_ctx:
        q_ref, k_ref, vt_ref, kc_ref, vct_ref, o_ref = refs
    else:
        q_ref, k_ref, vt_ref, o_ref = refs
    tk = k_ref.shape[1] // n_lat_tiles
    head = lambda h: slice(h * DH_A, (h + 1) * DH_A)
    tiles = [(lambda h, i=i: k_ref[0, i * tk:(i + 1) * tk, head(h)],
              lambda h, i=i: vt_ref[0, h, :, i * tk:(i + 1) * tk]) for i in range(n_lat_tiles)]
    if has_ctx:
        tiles.append((lambda h: kc_ref[0, :, head(h)], lambda h: vct_ref[0, h]))
    groups = [(h, g, r) for h in range(nkv) for r in range(q_ref.shape[1] // Q_CHAIN) for g in range(G_Q)]
    chains = [(t, i) for t in range(len(tiles)) for i in range(len(groups))]
    rows = lambda i: slice(groups[i][2] * Q_CHAIN, (groups[i][2] + 1) * Q_CHAIN)
    cols = lambda i: head(groups[i][0] * G_Q + groups[i][1])

    def scores(t, i):
        return _dot_nt(tiles[t][0](groups[i][0]), q_ref[0, rows(i), cols(i)])

    m = [None] * len(groups)
    acc = [None] * len(groups)
    pending = []
    for idx in range(len(chains) + ATTN_LOOKAHEAD):
        if idx < len(chains):
            pending.append(scores(*chains[idx]))
        if idx < ATTN_LOOKAHEAD:
            continue
        t, i = chains[idx - ATTN_LOOKAHEAD]
        st = pending.pop(0)
        m_tile = jnp.max(st, axis=0, keepdims=True)
        m_new = m_tile if t == 0 else jnp.maximum(m[i], m_tile)
        pv = _dot(tiles[t][1](groups[i][0]), jnp.exp2(st - m_new).astype(BF16))
        acc[i] = pv if t == 0 else jnp.exp2(m[i] - m_new) * acc[i] + pv
        m[i] = m_new

    for i in range(len(groups)):
        out = acc[i][0:DH_A] * (1.0 / acc[i][DH_A:DH_A + 1])
        o_ref[0, rows(i), cols(i)] = out.T.astype(BF16)


def _attn_cast_kernel(n_lat_tiles, n_cast, *refs):
    attn_in, w_f32 = refs[:5], refs[5:5 + n_cast]
    o_ref, w_bf16 = refs[5 + n_cast], refs[6 + n_cast:]
    _attn_kernel(n_lat_tiles, True, 1, *attn_in, o_ref)
    for src, dst in zip(w_f32, w_bf16):
        dst[...] = src[...].astype(BF16)


def _attention(qa, ka, vt, ctx_kv, cast_weights):
    B, T, _ = qa.shape
    tq = min(Q_TILE, T)
    nq = T // tq
    n_steps = B * N_KV * nq
    n_lat_tiles = max(1, T // K_TILE)
    qspec = pl.BlockSpec((1, tq, G_Q * DH_A), lambda b, h, i: (b, i, h))
    kspec = lambda tk: pl.BlockSpec((1, tk, DH_A), lambda b, h, i: (b, 0, h))
    vspec = lambda tk: pl.BlockSpec((1, 1, V_ROWS, tk), lambda b, h, i: (b, h, 0, 0))
    slab = lambda w: pl.BlockSpec((w.shape[0] // n_steps, w.shape[1]), lambda b, h, i: ((b * N_KV + h) * nq + i, 0))
    tc = ctx_kv[0].shape[1]
    in_specs = [qspec, kspec(T), vspec(T), kspec(tc), vspec(tc)] + [slab(w) for w in cast_weights]
    out_specs = [qspec] + [slab(w) for w in cast_weights]
    out_shape = [jax.ShapeDtypeStruct((B, T, D_MODEL), BF16)] + [jax.ShapeDtypeStruct(w.shape, BF16) for w in cast_weights]
    return _call(functools.partial(_attn_cast_kernel, n_lat_tiles, len(cast_weights)), "attention_lat",
                 (B, N_KV, nq), ("parallel", "parallel", "parallel"), in_specs,
                 [qa, ka, vt, *ctx_kv, *cast_weights], out_specs, out_shape)


def _layer_norm(y, g, b):
    mu = jnp.mean(y, axis=-1, keepdims=True)
    yc = y - mu
    var = jnp.mean(yc * yc, axis=-1, keepdims=True)
    return yc * lax.rsqrt(var + EPS) * g + b


def _tail_kernel(x_ref, mod_ref, hm_ref, ha_ref, sgm_ref, sga_ref, ln_ref,
                 wbm_ref, wba_ref, wout_ref, wup_ref, wdown_ref, o_ref):
    mod = mod_ref[0]
    ln = ln_ref[...]
    n_sub = x_ref.shape[1] // TAIL_SUB
    rows = lambda p: slice(p * TAIL_SUB, (p + 1) * TAIL_SUB)

    def merge(p):
        merged = (sgm_ref[0, rows(p), :] * _dot(hm_ref[0, rows(p), :], wbm_ref[...])
                  + sga_ref[0, rows(p), :] * _dot(ha_ref[0, rows(p), :], wba_ref[...]))
        return ALPHA * x_ref[0, rows(p), :] + mod[2:3] * _dot(merged.astype(BF16), wout_ref[...])

    def ffn(x1):
        h = (x1 * (1.0 + mod[4:5]) + mod[3:4]).astype(BF16)
        ff = jnp.zeros_like(x1)
        for j in range(D_FF // D_MODEL):
            u = jnp.maximum(_dot(h, wup_ref[:, j * D_MODEL:(j + 1) * D_MODEL]), 0.0)
            ff = ff + _dot((u * u).astype(BF16), wdown_ref[j * D_MODEL:(j + 1) * D_MODEL, :])
        return ALPHA * x1 + mod[5:6] * ff

    y1 = [merge(p) for p in range(n_sub)]
    y2 = [ffn(_layer_norm(y1[p], ln[0:1], ln[1:2])) for p in range(n_sub)]
    for p in range(n_sub):
        o_ref[0, rows(p), :] = _layer_norm(y2[p], ln[2:3], ln[3:4])


def _tail(x, mod, mod_rows, hm, ha, sgm, sga, wts, ln, name):
    if not mod_rows[1]:
        x, hm, ha, sgm, sga = (a.reshape(1, -1, D_MODEL) for a in (x, hm, ha, sgm, sga))
    B, T, _ = x.shape
    tm = TAIL_TILE
    tok = pl.BlockSpec((1, tm, D_MODEL), lambda b, t: (b, t, 0))
    in_specs = ([tok, _mod_spec(*mod_rows), tok, tok, tok, tok, _resident(ln.shape)]
                + [_resident(w.shape) for w in wts])
    return _call(_tail_kernel, name, (B, T // tm), ("parallel", "parallel"),
                 in_specs, (x, mod, hm, ha, sgm, sga, ln, *wts), tok, jax.ShapeDtypeStruct((B, T, D_MODEL), F32))


def _rope_tables(n_tokens):
    rows = n_tokens // GRID_W
    row = np.repeat(np.arange(rows), GRID_W)
    col = np.tile(np.arange(GRID_W), rows)
    inv = ROPE_BASE ** (-np.arange(N_FREQ, dtype=np.float64) / N_FREQ)
    ang = np.stack([row, col], -1).astype(np.float64)[..., None] * inv
    ang = np.broadcast_to(ang[:, :, None, :], (n_tokens, 2, 2, N_FREQ))
    sign = np.asarray([-1.0, 1.0])[None, None, :, None]
    return (jnp.asarray(np.cos(ang).reshape(n_tokens, DH_A), F32),
            jnp.asarray((np.sin(ang) * sign).reshape(n_tokens, DH_A), F32))


def kernel(x_prompt, x_sample, cache_k, cache_v, state_C, state_n, state_m, c, c_ctx, w_mod, b_mod, w_in,
           b_gates, mlstm_norm_g, q_norm_g, k_norm_g, w_bm, w_ba, w_out, ln1_g, ln1_b, w_up, w_down,
           ln2_g, ln2_b):
    B, T, _ = x_prompt.shape
    Bd, Td, _ = x_sample.shape
    l = 0

    w = w_in[l]
    o_g = 4 * D_MODEL
    o_a = o_g + 4 * NH_M
    o_mg = o_a + (N_Q + 2 * N_KV) * DH_A
    gate_rows = np.array([4, 5, 6, 7, 12, 13, 14, 15, 0, 1, 2, 3, 8, 9, 10, 11])
    proj_small = (w[:, o_g:o_a].T[gate_rows].astype(BF16),
                  b_gates[l][gate_rows].reshape(4 * NH_M, 1),
                  q_norm_g[l].reshape(1, DH_A),
                  k_norm_g[l].reshape(1, DH_A))
    proj_big = (w[:, :2 * D_MODEL].astype(BF16),
                w[:, 2 * D_MODEL:o_g].T.astype(BF16),
                w[:, o_a:o_mg].astype(BF16),
                w[:, o_a + (N_Q + N_KV) * DH_A:o_mg].T.astype(BF16),
                w[:, o_mg:].astype(BF16))
    tail_w_f32 = (w_bm[l], w_ba[l], w_out[l], w_up[l], w_down[l])
    ln = jnp.stack([ln1_g[l], ln1_b[l], ln2_g[l], ln2_b[l]])
    gnorm = mlstm_norm_g[l].reshape(D_MODEL, 1)

    c_rows = jnp.concatenate([c_ctx[None, :], c, jnp.zeros((MOD_ROWS - 1 - Bd, D_MODEL), F32)], axis=0)
    mod = _modulation(c_rows, w_mod[l], b_mod[l]).reshape(MOD_ROWS, 6, D_MODEL)
    rows_ctx, rows_lat = (0, False), (1, True)

    ctx = _context_front(x_prompt, mod, rows_ctx, proj_small, proj_big, gnorm)
    sgm_c, sga_c, k_new, v_new, hm_c, c_new, n_new, m_new, ha_c = ctx

    (qm, km, vmt, somt, gr, qa, ka, vt, sgm, sga) = _projection(x_sample, mod, rows_lat, proj_small, proj_big, _rope_tables(Td))
    past = cache_k.shape[2]
    init = (state_C[:, l], state_n[:, l].reshape(Bd, 2, NH_M, 1, DH_M), state_m[:, l].reshape(Bd, 2, NH_M, 1, 1))
    hm, = _mlstm(qm, km, vmt, gr, somt, gnorm, init, False, 1)
    vct = jnp.transpose(cache_v[:, l], (0, 2, 3, 1)).astype(BF16)
    vct = jnp.concatenate([vct, jnp.ones((Bd, N_KV, V_ROWS - DH_A, past), BF16)], axis=2)
    ctx_kv = (cache_k[:, l].reshape(Bd, past, N_KV * DH_A).astype(BF16), vct)
    ha, *tail_w = _attention(qa, ka, vt, ctx_kv, tail_w_f32)
    y_sample = _tail(x_sample, mod, rows_lat, hm, ha, sgm, sga, tail_w, ln, "tail_lat")
    y_prompt = _tail(x_prompt, mod, rows_ctx, hm_c, ha_c, sgm_c, sga_c, tail_w, ln, "tail_ctx").reshape(x_prompt.shape)

    return (y_prompt, y_sample,
            k_new.reshape(B, 1, T, N_KV, DH_A), v_new.reshape(B, 1, T, N_KV, DH_A),
            c_new.reshape(B, 1, 2, NH_M, DH_M, DH_M), n_new.reshape(B, 1, 2, NH_M, DH_M),
            m_new.reshape(B, 1, 2, NH_M))
```

```python
import functools

import jax
import jax.numpy as jnp
import numpy as np
from jax import lax
from jax.experimental import pallas as pl
from jax.experimental.pallas import tpu as pltpu

D_MODEL = 1024
NH_M = 4
DH_M = 256
N_Q = 8
N_KV = 2
G_Q = N_Q // N_KV
DH_A = 128
D_FF = 4 * D_MODEL
GRID_W = 64
N_FREQ = DH_A // 4
ROPE_BASE = 10000.0
EPS = 1e-6
DEPTH = 1
ALPHA = (2 * DEPTH) ** 0.25

CHUNK = 256
TOK_TILE = 512
TAIL_TILE = 512
TAIL_SUB = 256
Q_TILE = 1024
Q_CHAIN = 256
K_TILE = 512
LOG2E = float(np.log2(np.e))
Q_SCALE = DH_A ** -0.5 * LOG2E
ATTN_LOOKAHEAD = 4
V_ROWS = DH_A + 16
VM_ROWS = DH_M + 16
MOD_ROWS = 8

VMEM_WORK_BYTES = 20 * 1024 * 1024
VMEM_FLOOR_BYTES = 56 * 1024 * 1024

F32 = jnp.float32
BF16 = jnp.bfloat16


def _dot(a, b):
    return jnp.dot(a, b, preferred_element_type=F32)


def _dot_nt(a, b):
    return lax.dot_general(a, b, (((1,), (1,)), ((), ())), preferred_element_type=F32)


def _resident(shape):
    nd = len(shape)
    return pl.BlockSpec(shape, lambda *_: (0,) * nd, pipeline_mode=pl.Buffered(1))


def _mod_spec(first_row, per_batch):
    return pl.BlockSpec((1, 6, D_MODEL), (lambda b, t: (first_row + b, 0, 0)) if per_batch
                        else (lambda b, t: (first_row, 0, 0)))


def _call(body, name, grid, semantics, in_specs, args, out_specs, out_shape, scratch=(), fusible=()):
    def window_bytes(spec, a):
        buffers = 2 if spec.pipeline_mode is None else spec.pipeline_mode.buffer_count
        return buffers * int(np.prod(spec.block_shape)) * jnp.dtype(a.dtype).itemsize

    outs, out_sp = (out_shape, out_specs) if isinstance(out_shape, (list, tuple)) else ([out_shape], [out_specs])
    windows = sum(map(window_bytes, in_specs, args)) + sum(map(window_bytes, out_sp, outs))
    held = sum(int(np.prod(s.shape)) * jnp.dtype(s.dtype).itemsize for s in scratch)
    limit = max(windows + held + VMEM_WORK_BYTES, VMEM_FLOOR_BYTES)
    return pl.pallas_call(
        body, grid=grid, in_specs=in_specs, out_specs=out_specs, out_shape=out_shape, scratch_shapes=list(scratch),
        compiler_params=pltpu.CompilerParams(
            dimension_semantics=semantics, vmem_limit_bytes=limit,
            allow_input_fusion=[i in fusible for i in range(len(args))] if fusible else None),
        name=name,
    )(*args)


def _mod_kernel(c_ref, w_ref, b_ref, o_ref):
    c = c_ref[...]
    s = c * jax.nn.sigmoid(c)
    o_ref[...] = _dot(s.astype(BF16), w_ref[...].astype(BF16)) + b_ref[...]


def _modulation(c_rows, w_mod, b_mod):
    n_out = w_mod.shape[1]
    blk = D_MODEL
    in_specs = [pl.BlockSpec((MOD_ROWS, D_MODEL), lambda j: (0, 0)),
                pl.BlockSpec((D_MODEL, blk), lambda j: (0, j)),
                pl.BlockSpec((1, blk), lambda j: (0, j))]
    return _call(_mod_kernel, "modulation", (n_out // blk,), ("parallel",),
                 in_specs, (c_rows, w_mod, b_mod.reshape(1, n_out)),
                 pl.BlockSpec((MOD_ROWS, blk), lambda j: (0, j)), jax.ShapeDtypeStruct((MOD_ROWS, n_out), F32))


def _log_sigmoid(x):
    return jnp.minimum(x, 0.0) - jnp.log1p(jnp.exp(-jnp.abs(x)))


def _cummax_lanes(x, reverse):
    n = x.shape[-1]
    lane = lax.broadcasted_iota(jnp.int32, x.shape, x.ndim - 1)
    step = 1
    while step < n:
        if reverse:
            shifted, valid = pltpu.roll(x, n - step, x.ndim - 1), lane < n - step
        else:
            shifted, valid = pltpu.roll(x, step, x.ndim - 1), lane >= step
        x = jnp.maximum(x, jnp.where(valid, shifted, -jnp.inf))
        step *= 2
    return x


def _rms(t, g):
    return t * lax.rsqrt(jnp.mean(t * t, axis=-1, keepdims=True) + EPS) * g


def _proj_kernel(*refs):
    x_ref, mod_ref = refs[:2]
    mod = mod_ref[0]
    for p in range(x_ref.shape[1] // CHUNK):
        _proj_subtile(True, slice(p * CHUNK, (p + 1) * CHUNK), mod, refs)


def _context_front_kernel(*refs):
    (x_ref, mod_ref, wgt_ref, bg_ref, qg_ref, kg_ref, gn_ref, wm_ref, wmt_ref, wa_ref, wvt_ref, wmg_ref,
     sgm_o, sga_o, kc_o, vc_o, hm_o, c_o, n_o, m_o, ha_o,
     qm_s, km_s, vmt_s, somt_s, gr_s, qa_s, ka_s, vt_s, acc_s) = refs
    proj_refs = (x_ref, mod_ref, wgt_ref, bg_ref, qg_ref, kg_ref, wm_ref, wmt_ref, wa_ref, wvt_ref, wmg_ref,
                 qm_s, km_s, vmt_s, somt_s, gr_s, qa_s, ka_s, vt_s, sgm_o, sga_o, kc_o, vc_o)
    _proj_subtile(False, slice(0, CHUNK), mod_ref[0], proj_refs)
    _mlstm_kernel(False, True, 1, NH_M, qm_s, km_s, vmt_s, gr_s, somt_s, gn_ref, hm_o, c_o, n_o, m_o, acc_s)
    _attn_kernel(1, False, N_KV, qa_s, ka_s, vt_s, ha_o)


def _proj_subtile(rope, rows, mod, refs):
    if rope:
        (x_ref, mod_ref, wgt_ref, bg_ref, qg_ref, kg_ref, cos_ref, sin_ref, wm_ref, wmt_ref, wa_ref, wvt_ref, wmg_ref,
         qm_o, km_o, vmt_o, somt_o, gr_o, qa_o, ka_o, vt_o, sgm_o, sga_o) = refs
    else:
        (x_ref, mod_ref, wgt_ref, bg_ref, qg_ref, kg_ref, wm_ref, wmt_ref, wa_ref, wvt_ref, wmg_ref,
         qm_o, km_o, vmt_o, somt_o, gr_o, qa_o, ka_o, vt_o, sgm_o, sga_o, kc_o, vc_o) = refs
    tm = CHUNK
    h = (x_ref[0, rows, :] * (1.0 + mod[1:2]) + mod[0:1]).astype(BF16)

    gates = _dot_nt(wgt_ref[...], h) + bg_ref[...]
    lf = _log_sigmoid(gates)

    qm_o[0, rows, :] = _dot(h, wm_ref[:, 0:D_MODEL]).astype(BF16)
    km_o[0, rows, :] = (_dot(h, wm_ref[:, D_MODEL:2 * D_MODEL]) * (DH_M ** -0.5)).astype(BF16)
    for hh in range(NH_M):
        vmt = _dot_nt(wmt_ref[hh * DH_M:(hh + 1) * DH_M, :], h)
        vmt_o[0, hh, 0:DH_M, rows] = vmt.astype(BF16)
        vmt_o[0, hh, DH_M:VM_ROWS, rows] = jnp.ones((VM_ROWS - DH_M, tm), BF16)
    somt_o[0, :, rows] = jax.nn.sigmoid(_dot_nt(wmt_ref[D_MODEL:2 * D_MODEL, :], h)).astype(BF16)

    if rope:
        cos = cos_ref[rows, :]
        sin_s = sin_ref[rows, :]
        lane = lax.broadcasted_iota(jnp.int32, (tm, DH_A), 1)
        first_half = (lane % (2 * N_FREQ)) < N_FREQ

        def rot(t):
            partner = jnp.where(first_half, pltpu.roll(t, DH_A - N_FREQ, 1), pltpu.roll(t, N_FREQ, 1))
            return t * cos + partner * sin_s
    else:
        rot = lambda t: t

    qg = qg_ref[...]
    kg = kg_ref[...]
    head_rows = lambda g: pl.ds(rows.start * N_KV + g, tm, stride=N_KV)
    q_all = _dot(h, wa_ref[:, 0:N_Q * DH_A])
    k_all = _dot(h, wa_ref[:, N_Q * DH_A:(N_Q + N_KV) * DH_A])
    for g in range(N_Q):
        t = _rms(q_all[:, g * DH_A:(g + 1) * DH_A], qg)
        qa_o[0, rows, g * DH_A:(g + 1) * DH_A] = (rot(t) * Q_SCALE).astype(BF16)
    for g in range(N_KV):
        t = _rms(k_all[:, g * DH_A:(g + 1) * DH_A], kg)
        if not rope:
            kc_o[0, head_rows(g), :] = t
        ka_o[0, rows, g * DH_A:(g + 1) * DH_A] = rot(t).astype(BF16)
    if not rope:
        off = (N_Q + N_KV) * DH_A
        v_all = _dot(h, wa_ref[:, off:off + N_KV * DH_A])
        for g in range(N_KV):
            vc_o[0, head_rows(g), :] = v_all[:, g * DH_A:(g + 1) * DH_A]
    vt = _dot_nt(wvt_ref[...], h)
    for g in range(N_KV):
        vt_o[0, g, 0:DH_A, rows] = vt[g * DH_A:(g + 1) * DH_A].astype(BF16)
        vt_o[0, g, DH_A:V_ROWS, rows] = jnp.ones((V_ROWS - DH_A, tm), BF16)

    row = lax.broadcasted_iota(jnp.int32, (tm, tm), 0)
    col = lax.broadcasted_iota(jnp.int32, (tm, tm), 1)
    tri = jnp.where(row <= col, 1.0, 0.0).astype(BF16)
    hi = lf.astype(BF16)
    r1 = lf - hi.astype(F32)
    mid = r1.astype(BF16)
    lo = (r1 - mid.astype(F32)).astype(BF16)
    cum = (_dot(hi, tri) + _dot(mid, tri) + _dot(lo, tri))[0:8]
    lf8 = lf[0:8]
    tot = cum[:, tm - 1:tm]
    rev = tot - cum + lf8
    is_fwd = lax.broadcasted_iota(jnp.int32, (8, tm), 0) < NH_M
    a = jnp.where(is_fwd, cum, rev)
    cc = (gates[8:16] - a) * LOG2E
    a = a * LOG2E
    totb = jnp.broadcast_to(tot * LOG2E, (8, tm))
    c_pre = _cummax_lanes(cc, False)
    c_suf = _cummax_lanes(cc, True)
    for hh in range(NH_M):
        gate_rows = (a[hh:hh + 1], cc[hh:hh + 1], totb[hh:hh + 1],
                     a[NH_M + hh:NH_M + hh + 1], cc[NH_M + hh:NH_M + hh + 1], totb[NH_M + hh:NH_M + hh + 1],
                     c_pre[hh:hh + 1], c_suf[NH_M + hh:NH_M + hh + 1])
        for k, r in enumerate(gate_rows):
            gr_o[0, hh, k:k + 1, rows] = r

    sgm_o[0, rows, :] = jax.nn.sigmoid(_dot(h, wmg_ref[:, 0:D_MODEL])).astype(BF16)
    sga_o[0, rows, :] = jax.nn.sigmoid(_dot(h, wmg_ref[:, D_MODEL:2 * D_MODEL])).astype(BF16)


def _projection(x, mod, mod_rows, small, big, rope_tables):
    B, T, _ = x.shape
    tm = min(TOK_TILE, T)
    nt = T // tm
    tok = lambda width: pl.BlockSpec((1, tm, width), lambda b, t: (b, t, 0))
    in_specs = [tok(D_MODEL), _mod_spec(*mod_rows)]
    in_specs += [_resident(w.shape) for w in small]
    in_specs += [pl.BlockSpec((tm, DH_A), lambda b, t: (t, 0))] * 2
    in_specs += [_resident(w.shape) for w in big]
    args = [x, mod, *small, *rope_tables, *big]

    kv_w = N_KV * DH_A
    outs = [((B, T, D_MODEL), BF16, tok(D_MODEL)),
            ((B, T, D_MODEL), BF16, tok(D_MODEL)),
            ((B, NH_M, VM_ROWS, T), BF16,
             pl.BlockSpec((1, NH_M, VM_ROWS, tm), lambda b, t: (b, 0, 0, t))),
            ((B, D_MODEL, T), BF16, pl.BlockSpec((1, D_MODEL, tm), lambda b, t: (b, 0, t))),
            ((B, NH_M, 8, T), F32, pl.BlockSpec((1, NH_M, 8, tm), lambda b, t: (b, 0, 0, t))),
            ((B, T, D_MODEL), BF16, tok(D_MODEL)),
            ((B, T, kv_w), BF16, tok(kv_w)),
            ((B, N_KV, V_ROWS, T), BF16,
             pl.BlockSpec((1, N_KV, V_ROWS, tm), lambda b, t: (b, 0, 0, t))),
            ((B, T, D_MODEL), BF16, tok(D_MODEL)),
            ((B, T, D_MODEL), BF16, tok(D_MODEL))]
    return _call(_proj_kernel, "projection_lat", (B, nt), ("parallel", "parallel"), in_specs, args,
                 [o[2] for o in outs], [jax.ShapeDtypeStruct(o[0], o[1]) for o in outs])


def _context_front(x, mod, mod_rows, small, big, gnorm_col):
    B, T, _ = x.shape
    assert T == CHUNK, "the fused context front handles one chunk per batch row"
    kv_w = N_KV * DH_A
    tok = lambda width: pl.BlockSpec((1, T, width), lambda b, t: (b, 0, 0))
    state = lambda *tail: pl.BlockSpec((1, 2, NH_M) + tail, lambda b, t: (b, 0, 0, 0, 0))
    cache = pl.BlockSpec((1, T * N_KV, DH_A), lambda b, t: (b, 0, 0))
    in_specs = ([tok(D_MODEL), _mod_spec(*mod_rows)] + [_resident(w.shape) for w in small]
                + [_resident(gnorm_col.shape)] + [_resident(w.shape) for w in big])
    outs = [((B, T, D_MODEL), BF16, tok(D_MODEL)), ((B, T, D_MODEL), BF16, tok(D_MODEL)),
            ((B, T * N_KV, DH_A), F32, cache), ((B, T * N_KV, DH_A), F32, cache),
            ((B, T, D_MODEL), BF16, tok(D_MODEL)),
            ((B, 2, NH_M, DH_M, DH_M), F32, state(DH_M, DH_M)),
            ((B, 2, NH_M, 1, DH_M), F32, state(1, DH_M)),
            ((B, 2, NH_M, 1, 1), F32, state(1, 1)),
            ((B, T, D_MODEL), BF16, tok(D_MODEL))]
    scratch = [pltpu.VMEM((1, T, D_MODEL), BF16), pltpu.VMEM((1, T, D_MODEL), BF16),
               pltpu.VMEM((1, NH_M, VM_ROWS, T), BF16), pltpu.VMEM((1, D_MODEL, T), BF16),
               pltpu.VMEM((1, NH_M, 8, T), F32),
               pltpu.VMEM((1, T, D_MODEL), BF16), pltpu.VMEM((1, T, kv_w), BF16),
               pltpu.VMEM((1, N_KV, V_ROWS, T), BF16),
               pltpu.VMEM((NH_M, DH_M, CHUNK), F32)]
    return _call(_context_front_kernel, "context_front", (B, 1), ("parallel", "arbitrary"), in_specs,
                 [x, mod, *small, gnorm_col, *big],
                 [o[2] for o in outs], [jax.ShapeDtypeStruct(o[0], o[1]) for o in outs], scratch)


def _mlstm_kernel(has_init, emit_state, nc, nh, *refs):
    refs = list(refs)
    q_ref, k_ref, vt_ref, g_ref, somt_ref, gn_ref = refs[:6]
    refs = refs[6:]
    if has_init:
        c0_ref, n0_ref, m0_ref = refs[:3]
        refs = refs[3:]
    hm_o = refs[0]
    refs = refs[1:]
    if emit_state:
        c_o, n_o, m_o = refs[:3]
        refs = refs[3:]
    acc_ref, = refs

    L = CHUNK
    row = lax.broadcasted_iota(jnp.int32, (L, L), 0)
    col = lax.broadcasted_iota(jnp.int32, (L, L), 1)
    eye = row == col
    masks = (row <= col, row >= col)

    span = lambda c: slice(c * L, (c + 1) * L)
    feat = lambda hh: slice(hh * DH_M, (hh + 1) * DH_M)
    chunk_of = lambda d, s: s if d == 0 else nc - 1 - s
    steps = [(hh, s) for s in range(nc) for hh in range(nh)]

    def scores(hh, s):
        return [_dot_nt(k_ref[0, span(chunk_of(d, s)), feat(hh)], q_ref[0, span(chunk_of(d, s)), feat(hh)])
                for d in range(2)]

    def finish_chunk(hh, c, ht):
        ht = acc_ref[hh * nc + c] + ht
        hn = ht * lax.rsqrt(jnp.mean(ht * ht, axis=0, keepdims=True) + EPS) * gn_ref[feat(hh), :]
        hm_o[0, span(c), feat(hh)] = (hn * somt_ref[0, feat(hh), span(c)]).T.astype(BF16)

    states, m_runs, arriveds = {}, {}, {}
    for hh, s in steps:
        if s == 0:
            states[hh] = [None, None]
            if has_init:
                states[hh] = [jnp.concatenate([c0_ref[0, d, hh].T,
                                               jnp.broadcast_to(n0_ref[0, d, hh], (VM_ROWS - DH_M, DH_M))], axis=0)
                              for d in range(2)]
            m_runs[hh] = [m0_ref[0, d, hh] * LOG2E if has_init else jnp.zeros((1, 1), F32) for d in range(2)]
            arriveds[hh] = [False] * nc
        state, m_run, arrived = states[hh], m_runs[hh], arriveds[hh]
        st_cur = scores(hh, s)
        inter = [None, None]
        if state[0] is not None:
            inter = [_dot_nt(state[d].astype(BF16), q_ref[0, span(chunk_of(d, s)), feat(hh)]) for d in range(2)]
        for d in range(2):
            c = chunk_of(d, s)
            k = k_ref[0, span(c), feat(hh)]
            vt = vt_ref[0, hh, :, span(c)]
            g = g_ref[0, hh, :, span(c)]
            a_row = g[3 * d:3 * d + 1]
            c_row = g[3 * d + 1:3 * d + 2]
            tot = g[3 * d + 2:3 * d + 3, 0:1]
            m_prev = m_run[d]
            c_col = jnp.sum(jnp.where(eye, c_row, 0.0), axis=-1, keepdims=True)
            c_run = g[6 + d:7 + d]

            keep_state = emit_state or s + 1 < nc
            if keep_state:
                c_max = c_run[:, L - 1:L] if d == 0 else c_run[:, 0:1]
                m_new = tot + jnp.maximum(m_prev, c_max)
                wk = jnp.exp2(tot + c_col - m_new).astype(BF16) * k
                upd = _dot(vt, wk)
                new_state = upd if state[d] is None else jnp.exp2(tot + m_prev - m_new) * state[d] + upd

            m_rel = jnp.maximum(m_prev, c_run)
            sp = (st_cur[d] * jnp.exp2(jnp.where(masks[d], c_col, -jnp.inf) - m_rel)).astype(BF16)
            numt = _dot(vt, sp)
            if inter[d] is not None:
                numt = numt + jnp.exp2(m_prev - m_rel) * inter[d]
            den = numt[DH_M:DH_M + 1]
            ht = numt[0:DH_M] * (1.0 / jnp.maximum(jnp.abs(den), jnp.exp2(-(a_row + m_rel))))
            if arrived[c]:
                finish_chunk(hh, c, ht)
            else:
                acc_ref[hh * nc + c] = ht
                arrived[c] = True
            if keep_state:
                state[d] = new_state
                m_run[d] = m_new

        if emit_state and s == nc - 1:
            for d in range(2):
                c_o[0, d, hh] = state[d][0:DH_M].T
                n_o[0, d, hh] = state[d][DH_M:DH_M + 1]
                m_o[0, d, hh] = m_run[d] * (1.0 / LOG2E)


def _mlstm(qm, km, vmt, gr, somt, gnorm_col, init_state, emit_state, nh):
    B, T, _ = qm.shape
    nc = T // CHUNK
    has_init = init_state is not None

    seq = pl.BlockSpec((1, T, nh * DH_M), lambda b, h: (b, 0, h))
    c_spec = pl.BlockSpec((1, 2, nh, DH_M, DH_M), lambda b, h: (b, 0, h, 0, 0))
    n_spec = pl.BlockSpec((1, 2, nh, 1, DH_M), lambda b, h: (b, 0, h, 0, 0))
    m_spec = pl.BlockSpec((1, 2, nh, 1, 1), lambda b, h: (b, 0, h, 0, 0))

    in_specs = [seq, seq,
                pl.BlockSpec((1, nh, VM_ROWS, T), lambda b, h: (b, h, 0, 0)),
                pl.BlockSpec((1, nh, 8, T), lambda b, h: (b, h, 0, 0)),
                pl.BlockSpec((1, nh * DH_M, T), lambda b, h: (b, h, 0)),
                pl.BlockSpec((nh * DH_M, 1), lambda b, h: (h, 0))]
    args = [qm, km, vmt, gr, somt, gnorm_col]
    if has_init:
        in_specs += [c_spec, n_spec, m_spec]
        args += list(init_state)
    out_specs = [seq]
    out_shape = [jax.ShapeDtypeStruct((B, T, D_MODEL), BF16)]
    if emit_state:
        out_specs += [c_spec, n_spec, m_spec]
        out_shape += [jax.ShapeDtypeStruct((B, 2, NH_M, DH_M, DH_M), F32),
                      jax.ShapeDtypeStruct((B, 2, NH_M, 1, DH_M), F32),
                      jax.ShapeDtypeStruct((B, 2, NH_M, 1, 1), F32)]

    return _call(functools.partial(_mlstm_kernel, has_init, emit_state, nc, nh),
                 "mlstm_lat" if has_init else "mlstm_ctx", (B, NH_M // nh), ("parallel", "parallel"),
                 in_specs, args, out_specs, out_shape, [pltpu.VMEM((nh * nc, DH_M, CHUNK), F32)])


def _attn_kernel(n_lat_tiles, has_ctx, nkv, *refs):
    if has_ctx:
        q_ref, k_ref, vt_ref, kc_ref, vct_ref, o_ref = refs
    else:
        q_ref, k_ref, vt_ref, o_ref = refs
    tk = k_ref.shape[1] // n_lat_tiles
    head = lambda h: slice(h * DH_A, (h + 1) * DH_A)
    tiles = [(lambda h, i=i: k_ref[0, i * tk:(i + 1) * tk, head(h)],
              lambda h, i=i: vt_ref[0, h, :, i * tk:(i + 1) * tk]) for i in range(n_lat_tiles)]
    if has_ctx:
        tiles.append((lambda h: kc_ref[0, :, head(h)], lambda h: vct_ref[0, h]))
    groups = [(h, g, r) for h in range(nkv) for r in range(q_ref.shape[1] // Q_CHAIN) for g in range(G_Q)]
    chains = [(t, i) for t in range(len(tiles)) for i in range(len(groups))]
    rows = lambda i: slice(groups[i][2] * Q_CHAIN, (groups[i][2] + 1) * Q_CHAIN)
    cols = lambda i: head(groups[i][0] * G_Q + groups[i][1])

    def scores(t, i):
        return _dot_nt(tiles[t][0](groups[i][0]), q_ref[0, rows(i), cols(i)])

    m = [None] * len(groups)
    acc = [None] * len(groups)
    pending = []
    for idx in range(len(chains) + ATTN_LOOKAHEAD):
        if idx < len(chains):
            pending.append(scores(*chains[idx]))
        if idx < ATTN_LOOKAHEAD:
            continue
        t, i = chains[idx - ATTN_LOOKAHEAD]
        st = pending.pop(0)
        m_tile = jnp.max(st, axis=0, keepdims=True)
        m_new = m_tile if t == 0 else jnp.maximum(m[i], m_tile)
        pv = _dot(tiles[t][1](groups[i][0]), jnp.exp2(st - m_new).astype(BF16))
        acc[i] = pv if t == 0 else jnp.exp2(m[i] - m_new) * acc[i] + pv
        m[i] = m_new

    for i in range(len(groups)):
        out = acc[i][0:DH_A] * (1.0 / acc[i][DH_A:DH_A + 1])
        o_ref[0, rows(i), cols(i)] = out.T.astype(BF16)


def _attn_cast_kernel(n_lat_tiles, n_cast, *refs):
    attn_in, w_f32 = refs[:5], refs[5:5 + n_cast]
    o_ref, w_bf16 = refs[5 + n_cast], refs[6 + n_cast:]
    _attn_kernel(n_lat_tiles, True, 1, *attn_in, o_ref)
    for src, dst in zip(w_f32, w_bf16):
        dst[...] = src[...].astype(BF16)


def _attention(qa, ka, vt, ctx_kv, cast_weights):
    B, T, _ = qa.shape
    tq = min(Q_TILE, T)
    nq = T // tq
    n_steps = B * N_KV * nq
    n_lat_tiles = max(1, T // K_TILE)
    qspec = pl.BlockSpec((1, tq, G_Q * DH_A), lambda b, h, i: (b, i, h))
    kspec = lambda tk: pl.BlockSpec((1, tk, DH_A), lambda b, h, i: (b, 0, h))
    vspec = lambda tk: pl.BlockSpec((1, 1, V_ROWS, tk), lambda b, h, i: (b, h, 0, 0))
    slab = lambda w: pl.BlockSpec((w.shape[0] // n_steps, w.shape[1]), lambda b, h, i: ((b * N_KV + h) * nq + i, 0))
    tc = ctx_kv[0].shape[1]
    in_specs = [qspec, kspec(T), vspec(T), kspec(tc), vspec(tc)] + [slab(w) for w in cast_weights]
    out_specs = [qspec] + [slab(w) for w in cast_weights]
    out_shape = [jax.ShapeDtypeStruct((B, T, D_MODEL), BF16)] + [jax.ShapeDtypeStruct(w.shape, BF16) for w in cast_weights]
    return _call(functools.partial(_attn_cast_kernel, n_lat_tiles, len(cast_weights)), "attention_lat",
                 (B, N_KV, nq), ("parallel", "parallel", "parallel"), in_specs,
                 [qa, ka, vt, *ctx_kv, *cast_weights], out_specs, out_shape, fusible=(3, 4))


def _layer_norm(y, g, b):
    mu = jnp.mean(y, axis=-1, keepdims=True)
    yc = y - mu
    var = jnp.mean(yc * yc, axis=-1, keepdims=True)
    return yc * lax.rsqrt(var + EPS) * g + b


def _tail_kernel(x_ref, mod_ref, hm_ref, ha_ref, sgm_ref, sga_ref, ln_ref,
                 wbm_ref, wba_ref, wout_ref, wup_ref, wdown_ref, o_ref):
    mod = mod_ref[0]
    ln = ln_ref[...]
    n_sub = x_ref.shape[1] // TAIL_SUB
    rows = lambda p: slice(p * TAIL_SUB, (p + 1) * TAIL_SUB)

    def merge(p):
        merged = (sgm_ref[0, rows(p), :] * _dot(hm_ref[0, rows(p), :], wbm_ref[...])
                  + sga_ref[0, rows(p), :] * _dot(ha_ref[0, rows(p), :], wba_ref[...]))
        return ALPHA * x_ref[0, rows(p), :] + mod[2:3] * _dot(merged.astype(BF16), wout_ref[...])

    def ffn(x1):
        h = (x1 * (1.0 + mod[4:5]) + mod[3:4]).astype(BF16)
        ff = jnp.zeros_like(x1)
        for j in range(D_FF // D_MODEL):
            u = jnp.maximum(_dot(h, wup_ref[:, j * D_MODEL:(j + 1) * D_MODEL]), 0.0)
            ff = ff + _dot((u * u).astype(BF16), wdown_ref[j * D_MODEL:(j + 1) * D_MODEL, :])
        return ALPHA * x1 + mod[5:6] * ff

    y1 = [merge(p) for p in range(n_sub)]
    y2 = [ffn(_layer_norm(y1[p], ln[0:1], ln[1:2])) for p in range(n_sub)]
    for p in range(n_sub):
        o_ref[0, rows(p), :] = _layer_norm(y2[p], ln[2:3], ln[3:4])


def _tail(x, mod, mod_rows, hm, ha, sgm, sga, wts, ln, name):
    if not mod_rows[1]:
        x, hm, ha, sgm, sga = (a.reshape(1, -1, D_MODEL) for a in (x, hm, ha, sgm, sga))
    B, T, _ = x.shape
    tm = TAIL_TILE
    tok = pl.BlockSpec((1, tm, D_MODEL), lambda b, t: (b, t, 0))
    in_specs = ([tok, _mod_spec(*mod_rows), tok, tok, tok, tok, _resident(ln.shape)]
                + [_resident(w.shape) for w in wts])
    return _call(_tail_kernel, name, (B, T // tm), ("parallel", "parallel"),
                 in_specs, (x, mod, hm, ha, sgm, sga, ln, *wts), tok, jax.ShapeDtypeStruct((B, T, D_MODEL), F32))


def _rope_tables(n_tokens):
    rows = n_tokens // GRID_W
    row = np.repeat(np.arange(rows), GRID_W)
    col = np.tile(np.arange(GRID_W), rows)
    inv = ROPE_BASE ** (-np.arange(N_FREQ, dtype=np.float64) / N_FREQ)
    ang = np.stack([row, col], -1).astype(np.float64)[..., None] * inv
    ang = np.broadcast_to(ang[:, :, None, :], (n_tokens, 2, 2, N_FREQ))
    sign = np.asarray([-1.0, 1.0])[None, None, :, None]
    return (jnp.asarray(np.cos(ang).reshape(n_tokens, DH_A), F32),
            jnp.asarray((np.sin(ang) * sign).reshape(n_tokens, DH_A), F32))


def kernel(x_prompt, x_sample, cache_k, cache_v, state_C, state_n, state_m, c, c_ctx, w_mod, b_mod, w_in,
           b_gates, mlstm_norm_g, q_norm_g, k_norm_g, w_bm, w_ba, w_out, ln1_g, ln1_b, w_up, w_down,
           ln2_g, ln2_b):
    B, T, _ = x_prompt.shape
    Bd, Td, _ = x_sample.shape
    l = 0

    w = w_in[l]
    o_g = 4 * D_MODEL
    o_a = o_g + 4 * NH_M
    o_mg = o_a + (N_Q + 2 * N_KV) * DH_A
    gate_rows = np.array([4, 5, 6, 7, 12, 13, 14, 15, 0, 1, 2, 3, 8, 9, 10, 11])
    proj_small = (w[:, o_g:o_a].T[gate_rows].astype(BF16),
                  b_gates[l][gate_rows].reshape(4 * NH_M, 1),
                  q_norm_g[l].reshape(1, DH_A),
                  k_norm_g[l].reshape(1, DH_A))
    proj_big = (w[:, :2 * D_MODEL].astype(BF16),
                w[:, 2 * D_MODEL:o_g].T.astype(BF16),
                w[:, o_a:o_mg].astype(BF16),
                w[:, o_a + (N_Q + N_KV) * DH_A:o_mg].T.astype(BF16),
                w[:, o_mg:].astype(BF16))
    tail_w_f32 = (w_bm[l], w_ba[l], w_out[l], w_up[l], w_down[l])
    ln = jnp.stack([ln1_g[l], ln1_b[l], ln2_g[l], ln2_b[l]])
    gnorm = mlstm_norm_g[l].reshape(D_MODEL, 1)

    c_rows = jnp.concatenate([c_ctx[None, :], c, jnp.zeros((MOD_ROWS - 1 - Bd, D_MODEL), F32)], axis=0)
    mod = _modulation(c_rows, w_mod[l], b_mod[l]).reshape(MOD_ROWS, 6, D_MODEL)
    rows_ctx, rows_lat = (0, False), (1, True)

    ctx = _context_front(x_prompt, mod, rows_ctx, proj_small, proj_big, gnorm)
    sgm_c, sga_c, k_new, v_new, hm_c, c_new, n_new, m_new, ha_c = ctx

    (qm, km, vmt, somt, gr, qa, ka, vt, sgm, sga) = _projection(x_sample, mod, rows_lat, proj_small, proj_big, _rope_tables(Td))
    past = cache_k.shape[2]
    init = (state_C[:, l], state_n[:, l].reshape(Bd, 2, NH_M, 1, DH_M), state_m[:, l].reshape(Bd, 2, NH_M, 1, 1))
    hm, = _mlstm(qm, km, vmt, gr, somt, gnorm, init, False, 1)
    vct = jnp.transpose(cache_v[:, l], (0, 2, 3, 1)).astype(BF16)
    vct = jnp.concatenate([vct, jnp.ones((Bd, N_KV, V_ROWS - DH_A, past), BF16)], axis=2)
    ctx_kv = (cache_k[:, l].reshape(Bd, past, N_KV * DH_A).astype(BF16), vct)
    ha, *tail_w = _attention(qa, ka, vt, ctx_kv, tail_w_f32)
    y_sample = _tail(x_sample, mod, rows_lat, hm, ha, sgm, sga, tail_w, ln, "tail_lat")
    y_prompt = _tail(x_prompt, mod, rows_ctx, hm_c, ha_c, sgm_c, sga_c, tail_w, ln, "tail_ctx").reshape(x_prompt.shape)

    return (y_prompt, y_sample,
            k_new.reshape(B, 1, T, N_KV, DH_A), v_new.reshape(B, 1, T, N_KV, DH_A),
            c_new.reshape(B, 1, 2, NH_M, DH_M, DH_M), n_new.reshape(B, 1, 2, NH_M, DH_M),
            m_new.reshape(B, 1, 2, NH_M))
```
